```python
import math
import jax, jax.numpy as jnp
from jax import lax
import numpy as np

D_MODEL = 2048
BATCH = 16
SEQ = 2048
DEPTH = 4

CHUNK = 64
Q_BLOCK = 128
D_MIX = D_MODEL
GROUP_W = D_MIX // 4

FOX_HD = 64
FOX_HEADS = GROUP_W // FOX_HD

MLA_HEADS = 4
MLA_NOPE = 128
MLA_ROPE = 64
MLA_VD = GROUP_W // MLA_HEADS
MLA_Q_RANK = 384
MLA_KV_RANK = 128
ROPE_THETA = 10000.0

RWKV_HD = 64
RWKV_HEADS = GROUP_W // RWKV_HD
RWKV_W_RANK = 64
RWKV_A_RANK = 64
RWKV_G_RANK = 128
RWKV_GN_EPS = 64e-5

GDN_HD = 128
GDN_HEADS = GROUP_W // GDN_HD
GDN_CONV = 4

N_GROUPS = 4
EXP_PER_GROUP = 8
N_EXPERTS = N_GROUPS * EXP_PER_GROUP
TOP_K = 2
D_EXPERT = 512
MOE_BLOCK = 128

ALPHA = (2 * DEPTH) ** 0.25
BETA = (8 * DEPTH) ** -0.25
LN_EPS = 1e-5
RMS_EPS = 1e-6

FOX_COLS = (GROUP_W, GROUP_W, GROUP_W, FOX_HEADS)
MLA_COLS = (MLA_Q_RANK, MLA_KV_RANK, MLA_ROPE)
RWKV_COLS = (GROUP_W, GROUP_W, GROUP_W, RWKV_W_RANK, RWKV_A_RANK, RWKV_G_RANK)
GDN_COLS = (GROUP_W, GROUP_W, GROUP_W, GROUP_W, GDN_HEADS, GDN_HEADS)
GROUP_IN_COLS = (sum(FOX_COLS), sum(MLA_COLS), sum(RWKV_COLS), sum(GDN_COLS))
N_IN = sum(GROUP_IN_COLS)

kernel_name = 'hybrid_chunk_causal_hier_moe_trunk'


def _split(u, sizes):
    cuts = np.cumsum(np.asarray(sizes))[:-1].tolist()
    return jnp.split(u, cuts, axis=-1)


def layer_norm(x, g, b):
    xf = x.astype(jnp.float32)
    mu = jnp.mean(xf, axis=-1, keepdims=True)
    var = jnp.mean(jnp.square(xf - mu), axis=-1, keepdims=True)
    return ((xf - mu) * lax.rsqrt(var + LN_EPS) * g + b).astype(x.dtype)


def rms_norm(x, g):
    xf = x.astype(jnp.float32)
    return (xf * lax.rsqrt(jnp.mean(xf * xf, axis=-1, keepdims=True) + RMS_EPS) * g).astype(x.dtype)


def l2_normalize(x):
    xf = x.astype(jnp.float32)
    return (xf * lax.rsqrt(jnp.sum(xf * xf, axis=-1, keepdims=True) + 1e-6)).astype(x.dtype)


def time_shift(u):
    return jnp.pad(u, ((0, 0), (1, 0), (0, 0)))[:, :-1]


def causal_depthwise_conv(u, w):
    K = w.shape[0]
    S = u.shape[1]
    up = jnp.pad(u, ((0, 0), (K - 1, 0), (0, 0)))
    return sum(up[:, j:j + S] * w[j] for j in range(K))


def rope(x, positions):
    half = x.shape[-1] // 2
    inv_freq = ROPE_THETA ** (-jnp.arange(half, dtype=jnp.float32) / half)
    ang = positions.astype(jnp.float32)[..., None] * inv_freq
    cos = jnp.cos(ang)[:, :, None, :]
    sin = jnp.sin(ang)[:, :, None, :]
    x1 = x[..., :half].astype(jnp.float32)
    x2 = x[..., half:].astype(jnp.float32)
    return jnp.concatenate([x1 * cos - x2 * sin, x2 * cos + x1 * sin], axis=-1).astype(x.dtype)


def block_causal_attention(q, k, v, chunk, log_fgate_cum=None):
    B, H, S, Dk = q.shape
    nb = S // Q_BLOCK
    scale = Dk ** -0.5
    key_chunk = jnp.arange(S) // chunk
    qb = jnp.moveaxis(q.reshape(B, H, nb, Q_BLOCK, Dk), 2, 0)
    use_decay = log_fgate_cum is not None

    def attend(args):
        i, qi = args
        s = jnp.einsum('bhqd,bhkd->bhqk', qi, k, preferred_element_type=jnp.float32) * scale
        if use_decay:
            Fi = lax.dynamic_slice_in_dim(log_fgate_cum, i * Q_BLOCK, Q_BLOCK, axis=2)
            s = s + (Fi[..., :, None] - log_fgate_cum[..., None, :])
        q_chunk = (i * Q_BLOCK + jnp.arange(Q_BLOCK)) // chunk
        allowed = key_chunk[None, :] <= q_chunk[:, None]
        s = jnp.where(allowed, s, -1e30)
        p = jax.nn.softmax(s, axis=-1)
        return jnp.einsum('bhqk,bhkd->bhqd', p.astype(v.dtype), v)

    out = lax.map(attend, (jnp.arange(nb), qb))
    return jnp.moveaxis(out, 0, 2).reshape(B, H, S, v.shape[-1])


def fox_mixer(u, b_f, out_g):
    B, S, _ = u.shape
    q, k, v, f = _split(u, FOX_COLS)
    heads = lambda t: t.reshape(B, S, FOX_HEADS, FOX_HD).transpose(0, 2, 1, 3)
    log_f = jax.nn.log_sigmoid(f.astype(jnp.float32) + b_f)
    F = jnp.cumsum(log_f, axis=1).transpose(0, 2, 1)
    o = block_causal_attention(heads(q), heads(k), heads(v), 1, F)
    o = o.transpose(0, 2, 1, 3).reshape(B, S, GROUP_W)
    return rms_norm(o, out_g).astype(u.dtype)


def mla_mixer(u, positions, q_norm_g, kv_norm_g, w_uq, w_ukv, out_g):
    B, S, _ = u.shape
    c_q, c_kv, k_pe = _split(u, MLA_COLS)
    q = (rms_norm(c_q, q_norm_g) @ w_uq).reshape(B, S, MLA_HEADS, MLA_NOPE + MLA_ROPE)
    q_nope = q[..., :MLA_NOPE]
    q_pe = rope(q[..., MLA_NOPE:], positions)
    kv = (rms_norm(c_kv, kv_norm_g) @ w_ukv).reshape(B, S, MLA_HEADS, MLA_NOPE + MLA_VD)
    k_nope = kv[..., :MLA_NOPE]
    v = kv[..., MLA_NOPE:]
    k_pe = rope(k_pe[:, :, None, :], positions)
    q_full = jnp.concatenate([q_nope, q_pe], axis=-1)
    k_full = jnp.concatenate([k_nope, jnp.broadcast_to(k_pe, (B, S, MLA_HEADS, MLA_ROPE))], axis=-1)
    o = block_causal_attention(q_full.transpose(0, 2, 1, 3), k_full.transpose(0, 2, 1, 3),
                               v.transpose(0, 2, 1, 3), CHUNK)
    o = o.transpose(0, 2, 1, 3).reshape(B, S, GROUP_W)
    return rms_norm(o, out_g).astype(u.dtype)


def rwkv7_recurrence(r, decay, k, v, a, b):
    B, S, H, N = r.shape

    def step(state, inp):
        r_t, w_t, k_t, v_t, a_t, b_t = inp
        sa = jnp.einsum('bhij,bhj->bhi', state, a_t)
        state = (state * w_t[:, :, None, :] + sa[..., None] * b_t[:, :, None, :]
                 + v_t[..., None] * k_t[:, :, None, :])
        return state, jnp.einsum('bhij,bhj->bhi', state, r_t)

    xs = tuple(jnp.moveaxis(t.astype(jnp.float32), 1, 0) for t in (r, decay, k, v, a, b))
    _, y = lax.scan(step, jnp.zeros((B, H, N, N), jnp.float32), xs)
    return jnp.moveaxis(y, 0, 1)


def rwkv7_mixer(u, mu, w0, w2, a0, a2, g2, k_k, k_a, r_k, ln_g, ln_b):
    B, S, _ = u.shape
    u = u + mu * (time_shift(u) - u)
    r, k, v, w_lo, a_lo, g_lo = _split(u, RWKV_COLS)
    w = -jax.nn.softplus(-(w0 + jnp.tanh(w_lo) @ w2)) - 0.5
    decay = jnp.exp(-jnp.exp(w.astype(jnp.float32)))
    a = jax.nn.sigmoid(a0 + a_lo @ a2)
    g = jax.nn.sigmoid(g_lo) @ g2
    heads = lambda t: t.reshape(B, S, RWKV_HEADS, RWKV_HD)
    kk = l2_normalize(heads(k * k_k))
    k = k * (1.0 + (a - 1.0) * k_a)
    r_h, k_h, v_h, a_h = heads(r), heads(k), heads(v), heads(a)
    y = rwkv7_recurrence(r_h, heads(decay), k_h, v_h, -kk, kk * a_h)
    m = jnp.mean(y, axis=-1, keepdims=True)
    var = jnp.mean(jnp.square(y - m), axis=-1, keepdims=True)
    y = ((y - m) * lax.rsqrt(var + RWKV_GN_EPS)).reshape(B, S, GROUP_W) * ln_g + ln_b
    bonus = jnp.sum(r_h * k_h * r_k.reshape(RWKV_HEADS, RWKV_HD), axis=-1, keepdims=True) * v_h
    return ((y + bonus.reshape(B, S, GROUP_W)) * g).astype(u.dtype)


def gated_delta_chunked(q, k, v, g, beta):
    in_dtype = q.dtype
    B, H, S, Dk = q.shape
    Dv = v.shape[-1]
    n = S // CHUNK
    q, k, v, g, beta = (t.astype(jnp.float32) for t in (q, k, v, g, beta))
    q = q.reshape(B, H, n, CHUNK, Dk)
    k = k.reshape(B, H, n, CHUNK, Dk)
    v = v.reshape(B, H, n, CHUNK, Dv)
    beta = beta.reshape(B, H, n, CHUNK)
    gc = jnp.cumsum(g.reshape(B, H, n, CHUNK), axis=-1)
    lower = jnp.tril(jnp.ones((CHUNK, CHUNK), bool))
    strict = jnp.tril(jnp.ones((CHUNK, CHUNK), bool), -1)
    decay = jnp.exp(jnp.where(lower, gc[..., :, None] - gc[..., None, :], -jnp.inf))
    kb = k * beta[..., None]
    vb = v * beta[..., None]
    L = jnp.where(strict, jnp.einsum('bhncd,bhnsd->bhncs', kb, k) * decay, 0.0)
    rhs = jnp.concatenate([vb, kb * jnp.exp(gc)[..., None]], axis=-1)
    sol = lax.linalg.triangular_solve(L, rhs, left_side=True, lower=True, unit_diagonal=True)
    u_val, w_cum = sol[..., :Dv], sol[..., Dv:]
    attn = jnp.einsum('bhncd,bhnsd->bhncs', q, k) * decay
    q_dec = q * jnp.exp(gc)[..., None]
    k_dec = k * jnp.exp(gc[..., -1:] - gc)[..., None]
    g_last = jnp.exp(gc[..., -1])

    def step(state, inp):
        u_i, w_i, attn_i, qd_i, kd_i, gl_i = inp
        v_new = u_i - jnp.einsum('bhcd,bhde->bhce', w_i, state)
        o = jnp.einsum('bhcd,bhde->bhce', qd_i, state) + jnp.einsum('bhcs,bhse->bhce', attn_i, v_new)
        state = state * gl_i[..., None, None] + jnp.einsum('bhcd,bhce->bhde', kd_i, v_new)
        return state, o

    xs = tuple(jnp.moveaxis(t, 2, 0) for t in (u_val, w_cum, attn, q_dec, k_dec, g_last))
    _, o = lax.scan(step, jnp.zeros((B, H, Dk, Dv), jnp.float32), xs)
    return jnp.moveaxis(o, 0, 2).reshape(B, H, S, Dv).astype(in_dtype)


def gdn_mixer(u, conv_w, a_log, dt_bias, norm_g):
    B, S, _ = u.shape
    q, k, v, z, b, a = _split(u, GDN_COLS)
    qkv = jax.nn.silu(causal_depthwise_conv(jnp.concatenate([q, k, v], axis=-1), conv_w))
    q, k, v = jnp.split(qkv, 3, axis=-1)
    heads = lambda t: t.reshape(B, S, GDN_HEADS, GDN_HD).transpose(0, 2, 1, 3)
    q = l2_normalize(heads(q)) * (GDN_HD ** -0.5)
    k = l2_normalize(heads(k))
    beta = jax.nn.sigmoid(b.astype(jnp.float32)).transpose(0, 2, 1)
    g = (-jnp.exp(a_log.astype(jnp.float32)) * jax.nn.softplus(a.astype(jnp.float32) + dt_bias)).transpose(0, 2, 1)
    o = gated_delta_chunked(q, k, heads(v), g, beta)
    o = rms_norm(o, norm_g) * jax.nn.silu(heads(z))
    return o.transpose(0, 2, 1, 3).reshape(B, S, GROUP_W).astype(u.dtype)


def hier_moe(x, w_grp, b_grp, w_exp, b_exp, w_gate, w_up, w_down):
    B, S, D = x.shape
    T = B * S
    xf = x.reshape(T, D)
    grp_prob = jax.nn.softmax((xf @ w_grp).astype(jnp.float32) + b_grp, axis=-1)
    p_grp, grp = lax.top_k(grp_prob, 1)
    exp_logits = ((xf @ w_exp).astype(jnp.float32) + b_exp).reshape(T, N_GROUPS, EXP_PER_GROUP)
    in_grp = exp_logits[jnp.arange(T), grp[:, 0]]
    top_logit, top_local = lax.top_k(in_grp, TOP_K)
    gate = p_grp * jax.nn.softmax(top_logit, axis=-1)
    expert = grp * EXP_PER_GROUP + top_local

    A = T * TOP_K
    n_slots = ((A + N_EXPERTS * (MOE_BLOCK - 1) + MOE_BLOCK - 1) // MOE_BLOCK) * MOE_BLOCK
    n_blocks = n_slots // MOE_BLOCK
    flat_e = expert.reshape(A)
    order = jnp.argsort(flat_e)
    e_sorted = flat_e[order]
    tok_sorted = (order // TOP_K).astype(jnp.int32)
    w_sorted = gate.reshape(A)[order]
    counts = jnp.bincount(flat_e, length=N_EXPERTS)
    padded = (counts + MOE_BLOCK - 1) // MOE_BLOCK * MOE_BLOCK
    start = jnp.cumsum(counts) - counts
    pend = jnp.cumsum(padded)
    pstart = pend - padded
    dest = pstart[e_sorted] + jnp.arange(A) - start[e_sorted]
    token_of_slot = jnp.full((n_slots,), T, jnp.int32).at[dest].set(tok_sorted)
    weight_of_slot = jnp.zeros((n_slots,), jnp.float32).at[dest].set(w_sorted)
    block_expert = jnp.minimum(jnp.searchsorted(pend, jnp.arange(n_blocks) * MOE_BLOCK, side='right'),
                               N_EXPERTS - 1)
    x_pad = jnp.concatenate([xf, jnp.zeros((1, D), xf.dtype)], axis=0)

    def expert_block(args):
        tok, e = args
        xb = x_pad[tok]
        h = jax.nn.silu(xb @ w_gate[e]) * (xb @ w_up[e])
        return h @ w_down[e]

    y_slots = lax.map(expert_block, (token_of_slot.reshape(n_blocks, MOE_BLOCK), block_expert))
    y_slots = y_slots.reshape(n_slots, D) * weight_of_slot[:, None].astype(x.dtype)
    y = jax.ops.segment_sum(y_slots, token_of_slot, num_segments=T + 1)[:T]
    return y.reshape(B, S, D).astype(x.dtype)


def setup_inputs(seed: int = 0) -> dict:
    key = jax.random.key(seed)
    ks = iter(jax.random.split(key, 64))
    L = DEPTH
    nrm = lambda shape, scale: jax.random.normal(next(ks), shape, jnp.float32) * scale
    gain = lambda shape: 1.0 + nrm(shape, 0.02)
    unif = lambda shape, lo, hi: jax.random.uniform(next(ks), shape, jnp.float32, lo, hi)
    x = nrm((BATCH, SEQ, D_MODEL), 1.0)
    start = jax.random.randint(next(ks), (BATCH, 1), 0, 4096, dtype=jnp.int32)
    positions = start + jnp.arange(SEQ, dtype=jnp.int32)[None, :]
    dt = jnp.exp(unif((L, GDN_HEADS), math.log(1e-3), math.log(1e-1)))
    return {
        'x': x,
        'positions': positions,
        'w_in': nrm((L, D_MODEL, N_IN), D_MODEL ** -0.5),
        'fox_b_f': unif((L, FOX_HEADS), 1.0, 5.0),
        'fox_out_g': gain((L, GROUP_W)),
        'mla_q_norm_g': gain((L, MLA_Q_RANK)),
        'mla_kv_norm_g': gain((L, MLA_KV_RANK)),
        'mla_w_uq': nrm((L, MLA_Q_RANK, MLA_HEADS * (MLA_NOPE + MLA_ROPE)), MLA_Q_RANK ** -0.5),
        'mla_w_ukv': nrm((L, MLA_KV_RANK, MLA_HEADS * (MLA_NOPE + MLA_VD)), MLA_KV_RANK ** -0.5),
        'mla_out_g': gain((L, GROUP_W)),
        'rwkv_mu': unif((L, sum(RWKV_COLS)), 0.0, 1.0),
        'rwkv_w0': unif((L, GROUP_W), -6.0, -1.0),
        'rwkv_w2': nrm((L, RWKV_W_RANK, GROUP_W), 0.5 * RWKV_W_RANK ** -0.5),
        'rwkv_a0': nrm((L, GROUP_W), 0.1),
        'rwkv_a2': nrm((L, RWKV_A_RANK, GROUP_W), 0.5 * RWKV_A_RANK ** -0.5),
        'rwkv_g2': nrm((L, RWKV_G_RANK, GROUP_W), RWKV_G_RANK ** -0.5),
        'rwkv_k_k': 0.85 + nrm((L, GROUP_W), 0.05),
        'rwkv_k_a': gain((L, GROUP_W)),
        'rwkv_r_k': nrm((L, GROUP_W), 0.1),
        'rwkv_ln_g': gain((L, GROUP_W)),
        'rwkv_ln_b': nrm((L, GROUP_W), 0.02),
        'gdn_conv_w': nrm((L, GDN_CONV, 3 * GROUP_W), GDN_CONV ** -0.5),
        'gdn_a_log': jnp.log(unif((L, GDN_HEADS), 1.0, 16.0)),
        'gdn_dt_bias': jnp.log(jnp.expm1(dt)),
        'gdn_norm_g': gain((L, GDN_HD)),
        'w_out': nrm((L, D_MIX, D_MODEL), BETA * D_MIX ** -0.5),
        'ln1_g': gain((L, D_MODEL)),
        'ln1_b': nrm((L, D_MODEL), 0.02),
        'moe_w_grp': nrm((L, D_MODEL, N_GROUPS), D_MODEL ** -0.5),
        'moe_b_grp': nrm((L, N_GROUPS), 0.01),
        'moe_w_exp': nrm((L, D_MODEL, N_EXPERTS), D_MODEL ** -0.5),
        'moe_b_exp': nrm((L, N_EXPERTS), 0.01),
        'moe_w_gate': nrm((L, N_EXPERTS, D_MODEL, D_EXPERT), D_MODEL ** -0.5),
        'moe_w_up': nrm((L, N_EXPERTS, D_MODEL, D_EXPERT), D_MODEL ** -0.5),
        'moe_w_down': nrm((L, N_EXPERTS, D_EXPERT, D_MODEL), BETA * D_EXPERT ** -0.5),
        'ln2_g': gain((L, D_MODEL)),
        'ln2_b': nrm((L, D_MODEL), 0.02),
    }


def reference(x, positions, w_in, fox_b_f, fox_out_g, mla_q_norm_g, mla_kv_norm_g, mla_w_uq,
              mla_w_ukv, mla_out_g, rwkv_mu, rwkv_w0, rwkv_w2, rwkv_a0, rwkv_a2, rwkv_g2, rwkv_k_k,
              rwkv_k_a, rwkv_r_k, rwkv_ln_g, rwkv_ln_b, gdn_conv_w, gdn_a_log, gdn_dt_bias, gdn_norm_g,
              w_out, ln1_g, ln1_b, moe_w_grp, moe_b_grp, moe_w_exp, moe_b_exp, moe_w_gate, moe_w_up,
              moe_w_down, ln2_g, ln2_b):
    for i in range(DEPTH):
        u = x @ w_in[i]
        u_fox, u_mla, u_rwkv, u_gdn = _split(u, GROUP_IN_COLS)
        y_fox = fox_mixer(u_fox, fox_b_f[i], fox_out_g[i])
        y_mla = mla_mixer(u_mla, positions, mla_q_norm_g[i], mla_kv_norm_g[i], mla_w_uq[i],
                          mla_w_ukv[i], mla_out_g[i])
        y_rwkv = rwkv7_mixer(u_rwkv, rwkv_mu[i], rwkv_w0[i], rwkv_w2[i], rwkv_a0[i], rwkv_a2[i],
                             rwkv_g2[i], rwkv_k_k[i], rwkv_k_a[i], rwkv_r_k[i], rwkv_ln_g[i], rwkv_ln_b[i])
        y_gdn = gdn_mixer(u_gdn, gdn_conv_w[i], gdn_a_log[i], gdn_dt_bias[i], gdn_norm_g[i])
        mixed = jnp.concatenate([y_fox, y_mla, y_rwkv, y_gdn], axis=-1)
        x = layer_norm(ALPHA * x + mixed @ w_out[i], ln1_g[i], ln1_b[i])
        moe_out = hier_moe(x, moe_w_grp[i], moe_b_grp[i], moe_w_exp[i], moe_b_exp[i],
                           moe_w_gate[i], moe_w_up[i], moe_w_down[i])
        x = layer_norm(ALPHA * x + moe_out, ln2_g[i], ln2_b[i])
    return x
```

```python
import functools
import math

import jax
import jax.numpy as jnp
from jax import lax
from jax.experimental import pallas as pl
from jax.experimental.pallas import tpu as pltpu

F32 = jnp.float32
BF16 = jnp.bfloat16

D_MODEL = 2048
GROUP_W = 512
FOX_HD, FOX_HEADS = 64, 8
MLA_HEADS, MLA_NOPE, MLA_ROPE, MLA_VD = 4, 128, 64, 128
MLA_Q_RANK, MLA_KV_RANK = 384, 128
ROPE_THETA = 10000.0
RWKV_HD, RWKV_HEADS = 64, 8
RWKV_GN_EPS = 64e-5
GDN_HD, GDN_HEADS, GDN_CONV = 128, 4, 4
N_GROUPS, EXP_PER_GROUP, TOP_K, D_EXPERT = 4, 8, 2, 512
N_EXPERTS = N_GROUPS * EXP_PER_GROUP
CHUNK = 64
LN_EPS = 1e-5
RMS_EPS = 1e-6

LANE = 128
VMEM_LIMIT_BYTES = 56 * 1024 * 1024
ROW_TILE = 512
ATTN_TILE = 256
SCAN_TILE = 256
MOE_TILE = 256


def _cparams(*sem):
    return pltpu.CompilerParams(dimension_semantics=sem, vmem_limit_bytes=VMEM_LIMIT_BYTES)


def _bdot(a, b):
    return jnp.dot(a.astype(BF16), b.astype(BF16), preferred_element_type=F32)


def _bdot_nt(a, b):
    return lax.dot_general(a.astype(BF16), b.astype(BF16), (((1,), (1,)), ((), ())), preferred_element_type=F32)


def _bdot_tn(a, b):
    return lax.dot_general(a.astype(BF16), b.astype(BF16), (((0,), (0,)), ((), ())), preferred_element_type=F32)


def _split3(x):
    hi = x.astype(BF16)
    r1 = x - hi.astype(F32)
    mid = r1.astype(BF16)
    lo = (r1 - mid.astype(F32)).astype(BF16)
    return hi, mid, lo


def _dot_exact_lhs(m, x):
    hi, mid, lo = _split3(x)
    d = lambda p: jnp.dot(m, p, preferred_element_type=F32)
    return d(hi) + d(mid) + d(lo)


def _sigmoid(x):
    return 1.0 / (1.0 + jnp.exp(-x))


def _softplus(x):
    return jnp.maximum(x, 0.0) + jnp.log(1.0 + jnp.exp(-jnp.abs(x)))


def _chunk_masks(n, c):
    r = lax.broadcasted_iota(jnp.int32, (n, n), 0)
    col = lax.broadcasted_iota(jnp.int32, (n, n), 1)
    same = (r // c) == (col // c)
    lower = jnp.logical_and(same, col <= r)
    strict = jnp.logical_and(same, col < r)
    return r, col, same, lower, strict


def _tri_inv(m, r, col, c):
    blk = lambda s: (r // s) == (col // s)
    eye = jnp.where(r == col, 1.0, 0.0).astype(F32)
    md = jnp.where(blk(8), m, 0.0)
    x = eye + md
    m2 = _bdot(md, md)
    x = x + _bdot(m2, x)
    m4 = _bdot(m2, m2)
    x = x + _bdot(m4, x)
    s = 8
    while s < c:
        off = jnp.where(jnp.logical_and(blk(2 * s), jnp.logical_not(blk(s))), m, 0.0)
        x = x + _bdot(_bdot(x, off), x)
        s *= 2
    return x


def _inproj_kernel(x_ref, w_ref, o_ref, *tail_refs, tail):
    acc = jnp.dot(x_ref[...], w_ref[...], preferred_element_type=F32)
    n = acc.shape[1]
    if tail:
        o_ref[...] = acc[:, : n - tail].astype(o_ref.dtype)
        tail_refs[0][...] = acc[:, n - tail :]
    else:
        o_ref[...] = acc.astype(o_ref.dtype)


def _inproj(xb, w, tail):
    t, d = xb.shape
    n = w.shape[1]
    tm = min(ROW_TILE, t)
    out_shape = [jax.ShapeDtypeStruct((t, n - tail), BF16)]
    out_specs = [pl.BlockSpec((tm, n - tail), lambda i: (i, 0))]
    if tail:
        out_shape.append(jax.ShapeDtypeStruct((t, tail), F32))
        out_specs.append(pl.BlockSpec((tm, tail), lambda i: (i, 0)))
    return pl.pallas_call(
        functools.partial(_inproj_kernel, tail=tail),
        grid=(t // tm,),
        in_specs=[pl.BlockSpec((tm, d), lambda i: (i, 0)), pl.BlockSpec((d, n), lambda i: (0, 0))],
        out_specs=out_specs,
        out_shape=out_shape,
        compiler_params=_cparams("parallel"),
        name="inproj",
    )(xb, w)


def _attn_kernel(*refs, heads, dk, dv, chunk, tq, use_bias):
    if use_bias:
        q_ref, k_ref, v_ref, bias_ref, g_ref, o_ref = refs
    else:
        q_ref, k_ref, v_ref, g_ref, o_ref = refs
        bias_ref = None
    i = pl.program_id(1)
    rq = lax.broadcasted_iota(jnp.int32, (tq, tq), 0)
    ck = lax.broadcasted_iota(jnp.int32, (tq, tq), 1)
    allowed = (ck // chunk) <= (rq // chunk)
    diag = pl.multiple_of(i * tq, tq)

    outs = []
    for h in range(heads):
        q = q_ref[:, h * dk : (h + 1) * dk]

        def scores(off, q=q, h=h):
            k = k_ref[pl.ds(off, tq), h * dk : (h + 1) * dk]
            s = lax.dot_general(q, k, (((1,), (1,)), ((), ())), preferred_element_type=F32)
            if use_bias:
                s = s - bias_ref[h : h + 1, pl.ds(off, tq)]
            return s

        def update(carry, s, off, h=h):
            m, l, acc = carry
            m_new = jnp.maximum(m, jnp.max(s, axis=-1, keepdims=True))
            p = jnp.exp(s - m_new)
            alpha = jnp.exp(m - m_new)
            v = v_ref[pl.ds(off, tq), h * dv : (h + 1) * dv]
            acc = alpha * acc + jnp.dot(p.astype(BF16), v, preferred_element_type=F32)
            l = alpha * l + jnp.sum(p, axis=-1, keepdims=True)
            return m_new, l, acc

        def body(j, carry):
            off = pl.multiple_of(j * tq, tq)
            return update(carry, scores(off), off)

        init = (jnp.full((tq, 1), -1e30, F32), jnp.zeros((tq, 1), F32), jnp.zeros((tq, dv), F32))
        carry = lax.fori_loop(0, i, body, init)
        s = jnp.where(allowed, scores(diag), -1e30)
        _, l, acc = update(carry, s, diag)
        outs.append(acc / l)
    o = jnp.concatenate(outs, axis=-1)
    o = o * lax.rsqrt(jnp.mean(o * o, axis=-1, keepdims=True) + RMS_EPS) * g_ref[...]
    o_ref[...] = o.astype(o_ref.dtype)


def _attention(q_arr, q_col, k_arr, k_col, v_arr, v_col, bias, gain, *, batch, seq, heads, dk, dv, chunk):
    tq = min(ATTN_TILE, seq)
    nq = seq // tq
    t = batch * seq
    in_specs = [
        pl.BlockSpec((tq, heads * dk), lambda b, i: (b * nq + i, q_col)),
        pl.BlockSpec((seq, heads * dk), lambda b, i: (b, k_col)),
        pl.BlockSpec((seq, heads * dv), lambda b, i: (b, v_col)),
    ]
    args = [q_arr, k_arr, v_arr]
    if bias is not None:
        in_specs.append(pl.BlockSpec((None, heads, seq), lambda b, i: (b, 0, 0)))
        args.append(bias)
    in_specs.append(pl.BlockSpec((1, heads * dv), lambda b, i: (0, 0)))
    args.append(gain)
    return pl.pallas_call(
        functools.partial(_attn_kernel, heads=heads, dk=dk, dv=dv, chunk=chunk, tq=tq, use_bias=bias is not None),
        grid=(batch, nq),
        in_specs=in_specs,
        out_specs=pl.BlockSpec((tq, heads * dv), lambda b, i: (b * nq + i, 0)),
        out_shape=jax.ShapeDtypeStruct((t, heads * dv), BF16),
        compiler_params=_cparams("parallel", "arbitrary"),
        name="attention",
    )(*args)


def _mla_prep_kernel(u_ref, cs_ref, sn_ref, qg_ref, kvg_ref, wq_ref, wkv_ref, q_ref, k_ref, v_ref):
    u = u_ref[...].astype(F32)
    cs = cs_ref[...]
    sn = sn_ref[...]

    def rms(x, g):
        return x * lax.rsqrt(jnp.mean(x * x, axis=-1, keepdims=True) + RMS_EPS) * g

    qo = _bdot(rms(u[:, :MLA_Q_RANK], qg_ref[...]), wq_ref[...])
    kvo = _bdot(rms(u[:, MLA_Q_RANK : MLA_Q_RANK + MLA_KV_RANK], kvg_ref[...]), wkv_ref[...])
    c0 = MLA_Q_RANK + MLA_KV_RANK
    kpe = (u[:, c0 : c0 + LANE] * cs + u[:, c0 + LANE : c0 + 2 * LANE] * sn).astype(k_ref.dtype)
    nn = MLA_HEADS * MLA_NOPE
    for h in range(MLA_HEADS):
        a = h * 2 * LANE
        q_ref[:, a : a + LANE] = qo[:, h * LANE : (h + 1) * LANE].astype(q_ref.dtype)
        qpe = qo[:, nn + h * LANE : nn + (h + 1) * LANE] * cs + qo[:, 2 * nn + h * LANE : 2 * nn + (h + 1) * LANE] * sn
        q_ref[:, a + LANE : a + 2 * LANE] = qpe.astype(q_ref.dtype)
        k_ref[:, a : a + LANE] = kvo[:, h * LANE : (h + 1) * LANE].astype(k_ref.dtype)
        k_ref[:, a + LANE : a + 2 * LANE] = kpe
    v_ref[...] = kvo[:, nn:].astype(v_ref.dtype)


def _mla_prep(u_mla, cs, sn, qg, kvg, wq, wkv):
    t = u_mla.shape[0]
    tm = min(ROW_TILE, t)
    row = lambda n: pl.BlockSpec((tm, n), lambda i: (i, 0))
    full = lambda a: pl.BlockSpec(a.shape, lambda i: (0,) * a.ndim)
    wide = MLA_HEADS * 2 * LANE
    return pl.pallas_call(
        _mla_prep_kernel,
        grid=(t // tm,),
        in_specs=[row(u_mla.shape[1]), row(LANE), row(LANE), full(qg), full(kvg), full(wq), full(wkv)],
        out_specs=[row(wide), row(wide), row(MLA_HEADS * MLA_VD)],
        out_shape=[
            jax.ShapeDtypeStruct((t, wide), BF16),
            jax.ShapeDtypeStruct((t, wide), BF16),
            jax.ShapeDtypeStruct((t, MLA_HEADS * MLA_VD), BF16),
        ],
        compiler_params=_cparams("parallel"),
        name="mla_prep",
    )(u_mla, cs, sn, qg, kvg, wq, wkv)


def _gdn_kernel(u_ref, t_ref, cw_ref, alog_ref, dtb_ref, ng_ref, o_ref, xs_ref, st_ref, *, tt, c):
    i = pl.program_id(1)
    gw, hd = GROUP_W, GDN_HD

    @pl.when(i == 0)
    def _():
        xs_ref[0:8, :] = jnp.zeros((8, 3 * gw), F32)
        st_ref[...] = jnp.zeros(st_ref.shape, F32)

    xs_ref[8 : 8 + tt, :] = u_ref[:, : 3 * gw].astype(F32)
    cw = cw_ref[...]
    conv = cw[0:1, :] * xs_ref[5 : 5 + tt, :]
    for j in range(1, GDN_CONV):
        conv = conv + cw[j : j + 1, :] * xs_ref[5 + j : 5 + j + tt, :]
    xs_ref[0:8, :] = xs_ref[tt : tt + 8, :]
    qkv = conv * _sigmoid(conv)

    tail = t_ref[...]
    beta = _sigmoid(tail[:, :LANE])
    g = -jnp.exp(alog_ref[...]) * _softplus(tail[:, LANE:] + dtb_ref[...])

    r, col, same, lower, strict = _chunk_masks(tt, c)
    tri = jnp.where(lower, 1.0, 0.0).astype(BF16)
    ones_blk = jnp.where(same, 1.0, 0.0).astype(BF16)
    gc = _dot_exact_lhs(tri, g)
    gtot = _dot_exact_lhs(ones_blk, g)
    gct = gc.T

    def l2n(x):
        return x * lax.rsqrt(jnp.sum(x * x, axis=-1, keepdims=True) + 1e-6)

    for h in range(GDN_HEADS):
        sl = slice(h * hd, (h + 1) * hd)
        qh = l2n(qkv[:, sl]) * (hd**-0.5)
        kh = l2n(qkv[:, gw + h * hd : gw + (h + 1) * hd])
        vh = qkv[:, 2 * gw + h * hd : 2 * gw + (h + 1) * hd]
        gcol = gc[:, h : h + 1]
        dec = jnp.exp(jnp.minimum(gcol - gct[h : h + 1, :], 0.0))
        bcol = beta[:, h : h + 1]
        kb = kh * bcol
        vb = vh * bcol
        lmat = jnp.where(strict, _bdot_nt(kb, kh) * dec, 0.0)
        tinv = _tri_inv(-lmat, r, col, c)
        eg = jnp.exp(gcol)
        uval = _bdot(tinv, vb)
        wcum = _bdot(tinv, kb * eg)
        attn = jnp.where(lower, _bdot_nt(qh, kh) * dec, 0.0)
        qd = qh * eg
        kd = kh * jnp.exp(gtot[:, h : h + 1] - gcol)
        s = st_ref[h]
        outs = []
        for cc in range(tt // c):
            rs = slice(cc * c, (cc + 1) * c)
            sb = s.astype(BF16)
            vnew = uval[rs] - _bdot(wcum[rs], sb)
            outs.append(_bdot(qd[rs], sb) + _bdot(attn[rs, rs], vnew))
            glast = jnp.exp(gtot[cc * c : cc * c + 1, h : h + 1])
            s = s * glast + _bdot_tn(kd[rs], vnew)
        st_ref[h] = s
        o = jnp.concatenate(outs, axis=0)
        o = o * lax.rsqrt(jnp.mean(o * o, axis=-1, keepdims=True) + RMS_EPS) * ng_ref[...]
        z = u_ref[:, 3 * gw + h * hd : 3 * gw + (h + 1) * hd].astype(F32)
        o_ref[:, sl] = (o * (z * _sigmoid(z))).astype(o_ref.dtype)


def _gdn(u_gdn, tail, conv_w, a_log, dt_bias, norm_g, *, batch, seq):
    tt = min(SCAN_TILE, seq)
    nt = seq // tt
    t = batch * seq
    full = lambda a: pl.BlockSpec(a.shape, lambda b, i: (0,) * a.ndim)
    return pl.pallas_call(
        functools.partial(_gdn_kernel, tt=tt, c=CHUNK),
        grid=(batch, nt),
        in_specs=[
            pl.BlockSpec((tt, 4 * GROUP_W), lambda b, i: (b * nt + i, 0)),
            pl.BlockSpec((tt, 2 * LANE), lambda b, i: (b * nt + i, 0)),
            full(conv_w),
            full(a_log),
            full(dt_bias),
            full(norm_g),
        ],
        out_specs=pl.BlockSpec((tt, GROUP_W), lambda b, i: (b * nt + i, 0)),
        out_shape=jax.ShapeDtypeStruct((t, GROUP_W), BF16),
        scratch_shapes=[pltpu.VMEM((tt + 8, 3 * GROUP_W), F32), pltpu.VMEM((GDN_HEADS, GDN_HD, GDN_HD), F32)],
        compiler_params=_cparams("parallel", "arbitrary"),
        name="gdn",
    )(u_gdn, tail, conv_w, a_log, dt_bias, norm_g)


def _rwkv_kernel(u_ref, mu_ref, w0_ref, w2_ref, a0_ref, a2_ref, g2_ref, kk_ref, ka_ref, rk_ref, lng_ref, lnb_ref,
                 e_ref, o_ref, xs_ref, st_ref, *, tt, c):
    i = pl.program_id(1)
    gw, hd = GROUP_W, RWKV_HD

    @pl.when(i == 0)
    def _():
        xs_ref[0:8, :] = jnp.zeros((8, xs_ref.shape[1]), F32)
        st_ref[...] = jnp.zeros(st_ref.shape, F32)

    u = u_ref[...].astype(F32)
    xs_ref[8 : 8 + tt, :] = u
    prev = xs_ref[7 : 7 + tt, :]
    xs_ref[0:8, :] = xs_ref[tt : tt + 8, :]
    x = u + mu_ref[...] * (prev - u)
    rr = x[:, :gw]
    k = x[:, gw : 2 * gw]
    v = x[:, 2 * gw : 3 * gw]
    wa = x[:, 3 * gw : 3 * gw + LANE]
    glo = x[:, 3 * gw + LANE :]
    w = -_softplus(-(w0_ref[...] + _bdot(jnp.tanh(wa), w2_ref[...]))) - 0.5
    ld = -jnp.exp(w)
    gate_a = _sigmoid(a0_ref[...] + _bdot(wa, a2_ref[...]))
    gate_g = _bdot(_sigmoid(glo), g2_ref[...])
    e = e_ref[...]

    def segsum(y):
        hi = y.astype(BF16)
        lo = (y - hi.astype(F32)).astype(BF16)
        return jnp.dot(hi, e, preferred_element_type=F32) + jnp.dot(lo, e, preferred_element_type=F32)

    kkr = k * kk_ref[...]
    kk = kkr * lax.rsqrt(segsum(kkr * kkr) + 1e-6)
    k2 = k * (1.0 + (gate_a - 1.0) * ka_ref[...])
    bonus = segsum(rr * k2 * rk_ref[...]) * v

    r, col, same, lower, strict = _chunk_masks(tt, c)
    tri = jnp.where(lower, 1.0, 0.0).astype(BF16)
    ones_blk = jnp.where(same, 1.0, 0.0).astype(BF16)
    cum = _dot_exact_lhs(tri, ld)
    ctot = _dot_exact_lhs(ones_blk, ld)
    encum = jnp.exp(-cum)
    edec = jnp.exp(ctot - cum)
    kka = kk * gate_a
    rt = rr * jnp.exp(cum)
    at = -kk * jnp.exp(cum - ld)
    bt = kka * encum
    kt = k2 * encum
    bd = kka * edec
    kd = k2 * edec
    pc = jnp.exp(ctot)

    ys = []
    for h in range(RWKV_HEADS):
        sl = slice(h * hd, (h + 1) * hd)
        ah, rh, bh, kh, vh, bdh, kdh = at[:, sl], rt[:, sl], bt[:, sl], kt[:, sl], v[:, sl], bd[:, sl], kd[:, sl]
        mab = jnp.where(strict, _bdot_nt(ah, bh), 0.0)
        mak = jnp.where(strict, _bdot_nt(ah, kh), 0.0)
        arb = jnp.where(lower, _bdot_nt(rh, bh), 0.0)
        ark = jnp.where(lower, _bdot_nt(rh, kh), 0.0)
        tinv = _tri_inv(mab, r, col, c)
        wmat = _bdot(tinv, ah)
        umat = _bdot(tinv, _bdot(mak, vh))
        yconst = _bdot(ark, vh)
        s = st_ref[h]
        outs = []
        for cc in range(tt // c):
            rs = slice(cc * c, (cc + 1) * c)
            sb = s.astype(BF16)
            sa = _bdot_nt(wmat[rs], sb) + umat[rs]
            outs.append(_bdot_nt(rh[rs], sb) + _bdot(arb[rs, rs], sa) + yconst[rs])
            s = s * pc[cc * c : cc * c + 1, sl] + _bdot_tn(sa, bdh[rs]) + _bdot_tn(vh[rs], kdh[rs])
        st_ref[h] = s
        ys.append(jnp.concatenate(outs, axis=0))
    y = jnp.concatenate(ys, axis=-1)
    mean = segsum(y) * (1.0 / hd)
    d = y - mean
    var = segsum(d * d) * (1.0 / hd)
    yn = d * lax.rsqrt(var + RWKV_GN_EPS) * lng_ref[...] + lnb_ref[...]
    o_ref[...] = ((yn + bonus) * gate_g).astype(o_ref.dtype)


def _rwkv(u_rwkv, mu, w0, w2p, a0, a2p, g2, k_k, k_a, r_k, ln_g, ln_b, seg_ones, *, batch, seq):
    tt = min(SCAN_TILE, seq)
    nt = seq // tt
    t = batch * seq
    ncol = u_rwkv.shape[1]
    full = lambda a: pl.BlockSpec(a.shape, lambda b, i: (0,) * a.ndim)
    params = (mu, w0, w2p, a0, a2p, g2, k_k, k_a, r_k, ln_g, ln_b, seg_ones)
    return pl.pallas_call(
        functools.partial(_rwkv_kernel, tt=tt, c=CHUNK),
        grid=(batch, nt),
        in_specs=[pl.BlockSpec((tt, ncol), lambda b, i: (b * nt + i, 0))] + [full(p) for p in params],
        out_specs=pl.BlockSpec((tt, GROUP_W), lambda b, i: (b * nt + i, 0)),
        out_shape=jax.ShapeDtypeStruct((t, GROUP_W), BF16),
        scratch_shapes=[pltpu.VMEM((tt + 8, ncol), F32), pltpu.VMEM((RWKV_HEADS, RWKV_HD, RWKV_HD), F32)],
        compiler_params=_cparams("parallel", "arbitrary"),
        name="rwkv7",
    )(u_rwkv, *params)


def _layer_norm(h, g, b):
    mu = jnp.mean(h, axis=-1, keepdims=True)
    d = h - mu
    var = jnp.mean(d * d, axis=-1, keepdims=True)
    return d * lax.rsqrt(var + LN_EPS) * g + b


def _outproj_kernel(yf_ref, ym_ref, yr_ref, yg_ref, w_ref, x_ref, g_ref, b_ref, wrh_ref, wrl_ref,
                    xo_ref, xb_ref, lg_ref, *, alpha):
    gw = GROUP_W
    acc = jnp.dot(yf_ref[...], w_ref[0:gw, :], preferred_element_type=F32)
    acc = acc + jnp.dot(ym_ref[...], w_ref[gw : 2 * gw, :], preferred_element_type=F32)
    acc = acc + jnp.dot(yr_ref[...], w_ref[2 * gw : 3 * gw, :], preferred_element_type=F32)
    acc = acc + jnp.dot(yg_ref[...], w_ref[3 * gw : 4 * gw, :], preferred_element_type=F32)
    xn = _layer_norm(alpha * x_ref[...] + acc, g_ref[...], b_ref[...])
    xo_ref[...] = xn
    xh = xn.astype(BF16)
    xb_ref[...] = xh
    xl = (xn - xh.astype(F32)).astype(BF16)
    wrh = wrh_ref[...]
    d = lambda a, b: jnp.dot(a, b, preferred_element_type=F32)
    lg_ref[...] = d(xh, wrh) + d(xl, wrh) + d(xh, wrl_ref[...])


def _outproj(ys, w_out, x, ln_g, ln_b, wr_hi, wr_lo, alpha):
    t, d = x.shape
    tm = min(ROW_TILE, t)
    row = lambda n: pl.BlockSpec((tm, n), lambda i: (i, 0))
    full = lambda a: pl.BlockSpec(a.shape, lambda i: (0,) * a.ndim)
    return pl.pallas_call(
        functools.partial(_outproj_kernel, alpha=alpha),
        grid=(t // tm,),
        in_specs=[row(GROUP_W)] * 4 + [full(w_out), row(d), full(ln_g), full(ln_b), full(wr_hi), full(wr_lo)],
        out_specs=[row(d), row(d), row(LANE)],
        out_shape=[
            jax.ShapeDtypeStruct((t, d), F32),
            jax.ShapeDtypeStruct((t, d), BF16),
            jax.ShapeDtypeStruct((t, LANE), F32),
        ],
        compiler_params=_cparams("parallel"),
        name="outproj_ln_router",
    )(*ys, w_out, x, ln_g, ln_b, wr_hi, wr_lo)


def _expert_kernel(be_ref, x_ref, wg_ref, wu_ref, wd_ref, ws_ref, o_ref):
    del be_ref
    x = x_ref[...]
    a = jnp.dot(x, wg_ref[...], preferred_element_type=F32)
    b = jnp.dot(x, wu_ref[...], preferred_element_type=F32)
    hmid = (a * _sigmoid(a) * b).astype(BF16)
    y = jnp.dot(hmid, wd_ref[...], preferred_element_type=F32)
    o_ref[...] = (y * ws_ref[...]).astype(o_ref.dtype)


def _experts(block_expert, xs, w_gate, w_up, w_down, w_slot):
    n_slots, d = xs.shape
    tb = MOE_TILE
    de = w_gate.shape[-1]
    grid_spec = pltpu.PrefetchScalarGridSpec(
        num_scalar_prefetch=1,
        grid=(n_slots // tb,),
        in_specs=[
            pl.BlockSpec((tb, d), lambda i, be: (i, 0)),
            pl.BlockSpec((None, d, de), lambda i, be: (be[i], 0, 0)),
            pl.BlockSpec((None, d, de), lambda i, be: (be[i], 0, 0)),
            pl.BlockSpec((None, de, d), lambda i, be: (be[i], 0, 0)),
            pl.BlockSpec((tb, 1), lambda i, be: (i, 0)),
        ],
        out_specs=pl.BlockSpec((tb, d), lambda i, be: (i, 0)),
    )
    return pl.pallas_call(
        _expert_kernel,
        grid_spec=grid_spec,
        out_shape=jax.ShapeDtypeStruct((n_slots, d), F32),
        compiler_params=_cparams("arbitrary"),
        name="experts",
    )(block_expert, xs, w_gate, w_up, w_down, w_slot)


def _ln2_kernel(x_ref, m_ref, g_ref, b_ref, xo_ref, xb_ref, *, alpha):
    xn = _layer_norm(alpha * x_ref[...] + m_ref[...], g_ref[...], b_ref[...])
    xo_ref[...] = xn
    xb_ref[...] = xn.astype(BF16)


def _ln2(x, moe, g, b, alpha):
    t, d = x.shape
    tm = min(ROW_TILE, t)
    row = pl.BlockSpec((tm, d), lambda i: (i, 0))
    full = lambda a: pl.BlockSpec(a.shape, lambda i: (0,) * a.ndim)
    return pl.pallas_call(
        functools.partial(_ln2_kernel, alpha=alpha),
        grid=(t // tm,),
        in_specs=[row, row, full(g), full(b)],
        out_specs=[row, row],
        out_shape=[jax.ShapeDtypeStruct((t, d), F32), jax.ShapeDtypeStruct((t, d), BF16)],
        compiler_params=_cparams("parallel"),
        name="residual_ln2",
    )(x, moe, g, b)


def _pad_cols(w, n):
    return jnp.pad(w, [(0, 0)] * (w.ndim - 1) + [(0, n - w.shape[-1])])


def _rot_half_cols(w):
    half = w.shape[-1] // 2
    return jnp.concatenate([-w[..., half:], w[..., :half]], axis=-1)


def _prep_weights(p):
    w_in = p["w_in"]
    c_fox = 3 * GROUP_W + FOX_HEADS
    c_mla = MLA_Q_RANK + MLA_KV_RANK + MLA_ROPE
    c_rwkv = 3 * GROUP_W + 2 * 64 + 128
    o_mla = c_fox
    o_rwkv = o_mla + c_mla
    o_gdn = o_rwkv + c_rwkv
    row = lambda a: a[:, None, :].astype(F32)

    w_fox = jnp.concatenate(
        [w_in[..., :GROUP_W] * FOX_HD**-0.5, w_in[..., GROUP_W : 3 * GROUP_W], _pad_cols(w_in[..., 3 * GROUP_W : c_fox], LANE)],
        axis=-1,
    )
    kpe_w = w_in[..., o_mla + MLA_Q_RANK + MLA_KV_RANK : o_mla + c_mla]
    w_mla = jnp.concatenate(
        [w_in[..., o_mla : o_mla + MLA_Q_RANK + MLA_KV_RANK], _pad_cols(kpe_w, LANE), _pad_cols(_rot_half_cols(kpe_w), LANE)],
        axis=-1,
    )
    w_rwkv = w_in[..., o_rwkv:o_gdn]
    g0 = o_gdn + 4 * GROUP_W
    w_gdn = jnp.concatenate(
        [w_in[..., o_gdn:g0], _pad_cols(w_in[..., g0 : g0 + GDN_HEADS], LANE), _pad_cols(w_in[..., g0 + GDN_HEADS :], LANE)],
        axis=-1,
    )

    nl = w_in.shape[0]
    scale = (MLA_NOPE + MLA_ROPE) ** -0.5
    wq = p["mla_w_uq"].reshape(nl, MLA_Q_RANK, MLA_HEADS, MLA_NOPE + MLA_ROPE) * scale
    wq_nope = wq[..., :MLA_NOPE].reshape(nl, MLA_Q_RANK, -1)
    wq_pe = wq[..., MLA_NOPE:]
    wq_p = jnp.concatenate(
        [wq_nope, _pad_cols(wq_pe, LANE).reshape(nl, MLA_Q_RANK, -1), _pad_cols(_rot_half_cols(wq_pe), LANE).reshape(nl, MLA_Q_RANK, -1)],
        axis=-1,
    )
    wkv = p["mla_w_ukv"].reshape(nl, MLA_KV_RANK, MLA_HEADS, MLA_NOPE + MLA_VD)
    wkv_p = jnp.concatenate([wkv[..., :MLA_NOPE].reshape(nl, MLA_KV_RANK, -1), wkv[..., MLA_NOPE:].reshape(nl, MLA_KV_RANK, -1)], axis=-1)

    zeros64 = jnp.zeros((nl, 64, GROUP_W), F32)
    w_router = _pad_cols(jnp.concatenate([p["moe_w_grp"], p["moe_w_exp"]], axis=-1), LANE)
    wr_hi = w_router.astype(BF16)
    wr_lo = (w_router - wr_hi.astype(F32)).astype(BF16)
    return dict(
        w_fox=w_fox.astype(BF16), w_mla=w_mla.astype(BF16), w_rwkv=w_rwkv.astype(BF16), w_gdn=w_gdn.astype(BF16),
        fox_b_f=p["fox_b_f"], fox_out_g=row(p["fox_out_g"]),
        mla_qg=row(p["mla_q_norm_g"]), mla_kvg=row(p["mla_kv_norm_g"]), mla_wq=wq_p.astype(BF16), mla_wkv=wkv_p.astype(BF16),
        mla_out_g=row(p["mla_out_g"]),
        rwkv_mu=row(p["rwkv_mu"]), rwkv_w0=row(p["rwkv_w0"]),
        rwkv_w2=jnp.concatenate([p["rwkv_w2"], zeros64], axis=1).astype(BF16),
        rwkv_a0=row(p["rwkv_a0"]), rwkv_a2=jnp.concatenate([zeros64, p["rwkv_a2"]], axis=1).astype(BF16),
        rwkv_g2=p["rwkv_g2"].astype(BF16), rwkv_k_k=row(p["rwkv_k_k"]), rwkv_k_a=row(p["rwkv_k_a"]),
        rwkv_r_k=row(p["rwkv_r_k"]), rwkv_ln_g=row(p["rwkv_ln_g"]), rwkv_ln_b=row(p["rwkv_ln_b"]),
        gdn_conv_w=p["gdn_conv_w"].astype(F32), gdn_a_log=row(_pad_cols(p["gdn_a_log"], LANE)),
        gdn_dt_bias=row(_pad_cols(p["gdn_dt_bias"], LANE)), gdn_norm_g=row(p["gdn_norm_g"]),
        w_out=p["w_out"].astype(BF16), ln1_g=row(p["ln1_g"]), ln1_b=row(p["ln1_b"]),
        wr_hi=wr_hi, wr_lo=wr_lo, moe_b_grp=p["moe_b_grp"], moe_b_exp=p["moe_b_exp"],
        moe_w_gate=p["moe_w_gate"].astype(BF16), moe_w_up=p["moe_w_up"].astype(BF16), moe_w_down=p["moe_w_down"].astype(BF16),
        ln2_g=row(p["ln2_g"]), ln2_b=row(p["ln2_b"]),
    )


def _route(logits, b_grp, b_exp, tb):
    t = logits.shape[0]
    grp_prob = jax.nn.softmax(logits[:, :N_GROUPS] + b_grp, axis=-1)
    p_grp, grp = lax.top_k(grp_prob, 1)
    exp_logits = (logits[:, N_GROUPS : N_GROUPS + N_EXPERTS] + b_exp).reshape(t, N_GROUPS, EXP_PER_GROUP)
    in_grp = jnp.take_along_axis(exp_logits, grp[:, :, None], axis=1)[:, 0]
    top_logit, top_local = lax.top_k(in_grp, TOP_K)
    gate = p_grp * jax.nn.softmax(top_logit, axis=-1)
    expert = grp * EXP_PER_GROUP + top_local

    a = t * TOP_K
    n_blocks = (a + N_EXPERTS * (tb - 1) + tb - 1) // tb
    n_slots = n_blocks * tb
    flat_e = expert.reshape(a)
    order = jnp.argsort(flat_e)
    e_sorted = flat_e[order]
    counts = jnp.bincount(flat_e, length=N_EXPERTS)
    padded = (counts + tb - 1) // tb * tb
    start = jnp.cumsum(counts) - counts
    pend = jnp.cumsum(padded)
    pstart = pend - padded
    dest = (pstart[e_sorted] + jnp.arange(a) - start[e_sorted]).astype(jnp.int32)
    token_of_slot = jnp.full((n_slots,), t, jnp.int32).at[dest].set((order // TOP_K).astype(jnp.int32))
    weight_of_slot = jnp.zeros((n_slots,), F32).at[dest].set(gate.reshape(a)[order])
    block_expert = jnp.minimum(jnp.searchsorted(pend, jnp.arange(n_blocks) * tb, side="right"), N_EXPERTS - 1).astype(jnp.int32)
    slot_of_assignment = jnp.zeros((a,), jnp.int32).at[order].set(dest).reshape(t, TOP_K)
    return token_of_slot, weight_of_slot, block_expert, slot_of_assignment


def _layer(x, xb, cs, sn, w, *, batch, seq, alpha):
    t = batch * seq
    u_fox, fox_tail = _inproj(xb, w["w_fox"], LANE)
    (u_mla,) = _inproj(xb, w["w_mla"], 0)
    (u_rwkv,) = _inproj(xb, w["w_rwkv"], 0)
    u_gdn, gdn_tail = _inproj(xb, w["w_gdn"], 2 * LANE)

    log_f = -jax.nn.softplus(-(fox_tail[:, :FOX_HEADS] + w["fox_b_f"]))
    fcum = jnp.cumsum(log_f.reshape(batch, seq, FOX_HEADS), axis=1).transpose(0, 2, 1)
    y_fox = _attention(u_fox, 0, u_fox, 1, u_fox, 2, fcum, w["fox_out_g"], batch=batch, seq=seq,
                       heads=FOX_HEADS, dk=FOX_HD, dv=FOX_HD, chunk=1)

    q_mla, k_mla, v_mla = _mla_prep(u_mla, cs, sn, w["mla_qg"], w["mla_kvg"], w["mla_wq"], w["mla_wkv"])
    y_mla = _attention(q_mla, 0, k_mla, 0, v_mla, 0, None, w["mla_out_g"], batch=batch, seq=seq,
                       heads=MLA_HEADS, dk=2 * LANE, dv=MLA_VD, chunk=CHUNK)

    seg = jnp.arange(GROUP_W) // RWKV_HD
    seg_ones = (seg[:, None] == seg[None, :]).astype(BF16)
    y_rwkv = _rwkv(u_rwkv, w["rwkv_mu"], w["rwkv_w0"], w["rwkv_w2"], w["rwkv_a0"], w["rwkv_a2"], w["rwkv_g2"],
                   w["rwkv_k_k"], w["rwkv_k_a"], w["rwkv_r_k"], w["rwkv_ln_g"], w["rwkv_ln_b"], seg_ones,
                   batch=batch, seq=seq)
    y_gdn = _gdn(u_gdn, gdn_tail, w["gdn_conv_w"], w["gdn_a_log"], w["gdn_dt_bias"], w["gdn_norm_g"],
                 batch=batch, seq=seq)

    x1, x1b, logits = _outproj((y_fox, y_mla, y_rwkv, y_gdn), w["w_out"], x, w["ln1_g"], w["ln1_b"],
                               w["wr_hi"], w["wr_lo"], alpha)

    token_of_slot, weight_of_slot, block_expert, slot_of_assignment = _route(logits, w["moe_b_grp"], w["moe_b_exp"], MOE_TILE)
    x_pad = jnp.concatenate([x1b, jnp.zeros((1, x1b.shape[1]), BF16)], axis=0)
    y_slots = _experts(block_expert, x_pad[token_of_slot], w["moe_w_gate"], w["moe_w_up"], w["moe_w_down"],
                       weight_of_slot[:, None])
    moe = y_slots[slot_of_assignment[:, 0]] + y_slots[slot_of_assignment[:, 1]]
    return _ln2(x1, moe, w["ln2_g"], w["ln2_b"], alpha)


def kernel(x, positions, w_in, fox_b_f, fox_out_g, mla_q_norm_g, mla_kv_norm_g, mla_w_uq, mla_w_ukv, mla_out_g, rwkv_mu, rwkv_w0, rwkv_w2, rwkv_a0, rwkv_a2, rwkv_g2, rwkv_k_k, rwkv_k_a, rwkv_r_k, rwkv_ln_g, rwkv_ln_b, gdn_conv_w, gdn_a_log, gdn_dt_bias, gdn_norm_g, w_out, ln1_g, ln1_b, moe_w_grp, moe_b_grp, moe_w_exp, moe_b_exp, moe_w_gate, moe_w_up, moe_w_down, ln2_g, ln2_b):
    batch, seq, d = x.shape
    depth = w_in.shape[0]
    alpha = (2 * depth) ** 0.25
    params = dict(
        w_in=w_in, fox_b_f=fox_b_f, fox_out_g=fox_out_g, mla_q_norm_g=mla_q_norm_g, mla_kv_norm_g=mla_kv_norm_g,
        mla_w_uq=mla_w_uq, mla_w_ukv=mla_w_ukv, mla_out_g=mla_out_g, rwkv_mu=rwkv_mu, rwkv_w0=rwkv_w0, rwkv_w2=rwkv_w2,
        rwkv_a0=rwkv_a0, rwkv_a2=rwkv_a2, rwkv_g2=rwkv_g2, rwkv_k_k=rwkv_k_k, rwkv_k_a=rwkv_k_a, rwkv_r_k=rwkv_r_k,
        rwkv_ln_g=rwkv_ln_g, rwkv_ln_b=rwkv_ln_b, gdn_conv_w=gdn_conv_w, gdn_a_log=gdn_a_log, gdn_dt_bias=gdn_dt_bias,
        gdn_norm_g=gdn_norm_g, w_out=w_out, ln1_g=ln1_g, ln1_b=ln1_b, moe_w_grp=moe_w_grp, moe_b_grp=moe_b_grp,
        moe_w_exp=moe_w_exp, moe_b_exp=moe_b_exp, moe_w_gate=moe_w_gate, moe_w_up=moe_w_up, moe_w_down=moe_w_down,
        ln2_g=ln2_g, ln2_b=ln2_b,
    )
    weights = _prep_weights(params)

    half = MLA_ROPE // 2
    inv_freq = ROPE_THETA ** (-jnp.arange(half, dtype=F32) / half)
    ang = positions.astype(F32).reshape(batch * seq, 1) * inv_freq
    zpad = jnp.zeros((batch * seq, LANE - MLA_ROPE), F32)
    cs = jnp.concatenate([jnp.cos(ang), jnp.cos(ang), zpad], axis=-1)
    sn = jnp.concatenate([jnp.sin(ang), jnp.sin(ang), zpad], axis=-1)

    xf = x.reshape(batch * seq, d).astype(F32)

    def body(carry, w):
        xc, xcb = carry
        return _layer(xc, xcb, cs, sn, w, batch=batch, seq=seq, alpha=alpha), None

    (xf, _), _ = lax.scan(body, (xf, xf.astype(BF16)), weights)
    return xf.reshape(batch, seq, d).astype(x.dtype)
```

```python
import functools
import math

import jax
import jax.numpy as jnp
from jax import lax
from jax.experimental import pallas as pl
from jax.experimental.pallas import tpu as pltpu

F32 = jnp.float32
BF16 = jnp.bfloat16

D_MODEL = 2048
GROUP_W = 512
FOX_HD, FOX_HEADS = 64, 8
MLA_HEADS, MLA_NOPE, MLA_ROPE, MLA_VD = 4, 128, 64, 128
MLA_Q_RANK, MLA_KV_RANK = 384, 128
ROPE_THETA = 10000.0
RWKV_HD, RWKV_HEADS = 64, 8
RWKV_GN_EPS = 64e-5
GDN_HD, GDN_HEADS, GDN_CONV = 128, 4, 4
N_GROUPS, EXP_PER_GROUP, TOP_K, D_EXPERT = 4, 8, 2, 512
N_EXPERTS = N_GROUPS * EXP_PER_GROUP
CHUNK = 64
LN_EPS = 1e-5
RMS_EPS = 1e-6
LOG2E = math.log2(math.e)

LANE = 128
VMEM_LIMIT_BYTES = 56 * 1024 * 1024
ROW_TILE = 512
ATTN_TILE = 256
SCAN_TILE = 256
MOE_TILE = 256


def _cparams(*sem):
    return pltpu.CompilerParams(dimension_semantics=sem, vmem_limit_bytes=VMEM_LIMIT_BYTES)


def _bdot(a, b):
    return jnp.dot(a.astype(BF16), b.astype(BF16), preferred_element_type=F32)


def _bdot_nt(a, b):
    return lax.dot_general(a.astype(BF16), b.astype(BF16), (((1,), (1,)), ((), ())), preferred_element_type=F32)


def _bdot_tn(a, b):
    return lax.dot_general(a.astype(BF16), b.astype(BF16), (((0,), (0,)), ((), ())), preferred_element_type=F32)


def _split3(x):
    hi = x.astype(BF16)
    r1 = x - hi.astype(F32)
    mid = r1.astype(BF16)
    lo = (r1 - mid.astype(F32)).astype(BF16)
    return hi, mid, lo


def _dot_exact_lhs(m, x):
    hi, mid, lo = _split3(x)
    d = lambda p: jnp.dot(m, p, preferred_element_type=F32)
    return d(hi) + d(mid) + d(lo)


def _sigmoid(x):
    return 1.0 / (1.0 + jnp.exp(-x))


def _softplus(x):
    return jnp.maximum(x, 0.0) + jnp.log(1.0 + jnp.exp(-jnp.abs(x)))


def _chunk_masks(n, c):
    r = lax.broadcasted_iota(jnp.int32, (n, n), 0)
    col = lax.broadcasted_iota(jnp.int32, (n, n), 1)
    same = (r // c) == (col // c)
    lower = jnp.logical_and(same, col <= r)
    strict = jnp.logical_and(same, col < r)
    return r, col, same, lower, strict


def _tri_inv_masks(r, col, c):
    blk = lambda s: (r // s) == (col // s)
    one = lambda cond: jnp.where(cond, 1.0, 0.0).astype(F32)
    offs = []
    s = 8
    while s < c:
        offs.append(one(jnp.logical_and(blk(2 * s), jnp.logical_not(blk(s)))))
        s *= 2
    return one(r == col), one(blk(8)), offs


def _tri_inv(ms, masks):
    eye, blk8, offs = masks
    bf = lambda a: a.astype(BF16)
    dot = lambda a, b: jnp.dot(a, b, preferred_element_type=F32)
    mdf = [m * blk8 for m in ms]
    mds = [bf(m) for m in mdf]
    xs = [eye + m for m in mdf]
    m2s = [bf(dot(md, md)) for md in mds]
    xs = [x + dot(m2, bf(x)) for x, m2 in zip(xs, m2s)]
    m4s = [bf(dot(m2, m2)) for m2 in m2s]
    xs = [x + dot(m4, bf(x)) for x, m4 in zip(xs, m4s)]
    for off in offs:
        xbs = [bf(x) for x in xs]
        ts = [bf(dot(xb, bf(m * off))) for xb, m in zip(xbs, ms)]
        xs = [x + dot(t, xb) for x, t, xb in zip(xs, ts, xbs)]
    return xs


def _inproj_kernel(x_ref, w_ref, o_ref, *tail_refs, tail):
    acc = jnp.dot(x_ref[...], w_ref[...], preferred_element_type=F32)
    n = acc.shape[1]
    if tail:
        o_ref[...] = acc[:, : n - tail].astype(o_ref.dtype)
        tail_refs[0][...] = acc[:, n - tail :]
    else:
        o_ref[...] = acc.astype(o_ref.dtype)


def _inproj(xb, w, tail):
    t, d = xb.shape
    n = w.shape[1]
    tm = min(ROW_TILE, t)
    out_shape = [jax.ShapeDtypeStruct((t, n - tail), BF16)]
    out_specs = [pl.BlockSpec((tm, n - tail), lambda i: (i, 0))]
    if tail:
        out_shape.append(jax.ShapeDtypeStruct((t, tail), F32))
        out_specs.append(pl.BlockSpec((tm, tail), lambda i: (i, 0)))
    return pl.pallas_call(
        functools.partial(_inproj_kernel, tail=tail),
        grid=(t // tm,),
        in_specs=[pl.BlockSpec((tm, d), lambda i: (i, 0)), pl.BlockSpec((d, n), lambda i: (0, 0))],
        out_specs=out_specs,
        out_shape=out_shape,
        compiler_params=_cparams("parallel"),
        name="inproj",
    )(xb, w)


def _attn_kernel(*refs, heads, dk, dv, chunk, tq, seq, use_bias):
    if use_bias:
        q_ref, k_ref, vt_ref, bias_ref, g_ref, o_ref, brep_ref, m_ref, l_ref, acc_ref = refs
    else:
        q_ref, k_ref, vt_ref, g_ref, o_ref, m_ref, l_ref, acc_ref = refs
    paired = dk < LANE
    dkp = LANE if paired else dk
    i = pl.program_id(1)

    if use_bias:

        @pl.when(i == 0)
        def _():
            for h in range(heads):
                brep_ref[h] = jnp.broadcast_to(bias_ref[:, h : h + 1], (seq, LANE))

    m_ref[...] = jnp.full(m_ref.shape, -1e30, F32)
    l_ref[...] = jnp.zeros(l_ref.shape, F32)
    acc_ref[...] = jnp.zeros(acc_ref.shape, F32)

    qs = []
    for h in range(heads):
        if paired:
            slab = q_ref[:, (h // 2) * LANE : (h // 2 + 1) * LANE]
            lane_half = lax.broadcasted_iota(jnp.int32, (tq, LANE), 1) // dk
            qs.append(jnp.where(lane_half == h % 2, slab, jnp.zeros_like(slab)))
        else:
            qs.append(q_ref[:, h * dkp : (h + 1) * dkp])

    def step(off, masked):
        if masked:
            kr = lax.broadcasted_iota(jnp.int32, (tq, tq), 0)
            qc = lax.broadcasted_iota(jnp.int32, (tq, tq), 1)
            allowed = (kr // chunk) <= (qc // chunk)
        sts = []
        for h in range(heads):
            slab = h // 2 if paired else h
            k = k_ref[pl.ds(off, tq), slab * dkp : (slab + 1) * dkp]
            sts.append(lax.dot_general(k, qs[h], (((1,), (1,)), ((), ())), preferred_element_type=F32))
        ps, alphas = [], []
        for h in range(heads):
            st = sts[h]
            if use_bias:
                st = st - jnp.concatenate([brep_ref[h, pl.ds(off, tq), :]] * (tq // LANE), axis=1)
            if masked:
                st = jnp.where(allowed, st, -1e30)
            m_old = m_ref[h, 0:1, :]
            m_new = jnp.maximum(m_old, jnp.max(st, axis=0, keepdims=True))
            p = jnp.exp2(st - m_new)
            alpha = jnp.exp2(m_old - m_new)
            l_ref[h, 0:1, :] = alpha * l_ref[h, 0:1, :] + jnp.sum(p, axis=0, keepdims=True)
            m_ref[h, 0:1, :] = m_new
            ps.append(p.astype(BF16))
            alphas.append(alpha)
        for h in range(heads):
            rows = slice(h * dv, (h + 1) * dv)
            pv = jnp.dot(vt_ref[rows, pl.ds(off, tq)], ps[h], preferred_element_type=F32)
            acc_ref[rows, :] = alphas[h] * acc_ref[rows, :] + pv

    def body(j, carry):
        step(pl.multiple_of(j * tq, tq), False)
        return carry

    lax.fori_loop(0, i, body, 0)
    step(pl.multiple_of(i * tq, tq), True)

    ot = jnp.concatenate([acc_ref[h * dv : (h + 1) * dv, :] / l_ref[h, 0:1, :] for h in range(heads)], axis=0)
    ot = ot * lax.rsqrt(jnp.mean(ot * ot, axis=0, keepdims=True) + RMS_EPS)
    o_ref[...] = (ot.T * g_ref[...]).astype(o_ref.dtype)


def _attention(q_arr, q_col, k_arr, k_col, vt, bias, gain, *, batch, seq, heads, dk, dv, chunk):
    assert dk % LANE == 0 or (2 * dk == LANE and heads % 2 == 0)
    tq = min(ATTN_TILE, seq)
    nq = seq // tq
    t = batch * seq
    in_specs = [
        pl.BlockSpec((tq, heads * dk), lambda b, i: (b * nq + i, q_col)),
        pl.BlockSpec((seq, heads * dk), lambda b, i: (b, k_col)),
        pl.BlockSpec((None, heads * dv, seq), lambda b, i: (b, 0, 0)),
    ]
    args = [q_arr, k_arr, vt]
    scratch = []
    if bias is not None:
        in_specs.append(pl.BlockSpec((None, seq, heads), lambda b, i: (b, 0, 0)))
        args.append(bias)
        scratch.append(pltpu.VMEM((heads, seq, LANE), F32))
    in_specs.append(pl.BlockSpec((1, heads * dv), lambda b, i: (0, 0)))
    args.append(gain)
    stat = pltpu.VMEM((heads, 8, tq), F32)
    scratch += [stat, stat, pltpu.VMEM((heads * dv, tq), F32)]
    return pl.pallas_call(
        functools.partial(_attn_kernel, heads=heads, dk=dk, dv=dv, chunk=chunk, tq=tq, seq=seq, use_bias=bias is not None),
        grid=(batch, nq),
        in_specs=in_specs,
        out_specs=pl.BlockSpec((tq, heads * dv), lambda b, i: (b * nq + i, 0)),
        out_shape=jax.ShapeDtypeStruct((t, heads * dv), BF16),
        scratch_shapes=scratch,
        compiler_params=_cparams("parallel", "arbitrary"),
        name="attention",
    )(*args)


def _mla_prep_kernel(u_ref, cs_ref, sn_ref, qg_ref, kvg_ref, wq_ref, wkv_ref, q_ref, k_ref, v_ref):
    u = u_ref[...].astype(F32)
    cs = cs_ref[...]
    sn = sn_ref[...]

    def rms(x, g):
        return x * lax.rsqrt(jnp.mean(x * x, axis=-1, keepdims=True) + RMS_EPS) * g

    qo = _bdot(rms(u[:, :MLA_Q_RANK], qg_ref[...]), wq_ref[...])
    kvo = _bdot(rms(u[:, MLA_Q_RANK : MLA_Q_RANK + MLA_KV_RANK], kvg_ref[...]), wkv_ref[...])
    c0 = MLA_Q_RANK + MLA_KV_RANK
    kpe = (u[:, c0 : c0 + LANE] * cs + u[:, c0 + LANE : c0 + 2 * LANE] * sn).astype(k_ref.dtype)
    nn = MLA_HEADS * MLA_NOPE
    for h in range(MLA_HEADS):
        a = h * 2 * LANE
        q_ref[:, a : a + LANE] = qo[:, h * LANE : (h + 1) * LANE].astype(q_ref.dtype)
        qpe = qo[:, nn + h * LANE : nn + (h + 1) * LANE] * cs + qo[:, 2 * nn + h * LANE : 2 * nn + (h + 1) * LANE] * sn
        q_ref[:, a + LANE : a + 2 * LANE] = qpe.astype(q_ref.dtype)
        k_ref[:, a : a + LANE] = kvo[:, h * LANE : (h + 1) * LANE].astype(k_ref.dtype)
        k_ref[:, a + LANE : a + 2 * LANE] = kpe
    v_ref[...] = kvo[:, nn:].astype(v_ref.dtype)


def _mla_prep(u_mla, cs, sn, qg, kvg, wq, wkv):
    t = u_mla.shape[0]
    tm = min(ROW_TILE, t)
    row = lambda n: pl.BlockSpec((tm, n), lambda i: (i, 0))
    full = lambda a: pl.BlockSpec(a.shape, lambda i: (0,) * a.ndim)
    wide = MLA_HEADS * 2 * LANE
    return pl.pallas_call(
        _mla_prep_kernel,
        grid=(t // tm,),
        in_specs=[row(u_mla.shape[1]), row(LANE), row(LANE), full(qg), full(kvg), full(wq), full(wkv)],
        out_specs=[row(wide), row(wide), row(MLA_HEADS * MLA_VD)],
        out_shape=[
            jax.ShapeDtypeStruct((t, wide), BF16),
            jax.ShapeDtypeStruct((t, wide), BF16),
            jax.ShapeDtypeStruct((t, MLA_HEADS * MLA_VD), BF16),
        ],
        compiler_params=_cparams("parallel"),
        name="mla_prep",
    )(u_mla, cs, sn, qg, kvg, wq, wkv)


def _gdn_kernel(u_ref, t_ref, cw_ref, alog_ref, dtb_ref, ng_ref, o_ref, xs_ref, st_ref, *, tt, c):
    i = pl.program_id(1)
    gw, hd = GROUP_W, GDN_HD

    @pl.when(i == 0)
    def _():
        xs_ref[0:8, :] = jnp.zeros((8, 3 * gw), F32)
        st_ref[...] = jnp.zeros(st_ref.shape, F32)

    xs_ref[8 : 8 + tt, :] = u_ref[:, : 3 * gw].astype(F32)
    cw = cw_ref[...]
    conv = cw[0:1, :] * xs_ref[5 : 5 + tt, :]
    for j in range(1, GDN_CONV):
        conv = conv + cw[j : j + 1, :] * xs_ref[5 + j : 5 + j + tt, :]
    xs_ref[0:8, :] = xs_ref[tt : tt + 8, :]
    qkv = conv * _sigmoid(conv)

    tail = t_ref[...]
    beta = _sigmoid(tail[:, :LANE])
    g = -jnp.exp(alog_ref[...]) * _softplus(tail[:, LANE:] + dtb_ref[...])

    r, col, same, lower, strict = _chunk_masks(tt, c)
    tri = jnp.where(lower, 1.0, 0.0).astype(BF16)
    ones_blk = jnp.where(same, 1.0, 0.0).astype(BF16)
    gc = _dot_exact_lhs(tri, g)
    gtot = _dot_exact_lhs(ones_blk, g)
    gct = gc.T

    def l2n(x):
        return x * lax.rsqrt(jnp.sum(x * x, axis=-1, keepdims=True) + 1e-6)

    heads = range(GDN_HEADS)
    bf = lambda a: a.astype(BF16)
    lmats, attns, kbs, vbs, qds, kds, egs = [], [], [], [], [], [], []
    for h in heads:
        qf = l2n(qkv[:, h * hd : (h + 1) * hd]) * (hd**-0.5)
        qh = bf(qf)
        kf = l2n(qkv[:, gw + h * hd : gw + (h + 1) * hd])
        kh = bf(kf)
        vh = qkv[:, 2 * gw + h * hd : 2 * gw + (h + 1) * hd]
        gcol = gc[:, h : h + 1]
        dec = jnp.exp(jnp.minimum(gcol - gct[h : h + 1, :], 0.0))
        bcol = beta[:, h : h + 1]
        kb = kf * bcol
        eg = jnp.exp(gcol)
        lmats.append(jnp.where(strict, _bdot_nt(kb, kh) * dec, 0.0))
        attns.append(bf(jnp.where(lower, _bdot_nt(qh, kh) * dec, 0.0)))
        kbs.append(bf(kb * eg))
        vbs.append(bf(vh * bcol))
        qds.append(bf(qf * eg))
        kds.append(bf(kf * jnp.exp(gtot[:, h : h + 1] - gcol)))
    tinvs = [bf(t) for t in _tri_inv([-m for m in lmats], _tri_inv_masks(r, col, c))]
    uvals = [jnp.dot(t, vb, preferred_element_type=F32) for t, vb in zip(tinvs, vbs)]
    wcums = [bf(jnp.dot(t, kb, preferred_element_type=F32)) for t, kb in zip(tinvs, kbs)]
    states = [st_ref[h] for h in heads]
    outs = [[] for _ in heads]
    for cc in range(tt // c):
        rs = slice(cc * c, (cc + 1) * c)
        sbs = [bf(s) for s in states]
        vnews = [uvals[h][rs] - jnp.dot(wcums[h][rs], sbs[h], preferred_element_type=F32) for h in heads]
        for h in heads:
            outs[h].append(jnp.dot(qds[h][rs], sbs[h], preferred_element_type=F32) + _bdot(attns[h][rs, rs], vnews[h]))
            glast = jnp.exp(gtot[cc * c : cc * c + 1, h : h + 1])
            states[h] = states[h] * glast + _bdot_tn(kds[h][rs], vnews[h])
    for h in heads:
        st_ref[h] = states[h]
        o = jnp.concatenate(outs[h], axis=0)
        o = o * lax.rsqrt(jnp.mean(o * o, axis=-1, keepdims=True) + RMS_EPS) * ng_ref[...]
        z = u_ref[:, 3 * gw + h * hd : 3 * gw + (h + 1) * hd].astype(F32)
        o_ref[:, h * hd : (h + 1) * hd] = (o * (z * _sigmoid(z))).astype(o_ref.dtype)


def _gdn(u_gdn, tail, conv_w, a_log, dt_bias, norm_g, *, batch, seq):
    tt = min(SCAN_TILE, seq)
    nt = seq // tt
    t = batch * seq
    full = lambda a: pl.BlockSpec(a.shape, lambda b, i: (0,) * a.ndim)
    return pl.pallas_call(
        functools.partial(_gdn_kernel, tt=tt, c=CHUNK),
        grid=(batch, nt),
        in_specs=[
            pl.BlockSpec((tt, 4 * GROUP_W), lambda b, i: (b * nt + i, 0)),
            pl.BlockSpec((tt, 2 * LANE), lambda b, i: (b * nt + i, 0)),
            full(conv_w),
            full(a_log),
            full(dt_bias),
            full(norm_g),
        ],
        out_specs=pl.BlockSpec((tt, GROUP_W), lambda b, i: (b * nt + i, 0)),
        out_shape=jax.ShapeDtypeStruct((t, GROUP_W), BF16),
        scratch_shapes=[pltpu.VMEM((tt + 8, 3 * GROUP_W), F32), pltpu.VMEM((GDN_HEADS, GDN_HD, GDN_HD), F32)],
        compiler_params=_cparams("parallel", "arbitrary"),
        name="gdn",
    )(u_gdn, tail, conv_w, a_log, dt_bias, norm_g)


def _rwkv_kernel(u_ref, mu_ref, w0_ref, w2_ref, a0_ref, a2_ref, g2_ref, kk_ref, ka_ref, rk_ref, lng_ref, lnb_ref,
                 e_ref, o_ref, xs_ref, st_ref, *, tt, c):
    i = pl.program_id(1)
    gw, hd = GROUP_W, RWKV_HD

    @pl.when(i == 0)
    def _():
        xs_ref[0:8, :] = jnp.zeros((8, xs_ref.shape[1]), F32)
        st_ref[...] = jnp.zeros(st_ref.shape, F32)

    u = u_ref[...].astype(F32)
    xs_ref[8 : 8 + tt, :] = u
    prev = xs_ref[7 : 7 + tt, :]
    xs_ref[0:8, :] = xs_ref[tt : tt + 8, :]
    x = u + mu_ref[...] * (prev - u)
    rr = x[:, :gw]
    k = x[:, gw : 2 * gw]
    v = x[:, 2 * gw : 3 * gw]
    wa = x[:, 3 * gw : 3 * gw + LANE]
    glo = x[:, 3 * gw + LANE :]
    w = -_softplus(-(w0_ref[...] + _bdot(jnp.tanh(wa), w2_ref[...]))) - 0.5
    ld = -jnp.exp(w)
    gate_a = _sigmoid(a0_ref[...] + _bdot(wa, a2_ref[...]))
    gate_g = _bdot(_sigmoid(glo), g2_ref[...])
    e = e_ref[...]

    def segsum(y):
        hi = y.astype(BF16)
        lo = (y - hi.astype(F32)).astype(BF16)
        return jnp.dot(hi, e, preferred_element_type=F32) + jnp.dot(lo, e, preferred_element_type=F32)

    kkr = k * kk_ref[...]
    kk = kkr * lax.rsqrt(segsum(kkr * kkr) + 1e-6)
    k2 = k * (1.0 + (gate_a - 1.0) * ka_ref[...])
    bonus = segsum(rr * k2 * rk_ref[...]) * v

    r, col, same, lower, strict = _chunk_masks(tt, c)
    tri = jnp.where(lower, 1.0, 0.0).astype(BF16)
    ones_blk = jnp.where(same, 1.0, 0.0).astype(BF16)
    cum = _dot_exact_lhs(tri, ld)
    ctot = _dot_exact_lhs(ones_blk, ld)
    encum = jnp.exp(-cum)
    edec = jnp.exp(ctot - cum)
    kka = kk * gate_a
    rt = rr * jnp.exp(cum)
    at = -kk * jnp.exp(cum - ld)
    bt = kka * encum
    kt = k2 * encum
    bd = kka * edec
    kd = k2 * edec
    pc = jnp.exp(ctot)

    heads = range(RWKV_HEADS)
    bf = lambda a: a.astype(BF16)
    dot = lambda a, b: jnp.dot(a, b, preferred_element_type=F32)
    dot_nt = lambda a, b: lax.dot_general(a, b, (((1,), (1,)), ((), ())), preferred_element_type=F32)
    dot_tn = lambda a, b: lax.dot_general(a, b, (((0,), (0,)), ((), ())), preferred_element_type=F32)
    at_b, rt_b, bt_b, kt_b, v_b, bd_b, kd_b = bf(at), bf(rt), bf(bt), bf(kt), bf(v), bf(bd), bf(kd)
    sls = [slice(h * hd, (h + 1) * hd) for h in heads]
    strict_f = jnp.where(strict, 1.0, 0.0).astype(F32)
    lower_f = jnp.where(lower, 1.0, 0.0).astype(F32)
    mabs = [dot_nt(at_b[:, sl], bt_b[:, sl]) * strict_f for sl in sls]
    maks = [bf(dot_nt(at_b[:, sl], kt_b[:, sl]) * strict_f) for sl in sls]
    arbs = [bf(dot_nt(rt_b[:, sl], bt_b[:, sl]) * lower_f) for sl in sls]
    arks = [bf(dot_nt(rt_b[:, sl], kt_b[:, sl]) * lower_f) for sl in sls]
    tinvs = [bf(t) for t in _tri_inv(mabs, _tri_inv_masks(r, col, c))]
    wmats = [bf(dot(t, at_b[:, sl])) for t, sl in zip(tinvs, sls)]
    mkvs = [bf(dot(m, v_b[:, sl])) for m, sl in zip(maks, sls)]
    umats = [dot(t, mkv) for t, mkv in zip(tinvs, mkvs)]
    yconsts = [dot(m, v_b[:, sl]) for m, sl in zip(arks, sls)]
    states = [st_ref[h] for h in heads]
    outs = [[] for _ in heads]
    for cc in range(tt // c):
        rs = slice(cc * c, (cc + 1) * c)
        sbs = [bf(s) for s in states]
        sas = [dot_nt(wmats[h][rs], sbs[h]) + umats[h][rs] for h in heads]
        for h in heads:
            sab = bf(sas[h])
            outs[h].append(dot_nt(rt_b[rs, sls[h]], sbs[h]) + dot(arbs[h][rs, rs], sab) + yconsts[h][rs])
            states[h] = (states[h] * pc[cc * c : cc * c + 1, sls[h]] + dot_tn(sab, bd_b[rs, sls[h]])
                         + dot_tn(v_b[rs, sls[h]], kd_b[rs, sls[h]]))
    for h in heads:
        st_ref[h] = states[h]
    y = jnp.concatenate([jnp.concatenate(o, axis=0) for o in outs], axis=-1)
    mean = segsum(y) * (1.0 / hd)
    d = y - mean
    var = segsum(d * d) * (1.0 / hd)
    yn = d * lax.rsqrt(var + RWKV_GN_EPS) * lng_ref[...] + lnb_ref[...]
    o_ref[...] = ((yn + bonus) * gate_g).astype(o_ref.dtype)


def _rwkv(u_rwkv, mu, w0, w2p, a0, a2p, g2, k_k, k_a, r_k, ln_g, ln_b, seg_ones, *, batch, seq):
    tt = min(SCAN_TILE, seq)
    nt = seq // tt
    t = batch * seq
    ncol = u_rwkv.shape[1]
    full = lambda a: pl.BlockSpec(a.shape, lambda b, i: (0,) * a.ndim)
    params = (mu, w0, w2p, a0, a2p, g2, k_k, k_a, r_k, ln_g, ln_b, seg_ones)
    return pl.pallas_call(
        functools.partial(_rwkv_kernel, tt=tt, c=CHUNK),
        grid=(batch, nt),
        in_specs=[pl.BlockSpec((tt, ncol), lambda b, i: (b * nt + i, 0))] + [full(p) for p in params],
        out_specs=pl.BlockSpec((tt, GROUP_W), lambda b, i: (b * nt + i, 0)),
        out_shape=jax.ShapeDtypeStruct((t, GROUP_W), BF16),
        scratch_shapes=[pltpu.VMEM((tt + 8, ncol), F32), pltpu.VMEM((RWKV_HEADS, RWKV_HD, RWKV_HD), F32)],
        compiler_params=_cparams("parallel", "arbitrary"),
        name="rwkv7",
    )(u_rwkv, *params)


def _layer_norm(h, g, b):
    mu = jnp.mean(h, axis=-1, keepdims=True)
    d = h - mu
    var = jnp.mean(d * d, axis=-1, keepdims=True)
    return d * lax.rsqrt(var + LN_EPS) * g + b


def _route_rows(lg):
    neg = -1e30
    lane = lax.broadcasted_iota(jnp.int32, lg.shape, 1)
    lane_f = lane.astype(F32)
    first = lambda hit: jnp.min(jnp.where(hit, lane_f, float(LANE)), axis=-1, keepdims=True)
    is_g = lane < N_GROUPS
    gl = jnp.where(is_g, lg, neg)
    gmax = jnp.max(gl, axis=-1, keepdims=True)
    p_grp = 1.0 / jnp.sum(jnp.where(is_g, jnp.exp(gl - gmax), 0.0), axis=-1, keepdims=True)
    grp = first(gl == gmax)
    lo = N_GROUPS + grp * EXP_PER_GROUP
    el = jnp.where(jnp.logical_and(lane_f >= lo, lane_f < lo + EXP_PER_GROUP), lg, neg)
    v1 = jnp.max(el, axis=-1, keepdims=True)
    i1 = first(el == v1)
    el2 = jnp.where(lane_f == i1, neg, el)
    v2 = jnp.max(el2, axis=-1, keepdims=True)
    i2 = first(el2 == v2)
    r = jnp.exp(v2 - v1)
    g1 = p_grp / (1.0 + r)
    out = jnp.where(lane == 0, i1 - N_GROUPS, 0.0)
    out = jnp.where(lane == 1, i2 - N_GROUPS, out)
    out = jnp.where(lane == 2, g1, out)
    return jnp.where(lane == 3, g1 * r, out)


def _outproj_kernel(yf_ref, ym_ref, yr_ref, yg_ref, w_ref, x_ref, g_ref, b_ref, wrh_ref, wrl_ref, rb_ref,
                    xo_ref, xb_ref, lg_ref, *, alpha):
    gw = GROUP_W
    acc = jnp.dot(yf_ref[...], w_ref[0:gw, :], preferred_element_type=F32)
    acc = acc + jnp.dot(ym_ref[...], w_ref[gw : 2 * gw, :], preferred_element_type=F32)
    acc = acc + jnp.dot(yr_ref[...], w_ref[2 * gw : 3 * gw, :], preferred_element_type=F32)
    acc = acc + jnp.dot(yg_ref[...], w_ref[3 * gw : 4 * gw, :], preferred_element_type=F32)
    xn = _layer_norm(alpha * x_ref[...] + acc, g_ref[...], b_ref[...])
    xo_ref[...] = xn
    xh = xn.astype(BF16)
    xb_ref[...] = xh
    xl = (xn - xh.astype(F32)).astype(BF16)
    wrh = wrh_ref[...]
    d = lambda a, b: jnp.dot(a, b, preferred_element_type=F32)
    lg_ref[...] = _route_rows(d(xh, wrh) + d(xl, wrh) + d(xh, wrl_ref[...]) + rb_ref[...])


def _outproj(ys, w_out, x, ln_g, ln_b, wr_hi, wr_lo, r_bias, alpha):
    t, d = x.shape
    tm = min(ROW_TILE, t)
    row = lambda n: pl.BlockSpec((tm, n), lambda i: (i, 0))
    full = lambda a: pl.BlockSpec(a.shape, lambda i: (0,) * a.ndim)
    return pl.pallas_call(
        functools.partial(_outproj_kernel, alpha=alpha),
        grid=(t // tm,),
        in_specs=[row(GROUP_W)] * 4 + [full(w_out), row(d), full(ln_g), full(ln_b), full(wr_hi), full(wr_lo), full(r_bias)],
        out_specs=[row(d), row(d), row(LANE)],
        out_shape=[
            jax.ShapeDtypeStruct((t, d), F32),
            jax.ShapeDtypeStruct((t, d), BF16),
            jax.ShapeDtypeStruct((t, LANE), F32),
        ],
        compiler_params=_cparams("parallel"),
        name="outproj_ln_router",
    )(*ys, w_out, x, ln_g, ln_b, wr_hi, wr_lo, r_bias)


def _expert_kernel(be_ref, x_ref, wg_ref, wu_ref, wd_ref, ws_ref, o_ref):
    del be_ref
    x = x_ref[...]
    a = jnp.dot(x, wg_ref[...], preferred_element_type=F32)
    b = jnp.dot(x, wu_ref[...], preferred_element_type=F32)
    hmid = (a * _sigmoid(a) * b).astype(BF16)
    y = jnp.dot(hmid, wd_ref[...], preferred_element_type=F32)
    o_ref[...] = (y * ws_ref[...]).astype(o_ref.dtype)


def _experts(block_expert, xs, w_gate, w_up, w_down, w_slot):
    n_slots, d = xs.shape
    tb = MOE_TILE
    de = w_gate.shape[-1]
    grid_spec = pltpu.PrefetchScalarGridSpec(
        num_scalar_prefetch=1,
        grid=(n_slots // tb,),
        in_specs=[
            pl.BlockSpec((tb, d), lambda i, be: (i, 0)),
            pl.BlockSpec((None, d, de), lambda i, be: (be[i], 0, 0)),
            pl.BlockSpec((None, d, de), lambda i, be: (be[i], 0, 0)),
            pl.BlockSpec((None, de, d), lambda i, be: (be[i], 0, 0)),
            pl.BlockSpec((tb, 1), lambda i, be: (i, 0)),
        ],
        out_specs=pl.BlockSpec((tb, d), lambda i, be: (i, 0)),
    )
    return pl.pallas_call(
        _expert_kernel,
        grid_spec=grid_spec,
        out_shape=jax.ShapeDtypeStruct((n_slots, d), F32),
        compiler_params=_cparams("arbitrary"),
        name="experts",
    )(block_expert, xs, w_gate, w_up, w_down, w_slot)


def _ln2_kernel(x_ref, m_ref, g_ref, b_ref, xo_ref, xb_ref, *, alpha):
    xn = _layer_norm(alpha * x_ref[...] + m_ref[...], g_ref[...], b_ref[...])
    xo_ref[...] = xn
    xb_ref[...] = xn.astype(BF16)


def _ln2(x, moe, g, b, alpha):
    t, d = x.shape
    tm = min(ROW_TILE, t)
    row = pl.BlockSpec((tm, d), lambda i: (i, 0))
    full = lambda a: pl.BlockSpec(a.shape, lambda i: (0,) * a.ndim)
    return pl.pallas_call(
        functools.partial(_ln2_kernel, alpha=alpha),
        grid=(t // tm,),
        in_specs=[row, row, full(g), full(b)],
        out_specs=[row, row],
        out_shape=[jax.ShapeDtypeStruct((t, d), F32), jax.ShapeDtypeStruct((t, d), BF16)],
        compiler_params=_cparams("parallel"),
        name="residual_ln2",
    )(x, moe, g, b)


def _pad_cols(w, n):
    return jnp.pad(w, [(0, 0)] * (w.ndim - 1) + [(0, n - w.shape[-1])])


def _rot_half_cols(w):
    half = w.shape[-1] // 2
    return jnp.concatenate([-w[..., half:], w[..., :half]], axis=-1)


def _prep_weights(p):
    w_in = p["w_in"]
    c_fox = 3 * GROUP_W + FOX_HEADS
    c_mla = MLA_Q_RANK + MLA_KV_RANK + MLA_ROPE
    c_rwkv = 3 * GROUP_W + 2 * 64 + 128
    o_mla = c_fox
    o_rwkv = o_mla + c_mla
    o_gdn = o_rwkv + c_rwkv
    row = lambda a: a[:, None, :].astype(F32)

    w_fox = jnp.concatenate(
        [w_in[..., :GROUP_W] * (FOX_HD**-0.5 * LOG2E), w_in[..., GROUP_W : 3 * GROUP_W], _pad_cols(w_in[..., 3 * GROUP_W : c_fox], LANE)],
        axis=-1,
    )
    kpe_w = w_in[..., o_mla + MLA_Q_RANK + MLA_KV_RANK : o_mla + c_mla]
    w_mla = jnp.concatenate(
        [w_in[..., o_mla : o_mla + MLA_Q_RANK + MLA_KV_RANK], _pad_cols(kpe_w, LANE), _pad_cols(_rot_half_cols(kpe_w), LANE)],
        axis=-1,
    )
    w_rwkv = w_in[..., o_rwkv:o_gdn]
    g0 = o_gdn + 4 * GROUP_W
    w_gdn = jnp.concatenate(
        [w_in[..., o_gdn:g0], _pad_cols(w_in[..., g0 : g0 + GDN_HEADS], LANE), _pad_cols(w_in[..., g0 + GDN_HEADS :], LANE)],
        axis=-1,
    )

    nl = w_in.shape[0]
    scale = (MLA_NOPE + MLA_ROPE) ** -0.5 * LOG2E
    wq = p["mla_w_uq"].reshape(nl, MLA_Q_RANK, MLA_HEADS, MLA_NOPE + MLA_ROPE) * scale
    wq_nope = wq[..., :MLA_NOPE].reshape(nl, MLA_Q_RANK, -1)
    wq_pe = wq[..., MLA_NOPE:]
    wq_p = jnp.concatenate(
        [wq_nope, _pad_cols(wq_pe, LANE).reshape(nl, MLA_Q_RANK, -1), _pad_cols(_rot_half_cols(wq_pe), LANE).reshape(nl, MLA_Q_RANK, -1)],
        axis=-1,
    )
    wkv = p["mla_w_ukv"].reshape(nl, MLA_KV_RANK, MLA_HEADS, MLA_NOPE + MLA_VD)
    wkv_p = jnp.concatenate([wkv[..., :MLA_NOPE].reshape(nl, MLA_KV_RANK, -1), wkv[..., MLA_NOPE:].reshape(nl, MLA_KV_RANK, -1)], axis=-1)

    zeros64 = jnp.zeros((nl, 64, GROUP_W), F32)
    w_router = _pad_cols(jnp.concatenate([p["moe_w_grp"], p["moe_w_exp"]], axis=-1), LANE)
    wr_hi = w_router.astype(BF16)
    wr_lo = (w_router - wr_hi.astype(F32)).astype(BF16)
    return dict(
        w_fox=w_fox.astype(BF16), w_mla=w_mla.astype(BF16), w_rwkv=w_rwkv.astype(BF16), w_gdn=w_gdn.astype(BF16),
        fox_b_f=p["fox_b_f"], fox_out_g=row(p["fox_out_g"]),
        mla_qg=row(p["mla_q_norm_g"]), mla_kvg=row(p["mla_kv_norm_g"]), mla_wq=wq_p.astype(BF16), mla_wkv=wkv_p.astype(BF16),
        mla_out_g=row(p["mla_out_g"]),
        rwkv_mu=row(p["rwkv_mu"]), rwkv_w0=row(p["rwkv_w0"]),
        rwkv_w2=jnp.concatenate([p["rwkv_w2"], zeros64], axis=1).astype(BF16),
        rwkv_a0=row(p["rwkv_a0"]), rwkv_a2=jnp.concatenate([zeros64, p["rwkv_a2"]], axis=1).astype(BF16),
        rwkv_g2=p["rwkv_g2"].astype(BF16), rwkv_k_k=row(p["rwkv_k_k"]), rwkv_k_a=row(p["rwkv_k_a"]),
        rwkv_r_k=row(p["rwkv_r_k"]), rwkv_ln_g=row(p["rwkv_ln_g"]), rwkv_ln_b=row(p["rwkv_ln_b"]),
        gdn_conv_w=p["gdn_conv_w"].astype(F32), gdn_a_log=row(_pad_cols(p["gdn_a_log"], LANE)),
        gdn_dt_bias=row(_pad_cols(p["gdn_dt_bias"], LANE)), gdn_norm_g=row(p["gdn_norm_g"]),
        w_out=p["w_out"].astype(BF16), ln1_g=row(p["ln1_g"]), ln1_b=row(p["ln1_b"]),
        wr_hi=wr_hi, wr_lo=wr_lo, r_bias=row(_pad_cols(jnp.concatenate([p["moe_b_grp"], p["moe_b_exp"]], axis=-1), LANE)),
        moe_w_gate=p["moe_w_gate"].astype(BF16), moe_w_up=p["moe_w_up"].astype(BF16), moe_w_down=p["moe_w_down"].astype(BF16),
        ln2_g=row(p["ln2_g"]), ln2_b=row(p["ln2_b"]),
    )


def _route(routed, tb):
    t = routed.shape[0]
    expert = routed[:, :TOP_K].astype(jnp.int32)
    gate = routed[:, TOP_K : 2 * TOP_K]

    a = t * TOP_K
    n_blocks = (a + N_EXPERTS * (tb - 1) + tb - 1) // tb
    n_slots = n_blocks * tb
    flat_e = expert.reshape(a)
    order = jnp.argsort(flat_e)
    e_sorted = flat_e[order]
    counts = jnp.bincount(flat_e, length=N_EXPERTS)
    padded = (counts + tb - 1) // tb * tb
    start = jnp.cumsum(counts) - counts
    pend = jnp.cumsum(padded)
    pstart = pend - padded
    dest = (pstart[e_sorted] + jnp.arange(a) - start[e_sorted]).astype(jnp.int32)
    token_of_slot = jnp.full((n_slots,), t, jnp.int32).at[dest].set((order // TOP_K).astype(jnp.int32))
    weight_of_slot = jnp.zeros((n_slots,), F32).at[dest].set(gate.reshape(a)[order])
    block_expert = jnp.minimum(jnp.searchsorted(pend, jnp.arange(n_blocks) * tb, side="right"), N_EXPERTS - 1).astype(jnp.int32)
    slot_of_assignment = jnp.zeros((a,), jnp.int32).at[order].set(dest).reshape(t, TOP_K)
    return token_of_slot, weight_of_slot, block_expert, slot_of_assignment


def _layer(x, xb, cs, sn, w, *, batch, seq, alpha):
    t = batch * seq
    u_fox, fox_tail = _inproj(xb, w["w_fox"], LANE)
    (u_mla,) = _inproj(xb, w["w_mla"], 0)
    (u_rwkv,) = _inproj(xb, w["w_rwkv"], 0)
    u_gdn, gdn_tail = _inproj(xb, w["w_gdn"], 2 * LANE)

    log_f = -jax.nn.softplus(-(fox_tail[:, :FOX_HEADS] + w["fox_b_f"]))
    fcum = jnp.cumsum(log_f.reshape(batch, seq, FOX_HEADS), axis=1) * LOG2E
    vt_fox = u_fox[:, 2 * GROUP_W :].reshape(batch, seq, GROUP_W).transpose(0, 2, 1)
    y_fox = _attention(u_fox, 0, u_fox, 1, vt_fox, fcum, w["fox_out_g"], batch=batch, seq=seq,
                       heads=FOX_HEADS, dk=FOX_HD, dv=FOX_HD, chunk=1)

    q_mla, k_mla, v_mla = _mla_prep(u_mla, cs, sn, w["mla_qg"], w["mla_kvg"], w["mla_wq"], w["mla_wkv"])
    vt_mla = v_mla.reshape(batch, seq, GROUP_W).transpose(0, 2, 1)
    y_mla = _attention(q_mla, 0, k_mla, 0, vt_mla, None, w["mla_out_g"], batch=batch, seq=seq,
                       heads=MLA_HEADS, dk=2 * LANE, dv=MLA_VD, chunk=CHUNK)

    seg = jnp.arange(GROUP_W) // RWKV_HD
    seg_ones = (seg[:, None] == seg[None, :]).astype(BF16)
    y_rwkv = _rwkv(u_rwkv, w["rwkv_mu"], w["rwkv_w0"], w["rwkv_w2"], w["rwkv_a0"], w["rwkv_a2"], w["rwkv_g2"],
                   w["rwkv_k_k"], w["rwkv_k_a"], w["rwkv_r_k"], w["rwkv_ln_g"], w["rwkv_ln_b"], seg_ones,
                   batch=batch, seq=seq)
    y_gdn = _gdn(u_gdn, gdn_tail, w["gdn_conv_w"], w["gdn_a_log"], w["gdn_dt_bias"], w["gdn_norm_g"],
                 batch=batch, seq=seq)

    x1, x1b, routed = _outproj((y_fox, y_mla, y_rwkv, y_gdn), w["w_out"], x, w["ln1_g"], w["ln1_b"],
                               w["wr_hi"], w["wr_lo"], w["r_bias"], alpha)

    token_of_slot, weight_of_slot, block_expert, slot_of_assignment = _route(routed, MOE_TILE)
    x_pad = jnp.concatenate([x1b, jnp.zeros((1, x1b.shape[1]), BF16)], axis=0)
    y_slots = _experts(block_expert, x_pad[token_of_slot], w["moe_w_gate"], w["moe_w_up"], w["moe_w_down"],
                       weight_of_slot[:, None])
    moe = y_slots[slot_of_assignment[:, 0]] + y_slots[slot_of_assignment[:, 1]]
    return _ln2(x1, moe, w["ln2_g"], w["ln2_b"], alpha)


def kernel(x, positions, w_in, fox_b_f, fox_out_g, mla_q_norm_g, mla_kv_norm_g, mla_w_uq, mla_w_ukv, mla_out_g, rwkv_mu, rwkv_w0, rwkv_w2, rwkv_a0, rwkv_a2, rwkv_g2, rwkv_k_k, rwkv_k_a, rwkv_r_k, rwkv_ln_g, rwkv_ln_b, gdn_conv_w, gdn_a_log, gdn_dt_bias, gdn_norm_g, w_out, ln1_g, ln1_b, moe_w_grp, moe_b_grp, moe_w_exp, moe_b_exp, moe_w_gate, moe_w_up, moe_w_down, ln2_g, ln2_b):
    batch, seq, d = x.shape
    depth = w_in.shape[0]
    alpha = (2 * depth) ** 0.25
    params = dict(
        w_in=w_in, fox_b_f=fox_b_f, fox_out_g=fox_out_g, mla_q_norm_g=mla_q_norm_g, mla_kv_norm_g=mla_kv_norm_g,
        mla_w_uq=mla_w_uq, mla_w_ukv=mla_w_ukv, mla_out_g=mla_out_g, rwkv_mu=rwkv_mu, rwkv_w0=rwkv_w0, rwkv_w2=rwkv_w2,
        rwkv_a0=rwkv_a0, rwkv_a2=rwkv_a2, rwkv_g2=rwkv_g2, rwkv_k_k=rwkv_k_k, rwkv_k_a=rwkv_k_a, rwkv_r_k=rwkv_r_k,
        rwkv_ln_g=rwkv_ln_g, rwkv_ln_b=rwkv_ln_b, gdn_conv_w=gdn_conv_w, gdn_a_log=gdn_a_log, gdn_dt_bias=gdn_dt_bias,
        gdn_norm_g=gdn_norm_g, w_out=w_out, ln1_g=ln1_g, ln1_b=ln1_b, moe_w_grp=moe_w_grp, moe_b_grp=moe_b_grp,
        moe_w_exp=moe_w_exp, moe_b_exp=moe_b_exp, moe_w_gate=moe_w_gate, moe_w_up=moe_w_up, moe_w_down=moe_w_down,
        ln2_g=ln2_g, ln2_b=ln2_b,
    )
    weights = _prep_weights(params)

    half = MLA_ROPE // 2
    inv_freq = ROPE_THETA ** (-jnp.arange(half, dtype=F32) / half)
    ang = positions.astype(F32).reshape(batch * seq, 1) * inv_freq
    zpad = jnp.zeros((batch * seq, LANE - MLA_ROPE), F32)
    cs = jnp.concatenate([jnp.cos(ang), jnp.cos(ang), zpad], axis=-1)
    sn = jnp.concatenate([jnp.sin(ang), jnp.sin(ang), zpad], axis=-1)

    xf = x.reshape(batch * seq, d).astype(F32)

    def body(carry, w):
        xc, xcb = carry
        return _layer(xc, xcb, cs, sn, w, batch=batch, seq=seq, alpha=alpha), None

    (xf, _), _ = lax.scan(body, (xf, xf.astype(BF16)), weights)
    return xf.reshape(batch, seq, d).astype(x.dtype)
```

```python
import functools
import math

import jax
import jax.numpy as jnp
from jax import lax
from jax.experimental import pallas as pl
from jax.experimental.pallas import tpu as pltpu

F32 = jnp.float32
BF16 = jnp.bfloat16

D_MODEL = 2048
GROUP_W = 512
FOX_HD, FOX_HEADS = 64, 8
MLA_HEADS, MLA_NOPE, MLA_ROPE, MLA_VD = 4, 128, 64, 128
MLA_Q_RANK, MLA_KV_RANK = 384, 128
ROPE_THETA = 10000.0
RWKV_HD, RWKV_HEADS = 64, 8
RWKV_GN_EPS = 64e-5
GDN_HD, GDN_HEADS, GDN_CONV = 128, 4, 4
N_GROUPS, EXP_PER_GROUP, TOP_K, D_EXPERT = 4, 8, 2, 512
N_EXPERTS = N_GROUPS * EXP_PER_GROUP
CHUNK = 64
LN_EPS = 1e-5
RMS_EPS = 1e-6
LOG2E = math.log2(math.e)

LANE = 128
VMEM_LIMIT_BYTES = 56 * 1024 * 1024
ROW_TILE = 512
ATTN_TILE = 256
SCAN_TILE = 256
MOE_TILE = 256


def _cparams(*sem):
    return pltpu.CompilerParams(dimension_semantics=sem, vmem_limit_bytes=VMEM_LIMIT_BYTES)


def _bdot(a, b):
    return jnp.dot(a.astype(BF16), b.astype(BF16), preferred_element_type=F32)


def _bdot_nt(a, b):
    return lax.dot_general(a.astype(BF16), b.astype(BF16), (((1,), (1,)), ((), ())), preferred_element_type=F32)


def _bdot_tn(a, b):
    return lax.dot_general(a.astype(BF16), b.astype(BF16), (((0,), (0,)), ((), ())), preferred_element_type=F32)


def _split3(x):
    hi = x.astype(BF16)
    r1 = x - hi.astype(F32)
    mid = r1.astype(BF16)
    lo = (r1 - mid.astype(F32)).astype(BF16)
    return hi, mid, lo


def _dot_exact_lhs(m, x):
    hi, mid, lo = _split3(x)
    d = lambda p: jnp.dot(m, p, preferred_element_type=F32)
    return d(hi) + d(mid) + d(lo)


def _sigmoid(x):
    return 1.0 / (1.0 + jnp.exp(-x))


def _softplus(x):
    return jnp.maximum(x, 0.0) + jnp.log(1.0 + jnp.exp(-jnp.abs(x)))


def _chunk_masks(n, c):
    r = lax.broadcasted_iota(jnp.int32, (n, n), 0)
    col = lax.broadcasted_iota(jnp.int32, (n, n), 1)
    same = (r // c) == (col // c)
    lower = jnp.logical_and(same, col <= r)
    strict = jnp.logical_and(same, col < r)
    return r, col, same, lower, strict


def _tri_inv_masks(r, col, c):
    blk = lambda s: (r // s) == (col // s)
    one = lambda cond: jnp.where(cond, 1.0, 0.0).astype(F32)
    offs = []
    s = 8
    while s < c:
        offs.append(one(jnp.logical_and(blk(2 * s), jnp.logical_not(blk(s)))))
        s *= 2
    return one(r == col), one(blk(8)), offs


def _tri_inv(ms, masks):
    eye, blk8, offs = masks
    bf = lambda a: a.astype(BF16)
    dot = lambda a, b: jnp.dot(a, b, preferred_element_type=F32)
    mdf = [m * blk8 for m in ms]
    mds = [bf(m) for m in mdf]
    xs = [eye + m for m in mdf]
    m2s = [bf(dot(md, md)) for md in mds]
    xs = [x + dot(m2, bf(x)) for x, m2 in zip(xs, m2s)]
    m4s = [bf(dot(m2, m2)) for m2 in m2s]
    xs = [x + dot(m4, bf(x)) for x, m4 in zip(xs, m4s)]
    for off in offs:
        xbs = [bf(x) for x in xs]
        ts = [bf(dot(xb, bf(m * off))) for xb, m in zip(xbs, ms)]
        xs = [x + dot(t, xb) for x, t, xb in zip(xs, ts, xbs)]
    return xs


def _inproj_kernel(x_ref, w_ref, o_ref, *tail_refs, tail):
    acc = jnp.dot(x_ref[...], w_ref[...], preferred_element_type=F32)
    n = acc.shape[1]
    if tail:
        o_ref[...] = acc[:, : n - tail].astype(o_ref.dtype)
        tail_refs[0][...] = acc[:, n - tail :]
    else:
        o_ref[...] = acc.astype(o_ref.dtype)


def _inproj(xb, w, tail):
    t, d = xb.shape
    n = w.shape[1]
    tm = min(ROW_TILE, t)
    out_shape = [jax.ShapeDtypeStruct((t, n - tail), BF16)]
    out_specs = [pl.BlockSpec((tm, n - tail), lambda i: (i, 0))]
    if tail:
        out_shape.append(jax.ShapeDtypeStruct((t, tail), F32))
        out_specs.append(pl.BlockSpec((tm, tail), lambda i: (i, 0)))
    return pl.pallas_call(
        functools.partial(_inproj_kernel, tail=tail),
        grid=(t // tm,),
        in_specs=[pl.BlockSpec((tm, d), lambda i: (i, 0)), pl.BlockSpec((d, n), lambda i: (0, 0))],
        out_specs=out_specs,
        out_shape=out_shape,
        compiler_params=_cparams("parallel"),
        name="inproj",
    )(xb, w)


def _attn_kernel(*refs, heads, dk, dv, chunk, tq, seq, use_bias):
    if use_bias:
        q_ref, k_ref, vt_ref, gate_ref, gb_ref, g_ref, o_ref, brep_ref, m_ref, l_ref, acc_ref = refs
    else:
        q_ref, k_ref, vt_ref, g_ref, o_ref, m_ref, l_ref, acc_ref = refs
    paired = dk < LANE
    dkp = LANE if paired else dk
    i = pl.program_id(1)

    if use_bias:
        @pl.when(i == 0)
        def _():
            r = lax.broadcasted_iota(jnp.int32, (tq, tq), 0)
            c = lax.broadcasted_iota(jnp.int32, (tq, tq), 1)
            tri = jnp.where(c <= r, 1.0, 0.0).astype(BF16)
            carry = jnp.zeros((1, LANE), F32)
            for j in range(seq // tq):
                rows = slice(j * tq, (j + 1) * tq)
                log_f = -_softplus(-(gate_ref[rows, :] + gb_ref[...])) * LOG2E
                cum = _dot_exact_lhs(tri, log_f) + carry
                carry = cum[tq - 1 : tq, :]
                for h in range(heads):
                    brep_ref[h, rows, :] = jnp.broadcast_to(cum[:, h : h + 1], (tq, LANE))

    m_ref[...] = jnp.full(m_ref.shape, -1e30, F32)
    l_ref[...] = jnp.zeros(l_ref.shape, F32)
    acc_ref[...] = jnp.zeros(acc_ref.shape, F32)

    qs = []
    for h in range(heads):
        if paired:
            slab = q_ref[:, (h // 2) * LANE : (h // 2 + 1) * LANE]
            lane_half = lax.broadcasted_iota(jnp.int32, (tq, LANE), 1) // dk
            qs.append(jnp.where(lane_half == h % 2, slab, jnp.zeros_like(slab)))
        else:
            qs.append(q_ref[:, h * dkp : (h + 1) * dkp])

    def step(off, masked):
        if masked:
            kr = lax.broadcasted_iota(jnp.int32, (tq, tq), 0)
            qc = lax.broadcasted_iota(jnp.int32, (tq, tq), 1)
            allowed = (kr // chunk) <= (qc // chunk)
        sts = []
        for h in range(heads):
            slab = h // 2 if paired else h
            k = k_ref[pl.ds(off, tq), slab * dkp : (slab + 1) * dkp]
            sts.append(lax.dot_general(k, qs[h], (((1,), (1,)), ((), ())), preferred_element_type=F32))
        ps, alphas = [], []
        for h in range(heads):
            st = sts[h]
            if use_bias:
                st = st - jnp.concatenate([brep_ref[h, pl.ds(off, tq), :]] * (tq // LANE), axis=1)
            if masked:
                st = jnp.where(allowed, st, -1e30)
            m_old = m_ref[h, 0:1, :]
            m_new = jnp.maximum(m_old, jnp.max(st, axis=0, keepdims=True))
            p = jnp.exp2(st - m_new)
            alpha = jnp.exp2(m_old - m_new)
            l_ref[h, 0:1, :] = alpha * l_ref[h, 0:1, :] + jnp.sum(p, axis=0, keepdims=True)
            m_ref[h, 0:1, :] = m_new
            ps.append(p.astype(BF16))
            alphas.append(alpha)
        for h in range(heads):
            rows = slice(h * dv, (h + 1) * dv)
            pv = jnp.dot(vt_ref[rows, pl.ds(off, tq)], ps[h], preferred_element_type=F32)
            acc_ref[rows, :] = alphas[h] * acc_ref[rows, :] + pv

    def body(j, carry):
        step(pl.multiple_of(j * tq, tq), False)
        return carry

    lax.fori_loop(0, i, body, 0)
    step(pl.multiple_of(i * tq, tq), True)

    ot = jnp.concatenate([acc_ref[h * dv : (h + 1) * dv, :] / l_ref[h, 0:1, :] for h in range(heads)], axis=0)
    ot = ot * lax.rsqrt(jnp.mean(ot * ot, axis=0, keepdims=True) + RMS_EPS)
    o_ref[...] = (ot.T * g_ref[...]).astype(o_ref.dtype)


def _attention(q_arr, q_col, k_arr, k_col, vt, gate, gain, *, batch, seq, heads, dk, dv, chunk):
    bias = gate
    assert dk % LANE == 0 or (2 * dk == LANE and heads % 2 == 0)
    tq = min(ATTN_TILE, seq)
    nq = seq // tq
    t = batch * seq
    in_specs = [
        pl.BlockSpec((tq, heads * dk), lambda b, i: (b * nq + i, q_col)),
        pl.BlockSpec((seq, heads * dk), lambda b, i: (b, k_col)),
        pl.BlockSpec((None, heads * dv, seq), lambda b, i: (b, 0, 0)),
    ]
    args = [q_arr, k_arr, vt]
    scratch = []
    if bias is not None:
        in_specs += [pl.BlockSpec((seq, LANE), lambda b, i: (b, 0)), pl.BlockSpec((1, LANE), lambda b, i: (0, 0))]
        args += list(gate)
        scratch.append(pltpu.VMEM((heads, seq, LANE), F32))
    in_specs.append(pl.BlockSpec((1, heads * dv), lambda b, i: (0, 0)))
    args.append(gain)
    stat = pltpu.VMEM((heads, 8, tq), F32)
    scratch += [stat, stat, pltpu.VMEM((heads * dv, tq), F32)]
    return pl.pallas_call(
        functools.partial(_attn_kernel, heads=heads, dk=dk, dv=dv, chunk=chunk, tq=tq, seq=seq, use_bias=bias is not None),
        grid=(batch, nq),
        in_specs=in_specs,
        out_specs=pl.BlockSpec((tq, heads * dv), lambda b, i: (b * nq + i, 0)),
        out_shape=jax.ShapeDtypeStruct((t, heads * dv), BF16),
        scratch_shapes=scratch,
        compiler_params=_cparams("parallel", "arbitrary"),
        name="attention",
    )(*args)


def _mla_prep_kernel(u_ref, cs_ref, sn_ref, qg_ref, kvg_ref, wq_ref, wkv_ref, q_ref, k_ref, v_ref):
    u = u_ref[...].astype(F32)
    cs = cs_ref[...]
    sn = sn_ref[...]

    def rms(x, g):
        return x * lax.rsqrt(jnp.mean(x * x, axis=-1, keepdims=True) + RMS_EPS) * g

    qo = _bdot(rms(u[:, :MLA_Q_RANK], qg_ref[...]), wq_ref[...])
    kvo = _bdot(rms(u[:, MLA_Q_RANK : MLA_Q_RANK + MLA_KV_RANK], kvg_ref[...]), wkv_ref[...])
    c0 = MLA_Q_RANK + MLA_KV_RANK
    kpe = (u[:, c0 : c0 + LANE] * cs + u[:, c0 + LANE : c0 + 2 * LANE] * sn).astype(k_ref.dtype)
    nn = MLA_HEADS * MLA_NOPE
    for h in range(MLA_HEADS):
        a = h * 2 * LANE
        q_ref[:, a : a + LANE] = qo[:, h * LANE : (h + 1) * LANE].astype(q_ref.dtype)
        qpe = qo[:, nn + h * LANE : nn + (h + 1) * LANE] * cs + qo[:, 2 * nn + h * LANE : 2 * nn + (h + 1) * LANE] * sn
        q_ref[:, a + LANE : a + 2 * LANE] = qpe.astype(q_ref.dtype)
        k_ref[:, a : a + LANE] = kvo[:, h * LANE : (h + 1) * LANE].astype(k_ref.dtype)
        k_ref[:, a + LANE : a + 2 * LANE] = kpe
    v_ref[...] = kvo[:, nn:].T.astype(v_ref.dtype)


def _mla_prep(u_mla, cs, sn, qg, kvg, wq, wkv, *, batch, seq):
    t = u_mla.shape[0]
    tm = min(ROW_TILE, seq)
    nt = seq // tm
    row = lambda n: pl.BlockSpec((tm, n), lambda i: (i, 0))
    full = lambda a: pl.BlockSpec(a.shape, lambda i: (0,) * a.ndim)
    wide = MLA_HEADS * 2 * LANE
    dvs = MLA_HEADS * MLA_VD
    return pl.pallas_call(
        _mla_prep_kernel,
        grid=(t // tm,),
        in_specs=[row(u_mla.shape[1]), row(LANE), row(LANE), full(qg), full(kvg), full(wq), full(wkv)],
        out_specs=[row(wide), row(wide), pl.BlockSpec((None, dvs, tm), lambda i: (i // nt, 0, i % nt))],
        out_shape=[
            jax.ShapeDtypeStruct((t, wide), BF16),
            jax.ShapeDtypeStruct((t, wide), BF16),
            jax.ShapeDtypeStruct((batch, dvs, seq), BF16),
        ],
        compiler_params=_cparams("parallel"),
        name="mla_prep",
    )(u_mla, cs, sn, qg, kvg, wq, wkv)


def _gdn_kernel(u_ref, t_ref, cw_ref, alog_ref, dtb_ref, ng_ref, o_ref, xs_ref, st_ref, *, tt, c):
    i = pl.program_id(1)
    gw, hd = GROUP_W, GDN_HD

    @pl.when(i == 0)
    def _():
        xs_ref[0:8, :] = jnp.zeros((8, 3 * gw), F32)
        st_ref[...] = jnp.zeros(st_ref.shape, F32)

    xs_ref[8 : 8 + tt, :] = u_ref[:, : 3 * gw].astype(F32)
    cw = cw_ref[...]
    conv = cw[0:1, :] * xs_ref[5 : 5 + tt, :]
    for j in range(1, GDN_CONV):
        conv = conv + cw[j : j + 1, :] * xs_ref[5 + j : 5 + j + tt, :]
    xs_ref[0:8, :] = xs_ref[tt : tt + 8, :]
    qkv = conv * _sigmoid(conv)

    tail = t_ref[...]
    beta = _sigmoid(tail[:, :LANE])
    g = -jnp.exp(alog_ref[...]) * _softplus(tail[:, LANE:] + dtb_ref[...])

    r, col, same, lower, strict = _chunk_masks(tt, c)
    tri = jnp.where(lower, 1.0, 0.0).astype(BF16)
    ones_blk = jnp.where(same, 1.0, 0.0).astype(BF16)
    gc = _dot_exact_lhs(tri, g)
    gtot = _dot_exact_lhs(ones_blk, g)
    gct = gc.T

    def l2n(x):
        return x * lax.rsqrt(jnp.sum(x * x, axis=-1, keepdims=True) + 1e-6)

    heads = range(GDN_HEADS)
    bf = lambda a: a.astype(BF16)
    lmats, attns, kbs, vbs, qds, kds, egs = [], [], [], [], [], [], []
    for h in heads:
        qf = l2n(qkv[:, h * hd : (h + 1) * hd]) * (hd**-0.5)
        qh = bf(qf)
        kf = l2n(qkv[:, gw + h * hd : gw + (h + 1) * hd])
        kh = bf(kf)
        vh = qkv[:, 2 * gw + h * hd : 2 * gw + (h + 1) * hd]
        gcol = gc[:, h : h + 1]
        dec = jnp.exp(jnp.minimum(gcol - gct[h : h + 1, :], 0.0))
        bcol = beta[:, h : h + 1]
        kb = kf * bcol
        eg = jnp.exp(gcol)
        lmats.append(jnp.where(strict, _bdot_nt(kb, kh) * dec, 0.0))
        attns.append(bf(jnp.where(lower, _bdot_nt(qh, kh) * dec, 0.0)))
        kbs.append(bf(kb * eg))
        vbs.append(bf(vh * bcol))
        qds.append(bf(qf * eg))
        kds.append(bf(kf * jnp.exp(gtot[:, h : h + 1] - gcol)))
    tinvs = [bf(t) for t in _tri_inv([-m for m in lmats], _tri_inv_masks(r, col, c))]
    uvals = [jnp.dot(t, vb, preferred_element_type=F32) for t, vb in zip(tinvs, vbs)]
    wcums = [bf(jnp.dot(t, kb, preferred_element_type=F32)) for t, kb in zip(tinvs, kbs)]
    states = [st_ref[h] for h in heads]
    outs = [[] for _ in heads]
    for cc in range(tt // c):
        rs = slice(cc * c, (cc + 1) * c)
        sbs = [bf(s) for s in states]
        vnews = [uvals[h][rs] - jnp.dot(wcums[h][rs], sbs[h], preferred_element_type=F32) for h in heads]
        for h in heads:
            outs[h].append(jnp.dot(qds[h][rs], sbs[h], preferred_element_type=F32) + _bdot(attns[h][rs, rs], vnews[h]))
            glast = jnp.exp(gtot[cc * c : cc * c + 1, h : h + 1])
            states[h] = states[h] * glast + _bdot_tn(kds[h][rs], vnews[h])
    for h in heads:
        st_ref[h] = states[h]
        o = jnp.concatenate(outs[h], axis=0)
        o = o * lax.rsqrt(jnp.mean(o * o, axis=-1, keepdims=True) + RMS_EPS) * ng_ref[...]
        z = u_ref[:, 3 * gw + h * hd : 3 * gw + (h + 1) * hd].astype(F32)
        o_ref[:, h * hd : (h + 1) * hd] = (o * (z * _sigmoid(z))).astype(o_ref.dtype)


def _gdn(u_gdn, tail, conv_w, a_log, dt_bias, norm_g, *, batch, seq):
    tt = min(SCAN_TILE, seq)
    nt = seq // tt
    t = batch * seq
    full = lambda a: pl.BlockSpec(a.shape, lambda b, i: (0,) * a.ndim)
    return pl.pallas_call(
        functools.partial(_gdn_kernel, tt=tt, c=CHUNK),
        grid=(batch, nt),
        in_specs=[
            pl.BlockSpec((tt, 4 * GROUP_W), lambda b, i: (b * nt + i, 0)),
            pl.BlockSpec((tt, 2 * LANE), lambda b, i: (b * nt + i, 0)),
            full(conv_w),
            full(a_log),
            full(dt_bias),
            full(norm_g),
        ],
        out_specs=pl.BlockSpec((tt, GROUP_W), lambda b, i: (b * nt + i, 0)),
        out_shape=jax.ShapeDtypeStruct((t, GROUP_W), BF16),
        scratch_shapes=[pltpu.VMEM((tt + 8, 3 * GROUP_W), F32), pltpu.VMEM((GDN_HEADS, GDN_HD, GDN_HD), F32)],
        compiler_params=_cparams("parallel", "arbitrary"),
        name="gdn",
    )(u_gdn, tail, conv_w, a_log, dt_bias, norm_g)


def _rwkv_kernel(u_ref, mu_ref, w0_ref, w2_ref, a0_ref, a2_ref, g2_ref, kk_ref, ka_ref, rk_ref, lng_ref, lnb_ref,
                 e_ref, o_ref, xs_ref, st_ref, *, tt, c):
    i = pl.program_id(1)
    gw, hd = GROUP_W, RWKV_HD

    @pl.when(i == 0)
    def _():
        xs_ref[0:8, :] = jnp.zeros((8, xs_ref.shape[1]), F32)
        st_ref[...] = jnp.zeros(st_ref.shape, F32)

    u = u_ref[...].astype(F32)
    xs_ref[8 : 8 + tt, :] = u
    prev = xs_ref[7 : 7 + tt, :]
    xs_ref[0:8, :] = xs_ref[tt : tt + 8, :]
    x = u + mu_ref[...] * (prev - u)
    rr = x[:, :gw]
    k = x[:, gw : 2 * gw]
    v = x[:, 2 * gw : 3 * gw]
    wa = x[:, 3 * gw : 3 * gw + LANE]
    glo = x[:, 3 * gw + LANE :]
    w = -_softplus(-(w0_ref[...] + _bdot(jnp.tanh(wa), w2_ref[...]))) - 0.5
    ld = -jnp.exp(w)
    gate_a = _sigmoid(a0_ref[...] + _bdot(wa, a2_ref[...]))
    gate_g = _bdot(_sigmoid(glo), g2_ref[...])
    e = e_ref[...]

    def segsum(y):
        hi = y.astype(BF16)
        lo = (y - hi.astype(F32)).astype(BF16)
        return jnp.dot(hi, e, preferred_element_type=F32) + jnp.dot(lo, e, preferred_element_type=F32)

    kkr = k * kk_ref[...]
    kk = kkr * lax.rsqrt(segsum(kkr * kkr) + 1e-6)
    k2 = k * (1.0 + (gate_a - 1.0) * ka_ref[...])
    bonus = segsum(rr * k2 * rk_ref[...]) * v

    r, col, same, lower, strict = _chunk_masks(tt, c)
    tri = jnp.where(lower, 1.0, 0.0).astype(BF16)
    ones_blk = jnp.where(same, 1.0, 0.0).astype(BF16)
    cum = _dot_exact_lhs(tri, ld)
    ctot = _dot_exact_lhs(ones_blk, ld)
    encum = jnp.exp(-cum)
    edec = jnp.exp(ctot - cum)
    kka = kk * gate_a
    rt = rr * jnp.exp(cum)
    at = -kk * jnp.exp(cum - ld)
    bt = kka * encum
    kt = k2 * encum
    bd = kka * edec
    kd = k2 * edec
    pc = jnp.exp(ctot)

    heads = range(RWKV_HEADS)
    bf = lambda a: a.astype(BF16)
    dot = lambda a, b: jnp.dot(a, b, preferred_element_type=F32)
    dot_nt = lambda a, b: lax.dot_general(a, b, (((1,), (1,)), ((), ())), preferred_element_type=F32)
    dot_tn = lambda a, b: lax.dot_general(a, b, (((0,), (0,)), ((), ())), preferred_element_type=F32)
    at_b, rt_b, bt_b, kt_b, v_b, bd_b, kd_b = bf(at), bf(rt), bf(bt), bf(kt), bf(v), bf(bd), bf(kd)
    sls = [slice(h * hd, (h + 1) * hd) for h in heads]
    strict_f = jnp.where(strict, 1.0, 0.0).astype(F32)
    lower_f = jnp.where(lower, 1.0, 0.0).astype(F32)
    mabs = [dot_nt(at_b[:, sl], bt_b[:, sl]) * strict_f for sl in sls]
    maks = [bf(dot_nt(at_b[:, sl], kt_b[:, sl]) * strict_f) for sl in sls]
    arbs = [bf(dot_nt(rt_b[:, sl], bt_b[:, sl]) * lower_f) for sl in sls]
    arks = [bf(dot_nt(rt_b[:, sl], kt_b[:, sl]) * lower_f) for sl in sls]
    tinvs = [bf(t) for t in _tri_inv(mabs, _tri_inv_masks(r, col, c))]
    wmats = [bf(dot(t, at_b[:, sl])) for t, sl in zip(tinvs, sls)]
    mkvs = [bf(dot(m, v_b[:, sl])) for m, sl in zip(maks, sls)]
    umats = [dot(t, mkv) for t, mkv in zip(tinvs, mkvs)]
    yconsts = [dot(m, v_b[:, sl]) for m, sl in zip(arks, sls)]
    states = [st_ref[h] for h in heads]
    outs = [[] for _ in heads]
    for cc in range(tt // c):
        rs = slice(cc * c, (cc + 1) * c)
        sbs = [bf(s) for s in states]
        sas = [dot_nt(wmats[h][rs], sbs[h]) + umats[h][rs] for h in heads]
        for h in heads:
            sab = bf(sas[h])
            outs[h].append(dot_nt(rt_b[rs, sls[h]], sbs[h]) + dot(arbs[h][rs, rs], sab) + yconsts[h][rs])
            states[h] = (states[h] * pc[cc * c : cc * c + 1, sls[h]] + dot_tn(sab, bd_b[rs, sls[h]])
                         + dot_tn(v_b[rs, sls[h]], kd_b[rs, sls[h]]))
    for h in heads:
        st_ref[h] = states[h]
    y = jnp.concatenate([jnp.concatenate(o, axis=0) for o in outs], axis=-1)
    mean = segsum(y) * (1.0 / hd)
    d = y - mean
    var = segsum(d * d) * (1.0 / hd)
    yn = d * lax.rsqrt(var + RWKV_GN_EPS) * lng_ref[...] + lnb_ref[...]
    o_ref[...] = ((yn + bonus) * gate_g).astype(o_ref.dtype)


def _rwkv(u_rwkv, mu, w0, w2p, a0, a2p, g2, k_k, k_a, r_k, ln_g, ln_b, seg_ones, *, batch, seq):
    tt = min(SCAN_TILE, seq)
    nt = seq // tt
    t = batch * seq
    ncol = u_rwkv.shape[1]
    full = lambda a: pl.BlockSpec(a.shape, lambda b, i: (0,) * a.ndim)
    params = (mu, w0, w2p, a0, a2p, g2, k_k, k_a, r_k, ln_g, ln_b, seg_ones)
    return pl.pallas_call(
        functools.partial(_rwkv_kernel, tt=tt, c=CHUNK),
        grid=(batch, nt),
        in_specs=[pl.BlockSpec((tt, ncol), lambda b, i: (b * nt + i, 0))] + [full(p) for p in params],
        out_specs=pl.BlockSpec((tt, GROUP_W), lambda b, i: (b * nt + i, 0)),
        out_shape=jax.ShapeDtypeStruct((t, GROUP_W), BF16),
        scratch_shapes=[pltpu.VMEM((tt + 8, ncol), F32), pltpu.VMEM((RWKV_HEADS, RWKV_HD, RWKV_HD), F32)],
        compiler_params=_cparams("parallel", "arbitrary"),
        name="rwkv7",
    )(u_rwkv, *params)


def _layer_norm(h, g, b):
    mu = jnp.mean(h, axis=-1, keepdims=True)
    d = h - mu
    var = jnp.mean(d * d, axis=-1, keepdims=True)
    return d * lax.rsqrt(var + LN_EPS) * g + b


def _route_rows(lg):
    neg = -1e30
    lane = lax.broadcasted_iota(jnp.int32, lg.shape, 1)
    lane_f = lane.astype(F32)
    first = lambda hit: jnp.min(jnp.where(hit, lane_f, float(LANE)), axis=-1, keepdims=True)
    is_g = lane < N_GROUPS
    gl = jnp.where(is_g, lg, neg)
    gmax = jnp.max(gl, axis=-1, keepdims=True)
    p_grp = 1.0 / jnp.sum(jnp.where(is_g, jnp.exp(gl - gmax), 0.0), axis=-1, keepdims=True)
    grp = first(gl == gmax)
    lo = N_GROUPS + grp * EXP_PER_GROUP
    el = jnp.where(jnp.logical_and(lane_f >= lo, lane_f < lo + EXP_PER_GROUP), lg, neg)
    v1 = jnp.max(el, axis=-1, keepdims=True)
    i1 = first(el == v1)
    el2 = jnp.where(lane_f == i1, neg, el)
    v2 = jnp.max(el2, axis=-1, keepdims=True)
    i2 = first(el2 == v2)
    r = jnp.exp(v2 - v1)
    g1 = p_grp / (1.0 + r)
    out = jnp.where(lane == 0, i1 - N_GROUPS, 0.0)
    out = jnp.where(lane == 1, i2 - N_GROUPS, out)
    out = jnp.where(lane == 2, g1, out)
    return jnp.where(lane == 3, g1 * r, out)


def _outproj_kernel(yf_ref, ym_ref, yr_ref, yg_ref, w_ref, x_ref, g_ref, b_ref, wrh_ref, wrl_ref, rb_ref, tril_ref,
                    xo_ref, xb_ref, lg_ref, cnt_ref, *, alpha):
    gw = GROUP_W
    acc = jnp.dot(yf_ref[...], w_ref[0:gw, :], preferred_element_type=F32)
    acc = acc + jnp.dot(ym_ref[...], w_ref[gw : 2 * gw, :], preferred_element_type=F32)
    acc = acc + jnp.dot(yr_ref[...], w_ref[2 * gw : 3 * gw, :], preferred_element_type=F32)
    acc = acc + jnp.dot(yg_ref[...], w_ref[3 * gw : 4 * gw, :], preferred_element_type=F32)
    xn = _layer_norm(alpha * x_ref[...] + acc, g_ref[...], b_ref[...])
    xo_ref[...] = xn
    xh = xn.astype(BF16)
    xb_ref[...] = xh
    xl = (xn - xh.astype(F32)).astype(BF16)
    wrh = wrh_ref[...]
    d = lambda a, b: jnp.dot(a, b, preferred_element_type=F32)
    routed = _route_rows(d(xh, wrh) + d(xl, wrh) + d(xh, wrl_ref[...]) + rb_ref[...])

    @pl.when(pl.program_id(0) == 0)
    def _():
        cnt_ref[...] = jnp.zeros(cnt_ref.shape, F32)

    lane = lax.broadcasted_iota(jnp.int32, routed.shape, 1).astype(F32)
    oh = [jnp.where(lane == routed[:, k : k + 1], 1.0, 0.0) for k in range(TOP_K)]
    both = oh[0] + oh[1]
    before = jnp.dot(tril_ref[...], both.astype(BF16), preferred_element_type=F32) + cnt_ref[0:1, :]
    for k in range(TOP_K):
        rank = jnp.sum(before * oh[k], axis=-1, keepdims=True)
        routed = jnp.where(lane == 2 * TOP_K + k, rank, routed)
    cnt_ref[...] = cnt_ref[...] + jnp.sum(both, axis=0, keepdims=True)
    lg_ref[...] = routed


def _outproj(ys, w_out, x, ln_g, ln_b, wr_hi, wr_lo, r_bias, alpha):
    t, d = x.shape
    tm = min(ROW_TILE, t)
    row = lambda n: pl.BlockSpec((tm, n), lambda i: (i, 0))
    full = lambda a: pl.BlockSpec(a.shape, lambda i: (0,) * a.ndim)
    idx = jnp.arange(tm)
    tril = (idx[None, :] < idx[:, None]).astype(BF16)
    return pl.pallas_call(
        functools.partial(_outproj_kernel, alpha=alpha),
        grid=(t // tm,),
        in_specs=[row(GROUP_W)] * 4
        + [full(w_out), row(d), full(ln_g), full(ln_b), full(wr_hi), full(wr_lo), full(r_bias), full(tril)],
        out_specs=[row(d), row(d), row(LANE), pl.BlockSpec((8, LANE), lambda i: (0, 0))],
        out_shape=[
            jax.ShapeDtypeStruct((t, d), F32),
            jax.ShapeDtypeStruct((t, d), BF16),
            jax.ShapeDtypeStruct((t, LANE), F32),
            jax.ShapeDtypeStruct((8, LANE), F32),
        ],
        compiler_params=_cparams("arbitrary"),
        name="outproj_ln_router",
    )(*ys, w_out, x, ln_g, ln_b, wr_hi, wr_lo, r_bias, tril)


def _expert_kernel(be_ref, x_ref, wg_ref, wu_ref, wd_ref, o_ref):
    del be_ref
    x = x_ref[...]
    a = jnp.dot(x, wg_ref[...], preferred_element_type=F32)
    b = jnp.dot(x, wu_ref[...], preferred_element_type=F32)
    hmid = (a * _sigmoid(a) * b).astype(BF16)
    o_ref[...] = jnp.dot(hmid, wd_ref[...], preferred_element_type=F32).astype(o_ref.dtype)


def _experts(block_weight, xs, w_gate, w_up, w_down):
    n_slots, d = xs.shape
    tb = MOE_TILE
    de = w_gate.shape[-1]
    grid_spec = pltpu.PrefetchScalarGridSpec(
        num_scalar_prefetch=1,
        grid=(n_slots // tb,),
        in_specs=[
            pl.BlockSpec((tb, d), lambda i, be: (i, 0)),
            pl.BlockSpec((None, d, de), lambda i, be: (be[i], 0, 0)),
            pl.BlockSpec((None, d, de), lambda i, be: (be[i], 0, 0)),
            pl.BlockSpec((None, de, d), lambda i, be: (be[i], 0, 0)),
        ],
        out_specs=pl.BlockSpec((tb, d), lambda i, be: (i, 0)),
    )
    return pl.pallas_call(
        _expert_kernel,
        grid_spec=grid_spec,
        out_shape=jax.ShapeDtypeStruct((n_slots, d), BF16),
        compiler_params=_cparams("arbitrary"),
        name="experts",
    )(block_weight, xs, w_gate, w_up, w_down)


def _ln2_kernel(x_ref, y1_ref, y2_ref, r_ref, g_ref, b_ref, xo_ref, xb_ref, *, alpha):
    routed = r_ref[...]
    moe = routed[:, TOP_K : TOP_K + 1] * y1_ref[...].astype(F32) + routed[:, TOP_K + 1 : TOP_K + 2] * y2_ref[...].astype(F32)
    xn = _layer_norm(alpha * x_ref[...] + moe, g_ref[...], b_ref[...])
    xo_ref[...] = xn
    xb_ref[...] = xn.astype(BF16)


def _ln2(x, y1, y2, routed, g, b, alpha):
    t, d = x.shape
    tm = min(ROW_TILE, t)
    row = lambda n: pl.BlockSpec((tm, n), lambda i: (i, 0))
    full = lambda a: pl.BlockSpec(a.shape, lambda i: (0,) * a.ndim)
    return pl.pallas_call(
        functools.partial(_ln2_kernel, alpha=alpha),
        grid=(t // tm,),
        in_specs=[row(d), row(d), row(d), row(LANE), full(g), full(b)],
        out_specs=[row(d), row(d)],
        out_shape=[jax.ShapeDtypeStruct((t, d), F32), jax.ShapeDtypeStruct((t, d), BF16)],
        compiler_params=_cparams("parallel"),
        name="residual_ln2",
    )(x, y1, y2, routed, g, b)


def _pad_cols(w, n):
    return jnp.pad(w, [(0, 0)] * (w.ndim - 1) + [(0, n - w.shape[-1])])


def _rot_half_cols(w):
    half = w.shape[-1] // 2
    return jnp.concatenate([-w[..., half:], w[..., :half]], axis=-1)


def _prep_weights(p):
    w_in = p["w_in"]
    c_fox = 3 * GROUP_W + FOX_HEADS
    c_mla = MLA_Q_RANK + MLA_KV_RANK + MLA_ROPE
    c_rwkv = 3 * GROUP_W + 2 * 64 + 128
    o_mla = c_fox
    o_rwkv = o_mla + c_mla
    o_gdn = o_rwkv + c_rwkv
    row = lambda a: a[:, None, :].astype(F32)

    w_fox = jnp.concatenate(
        [w_in[..., :GROUP_W] * (FOX_HD**-0.5 * LOG2E), w_in[..., GROUP_W : 3 * GROUP_W], _pad_cols(w_in[..., 3 * GROUP_W : c_fox], LANE)],
        axis=-1,
    )
    kpe_w = w_in[..., o_mla + MLA_Q_RANK + MLA_KV_RANK : o_mla + c_mla]
    w_mla = jnp.concatenate(
        [w_in[..., o_mla : o_mla + MLA_Q_RANK + MLA_KV_RANK], _pad_cols(kpe_w, LANE), _pad_cols(_rot_half_cols(kpe_w), LANE)],
        axis=-1,
    )
    w_rwkv = w_in[..., o_rwkv:o_gdn]
    g0 = o_gdn + 4 * GROUP_W
    w_gdn = jnp.concatenate(
        [w_in[..., o_gdn:g0], _pad_cols(w_in[..., g0 : g0 + GDN_HEADS], LANE), _pad_cols(w_in[..., g0 + GDN_HEADS :], LANE)],
        axis=-1,
    )

    nl = w_in.shape[0]
    scale = (MLA_NOPE + MLA_ROPE) ** -0.5 * LOG2E
    wq = p["mla_w_uq"].reshape(nl, MLA_Q_RANK, MLA_HEADS, MLA_NOPE + MLA_ROPE) * scale
    wq_nope = wq[..., :MLA_NOPE].reshape(nl, MLA_Q_RANK, -1)
    wq_pe = wq[..., MLA_NOPE:]
    wq_p = jnp.concatenate(
        [wq_nope, _pad_cols(wq_pe, LANE).reshape(nl, MLA_Q_RANK, -1), _pad_cols(_rot_half_cols(wq_pe), LANE).reshape(nl, MLA_Q_RANK, -1)],
        axis=-1,
    )
    wkv = p["mla_w_ukv"].reshape(nl, MLA_KV_RANK, MLA_HEADS, MLA_NOPE + MLA_VD)
    wkv_p = jnp.concatenate([wkv[..., :MLA_NOPE].reshape(nl, MLA_KV_RANK, -1), wkv[..., MLA_NOPE:].reshape(nl, MLA_KV_RANK, -1)], axis=-1)

    zeros64 = jnp.zeros((nl, 64, GROUP_W), F32)
    w_router = _pad_cols(jnp.concatenate([p["moe_w_grp"], p["moe_w_exp"]], axis=-1), LANE)
    wr_hi = w_router.astype(BF16)
    wr_lo = (w_router - wr_hi.astype(F32)).astype(BF16)
    return dict(
        w_fox=w_fox.astype(BF16), w_mla=w_mla.astype(BF16), w_rwkv=w_rwkv.astype(BF16), w_gdn=w_gdn.astype(BF16),
        layer=jnp.arange(nl, dtype=jnp.int32),
        fox_b_f=row(_pad_cols(p["fox_b_f"], LANE)), fox_out_g=row(p["fox_out_g"]),
        mla_qg=row(p["mla_q_norm_g"]), mla_kvg=row(p["mla_kv_norm_g"]), mla_wq=wq_p.astype(BF16), mla_wkv=wkv_p.astype(BF16),
        mla_out_g=row(p["mla_out_g"]),
        rwkv_mu=row(p["rwkv_mu"]), rwkv_w0=row(p["rwkv_w0"]),
        rwkv_w2=jnp.concatenate([p["rwkv_w2"], zeros64], axis=1).astype(BF16),
        rwkv_a0=row(p["rwkv_a0"]), rwkv_a2=jnp.concatenate([zeros64, p["rwkv_a2"]], axis=1).astype(BF16),
        rwkv_g2=p["rwkv_g2"].astype(BF16), rwkv_k_k=row(p["rwkv_k_k"]), rwkv_k_a=row(p["rwkv_k_a"]),
        rwkv_r_k=row(p["rwkv_r_k"]), rwkv_ln_g=row(p["rwkv_ln_g"]), rwkv_ln_b=row(p["rwkv_ln_b"]),
        gdn_conv_w=p["gdn_conv_w"].astype(F32), gdn_a_log=row(_pad_cols(p["gdn_a_log"], LANE)),
        gdn_dt_bias=row(_pad_cols(p["gdn_dt_bias"], LANE)), gdn_norm_g=row(p["gdn_norm_g"]),
        w_out=p["w_out"].astype(BF16), ln1_g=row(p["ln1_g"]), ln1_b=row(p["ln1_b"]),
        wr_hi=wr_hi, wr_lo=wr_lo, r_bias=row(_pad_cols(jnp.concatenate([p["moe_b_grp"], p["moe_b_exp"]], axis=-1), LANE)),
        ln2_g=row(p["ln2_g"]), ln2_b=row(p["ln2_b"]),
    )


def _route(routed, counts, tb):
    t = routed.shape[0]
    a = t * TOP_K
    n_blocks = (a + N_EXPERTS * (tb - 1) + tb - 1) // tb
    n_slots = n_blocks * tb
    expert = routed[:, :TOP_K].astype(jnp.int32)
    rank = routed[:, 2 * TOP_K : 3 * TOP_K].astype(jnp.int32)
    padded = (counts + tb - 1) // tb * tb
    start = jnp.cumsum(counts) - counts
    pend = jnp.cumsum(padded)
    pstart = pend - padded
    slot_of_assignment = pstart[expert] + rank
    block_expert = jnp.minimum(jnp.searchsorted(pend, jnp.arange(n_blocks) * tb, side="right"), N_EXPERTS - 1).astype(jnp.int32)
    order = jnp.argsort(expert.reshape(a), stable=True)
    slot = jnp.arange(n_slots)
    e_slot = jnp.repeat(block_expert, tb)
    off = slot - pstart[e_slot]
    src = jnp.clip(start[e_slot] + off, 0, a - 1)
    token_of_slot = jnp.where(off < counts[e_slot], order[src] // TOP_K, t).astype(jnp.int32)
    return token_of_slot, block_expert, slot_of_assignment


def _layer(x, xb, cs, sn, w, moe_w, seg_ones, *, batch, seq, alpha):
    u_fox, fox_tail = _inproj(xb, w["w_fox"], LANE)
    (u_mla,) = _inproj(xb, w["w_mla"], 0)
    (u_rwkv,) = _inproj(xb, w["w_rwkv"], 0)
    u_gdn, gdn_tail = _inproj(xb, w["w_gdn"], 2 * LANE)

    vt_fox = u_fox[:, 2 * GROUP_W :].reshape(batch, seq, GROUP_W).transpose(0, 2, 1)
    y_fox = _attention(u_fox, 0, u_fox, 1, vt_fox, (fox_tail, w["fox_b_f"]), w["fox_out_g"], batch=batch, seq=seq,
                       heads=FOX_HEADS, dk=FOX_HD, dv=FOX_HD, chunk=1)

    q_mla, k_mla, vt_mla = _mla_prep(u_mla, cs, sn, w["mla_qg"], w["mla_kvg"], w["mla_wq"], w["mla_wkv"],
                                     batch=batch, seq=seq)
    y_mla = _attention(q_mla, 0, k_mla, 0, vt_mla, None, w["mla_out_g"], batch=batch, seq=seq,
                       heads=MLA_HEADS, dk=2 * LANE, dv=MLA_VD, chunk=CHUNK)

    y_rwkv = _rwkv(u_rwkv, w["rwkv_mu"], w["rwkv_w0"], w["rwkv_w2"], w["rwkv_a0"], w["rwkv_a2"], w["rwkv_g2"],
                   w["rwkv_k_k"], w["rwkv_k_a"], w["rwkv_r_k"], w["rwkv_ln_g"], w["rwkv_ln_b"], seg_ones,
                   batch=batch, seq=seq)
    y_gdn = _gdn(u_gdn, gdn_tail, w["gdn_conv_w"], w["gdn_a_log"], w["gdn_dt_bias"], w["gdn_norm_g"],
                 batch=batch, seq=seq)

    x1, x1b, routed, counts = _outproj((y_fox, y_mla, y_rwkv, y_gdn), w["w_out"], x, w["ln1_g"], w["ln1_b"],
                                       w["wr_hi"], w["wr_lo"], w["r_bias"], alpha)

    token_of_slot, block_expert, slot_of_assignment = _route(routed, counts[0, :N_EXPERTS].astype(jnp.int32), MOE_TILE)
    x_pad = jnp.concatenate([x1b, jnp.zeros((1, x1b.shape[1]), BF16)], axis=0)
    y_slots = _experts(block_expert + w["layer"] * N_EXPERTS, x_pad[token_of_slot], *moe_w)
    return _ln2(x1, y_slots[slot_of_assignment[:, 0]], y_slots[slot_of_assignment[:, 1]], routed,
                w["ln2_g"], w["ln2_b"], alpha)


def kernel(x, positions, w_in, fox_b_f, fox_out_g, mla_q_norm_g, mla_kv_norm_g, mla_w_uq, mla_w_ukv, mla_out_g, rwkv_mu, rwkv_w0, rwkv_w2, rwkv_a0, rwkv_a2, rwkv_g2, rwkv_k_k, rwkv_k_a, rwkv_r_k, rwkv_ln_g, rwkv_ln_b, gdn_conv_w, gdn_a_log, gdn_dt_bias, gdn_norm_g, w_out, ln1_g, ln1_b, moe_w_grp, moe_b_grp, moe_w_exp, moe_b_exp, moe_w_gate, moe_w_up, moe_w_down, ln2_g, ln2_b):
    batch, seq, d = x.shape
    depth = w_in.shape[0]
    alpha = (2 * depth) ** 0.25
    params = dict(
        w_in=w_in, fox_b_f=fox_b_f, fox_out_g=fox_out_g, mla_q_norm_g=mla_q_norm_g, mla_kv_norm_g=mla_kv_norm_g,
        mla_w_uq=mla_w_uq, mla_w_ukv=mla_w_ukv, mla_out_g=mla_out_g, rwkv_mu=rwkv_mu, rwkv_w0=rwkv_w0, rwkv_w2=rwkv_w2,
        rwkv_a0=rwkv_a0, rwkv_a2=rwkv_a2, rwkv_g2=rwkv_g2, rwkv_k_k=rwkv_k_k, rwkv_k_a=rwkv_k_a, rwkv_r_k=rwkv_r_k,
        rwkv_ln_g=rwkv_ln_g, rwkv_ln_b=rwkv_ln_b, gdn_conv_w=gdn_conv_w, gdn_a_log=gdn_a_log, gdn_dt_bias=gdn_dt_bias,
        gdn_norm_g=gdn_norm_g, w_out=w_out, ln1_g=ln1_g, ln1_b=ln1_b, moe_w_grp=moe_w_grp, moe_b_grp=moe_b_grp,
        moe_w_exp=moe_w_exp, moe_b_exp=moe_b_exp, moe_w_gate=moe_w_gate, moe_w_up=moe_w_up, moe_w_down=moe_w_down,
        ln2_g=ln2_g, ln2_b=ln2_b,
    )
    weights = _prep_weights(params)

    half = MLA_ROPE // 2
    inv_freq = ROPE_THETA ** (-jnp.arange(half, dtype=F32) / half)
    ang = positions.astype(F32).reshape(batch * seq, 1) * inv_freq
    zpad = jnp.zeros((batch * seq, LANE - MLA_ROPE), F32)
    cs = jnp.concatenate([jnp.cos(ang), jnp.cos(ang), zpad], axis=-1)
    sn = jnp.concatenate([jnp.sin(ang), jnp.sin(ang), zpad], axis=-1)

    xf = x.reshape(batch * seq, d).astype(F32)

    stack = lambda a: a.reshape((-1,) + a.shape[2:]).astype(BF16)
    moe_w = (stack(moe_w_gate), stack(moe_w_up), stack(moe_w_down))
    seg = jnp.arange(GROUP_W) // RWKV_HD
    seg_ones = (seg[:, None] == seg[None, :]).astype(BF16)

    def body(carry, w):
        xc, xcb = carry
        return _layer(xc, xcb, cs, sn, w, moe_w, seg_ones, batch=batch, seq=seq, alpha=alpha), None

    (xf, _), _ = lax.scan(body, (xf, xf.astype(BF16)), weights)
    return xf.reshape(batch, seq, d).astype(x.dtype)
```

```python
import functools
import math

import jax
import jax.numpy as jnp
from jax import lax
from jax.experimental import pallas as pl
from jax.experimental.pallas import tpu as pltpu

F32 = jnp.float32
BF16 = jnp.bfloat16

D_MODEL = 2048
GROUP_W = 512
FOX_HD, FOX_HEADS = 64, 8
MLA_HEADS, MLA_NOPE, MLA_ROPE, MLA_VD = 4, 128, 64, 128
MLA_Q_RANK, MLA_KV_RANK = 384, 128
ROPE_THETA = 10000.0
RWKV_HD, RWKV_HEADS = 64, 8
RWKV_GN_EPS = 64e-5
GDN_HD, GDN_HEADS, GDN_CONV = 128, 4, 4
N_GROUPS, EXP_PER_GROUP, TOP_K, D_EXPERT = 4, 8, 2, 512
N_EXPERTS = N_GROUPS * EXP_PER_GROUP
CHUNK = 64
LN_EPS = 1e-5
RMS_EPS = 1e-6
LOG2E = math.log2(math.e)

LANE = 128
VMEM_LIMIT_BYTES = 56 * 1024 * 1024
ROW_TILE = 512
ATTN_TILE = 256
SCAN_TILE = 256
MOE_TILE = 256


def _cparams(*sem):
    return pltpu.CompilerParams(dimension_semantics=sem, vmem_limit_bytes=VMEM_LIMIT_BYTES)


def _bdot(a, b):
    return jnp.dot(a.astype(BF16), b.astype(BF16), preferred_element_type=F32)


def _bdot_nt(a, b):
    return lax.dot_general(a.astype(BF16), b.astype(BF16), (((1,), (1,)), ((), ())), preferred_element_type=F32)


def _bdot_tn(a, b):
    return lax.dot_general(a.astype(BF16), b.astype(BF16), (((0,), (0,)), ((), ())), preferred_element_type=F32)


def _split3(x):
    hi = x.astype(BF16)
    r1 = x - hi.astype(F32)
    mid = r1.astype(BF16)
    lo = (r1 - mid.astype(F32)).astype(BF16)
    return hi, mid, lo


def _dot_exact_lhs(m, x):
    hi, mid, lo = _split3(x)
    d = lambda p: jnp.dot(m, p, preferred_element_type=F32)
    return d(hi) + d(mid) + d(lo)


def _sigmoid(x):
    return 1.0 / (1.0 + jnp.exp(-x))


def _softplus(x):
    return jnp.maximum(x, 0.0) + jnp.log(1.0 + jnp.exp(-jnp.abs(x)))


def _chunk_masks(n, c):
    r = lax.broadcasted_iota(jnp.int32, (n, n), 0)
    col = lax.broadcasted_iota(jnp.int32, (n, n), 1)
    same = (r // c) == (col // c)
    lower = jnp.logical_and(same, col <= r)
    strict = jnp.logical_and(same, col < r)
    return r, col, same, lower, strict


def _tri_inv_masks(r, col, c):
    blk = lambda s: (r // s) == (col // s)
    one = lambda cond: jnp.where(cond, 1.0, 0.0).astype(F32)
    offs = []
    s = 8
    while s < c:
        offs.append(one(jnp.logical_and(blk(2 * s), jnp.logical_not(blk(s)))))
        s *= 2
    return one(r == col), one(blk(8)), offs


def _tri_inv(ms, masks):
    eye, blk8, offs = masks
    bf = lambda a: a.astype(BF16)
    dot = lambda a, b: jnp.dot(a, b, preferred_element_type=F32)
    mdf = [m * blk8 for m in ms]
    mds = [bf(m) for m in mdf]
    xs = [eye + m for m in mdf]
    m2s = [bf(dot(md, md)) for md in mds]
    xs = [x + dot(m2, bf(x)) for x, m2 in zip(xs, m2s)]
    m4s = [bf(dot(m2, m2)) for m2 in m2s]
    xs = [x + dot(m4, bf(x)) for x, m4 in zip(xs, m4s)]
    for off in offs:
        xbs = [bf(x) for x in xs]
        ts = [bf(dot(xb, bf(m * off))) for xb, m in zip(xbs, ms)]
        xs = [x + dot(t, xb) for x, t, xb in zip(xs, ts, xbs)]
    return xs


def _inproj_kernel(x_ref, w_ref, o_ref, *tail_refs, tail):
    acc = jnp.dot(x_ref[...], w_ref[...], preferred_element_type=F32)
    n = acc.shape[1]
    if tail:
        o_ref[...] = acc[:, : n - tail].astype(o_ref.dtype)
        tail_refs[0][...] = acc[:, n - tail :]
    else:
        o_ref[...] = acc.astype(o_ref.dtype)


def _inproj(xb, w, tail):
    t, d = xb.shape
    n = w.shape[1]
    tm = min(ROW_TILE, t)
    out_shape = [jax.ShapeDtypeStruct((t, n - tail), BF16)]
    out_specs = [pl.BlockSpec((tm, n - tail), lambda i: (i, 0))]
    if tail:
        out_shape.append(jax.ShapeDtypeStruct((t, tail), F32))
        out_specs.append(pl.BlockSpec((tm, tail), lambda i: (i, 0)))
    return pl.pallas_call(
        functools.partial(_inproj_kernel, tail=tail),
        grid=(t // tm,),
        in_specs=[pl.BlockSpec((tm, d), lambda i: (i, 0)), pl.BlockSpec((d, n), lambda i: (0, 0))],
        out_specs=out_specs,
        out_shape=out_shape,
        compiler_params=_cparams("parallel"),
        name="inproj",
    )(xb, w)


def _attn_kernel(*refs, heads, dk, dv, chunk, tq, seq, use_bias):
    if use_bias:
        q_ref, k_ref, vt_ref, gate_ref, gb_ref, g_ref, o_ref, brep_ref, m_ref, l_ref, acc_ref = refs
    else:
        q_ref, k_ref, vt_ref, g_ref, o_ref, m_ref, l_ref, acc_ref = refs
    paired = dk < LANE
    dkp = LANE if paired else dk
    i = pl.program_id(1)

    if use_bias:
        @pl.when(i == 0)
        def _():
            r = lax.broadcasted_iota(jnp.int32, (tq, tq), 0)
            c = lax.broadcasted_iota(jnp.int32, (tq, tq), 1)
            tri = jnp.where(c <= r, 1.0, 0.0).astype(BF16)
            carry = jnp.zeros((1, LANE), F32)
            for j in range(seq // tq):
                rows = slice(j * tq, (j + 1) * tq)
                log_f = -_softplus(-(gate_ref[rows, :] + gb_ref[...])) * LOG2E
                cum = _dot_exact_lhs(tri, log_f) + carry
                carry = cum[tq - 1 : tq, :]
                for h in range(heads):
                    brep_ref[h, rows, :] = jnp.broadcast_to(cum[:, h : h + 1], (tq, LANE))

    m_ref[...] = jnp.full(m_ref.shape, -1e30, F32)
    l_ref[...] = jnp.zeros(l_ref.shape, F32)
    acc_ref[...] = jnp.zeros(acc_ref.shape, F32)

    qs = []
    for h in range(heads):
        if paired:
            slab = q_ref[:, (h // 2) * LANE : (h // 2 + 1) * LANE]
            lane_half = lax.broadcasted_iota(jnp.int32, (tq, LANE), 1) // dk
            qs.append(jnp.where(lane_half == h % 2, slab, jnp.zeros_like(slab)))
        else:
            qs.append(q_ref[:, h * dkp : (h + 1) * dkp])

    def step(off, masked):
        if masked:
            kr = lax.broadcasted_iota(jnp.int32, (tq, tq), 0)
            qc = lax.broadcasted_iota(jnp.int32, (tq, tq), 1)
            allowed = (kr // chunk) <= (qc // chunk)
        sts = []
        for h in range(heads):
            slab = h // 2 if paired else h
            k = k_ref[pl.ds(off, tq), slab * dkp : (slab + 1) * dkp]
            sts.append(lax.dot_general(k, qs[h], (((1,), (1,)), ((), ())), preferred_element_type=F32))
        ps, alphas = [], []
        for h in range(heads):
            st = sts[h]
            if use_bias:
                st = st - jnp.concatenate([brep_ref[h, pl.ds(off, tq), :]] * (tq // LANE), axis=1)
            if masked:
                st = jnp.where(allowed, st, -1e30)
            m_old = m_ref[h, 0:1, :]
            m_new = jnp.maximum(m_old, jnp.max(st, axis=0, keepdims=True))
            p = jnp.exp2(st - m_new)
            alpha = jnp.exp2(m_old - m_new)
            l_ref[h, 0:1, :] = alpha * l_ref[h, 0:1, :] + jnp.sum(p, axis=0, keepdims=True)
            m_ref[h, 0:1, :] = m_new
            ps.append(p.astype(BF16))
            alphas.append(alpha)
        for h in range(heads):
            rows = slice(h * dv, (h + 1) * dv)
            pv = jnp.dot(vt_ref[rows, pl.ds(off, tq)], ps[h], preferred_element_type=F32)
            acc_ref[rows, :] = alphas[h] * acc_ref[rows, :] + pv

    def body(j, carry):
        step(pl.multiple_of(j * tq, tq), False)
        return carry

    lax.fori_loop(0, i, body, 0)
    step(pl.multiple_of(i * tq, tq), True)

    ot = jnp.concatenate([acc_ref[h * dv : (h + 1) * dv, :] / l_ref[h, 0:1, :] for h in range(heads)], axis=0)
    ot = ot * lax.rsqrt(jnp.mean(ot * ot, axis=0, keepdims=True) + RMS_EPS)
    o_ref[...] = (ot.T * g_ref[...]).astype(o_ref.dtype)


def _attention(q_arr, q_col, k_arr, k_col, vt, gate, gain, *, batch, seq, heads, dk, dv, chunk):
    bias = gate
    assert dk % LANE == 0 or (2 * dk == LANE and heads % 2 == 0)
    tq = min(ATTN_TILE, seq)
    nq = seq // tq
    t = batch * seq
    in_specs = [
        pl.BlockSpec((tq, heads * dk), lambda b, i: (b * nq + i, q_col)),
        pl.BlockSpec((seq, heads * dk), lambda b, i: (b, k_col)),
        pl.BlockSpec((None, heads * dv, seq), lambda b, i: (b, 0, 0)),
    ]
    args = [q_arr, k_arr, vt]
    scratch = []
    if bias is not None:
        in_specs += [pl.BlockSpec((seq, LANE), lambda b, i: (b, 0)), pl.BlockSpec((1, LANE), lambda b, i: (0, 0))]
        args += list(gate)
        scratch.append(pltpu.VMEM((heads, seq, LANE), F32))
    in_specs.append(pl.BlockSpec((1, heads * dv), lambda b, i: (0, 0)))
    args.append(gain)
    stat = pltpu.VMEM((heads, 8, tq), F32)
    scratch += [stat, stat, pltpu.VMEM((heads * dv, tq), F32)]
    return pl.pallas_call(
        functools.partial(_attn_kernel, heads=heads, dk=dk, dv=dv, chunk=chunk, tq=tq, seq=seq, use_bias=bias is not None),
        grid=(batch, nq),
        in_specs=in_specs,
        out_specs=pl.BlockSpec((tq, heads * dv), lambda b, i: (b * nq + i, 0)),
        out_shape=jax.ShapeDtypeStruct((t, heads * dv), BF16),
        scratch_shapes=scratch,
        compiler_params=_cparams("parallel", "arbitrary"),
        name="attention",
    )(*args)


def _mla_prep_kernel(u_ref, cs_ref, sn_ref, qg_ref, kvg_ref, wq_ref, wkv_ref, q_ref, k_ref, v_ref):
    u = u_ref[...].astype(F32)
    cs = cs_ref[...]
    sn = sn_ref[...]

    def rms(x, g):
        return x * lax.rsqrt(jnp.mean(x * x, axis=-1, keepdims=True) + RMS_EPS) * g

    qo = _bdot(rms(u[:, :MLA_Q_RANK], qg_ref[...]), wq_ref[...])
    kvo = _bdot(rms(u[:, MLA_Q_RANK : MLA_Q_RANK + MLA_KV_RANK], kvg_ref[...]), wkv_ref[...])
    c0 = MLA_Q_RANK + MLA_KV_RANK
    kpe = (u[:, c0 : c0 + LANE] * cs + u[:, c0 + LANE : c0 + 2 * LANE] * sn).astype(k_ref.dtype)
    nn = MLA_HEADS * MLA_NOPE
    for h in range(MLA_HEADS):
        a = h * 2 * LANE
        q_ref[:, a : a + LANE] = qo[:, h * LANE : (h + 1) * LANE].astype(q_ref.dtype)
        qpe = qo[:, nn + h * LANE : nn + (h + 1) * LANE] * cs + qo[:, 2 * nn + h * LANE : 2 * nn + (h + 1) * LANE] * sn
        q_ref[:, a + LANE : a + 2 * LANE] = qpe.astype(q_ref.dtype)
        k_ref[:, a : a + LANE] = kvo[:, h * LANE : (h + 1) * LANE].astype(k_ref.dtype)
        k_ref[:, a + LANE : a + 2 * LANE] = kpe
    v_ref[...] = kvo[:, nn:].T.astype(v_ref.dtype)


def _mla_prep(u_mla, cs, sn, qg, kvg, wq, wkv, *, batch, seq):
    t = u_mla.shape[0]
    tm = min(ROW_TILE, seq)
    nt = seq // tm
    row = lambda n: pl.BlockSpec((tm, n), lambda i: (i, 0))
    full = lambda a: pl.BlockSpec(a.shape, lambda i: (0,) * a.ndim)
    wide = MLA_HEADS * 2 * LANE
    dvs = MLA_HEADS * MLA_VD
    return pl.pallas_call(
        _mla_prep_kernel,
        grid=(t // tm,),
        in_specs=[row(u_mla.shape[1]), row(LANE), row(LANE), full(qg), full(kvg), full(wq), full(wkv)],
        out_specs=[row(wide), row(wide), pl.BlockSpec((None, dvs, tm), lambda i: (i // nt, 0, i % nt))],
        out_shape=[
            jax.ShapeDtypeStruct((t, wide), BF16),
            jax.ShapeDtypeStruct((t, wide), BF16),
            jax.ShapeDtypeStruct((batch, dvs, seq), BF16),
        ],
        compiler_params=_cparams("parallel"),
        name="mla_prep",
    )(u_mla, cs, sn, qg, kvg, wq, wkv)


def _gdn_kernel(u_ref, t_ref, cw_ref, alog_ref, dtb_ref, ng_ref, o_ref, xs_ref, st_ref, *, tt, c):
    i = pl.program_id(1)
    gw, hd = GROUP_W, GDN_HD

    @pl.when(i == 0)
    def _():
        xs_ref[0:8, :] = jnp.zeros((8, 3 * gw), F32)
        st_ref[...] = jnp.zeros(st_ref.shape, F32)

    xs_ref[8 : 8 + tt, :] = u_ref[:, : 3 * gw].astype(F32)
    cw = cw_ref[...]
    conv = cw[0:1, :] * xs_ref[5 : 5 + tt, :]
    for j in range(1, GDN_CONV):
        conv = conv + cw[j : j + 1, :] * xs_ref[5 + j : 5 + j + tt, :]
    xs_ref[0:8, :] = xs_ref[tt : tt + 8, :]
    qkv = conv * _sigmoid(conv)

    tail = t_ref[...]
    beta = _sigmoid(tail[:, :LANE])
    g = -jnp.exp(alog_ref[...]) * _softplus(tail[:, LANE:] + dtb_ref[...])

    r, col, same, lower, strict = _chunk_masks(tt, c)
    tri = jnp.where(lower, 1.0, 0.0).astype(BF16)
    ones_blk = jnp.where(same, 1.0, 0.0).astype(BF16)
    gc = _dot_exact_lhs(tri, g)
    gtot = _dot_exact_lhs(ones_blk, g)
    gct = gc.T

    def l2n(x):
        return x * lax.rsqrt(jnp.sum(x * x, axis=-1, keepdims=True) + 1e-6)

    heads = range(GDN_HEADS)
    bf = lambda a: a.astype(BF16)
    lmats, attns, kbs, vbs, qds, kds, egs = [], [], [], [], [], [], []
    for h in heads:
        qf = l2n(qkv[:, h * hd : (h + 1) * hd]) * (hd**-0.5)
        qh = bf(qf)
        kf = l2n(qkv[:, gw + h * hd : gw + (h + 1) * hd])
        kh = bf(kf)
        vh = qkv[:, 2 * gw + h * hd : 2 * gw + (h + 1) * hd]
        gcol = gc[:, h : h + 1]
        dec = jnp.exp(jnp.minimum(gcol - gct[h : h + 1, :], 0.0))
        bcol = beta[:, h : h + 1]
        kb = kf * bcol
        eg = jnp.exp(gcol)
        lmats.append(jnp.where(strict, _bdot_nt(kb, kh) * dec, 0.0))
        attns.append(bf(jnp.where(lower, _bdot_nt(qh, kh) * dec, 0.0)))
        kbs.append(bf(kb * eg))
        vbs.append(bf(vh * bcol))
        qds.append(bf(qf * eg))
        kds.append(bf(kf * jnp.exp(gtot[:, h : h + 1] - gcol)))
    tinvs = [bf(t) for t in _tri_inv([-m for m in lmats], _tri_inv_masks(r, col, c))]
    uvals = [jnp.dot(t, vb, preferred_element_type=F32) for t, vb in zip(tinvs, vbs)]
    wcums = [bf(jnp.dot(t, kb, preferred_element_type=F32)) for t, kb in zip(tinvs, kbs)]
    states = [st_ref[h] for h in heads]
    outs = [[] for _ in heads]
    for cc in range(tt // c):
        rs = slice(cc * c, (cc + 1) * c)
        sbs = [bf(s) for s in states]
        vnews = [uvals[h][rs] - jnp.dot(wcums[h][rs], sbs[h], preferred_element_type=F32) for h in heads]
        for h in heads:
            outs[h].append(jnp.dot(qds[h][rs], sbs[h], preferred_element_type=F32) + _bdot(attns[h][rs, rs], vnews[h]))
            glast = jnp.exp(gtot[cc * c : cc * c + 1, h : h + 1])
            states[h] = states[h] * glast + _bdot_tn(kds[h][rs], vnews[h])
    for h in heads:
        st_ref[h] = states[h]
        o = jnp.concatenate(outs[h], axis=0)
        o = o * lax.rsqrt(jnp.mean(o * o, axis=-1, keepdims=True) + RMS_EPS) * ng_ref[...]
        z = u_ref[:, 3 * gw + h * hd : 3 * gw + (h + 1) * hd].astype(F32)
        o_ref[:, h * hd : (h + 1) * hd] = (o * (z * _sigmoid(z))).astype(o_ref.dtype)


def _gdn(u_gdn, tail, conv_w, a_log, dt_bias, norm_g, *, batch, seq):
    tt = min(SCAN_TILE, seq)
    nt = seq // tt
    t = batch * seq
    full = lambda a: pl.BlockSpec(a.shape, lambda b, i: (0,) * a.ndim)
    return pl.pallas_call(
        functools.partial(_gdn_kernel, tt=tt, c=CHUNK),
        grid=(batch, nt),
        in_specs=[
            pl.BlockSpec((tt, 4 * GROUP_W), lambda b, i: (b * nt + i, 0)),
            pl.BlockSpec((tt, 2 * LANE), lambda b, i: (b * nt + i, 0)),
            full(conv_w),
            full(a_log),
            full(dt_bias),
            full(norm_g),
        ],
        out_specs=pl.BlockSpec((tt, GROUP_W), lambda b, i: (b * nt + i, 0)),
        out_shape=jax.ShapeDtypeStruct((t, GROUP_W), BF16),
        scratch_shapes=[pltpu.VMEM((tt + 8, 3 * GROUP_W), F32), pltpu.VMEM((GDN_HEADS, GDN_HD, GDN_HD), F32)],
        compiler_params=_cparams("parallel", "arbitrary"),
        name="gdn",
    )(u_gdn, tail, conv_w, a_log, dt_bias, norm_g)


def _rwkv_kernel(u_ref, mu_ref, w0_ref, w2_ref, a0_ref, a2_ref, g2_ref, kk_ref, ka_ref, rk_ref, lng_ref, lnb_ref,
                 e_ref, o_ref, xs_ref, st_ref, *, tt, c):
    i = pl.program_id(1)
    gw, hd = GROUP_W, RWKV_HD

    @pl.when(i == 0)
    def _():
        xs_ref[0:8, :] = jnp.zeros((8, xs_ref.shape[1]), F32)
        st_ref[...] = jnp.zeros(st_ref.shape, F32)

    u = u_ref[...].astype(F32)
    xs_ref[8 : 8 + tt, :] = u
    prev = xs_ref[7 : 7 + tt, :]
    xs_ref[0:8, :] = xs_ref[tt : tt + 8, :]
    x = u + mu_ref[...] * (prev - u)
    rr = x[:, :gw]
    k = x[:, gw : 2 * gw]
    v = x[:, 2 * gw : 3 * gw]
    wa = x[:, 3 * gw : 3 * gw + LANE]
    glo = x[:, 3 * gw + LANE :]
    w = -_softplus(-(w0_ref[...] + _bdot(jnp.tanh(wa), w2_ref[...]))) - 0.5
    ld = -jnp.exp(w)
    gate_a = _sigmoid(a0_ref[...] + _bdot(wa, a2_ref[...]))
    gate_g = _bdot(_sigmoid(glo), g2_ref[...])
    e = e_ref[...]

    def segsum(y):
        hi = y.astype(BF16)
        lo = (y - hi.astype(F32)).astype(BF16)
        return jnp.dot(hi, e, preferred_element_type=F32) + jnp.dot(lo, e, preferred_element_type=F32)

    kkr = k * kk_ref[...]
    kk = kkr * lax.rsqrt(segsum(kkr * kkr) + 1e-6)
    k2 = k * (1.0 + (gate_a - 1.0) * ka_ref[...])
    bonus = segsum(rr * k2 * rk_ref[...]) * v

    r, col, same, lower, strict = _chunk_masks(tt, c)
    tri = jnp.where(lower, 1.0, 0.0).astype(BF16)
    ones_blk = jnp.where(same, 1.0, 0.0).astype(BF16)
    cum = _dot_exact_lhs(tri, ld)
    ctot = _dot_exact_lhs(ones_blk, ld)
    encum = jnp.exp(-cum)
    edec = jnp.exp(ctot - cum)
    kka = kk * gate_a
    rt = rr * jnp.exp(cum)
    at = -kk * jnp.exp(cum - ld)
    bt = kka * encum
    kt = k2 * encum
    bd = kka * edec
    kd = k2 * edec
    pc = jnp.exp(ctot)

    heads = range(RWKV_HEADS)
    bf = lambda a: a.astype(BF16)
    dot = lambda a, b: jnp.dot(a, b, preferred_element_type=F32)
    dot_nt = lambda a, b: lax.dot_general(a, b, (((1,), (1,)), ((), ())), preferred_element_type=F32)
    dot_tn = lambda a, b: lax.dot_general(a, b, (((0,), (0,)), ((), ())), preferred_element_type=F32)
    at_b, rt_b, bt_b, kt_b, v_b, bd_b, kd_b = bf(at), bf(rt), bf(bt), bf(kt), bf(v), bf(bd), bf(kd)
    sls = [slice(h * hd, (h + 1) * hd) for h in heads]
    strict_f = jnp.where(strict, 1.0, 0.0).astype(F32)
    lower_f = jnp.where(lower, 1.0, 0.0).astype(F32)
    mabs = [dot_nt(at_b[:, sl], bt_b[:, sl]) * strict_f for sl in sls]
    maks = [bf(dot_nt(at_b[:, sl], kt_b[:, sl]) * strict_f) for sl in sls]
    arbs = [bf(dot_nt(rt_b[:, sl], bt_b[:, sl]) * lower_f) for sl in sls]
    arks = [bf(dot_nt(rt_b[:, sl], kt_b[:, sl]) * lower_f) for sl in sls]
    tinvs = [bf(t) for t in _tri_inv(mabs, _tri_inv_masks(r, col, c))]
    wmats = [bf(dot(t, at_b[:, sl])) for t, sl in zip(tinvs, sls)]
    mkvs = [bf(dot(m, v_b[:, sl])) for m, sl in zip(maks, sls)]
    umats = [dot(t, mkv) for t, mkv in zip(tinvs, mkvs)]
    yconsts = [dot(m, v_b[:, sl]) for m, sl in zip(arks, sls)]
    states = [st_ref[h] for h in heads]
    outs = [[] for _ in heads]
    for cc in range(tt // c):
        rs = slice(cc * c, (cc + 1) * c)
        sbs = [bf(s) for s in states]
        sas = [dot_nt(wmats[h][rs], sbs[h]) + umats[h][rs] for h in heads]
        for h in heads:
            sab = bf(sas[h])
            outs[h].append(dot_nt(rt_b[rs, sls[h]], sbs[h]) + dot(arbs[h][rs, rs], sab) + yconsts[h][rs])
            states[h] = (states[h] * pc[cc * c : cc * c + 1, sls[h]] + dot_tn(sab, bd_b[rs, sls[h]])
                         + dot_tn(v_b[rs, sls[h]], kd_b[rs, sls[h]]))
    for h in heads:
        st_ref[h] = states[h]
    y = jnp.concatenate([jnp.concatenate(o, axis=0) for o in outs], axis=-1)
    mean = segsum(y) * (1.0 / hd)
    d = y - mean
    var = segsum(d * d) * (1.0 / hd)
    yn = d * lax.rsqrt(var + RWKV_GN_EPS) * lng_ref[...] + lnb_ref[...]
    o_ref[...] = ((yn + bonus) * gate_g).astype(o_ref.dtype)


def _rwkv(u_rwkv, mu, w0, w2p, a0, a2p, g2, k_k, k_a, r_k, ln_g, ln_b, seg_ones, *, batch, seq):
    tt = min(SCAN_TILE, seq)
    nt = seq // tt
    t = batch * seq
    ncol = u_rwkv.shape[1]
    full = lambda a: pl.BlockSpec(a.shape, lambda b, i: (0,) * a.ndim)
    params = (mu, w0, w2p, a0, a2p, g2, k_k, k_a, r_k, ln_g, ln_b, seg_ones)
    return pl.pallas_call(
        functools.partial(_rwkv_kernel, tt=tt, c=CHUNK),
        grid=(batch, nt),
        in_specs=[pl.BlockSpec((tt, ncol), lambda b, i: (b * nt + i, 0))] + [full(p) for p in params],
        out_specs=pl.BlockSpec((tt, GROUP_W), lambda b, i: (b * nt + i, 0)),
        out_shape=jax.ShapeDtypeStruct((t, GROUP_W), BF16),
        scratch_shapes=[pltpu.VMEM((tt + 8, ncol), F32), pltpu.VMEM((RWKV_HEADS, RWKV_HD, RWKV_HD), F32)],
        compiler_params=_cparams("parallel", "arbitrary"),
        name="rwkv7",
    )(u_rwkv, *params)


def _layer_norm(h, g, b):
    mu = jnp.mean(h, axis=-1, keepdims=True)
    d = h - mu
    var = jnp.mean(d * d, axis=-1, keepdims=True)
    return d * lax.rsqrt(var + LN_EPS) * g + b


def _route_rows(lg):
    neg = -1e30
    lane = lax.broadcasted_iota(jnp.int32, lg.shape, 1)
    lane_f = lane.astype(F32)
    first = lambda hit: jnp.min(jnp.where(hit, lane_f, float(LANE)), axis=-1, keepdims=True)
    is_g = lane < N_GROUPS
    gl = jnp.where(is_g, lg, neg)
    gmax = jnp.max(gl, axis=-1, keepdims=True)
    p_grp = 1.0 / jnp.sum(jnp.where(is_g, jnp.exp(gl - gmax), 0.0), axis=-1, keepdims=True)
    grp = first(gl == gmax)
    lo = N_GROUPS + grp * EXP_PER_GROUP
    el = jnp.where(jnp.logical_and(lane_f >= lo, lane_f < lo + EXP_PER_GROUP), lg, neg)
    v1 = jnp.max(el, axis=-1, keepdims=True)
    i1 = first(el == v1)
    el2 = jnp.where(lane_f == i1, neg, el)
    v2 = jnp.max(el2, axis=-1, keepdims=True)
    i2 = first(el2 == v2)
    r = jnp.exp(v2 - v1)
    g1 = p_grp / (1.0 + r)
    out = jnp.where(lane == 0, i1 - N_GROUPS, 0.0)
    out = jnp.where(lane == 1, i2 - N_GROUPS, out)
    out = jnp.where(lane == 2, g1, out)
    return jnp.where(lane == 3, g1 * r, out)


def _outproj_kernel(yf_ref, ym_ref, yr_ref, yg_ref, w_ref, x_ref, g_ref, b_ref, wrh_ref, wrl_ref, rb_ref, tril_ref,
                    xo_ref, xb_ref, lg_ref, cnt_ref, *, alpha):
    gw = GROUP_W
    acc = jnp.dot(yf_ref[...], w_ref[0:gw, :], preferred_element_type=F32)
    acc = acc + jnp.dot(ym_ref[...], w_ref[gw : 2 * gw, :], preferred_element_type=F32)
    acc = acc + jnp.dot(yr_ref[...], w_ref[2 * gw : 3 * gw, :], preferred_element_type=F32)
    acc = acc + jnp.dot(yg_ref[...], w_ref[3 * gw : 4 * gw, :], preferred_element_type=F32)
    xn = _layer_norm(alpha * x_ref[...] + acc, g_ref[...], b_ref[...])
    xo_ref[...] = xn
    xh = xn.astype(BF16)
    xb_ref[...] = xh
    xl = (xn - xh.astype(F32)).astype(BF16)
    wrh = wrh_ref[...]
    d = lambda a, b: jnp.dot(a, b, preferred_element_type=F32)
    routed = _route_rows(d(xh, wrh) + d(xl, wrh) + d(xh, wrl_ref[...]) + rb_ref[...])

    @pl.when(pl.program_id(0) == 0)
    def _():
        cnt_ref[...] = jnp.zeros(cnt_ref.shape, F32)

    lane = lax.broadcasted_iota(jnp.int32, routed.shape, 1).astype(F32)
    oh = [jnp.where(lane == routed[:, k : k + 1], 1.0, 0.0) for k in range(TOP_K)]
    both = oh[0] + oh[1]
    before = jnp.dot(tril_ref[...], both.astype(BF16), preferred_element_type=F32) + cnt_ref[0:1, :]
    for k in range(TOP_K):
        rank = jnp.sum(before * oh[k], axis=-1, keepdims=True)
        routed = jnp.where(lane == 2 * TOP_K + k, rank, routed)
    cnt_ref[...] = cnt_ref[...] + jnp.sum(both, axis=0, keepdims=True)
    lg_ref[...] = routed


def _outproj(ys, w_out, x, ln_g, ln_b, wr_hi, wr_lo, r_bias, alpha):
    t, d = x.shape
    tm = min(ROW_TILE, t)
    row = lambda n: pl.BlockSpec((tm, n), lambda i: (i, 0))
    full = lambda a: pl.BlockSpec(a.shape, lambda i: (0,) * a.ndim)
    idx = jnp.arange(tm)
    tril = (idx[None, :] < idx[:, None]).astype(BF16)
    return pl.pallas_call(
        functools.partial(_outproj_kernel, alpha=alpha),
        grid=(t // tm,),
        in_specs=[row(GROUP_W)] * 4
        + [full(w_out), row(d), full(ln_g), full(ln_b), full(wr_hi), full(wr_lo), full(r_bias), full(tril)],
        out_specs=[row(d), row(d), row(LANE), pl.BlockSpec((8, LANE), lambda i: (0, 0))],
        out_shape=[
            jax.ShapeDtypeStruct((t, d), F32),
            jax.ShapeDtypeStruct((t, d), BF16),
            jax.ShapeDtypeStruct((t, LANE), F32),
            jax.ShapeDtypeStruct((8, LANE), F32),
        ],
        compiler_params=_cparams("arbitrary"),
        name="outproj_ln_router",
    )(*ys, w_out, x, ln_g, ln_b, wr_hi, wr_lo, r_bias, tril)


def _expert_kernel(be_ref, x_ref, wg_ref, wu_ref, wd_ref, o_ref, wg_s, wu_s, wd_s):
    i = pl.program_id(0)

    @pl.when(jnp.logical_or(i == 0, be_ref[i] != be_ref[jnp.maximum(i - 1, 0)]))
    def _():
        wg_s[...] = wg_ref[...].astype(BF16)
        wu_s[...] = wu_ref[...].astype(BF16)
        wd_s[...] = wd_ref[...].astype(BF16)

    x = x_ref[...]
    a = jnp.dot(x, wg_s[...], preferred_element_type=F32)
    b = jnp.dot(x, wu_s[...], preferred_element_type=F32)
    hmid = (a * _sigmoid(a) * b).astype(BF16)
    o_ref[...] = jnp.dot(hmid, wd_s[...], preferred_element_type=F32).astype(o_ref.dtype)


def _experts(block_weight, xs, w_gate, w_up, w_down):
    n_slots, d = xs.shape
    tb = MOE_TILE
    de = w_gate.shape[-1]
    grid_spec = pltpu.PrefetchScalarGridSpec(
        num_scalar_prefetch=1,
        grid=(n_slots // tb,),
        in_specs=[
            pl.BlockSpec((tb, d), lambda i, be: (i, 0)),
            pl.BlockSpec((None, d, de), lambda i, be: (be[i], 0, 0)),
            pl.BlockSpec((None, d, de), lambda i, be: (be[i], 0, 0)),
            pl.BlockSpec((None, de, d), lambda i, be: (be[i], 0, 0)),
        ],
        out_specs=pl.BlockSpec((tb, d), lambda i, be: (i, 0)),
        scratch_shapes=[pltpu.VMEM((d, de), BF16), pltpu.VMEM((d, de), BF16), pltpu.VMEM((de, d), BF16)],
    )
    return pl.pallas_call(
        _expert_kernel,
        grid_spec=grid_spec,
        out_shape=jax.ShapeDtypeStruct((n_slots, d), BF16),
        compiler_params=_cparams("arbitrary"),
        name="experts",
    )(block_weight, xs, w_gate, w_up, w_down)


def _ln2_kernel(x_ref, y1_ref, y2_ref, r_ref, g_ref, b_ref, xo_ref, xb_ref, *, alpha):
    routed = r_ref[...]
    moe = routed[:, TOP_K : TOP_K + 1] * y1_ref[...].astype(F32) + routed[:, TOP_K + 1 : TOP_K + 2] * y2_ref[...].astype(F32)
    xn = _layer_norm(alpha * x_ref[...] + moe, g_ref[...], b_ref[...])
    xo_ref[...] = xn
    xb_ref[...] = xn.astype(BF16)


def _ln2(x, y1, y2, routed, g, b, alpha):
    t, d = x.shape
    tm = min(ROW_TILE, t)
    row = lambda n: pl.BlockSpec((tm, n), lambda i: (i, 0))
    full = lambda a: pl.BlockSpec(a.shape, lambda i: (0,) * a.ndim)
    return pl.pallas_call(
        functools.partial(_ln2_kernel, alpha=alpha),
        grid=(t // tm,),
        in_specs=[row(d), row(d), row(d), row(LANE), full(g), full(b)],
        out_specs=[row(d), row(d)],
        out_shape=[jax.ShapeDtypeStruct((t, d), F32), jax.ShapeDtypeStruct((t, d), BF16)],
        compiler_params=_cparams("parallel"),
        name="residual_ln2",
    )(x, y1, y2, routed, g, b)


def _pad_cols(w, n):
    return jnp.pad(w, [(0, 0)] * (w.ndim - 1) + [(0, n - w.shape[-1])])


def _rot_half_cols(w):
    half = w.shape[-1] // 2
    return jnp.concatenate([-w[..., half:], w[..., :half]], axis=-1)


def _prep_weights(p):
    w_in = p["w_in"]
    c_fox = 3 * GROUP_W + FOX_HEADS
    c_mla = MLA_Q_RANK + MLA_KV_RANK + MLA_ROPE
    c_rwkv = 3 * GROUP_W + 2 * 64 + 128
    o_mla = c_fox
    o_rwkv = o_mla + c_mla
    o_gdn = o_rwkv + c_rwkv
    row = lambda a: a[:, None, :].astype(F32)

    w_fox = jnp.concatenate(
        [w_in[..., :GROUP_W] * (FOX_HD**-0.5 * LOG2E), w_in[..., GROUP_W : 3 * GROUP_W], _pad_cols(w_in[..., 3 * GROUP_W : c_fox], LANE)],
        axis=-1,
    )
    kpe_w = w_in[..., o_mla + MLA_Q_RANK + MLA_KV_RANK : o_mla + c_mla]
    w_mla = jnp.concatenate(
        [w_in[..., o_mla : o_mla + MLA_Q_RANK + MLA_KV_RANK], _pad_cols(kpe_w, LANE), _pad_cols(_rot_half_cols(kpe_w), LANE)],
        axis=-1,
    )
    w_rwkv = w_in[..., o_rwkv:o_gdn]
    g0 = o_gdn + 4 * GROUP_W
    w_gdn = jnp.concatenate(
        [w_in[..., o_gdn:g0], _pad_cols(w_in[..., g0 : g0 + GDN_HEADS], LANE), _pad_cols(w_in[..., g0 + GDN_HEADS :], LANE)],
        axis=-1,
    )

    nl = w_in.shape[0]
    scale = (MLA_NOPE + MLA_ROPE) ** -0.5 * LOG2E
    wq = p["mla_w_uq"].reshape(nl, MLA_Q_RANK, MLA_HEADS, MLA_NOPE + MLA_ROPE) * scale
    wq_nope = wq[..., :MLA_NOPE].reshape(nl, MLA_Q_RANK, -1)
    wq_pe = wq[..., MLA_NOPE:]
    wq_p = jnp.concatenate(
        [wq_nope, _pad_cols(wq_pe, LANE).reshape(nl, MLA_Q_RANK, -1), _pad_cols(_rot_half_cols(wq_pe), LANE).reshape(nl, MLA_Q_RANK, -1)],
        axis=-1,
    )
    wkv = p["mla_w_ukv"].reshape(nl, MLA_KV_RANK, MLA_HEADS, MLA_NOPE + MLA_VD)
    wkv_p = jnp.concatenate([wkv[..., :MLA_NOPE].reshape(nl, MLA_KV_RANK, -1), wkv[..., MLA_NOPE:].reshape(nl, MLA_KV_RANK, -1)], axis=-1)

    zeros64 = jnp.zeros((nl, 64, GROUP_W), F32)
    w_router = _pad_cols(jnp.concatenate([p["moe_w_grp"], p["moe_w_exp"]], axis=-1), LANE)
    wr_hi = w_router.astype(BF16)
    wr_lo = (w_router - wr_hi.astype(F32)).astype(BF16)
    return dict(
        w_fox=w_fox.astype(BF16), w_mla=w_mla.astype(BF16), w_rwkv=w_rwkv.astype(BF16), w_gdn=w_gdn.astype(BF16),
        layer=jnp.arange(nl, dtype=jnp.int32),
        fox_b_f=row(_pad_cols(p["fox_b_f"], LANE)), fox_out_g=row(p["fox_out_g"]),
        mla_qg=row(p["mla_q_norm_g"]), mla_kvg=row(p["mla_kv_norm_g"]), mla_wq=wq_p.astype(BF16), mla_wkv=wkv_p.astype(BF16),
        mla_out_g=row(p["mla_out_g"]),
        rwkv_mu=row(p["rwkv_mu"]), rwkv_w0=row(p["rwkv_w0"]),
        rwkv_w2=jnp.concatenate([p["rwkv_w2"], zeros64], axis=1).astype(BF16),
        rwkv_a0=row(p["rwkv_a0"]), rwkv_a2=jnp.concatenate([zeros64, p["rwkv_a2"]], axis=1).astype(BF16),
        rwkv_g2=p["rwkv_g2"].astype(BF16), rwkv_k_k=row(p["rwkv_k_k"]), rwkv_k_a=row(p["rwkv_k_a"]),
        rwkv_r_k=row(p["rwkv_r_k"]), rwkv_ln_g=row(p["rwkv_ln_g"]), rwkv_ln_b=row(p["rwkv_ln_b"]),
        gdn_conv_w=p["gdn_conv_w"].astype(F32), gdn_a_log=row(_pad_cols(p["gdn_a_log"], LANE)),
        gdn_dt_bias=row(_pad_cols(p["gdn_dt_bias"], LANE)), gdn_norm_g=row(p["gdn_norm_g"]),
        w_out=p["w_out"].astype(BF16), ln1_g=row(p["ln1_g"]), ln1_b=row(p["ln1_b"]),
        wr_hi=wr_hi, wr_lo=wr_lo, r_bias=row(_pad_cols(jnp.concatenate([p["moe_b_grp"], p["moe_b_exp"]], axis=-1), LANE)),
        ln2_g=row(p["ln2_g"]), ln2_b=row(p["ln2_b"]),
    )


def _route(routed, counts, tb):
    t = routed.shape[0]
    a = t * TOP_K
    n_blocks = (a + N_EXPERTS * (tb - 1) + tb - 1) // tb
    n_slots = n_blocks * tb
    expert = routed[:, :TOP_K].astype(jnp.int32)
    rank = routed[:, 2 * TOP_K : 3 * TOP_K].astype(jnp.int32)
    padded = (counts + tb - 1) // tb * tb
    pend = jnp.cumsum(padded)
    pstart = pend - padded
    ids = jnp.arange(N_EXPERTS, dtype=jnp.int32)
    slot_of_assignment = jnp.sum(jnp.where(expert[..., None] == ids, pstart, 0), axis=-1) + rank
    block_start = jnp.arange(n_blocks, dtype=jnp.int32) * tb
    block_expert = jnp.minimum(jnp.sum(block_start[:, None] >= pend, axis=-1), N_EXPERTS - 1).astype(jnp.int32)
    fill_end = jnp.cumsum(padded - counts)
    fill_key = jnp.sum(jnp.arange(n_slots - a, dtype=jnp.int32)[:, None] >= fill_end, axis=-1).astype(jnp.int32)
    keys = jnp.concatenate([expert.reshape(a), fill_key])
    vals = jnp.concatenate([jnp.arange(a, dtype=jnp.int32) // TOP_K, jnp.full((n_slots - a,), t, jnp.int32)])
    _, token_of_slot = lax.sort((keys, vals), num_keys=1, is_stable=True)
    return token_of_slot, block_expert, slot_of_assignment


def _layer(x, xb, cs, sn, w, moe_w, seg_ones, *, batch, seq, alpha):
    u_fox, fox_tail = _inproj(xb, w["w_fox"], LANE)
    (u_mla,) = _inproj(xb, w["w_mla"], 0)
    (u_rwkv,) = _inproj(xb, w["w_rwkv"], 0)
    u_gdn, gdn_tail = _inproj(xb, w["w_gdn"], 2 * LANE)

    vt_fox = u_fox[:, 2 * GROUP_W :].reshape(batch, seq, GROUP_W).transpose(0, 2, 1)
    y_fox = _attention(u_fox, 0, u_fox, 1, vt_fox, (fox_tail, w["fox_b_f"]), w["fox_out_g"], batch=batch, seq=seq,
                       heads=FOX_HEADS, dk=FOX_HD, dv=FOX_HD, chunk=1)

    q_mla, k_mla, vt_mla = _mla_prep(u_mla, cs, sn, w["mla_qg"], w["mla_kvg"], w["mla_wq"], w["mla_wkv"],
                                     batch=batch, seq=seq)
    y_mla = _attention(q_mla, 0, k_mla, 0, vt_mla, None, w["mla_out_g"], batch=batch, seq=seq,
                       heads=MLA_HEADS, dk=2 * LANE, dv=MLA_VD, chunk=CHUNK)

    y_rwkv = _rwkv(u_rwkv, w["rwkv_mu"], w["rwkv_w0"], w["rwkv_w2"], w["rwkv_a0"], w["rwkv_a2"], w["rwkv_g2"],
                   w["rwkv_k_k"], w["rwkv_k_a"], w["rwkv_r_k"], w["rwkv_ln_g"], w["rwkv_ln_b"], seg_ones,
                   batch=batch, seq=seq)
    y_gdn = _gdn(u_gdn, gdn_tail, w["gdn_conv_w"], w["gdn_a_log"], w["gdn_dt_bias"], w["gdn_norm_g"],
                 batch=batch, seq=seq)

    x1, x1b, routed, counts = _outproj((y_fox, y_mla, y_rwkv, y_gdn), w["w_out"], x, w["ln1_g"], w["ln1_b"],
                                       w["wr_hi"], w["wr_lo"], w["r_bias"], alpha)

    token_of_slot, block_expert, slot_of_assignment = _route(routed, counts[0, :N_EXPERTS].astype(jnp.int32), MOE_TILE)
    x_pad = jnp.concatenate([x1b, jnp.zeros((1, x1b.shape[1]), BF16)], axis=0)
    y_slots = _experts(block_expert + w["layer"] * N_EXPERTS, x_pad[token_of_slot], *moe_w)
    return _ln2(x1, y_slots[slot_of_assignment[:, 0]], y_slots[slot_of_assignment[:, 1]], routed,
                w["ln2_g"], w["ln2_b"], alpha)


def kernel(x, positions, w_in, fox_b_f, fox_out_g, mla_q_norm_g, mla_kv_norm_g, mla_w_uq, mla_w_ukv, mla_out_g, rwkv_mu, rwkv_w0, rwkv_w2, rwkv_a0, rwkv_a2, rwkv_g2, rwkv_k_k, rwkv_k_a, rwkv_r_k, rwkv_ln_g, rwkv_ln_b, gdn_conv_w, gdn_a_log, gdn_dt_bias, gdn_norm_g, w_out, ln1_g, ln1_b, moe_w_grp, moe_b_grp, moe_w_exp, moe_b_exp, moe_w_gate, moe_w_up, moe_w_down, ln2_g, ln2_b):
    batch, seq, d = x.shape
    depth = w_in.shape[0]
    alpha = (2 * depth) ** 0.25
    params = dict(
        w_in=w_in, fox_b_f=fox_b_f, fox_out_g=fox_out_g, mla_q_norm_g=mla_q_norm_g, mla_kv_norm_g=mla_kv_norm_g,
        mla_w_uq=mla_w_uq, mla_w_ukv=mla_w_ukv, mla_out_g=mla_out_g, rwkv_mu=rwkv_mu, rwkv_w0=rwkv_w0, rwkv_w2=rwkv_w2,
        rwkv_a0=rwkv_a0, rwkv_a2=rwkv_a2, rwkv_g2=rwkv_g2, rwkv_k_k=rwkv_k_k, rwkv_k_a=rwkv_k_a, rwkv_r_k=rwkv_r_k,
        rwkv_ln_g=rwkv_ln_g, rwkv_ln_b=rwkv_ln_b, gdn_conv_w=gdn_conv_w, gdn_a_log=gdn_a_log, gdn_dt_bias=gdn_dt_bias,
        gdn_norm_g=gdn_norm_g, w_out=w_out, ln1_g=ln1_g, ln1_b=ln1_b, moe_w_grp=moe_w_grp, moe_b_grp=moe_b_grp,
        moe_w_exp=moe_w_exp, moe_b_exp=moe_b_exp, moe_w_gate=moe_w_gate, moe_w_up=moe_w_up, moe_w_down=moe_w_down,
        ln2_g=ln2_g, ln2_b=ln2_b,
    )
    weights = _prep_weights(params)

    half = MLA_ROPE // 2
    inv_freq = ROPE_THETA ** (-jnp.arange(half, dtype=F32) / half)
    ang = positions.astype(F32).reshape(batch * seq, 1) * inv_freq
    zpad = jnp.zeros((batch * seq, LANE - MLA_ROPE), F32)
    cs = jnp.concatenate([jnp.cos(ang), jnp.cos(ang), zpad], axis=-1)
    sn = jnp.concatenate([jnp.sin(ang), jnp.sin(ang), zpad], axis=-1)

    xf = x.reshape(batch * seq, d).astype(F32)

    stack = lambda a: a.reshape((-1,) + a.shape[2:])
    moe_w = (stack(moe_w_gate), stack(moe_w_up), stack(moe_w_down))
    seg = jnp.arange(GROUP_W) // RWKV_HD
    seg_ones = (seg[:, None] == seg[None, :]).astype(BF16)

    def body(carry, w):
        xc, xcb = carry
        return _layer(xc, xcb, cs, sn, w, moe_w, seg_ones, batch=batch, seq=seq, alpha=alpha), None

    (xf, _), _ = lax.scan(body, (xf, xf.astype(BF16)), weights)
    return xf.reshape(batch, seq, d).astype(x.dtype)
```

```python
import functools
import math

import jax
import jax.numpy as jnp
from jax import lax
from jax.experimental import pallas as pl
from jax.experimental.pallas import tpu as pltpu

F32 = jnp.float32
BF16 = jnp.bfloat16

D_MODEL = 2048
GROUP_W = 512
FOX_HD, FOX_HEADS = 64, 8
MLA_HEADS, MLA_NOPE, MLA_ROPE, MLA_VD = 4, 128, 64, 128
MLA_Q_RANK, MLA_KV_RANK = 384, 128
ROPE_THETA = 10000.0
RWKV_HD, RWKV_HEADS = 64, 8
RWKV_GN_EPS = 64e-5
GDN_HD, GDN_HEADS, GDN_CONV = 128, 4, 4
N_GROUPS, EXP_PER_GROUP, TOP_K, D_EXPERT = 4, 8, 2, 512
N_EXPERTS = N_GROUPS * EXP_PER_GROUP
CHUNK = 64
LN_EPS = 1e-5
RMS_EPS = 1e-6
LOG2E = math.log2(math.e)

LANE = 128
VMEM_LIMIT_BYTES = 56 * 1024 * 1024
ROW_TILE = 512
ATTN_TILE = 256
SCAN_TILE = 256
MOE_TILE = 256


def _cparams(*sem):
    return pltpu.CompilerParams(dimension_semantics=sem, vmem_limit_bytes=VMEM_LIMIT_BYTES)


def _bdot(a, b):
    return jnp.dot(a.astype(BF16), b.astype(BF16), preferred_element_type=F32)


def _bdot_nt(a, b):
    return lax.dot_general(a.astype(BF16), b.astype(BF16), (((1,), (1,)), ((), ())), preferred_element_type=F32)


def _bdot_tn(a, b):
    return lax.dot_general(a.astype(BF16), b.astype(BF16), (((0,), (0,)), ((), ())), preferred_element_type=F32)


def _split3(x):
    hi = x.astype(BF16)
    r1 = x - hi.astype(F32)
    mid = r1.astype(BF16)
    lo = (r1 - mid.astype(F32)).astype(BF16)
    return hi, mid, lo


def _dot_exact_lhs(m, x):
    hi, mid, lo = _split3(x)
    d = lambda p: jnp.dot(m, p, preferred_element_type=F32)
    return d(hi) + d(mid) + d(lo)


def _sigmoid(x):
    return 1.0 / (1.0 + jnp.exp(-x))


def _softplus(x):
    return jnp.maximum(x, 0.0) + jnp.log(1.0 + jnp.exp(-jnp.abs(x)))


def _chunk_masks(n, c):
    r = lax.broadcasted_iota(jnp.int32, (n, n), 0)
    col = lax.broadcasted_iota(jnp.int32, (n, n), 1)
    same = (r // c) == (col // c)
    lower = jnp.logical_and(same, col <= r)
    strict = jnp.logical_and(same, col < r)
    return r, col, same, lower, strict


def _tri_inv_masks(r, col, c):
    blk = lambda s: (r // s) == (col // s)
    one = lambda cond: jnp.where(cond, 1.0, 0.0).astype(F32)
    offs = []
    s = 8
    while s < c:
        offs.append(one(jnp.logical_and(blk(2 * s), jnp.logical_not(blk(s)))))
        s *= 2
    return one(r == col), one(blk(8)), offs


def _tri_inv(ms, masks):
    eye, blk8, offs = masks
    bf = lambda a: a.astype(BF16)
    dot = lambda a, b: jnp.dot(a, b, preferred_element_type=F32)
    mdf = [m * blk8 for m in ms]
    mds = [bf(m) for m in mdf]
    xs = [eye + m for m in mdf]
    m2s = [bf(dot(md, md)) for md in mds]
    xs = [x + dot(m2, bf(x)) for x, m2 in zip(xs, m2s)]
    m4s = [bf(dot(m2, m2)) for m2 in m2s]
    xs = [x + dot(m4, bf(x)) for x, m4 in zip(xs, m4s)]
    for off in offs:
        xbs = [bf(x) for x in xs]
        ts = [bf(dot(xb, bf(m * off))) for xb, m in zip(xbs, ms)]
        xs = [x + dot(t, xb) for x, t, xb in zip(xs, ts, xbs)]
    return xs


def _inproj_kernel(x_ref, w_ref, o_ref, *tail_refs, tail):
    acc = jnp.dot(x_ref[...], w_ref[...], preferred_element_type=F32)
    n = acc.shape[1]
    if tail:
        o_ref[...] = acc[:, : n - tail].astype(o_ref.dtype)
        tail_refs[0][...] = acc[:, n - tail :]
    else:
        o_ref[...] = acc.astype(o_ref.dtype)


def _inproj(xb, w, tail):
    t, d = xb.shape
    n = w.shape[1]
    tm = min(ROW_TILE, t)
    out_shape = [jax.ShapeDtypeStruct((t, n - tail), BF16)]
    out_specs = [pl.BlockSpec((tm, n - tail), lambda i: (i, 0))]
    if tail:
        out_shape.append(jax.ShapeDtypeStruct((t, tail), F32))
        out_specs.append(pl.BlockSpec((tm, tail), lambda i: (i, 0)))
    return pl.pallas_call(
        functools.partial(_inproj_kernel, tail=tail),
        grid=(t // tm,),
        in_specs=[pl.BlockSpec((tm, d), lambda i: (i, 0)), pl.BlockSpec((d, n), lambda i: (0, 0))],
        out_specs=out_specs,
        out_shape=out_shape,
        compiler_params=_cparams("parallel"),
        name="inproj",
    )(xb, w)


def _attn_kernel(*refs, heads, dk, dv, chunk, tq, seq, use_bias):
    if use_bias:
        q_ref, k_ref, vt_ref, gate_ref, gb_ref, g_ref, o_ref, brep_ref, m_ref, l_ref, acc_ref = refs
    else:
        q_ref, k_ref, vt_ref, g_ref, o_ref, m_ref, l_ref, acc_ref = refs
    paired = dk < LANE
    dkp = LANE if paired else dk
    i = pl.program_id(1)

    if use_bias:
        @pl.when(i == 0)
        def _():
            r = lax.broadcasted_iota(jnp.int32, (tq, tq), 0)
            c = lax.broadcasted_iota(jnp.int32, (tq, tq), 1)
            tri = jnp.where(c <= r, 1.0, 0.0).astype(BF16)
            carry = jnp.zeros((1, LANE), F32)
            for j in range(seq // tq):
                rows = slice(j * tq, (j + 1) * tq)
                log_f = -_softplus(-(gate_ref[rows, :] + gb_ref[...])) * LOG2E
                cum = _dot_exact_lhs(tri, log_f) + carry
                carry = cum[tq - 1 : tq, :]
                for h in range(heads):
                    brep_ref[h, rows, :] = jnp.broadcast_to(cum[:, h : h + 1], (tq, LANE))

    m_ref[...] = jnp.full(m_ref.shape, -1e30, F32)
    l_ref[...] = jnp.zeros(l_ref.shape, F32)
    acc_ref[...] = jnp.zeros(acc_ref.shape, F32)

    qs = []
    for h in range(heads):
        if paired:
            slab = q_ref[:, (h // 2) * LANE : (h // 2 + 1) * LANE]
            lane_half = lax.broadcasted_iota(jnp.int32, (tq, LANE), 1) // dk
            qs.append(jnp.where(lane_half == h % 2, slab, jnp.zeros_like(slab)))
        else:
            qs.append(q_ref[:, h * dkp : (h + 1) * dkp])

    def step(off, masked):
        if masked:
            kr = lax.broadcasted_iota(jnp.int32, (tq, tq), 0)
            qc = lax.broadcasted_iota(jnp.int32, (tq, tq), 1)
            allowed = (kr // chunk) <= (qc // chunk)
        sts = []
        for h in range(heads):
            slab = h // 2 if paired else h
            k = k_ref[pl.ds(off, tq), slab * dkp : (slab + 1) * dkp]
            sts.append(lax.dot_general(k, qs[h], (((1,), (1,)), ((), ())), preferred_element_type=F32))
        ps, alphas = [], []
        for h in range(heads):
            st = sts[h]
            if use_bias:
                st = st - jnp.concatenate([brep_ref[h, pl.ds(off, tq), :]] * (tq // LANE), axis=1)
            if masked:
                st = jnp.where(allowed, st, -1e30)
            m_old = m_ref[h, 0:1, :]
            m_new = jnp.maximum(m_old, jnp.max(st, axis=0, keepdims=True))
            p = jnp.exp2(st - m_new)
            alpha = jnp.exp2(m_old - m_new)
            l_ref[h, 0:1, :] = alpha * l_ref[h, 0:1, :] + jnp.sum(p, axis=0, keepdims=True)
            m_ref[h, 0:1, :] = m_new
            ps.append(p.astype(BF16))
            alphas.append(alpha)
        for h in range(heads):
            rows = slice(h * dv, (h + 1) * dv)
            pv = jnp.dot(vt_ref[rows, pl.ds(off, tq)], ps[h], preferred_element_type=F32)
            acc_ref[rows, :] = alphas[h] * acc_ref[rows, :] + pv

    def body(j, carry):
        step(pl.multiple_of(j * tq, tq), False)
        return carry

    lax.fori_loop(0, i, body, 0)
    step(pl.multiple_of(i * tq, tq), True)

    ot = jnp.concatenate([acc_ref[h * dv : (h + 1) * dv, :] / l_ref[h, 0:1, :] for h in range(heads)], axis=0)
    ot = ot * lax.rsqrt(jnp.mean(ot * ot, axis=0, keepdims=True) + RMS_EPS)
    o_ref[...] = (ot.T * g_ref[...]).astype(o_ref.dtype)


def _attention(q_arr, q_col, k_arr, k_col, vt, gate, gain, *, batch, seq, heads, dk, dv, chunk):
    bias = gate
    assert dk % LANE == 0 or (2 * dk == LANE and heads % 2 == 0)
    tq = min(ATTN_TILE, seq)
    nq = seq // tq
    t = batch * seq
    in_specs = [
        pl.BlockSpec((tq, heads * dk), lambda b, i: (b * nq + i, q_col)),
        pl.BlockSpec((seq, heads * dk), lambda b, i: (b, k_col)),
        pl.BlockSpec((None, heads * dv, seq), lambda b, i: (b, 0, 0)),
    ]
    args = [q_arr, k_arr, vt]
    scratch = []
    if bias is not None:
        in_specs += [pl.BlockSpec((seq, LANE), lambda b, i: (b, 0)), pl.BlockSpec((1, LANE), lambda b, i: (0, 0))]
        args += list(gate)
        scratch.append(pltpu.VMEM((heads, seq, LANE), F32))
    in_specs.append(pl.BlockSpec((1, heads * dv), lambda b, i: (0, 0)))
    args.append(gain)
    stat = pltpu.VMEM((heads, 8, tq), F32)
    scratch += [stat, stat, pltpu.VMEM((heads * dv, tq), F32)]
    return pl.pallas_call(
        functools.partial(_attn_kernel, heads=heads, dk=dk, dv=dv, chunk=chunk, tq=tq, seq=seq, use_bias=bias is not None),
        grid=(batch, nq),
        in_specs=in_specs,
        out_specs=pl.BlockSpec((tq, heads * dv), lambda b, i: (b * nq + i, 0)),
        out_shape=jax.ShapeDtypeStruct((t, heads * dv), BF16),
        scratch_shapes=scratch,
        compiler_params=_cparams("parallel", "arbitrary"),
        name="attention",
    )(*args)


def _mla_prep_kernel(u_ref, cs_ref, sn_ref, qg_ref, kvg_ref, wq_ref, wkv_ref, q_ref, k_ref, v_ref):
    u = u_ref[...].astype(F32)
    cs = cs_ref[...]
    sn = sn_ref[...]

    def rms(x, g):
        return x * lax.rsqrt(jnp.mean(x * x, axis=-1, keepdims=True) + RMS_EPS) * g

    qo = _bdot(rms(u[:, :MLA_Q_RANK], qg_ref[...]), wq_ref[...])
    kvo = _bdot(rms(u[:, MLA_Q_RANK : MLA_Q_RANK + MLA_KV_RANK], kvg_ref[...]), wkv_ref[...])
    c0 = MLA_Q_RANK + MLA_KV_RANK
    kpe = (u[:, c0 : c0 + LANE] * cs + u[:, c0 + LANE : c0 + 2 * LANE] * sn).astype(k_ref.dtype)
    nn = MLA_HEADS * MLA_NOPE
    for h in range(MLA_HEADS):
        a = h * 2 * LANE
        q_ref[:, a : a + LANE] = qo[:, h * LANE : (h + 1) * LANE].astype(q_ref.dtype)
        qpe = qo[:, nn + h * LANE : nn + (h + 1) * LANE] * cs + qo[:, 2 * nn + h * LANE : 2 * nn + (h + 1) * LANE] * sn
        q_ref[:, a + LANE : a + 2 * LANE] = qpe.astype(q_ref.dtype)
        k_ref[:, a : a + LANE] = kvo[:, h * LANE : (h + 1) * LANE].astype(k_ref.dtype)
        k_ref[:, a + LANE : a + 2 * LANE] = kpe
    v_ref[...] = kvo[:, nn:].T.astype(v_ref.dtype)


def _mla_prep(u_mla, cs, sn, qg, kvg, wq, wkv, *, batch, seq):
    t = u_mla.shape[0]
    tm = min(ROW_TILE, seq)
    nt = seq // tm
    row = lambda n: pl.BlockSpec((tm, n), lambda i: (i, 0))
    full = lambda a: pl.BlockSpec(a.shape, lambda i: (0,) * a.ndim)
    wide = MLA_HEADS * 2 * LANE
    dvs = MLA_HEADS * MLA_VD
    return pl.pallas_call(
        _mla_prep_kernel,
        grid=(t // tm,),
        in_specs=[row(u_mla.shape[1]), row(LANE), row(LANE), full(qg), full(kvg), full(wq), full(wkv)],
        out_specs=[row(wide), row(wide), pl.BlockSpec((None, dvs, tm), lambda i: (i // nt, 0, i % nt))],
        out_shape=[
            jax.ShapeDtypeStruct((t, wide), BF16),
            jax.ShapeDtypeStruct((t, wide), BF16),
            jax.ShapeDtypeStruct((batch, dvs, seq), BF16),
        ],
        compiler_params=_cparams("parallel"),
        name="mla_prep",
    )(u_mla, cs, sn, qg, kvg, wq, wkv)


def _gdn_kernel(u_ref, t_ref, cw_ref, alog_ref, dtb_ref, ng_ref, o_ref, xs_ref, st_ref, *, tt, c):
    i = pl.program_id(1)
    gw, hd = GROUP_W, GDN_HD

    @pl.when(i == 0)
    def _():
        xs_ref[0:8, :] = jnp.zeros((8, 3 * gw), F32)
        st_ref[...] = jnp.zeros(st_ref.shape, F32)

    xs_ref[8 : 8 + tt, :] = u_ref[:, : 3 * gw].astype(F32)
    cw = cw_ref[...]
    conv = cw[0:1, :] * xs_ref[5 : 5 + tt, :]
    for j in range(1, GDN_CONV):
        conv = conv + cw[j : j + 1, :] * xs_ref[5 + j : 5 + j + tt, :]
    xs_ref[0:8, :] = xs_ref[tt : tt + 8, :]
    qkv = conv * _sigmoid(conv)

    tail = t_ref[...]
    beta = _sigmoid(tail[:, :LANE])
    g = -jnp.exp(alog_ref[...]) * _softplus(tail[:, LANE:] + dtb_ref[...])

    r, col, same, lower, strict = _chunk_masks(tt, c)
    tri = jnp.where(lower, 1.0, 0.0).astype(BF16)
    ones_blk = jnp.where(same, 1.0, 0.0).astype(BF16)
    gc = _dot_exact_lhs(tri, g)
    gtot = _dot_exact_lhs(ones_blk, g)
    gct = gc.T

    def l2n(x):
        return x * lax.rsqrt(jnp.sum(x * x, axis=-1, keepdims=True) + 1e-6)

    heads = range(GDN_HEADS)
    bf = lambda a: a.astype(BF16)
    lmats, attns, kbs, vbs, qds, kds, egs = [], [], [], [], [], [], []
    for h in heads:
        qf = l2n(qkv[:, h * hd : (h + 1) * hd]) * (hd**-0.5)
        qh = bf(qf)
        kf = l2n(qkv[:, gw + h * hd : gw + (h + 1) * hd])
        kh = bf(kf)
        vh = qkv[:, 2 * gw + h * hd : 2 * gw + (h + 1) * hd]
        gcol = gc[:, h : h + 1]
        dec = jnp.exp(jnp.minimum(gcol - gct[h : h + 1, :], 0.0))
        bcol = beta[:, h : h + 1]
        kb = kf * bcol
        eg = jnp.exp(gcol)
        lmats.append(jnp.where(strict, _bdot_nt(kb, kh) * dec, 0.0))
        attns.append(bf(jnp.where(lower, _bdot_nt(qh, kh) * dec, 0.0)))
        kbs.append(bf(kb * eg))
        vbs.append(bf(vh * bcol))
        qds.append(bf(qf * eg))
        kds.append(bf(kf * jnp.exp(gtot[:, h : h + 1] - gcol)))
    tinvs = [bf(t) for t in _tri_inv([-m for m in lmats], _tri_inv_masks(r, col, c))]
    uvals = [jnp.dot(t, vb, preferred_element_type=F32) for t, vb in zip(tinvs, vbs)]
    wcums = [bf(jnp.dot(t, kb, preferred_element_type=F32)) for t, kb in zip(tinvs, kbs)]
    states = [st_ref[h] for h in heads]
    outs = [[] for _ in heads]
    for cc in range(tt // c):
        rs = slice(cc * c, (cc + 1) * c)
        sbs = [bf(s) for s in states]
        vnews = [uvals[h][rs] - jnp.dot(wcums[h][rs], sbs[h], preferred_element_type=F32) for h in heads]
        for h in heads:
            outs[h].append(jnp.dot(qds[h][rs], sbs[h], preferred_element_type=F32) + _bdot(attns[h][rs, rs], vnews[h]))
            glast = jnp.exp(gtot[cc * c : cc * c + 1, h : h + 1])
            states[h] = states[h] * glast + _bdot_tn(kds[h][rs], vnews[h])
    for h in heads:
        st_ref[h] = states[h]
        o = jnp.concatenate(outs[h], axis=0)
        o = o * lax.rsqrt(jnp.mean(o * o, axis=-1, keepdims=True) + RMS_EPS) * ng_ref[...]
        z = u_ref[:, 3 * gw + h * hd : 3 * gw + (h + 1) * hd].astype(F32)
        o_ref[:, h * hd : (h + 1) * hd] = (o * (z * _sigmoid(z))).astype(o_ref.dtype)


def _gdn(u_gdn, tail, conv_w, a_log, dt_bias, norm_g, *, batch, seq):
    tt = min(SCAN_TILE, seq)
    nt = seq // tt
    t = batch * seq
    full = lambda a: pl.BlockSpec(a.shape, lambda b, i: (0,) * a.ndim)
    return pl.pallas_call(
        functools.partial(_gdn_kernel, tt=tt, c=CHUNK),
        grid=(batch, nt),
        in_specs=[
            pl.BlockSpec((tt, 4 * GROUP_W), lambda b, i: (b * nt + i, 0)),
            pl.BlockSpec((tt, 2 * LANE), lambda b, i: (b * nt + i, 0)),
            full(conv_w),
            full(a_log),
            full(dt_bias),
            full(norm_g),
        ],
        out_specs=pl.BlockSpec((tt, GROUP_W), lambda b, i: (b * nt + i, 0)),
        out_shape=jax.ShapeDtypeStruct((t, GROUP_W), BF16),
        scratch_shapes=[pltpu.VMEM((tt + 8, 3 * GROUP_W), F32), pltpu.VMEM((GDN_HEADS, GDN_HD, GDN_HD), F32)],
        compiler_params=_cparams("parallel", "arbitrary"),
        name="gdn",
    )(u_gdn, tail, conv_w, a_log, dt_bias, norm_g)


def _rwkv_kernel(u_ref, mu_ref, w0_ref, w2_ref, a0_ref, a2_ref, g2_ref, kk_ref, ka_ref, rk_ref, lng_ref, lnb_ref,
                 e_ref, o_ref, xs_ref, st_ref, *, tt, c):
    i = pl.program_id(1)
    gw, hd = GROUP_W, RWKV_HD

    @pl.when(i == 0)
    def _():
        xs_ref[0:8, :] = jnp.zeros((8, xs_ref.shape[1]), F32)
        st_ref[...] = jnp.zeros(st_ref.shape, F32)

    u = u_ref[...].astype(F32)
    xs_ref[8 : 8 + tt, :] = u
    prev = xs_ref[7 : 7 + tt, :]
    xs_ref[0:8, :] = xs_ref[tt : tt + 8, :]
    x = u + mu_ref[...] * (prev - u)
    rr = x[:, :gw]
    k = x[:, gw : 2 * gw]
    v = x[:, 2 * gw : 3 * gw]
    wa = x[:, 3 * gw : 3 * gw + LANE]
    glo = x[:, 3 * gw + LANE :]
    w = -_softplus(-(w0_ref[...] + _bdot(jnp.tanh(wa), w2_ref[...]))) - 0.5
    ld = -jnp.exp(w)
    gate_a = _sigmoid(a0_ref[...] + _bdot(wa, a2_ref[...]))
    gate_g = _bdot(_sigmoid(glo), g2_ref[...])
    e = e_ref[...]

    def segsum(y):
        hi = y.astype(BF16)
        lo = (y - hi.astype(F32)).astype(BF16)
        return jnp.dot(hi, e, preferred_element_type=F32) + jnp.dot(lo, e, preferred_element_type=F32)

    kkr = k * kk_ref[...]
    kk = kkr * lax.rsqrt(segsum(kkr * kkr) + 1e-6)
    k2 = k * (1.0 + (gate_a - 1.0) * ka_ref[...])
    bonus = segsum(rr * k2 * rk_ref[...]) * v

    r, col, same, lower, strict = _chunk_masks(tt, c)
    tri = jnp.where(lower, 1.0, 0.0).astype(BF16)
    ones_blk = jnp.where(same, 1.0, 0.0).astype(BF16)
    cum = _dot_exact_lhs(tri, ld)
    ctot = _dot_exact_lhs(ones_blk, ld)
    encum = jnp.exp(-cum)
    edec = jnp.exp(ctot - cum)
    kka = kk * gate_a
    rt = rr * jnp.exp(cum)
    at = -kk * jnp.exp(cum - ld)
    bt = kka * encum
    kt = k2 * encum
    bd = kka * edec
    kd = k2 * edec
    pc = jnp.exp(ctot)

    heads = range(RWKV_HEADS)
    bf = lambda a: a.astype(BF16)
    dot = lambda a, b: jnp.dot(a, b, preferred_element_type=F32)
    dot_nt = lambda a, b: lax.dot_general(a, b, (((1,), (1,)), ((), ())), preferred_element_type=F32)
    dot_tn = lambda a, b: lax.dot_general(a, b, (((0,), (0,)), ((), ())), preferred_element_type=F32)
    at_b, rt_b, bt_b, kt_b, v_b, bd_b, kd_b = bf(at), bf(rt), bf(bt), bf(kt), bf(v), bf(bd), bf(kd)
    sls = [slice(h * hd, (h + 1) * hd) for h in heads]
    strict_f = jnp.where(strict, 1.0, 0.0).astype(F32)
    lower_f = jnp.where(lower, 1.0, 0.0).astype(F32)
    mabs = [dot_nt(at_b[:, sl], bt_b[:, sl]) * strict_f for sl in sls]
    maks = [bf(dot_nt(at_b[:, sl], kt_b[:, sl]) * strict_f) for sl in sls]
    arbs = [bf(dot_nt(rt_b[:, sl], bt_b[:, sl]) * lower_f) for sl in sls]
    arks = [bf(dot_nt(rt_b[:, sl], kt_b[:, sl]) * lower_f) for sl in sls]
    tinvs = [bf(t) for t in _tri_inv(mabs, _tri_inv_masks(r, col, c))]
    wmats = [bf(dot(t, at_b[:, sl])) for t, sl in zip(tinvs, sls)]
    mkvs = [bf(dot(m, v_b[:, sl])) for m, sl in zip(maks, sls)]
    umats = [dot(t, mkv) for t, mkv in zip(tinvs, mkvs)]
    yconsts = [dot(m, v_b[:, sl]) for m, sl in zip(arks, sls)]
    states = [st_ref[h] for h in heads]
    outs = [[] for _ in heads]
    for cc in range(tt // c):
        rs = slice(cc * c, (cc + 1) * c)
        sbs = [bf(s) for s in states]
        sas = [dot_nt(wmats[h][rs], sbs[h]) + umats[h][rs] for h in heads]
        for h in heads:
            sab = bf(sas[h])
            outs[h].append(dot_nt(rt_b[rs, sls[h]], sbs[h]) + dot(arbs[h][rs, rs], sab) + yconsts[h][rs])
            states[h] = (states[h] * pc[cc * c : cc * c + 1, sls[h]] + dot_tn(sab, bd_b[rs, sls[h]])
                         + dot_tn(v_b[rs, sls[h]], kd_b[rs, sls[h]]))
    for h in heads:
        st_ref[h] = states[h]
    y = jnp.concatenate([jnp.concatenate(o, axis=0) for o in outs], axis=-1)
    mean = segsum(y) * (1.0 / hd)
    d = y - mean
    var = segsum(d * d) * (1.0 / hd)
    yn = d * lax.rsqrt(var + RWKV_GN_EPS) * lng_ref[...] + lnb_ref[...]
    o_ref[...] = ((yn + bonus) * gate_g).astype(o_ref.dtype)


def _rwkv(u_rwkv, mu, w0, w2p, a0, a2p, g2, k_k, k_a, r_k, ln_g, ln_b, seg_ones, *, batch, seq):
    tt = min(SCAN_TILE, seq)
    nt = seq // tt
    t = batch * seq
    ncol = u_rwkv.shape[1]
    full = lambda a: pl.BlockSpec(a.shape, lambda b, i: (0,) * a.ndim)
    params = (mu, w0, w2p, a0, a2p, g2, k_k, k_a, r_k, ln_g, ln_b, seg_ones)
    return pl.pallas_call(
        functools.partial(_rwkv_kernel, tt=tt, c=CHUNK),
        grid=(batch, nt),
        in_specs=[pl.BlockSpec((tt, ncol), lambda b, i: (b * nt + i, 0))] + [full(p) for p in params],
        out_specs=pl.BlockSpec((tt, GROUP_W), lambda b, i: (b * nt + i, 0)),
        out_shape=jax.ShapeDtypeStruct((t, GROUP_W), BF16),
        scratch_shapes=[pltpu.VMEM((tt + 8, ncol), F32), pltpu.VMEM((RWKV_HEADS, RWKV_HD, RWKV_HD), F32)],
        compiler_params=_cparams("parallel", "arbitrary"),
        name="rwkv7",
    )(u_rwkv, *params)


def _layer_norm(h, g, b):
    mu = jnp.mean(h, axis=-1, keepdims=True)
    d = h - mu
    var = jnp.mean(d * d, axis=-1, keepdims=True)
    return d * lax.rsqrt(var + LN_EPS) * g + b


def _route_rows(lg):
    neg = -1e30
    lane = lax.broadcasted_iota(jnp.int32, lg.shape, 1)
    lane_f = lane.astype(F32)
    first = lambda hit: jnp.min(jnp.where(hit, lane_f, float(LANE)), axis=-1, keepdims=True)
    is_g = lane < N_GROUPS
    gl = jnp.where(is_g, lg, neg)
    gmax = jnp.max(gl, axis=-1, keepdims=True)
    p_grp = 1.0 / jnp.sum(jnp.where(is_g, jnp.exp(gl - gmax), 0.0), axis=-1, keepdims=True)
    grp = first(gl == gmax)
    lo = N_GROUPS + grp * EXP_PER_GROUP
    el = jnp.where(jnp.logical_and(lane_f >= lo, lane_f < lo + EXP_PER_GROUP), lg, neg)
    v1 = jnp.max(el, axis=-1, keepdims=True)
    i1 = first(el == v1)
    el2 = jnp.where(lane_f == i1, neg, el)
    v2 = jnp.max(el2, axis=-1, keepdims=True)
    i2 = first(el2 == v2)
    r = jnp.exp(v2 - v1)
    g1 = p_grp / (1.0 + r)
    out = jnp.where(lane == 0, i1 - N_GROUPS, 0.0)
    out = jnp.where(lane == 1, i2 - N_GROUPS, out)
    out = jnp.where(lane == 2, g1, out)
    return jnp.where(lane == 3, g1 * r, out)


def _outproj_kernel(yf_ref, ym_ref, yr_ref, yg_ref, w_ref, x_ref, g_ref, b_ref, wrh_ref, wrl_ref, rb_ref, tril_ref,
                    xo_ref, xb_ref, lg_ref, cnt_ref, *, alpha):
    gw = GROUP_W
    acc = jnp.dot(yf_ref[...], w_ref[0:gw, :], preferred_element_type=F32)
    acc = acc + jnp.dot(ym_ref[...], w_ref[gw : 2 * gw, :], preferred_element_type=F32)
    acc = acc + jnp.dot(yr_ref[...], w_ref[2 * gw : 3 * gw, :], preferred_element_type=F32)
    acc = acc + jnp.dot(yg_ref[...], w_ref[3 * gw : 4 * gw, :], preferred_element_type=F32)
    xn = _layer_norm(alpha * x_ref[...] + acc, g_ref[...], b_ref[...])
    xo_ref[...] = xn
    xh = xn.astype(BF16)
    xb_ref[...] = xh
    xl = (xn - xh.astype(F32)).astype(BF16)
    wrh = wrh_ref[...]
    d = lambda a, b: jnp.dot(a, b, preferred_element_type=F32)
    routed = _route_rows(d(xh, wrh) + d(xl, wrh) + d(xh, wrl_ref[...]) + rb_ref[...])

    @pl.when(pl.program_id(0) == 0)
    def _():
        cnt_ref[...] = jnp.zeros(cnt_ref.shape, F32)

    lane = lax.broadcasted_iota(jnp.int32, routed.shape, 1).astype(F32)
    oh = [jnp.where(lane == routed[:, k : k + 1], 1.0, 0.0) for k in range(TOP_K)]
    both = oh[0] + oh[1]
    before = jnp.dot(tril_ref[...], both.astype(BF16), preferred_element_type=F32) + cnt_ref[0:1, :]
    for k in range(TOP_K):
        rank = jnp.sum(before * oh[k], axis=-1, keepdims=True)
        routed = jnp.where(lane == 2 * TOP_K + k, rank, routed)
    cnt_ref[...] = cnt_ref[...] + jnp.sum(both, axis=0, keepdims=True)
    lg_ref[...] = routed


def _outproj(ys, w_out, x, ln_g, ln_b, wr_hi, wr_lo, r_bias, alpha):
    t, d = x.shape
    tm = min(ROW_TILE, t)
    row = lambda n: pl.BlockSpec((tm, n), lambda i: (i, 0))
    full = lambda a: pl.BlockSpec(a.shape, lambda i: (0,) * a.ndim)
    idx = jnp.arange(tm)
    tril = (idx[None, :] < idx[:, None]).astype(BF16)
    return pl.pallas_call(
        functools.partial(_outproj_kernel, alpha=alpha),
        grid=(t // tm,),
        in_specs=[row(GROUP_W)] * 4
        + [full(w_out), row(d), full(ln_g), full(ln_b), full(wr_hi), full(wr_lo), full(r_bias), full(tril)],
        out_specs=[row(d), row(d), row(LANE), pl.BlockSpec((8, LANE), lambda i: (0, 0))],
        out_shape=[
            jax.ShapeDtypeStruct((t, d), F32),
            jax.ShapeDtypeStruct((t, d), BF16),
            jax.ShapeDtypeStruct((t, LANE), F32),
            jax.ShapeDtypeStruct((8, LANE), F32),
        ],
        compiler_params=_cparams("arbitrary"),
        name="outproj_ln_router",
    )(*ys, w_out, x, ln_g, ln_b, wr_hi, wr_lo, r_bias, tril)


def _expert_kernel(be_ref, x_ref, wg_ref, wu_ref, wd_ref, o_ref, wg_s, wu_s, wd_s):
    i = pl.program_id(0)

    @pl.when(jnp.logical_or(i == 0, be_ref[i] != be_ref[jnp.maximum(i - 1, 0)]))
    def _():
        wg_s[...] = wg_ref[...].astype(BF16)
        wu_s[...] = wu_ref[...].astype(BF16)
        wd_s[...] = wd_ref[...].astype(BF16)

    x = x_ref[...]
    a = jnp.dot(x, wg_s[...], preferred_element_type=F32)
    b = jnp.dot(x, wu_s[...], preferred_element_type=F32)
    hmid = (a * _sigmoid(a) * b).astype(BF16)
    o_ref[...] = jnp.dot(hmid, wd_s[...], preferred_element_type=F32).astype(o_ref.dtype)


def _experts(block_weight, xs, w_gate, w_up, w_down):
    n_slots, d = xs.shape
    tb = MOE_TILE
    de = w_gate.shape[-1]
    grid_spec = pltpu.PrefetchScalarGridSpec(
        num_scalar_prefetch=1,
        grid=(n_slots // tb,),
        in_specs=[
            pl.BlockSpec((tb, d), lambda i, be: (i, 0)),
            pl.BlockSpec((None, d, de), lambda i, be: (be[i], 0, 0)),
            pl.BlockSpec((None, d, de), lambda i, be: (be[i], 0, 0)),
            pl.BlockSpec((None, de, d), lambda i, be: (be[i], 0, 0)),
        ],
        out_specs=pl.BlockSpec((tb, d), lambda i, be: (i, 0)),
        scratch_shapes=[pltpu.VMEM((d, de), BF16), pltpu.VMEM((d, de), BF16), pltpu.VMEM((de, d), BF16)],
    )
    return pl.pallas_call(
        _expert_kernel,
        grid_spec=grid_spec,
        out_shape=jax.ShapeDtypeStruct((n_slots, d), BF16),
        compiler_params=_cparams("arbitrary"),
        name="experts",
    )(block_weight, xs, w_gate, w_up, w_down)


def _ln2_kernel(x_ref, y1_ref, y2_ref, r_ref, g_ref, b_ref, xo_ref, xb_ref, *, alpha):
    routed = r_ref[...]
    moe = routed[:, TOP_K : TOP_K + 1] * y1_ref[...].astype(F32) + routed[:, TOP_K + 1 : TOP_K + 2] * y2_ref[...].astype(F32)
    xn = _layer_norm(alpha * x_ref[...] + moe, g_ref[...], b_ref[...])
    xo_ref[...] = xn
    xb_ref[...] = xn.astype(BF16)


def _ln2(x, y1, y2, routed, g, b, alpha):
    t, d = x.shape
    tm = min(ROW_TILE, t)
    row = lambda n: pl.BlockSpec((tm, n), lambda i: (i, 0))
    full = lambda a: pl.BlockSpec(a.shape, lambda i: (0,) * a.ndim)
    return pl.pallas_call(
        functools.partial(_ln2_kernel, alpha=alpha),
        grid=(t // tm,),
        in_specs=[row(d), row(d), row(d), row(LANE), full(g), full(b)],
        out_specs=[row(d), row(d)],
        out_shape=[jax.ShapeDtypeStruct((t, d), F32), jax.ShapeDtypeStruct((t, d), BF16)],
        compiler_params=_cparams("parallel"),
        name="residual_ln2",
    )(x, y1, y2, routed, g, b)


def _pad_cols(w, n):
    return jnp.pad(w, [(0, 0)] * (w.ndim - 1) + [(0, n - w.shape[-1])])


def _rot_half_cols(w):
    half = w.shape[-1] // 2
    return jnp.concatenate([-w[..., half:], w[..., :half]], axis=-1)


def _prep_weights(p):
    w_in = p["w_in"]
    c_fox = 3 * GROUP_W + FOX_HEADS
    c_mla = MLA_Q_RANK + MLA_KV_RANK + MLA_ROPE
    c_rwkv = 3 * GROUP_W + 2 * 64 + 128
    o_mla = c_fox
    o_rwkv = o_mla + c_mla
    o_gdn = o_rwkv + c_rwkv
    row = lambda a: a[:, None, :].astype(F32)

    w_fox = jnp.concatenate(
        [w_in[..., :GROUP_W] * (FOX_HD**-0.5 * LOG2E), w_in[..., GROUP_W : 3 * GROUP_W], _pad_cols(w_in[..., 3 * GROUP_W : c_fox], LANE)],
        axis=-1,
    )
    kpe_w = w_in[..., o_mla + MLA_Q_RANK + MLA_KV_RANK : o_mla + c_mla]
    w_mla = jnp.concatenate(
        [w_in[..., o_mla : o_mla + MLA_Q_RANK + MLA_KV_RANK], _pad_cols(kpe_w, LANE), _pad_cols(_rot_half_cols(kpe_w), LANE)],
        axis=-1,
    )
    w_rwkv = w_in[..., o_rwkv:o_gdn]
    g0 = o_gdn + 4 * GROUP_W
    w_gdn = jnp.concatenate(
        [w_in[..., o_gdn:g0], _pad_cols(w_in[..., g0 : g0 + GDN_HEADS], LANE), _pad_cols(w_in[..., g0 + GDN_HEADS :], LANE)],
        axis=-1,
    )

    nl = w_in.shape[0]
    scale = (MLA_NOPE + MLA_ROPE) ** -0.5 * LOG2E
    wq = p["mla_w_uq"].reshape(nl, MLA_Q_RANK, MLA_HEADS, MLA_NOPE + MLA_ROPE) * scale
    wq_nope = wq[..., :MLA_NOPE].reshape(nl, MLA_Q_RANK, -1)
    wq_pe = wq[..., MLA_NOPE:]
    wq_p = jnp.concatenate(
        [wq_nope, _pad_cols(wq_pe, LANE).reshape(nl, MLA_Q_RANK, -1), _pad_cols(_rot_half_cols(wq_pe), LANE).reshape(nl, MLA_Q_RANK, -1)],
        axis=-1,
    )
    wkv = p["mla_w_ukv"].reshape(nl, MLA_KV_RANK, MLA_HEADS, MLA_NOPE + MLA_VD)
    wkv_p = jnp.concatenate([wkv[..., :MLA_NOPE].reshape(nl, MLA_KV_RANK, -1), wkv[..., MLA_NOPE:].reshape(nl, MLA_KV_RANK, -1)], axis=-1)

    zeros64 = jnp.zeros((nl, 64, GROUP_W), F32)
    w_router = _pad_cols(jnp.concatenate([p["moe_w_grp"], p["moe_w_exp"]], axis=-1), LANE)
    wr_hi = w_router.astype(BF16)
    wr_lo = (w_router - wr_hi.astype(F32)).astype(BF16)
    return dict(
        w_fox=w_fox.astype(BF16), w_mla=w_mla.astype(BF16), w_rwkv=w_rwkv.astype(BF16), w_gdn=w_gdn.astype(BF16),
        layer=jnp.arange(nl, dtype=jnp.int32),
        fox_b_f=row(_pad_cols(p["fox_b_f"], LANE)), fox_out_g=row(p["fox_out_g"]),
        mla_qg=row(p["mla_q_norm_g"]), mla_kvg=row(p["mla_kv_norm_g"]), mla_wq=wq_p.astype(BF16), mla_wkv=wkv_p.astype(BF16),
        mla_out_g=row(p["mla_out_g"]),
        rwkv_mu=row(p["rwkv_mu"]), rwkv_w0=row(p["rwkv_w0"]),
        rwkv_w2=jnp.concatenate([p["rwkv_w2"], zeros64], axis=1).astype(BF16),
        rwkv_a0=row(p["rwkv_a0"]), rwkv_a2=jnp.concatenate([zeros64, p["rwkv_a2"]], axis=1).astype(BF16),
        rwkv_g2=p["rwkv_g2"].astype(BF16), rwkv_k_k=row(p["rwkv_k_k"]), rwkv_k_a=row(p["rwkv_k_a"]),
        rwkv_r_k=row(p["rwkv_r_k"]), rwkv_ln_g=row(p["rwkv_ln_g"]), rwkv_ln_b=row(p["rwkv_ln_b"]),
        gdn_conv_w=p["gdn_conv_w"].astype(F32), gdn_a_log=row(_pad_cols(p["gdn_a_log"], LANE)),
        gdn_dt_bias=row(_pad_cols(p["gdn_dt_bias"], LANE)), gdn_norm_g=row(p["gdn_norm_g"]),
        w_out=p["w_out"].astype(BF16), ln1_g=row(p["ln1_g"]), ln1_b=row(p["ln1_b"]),
        wr_hi=wr_hi, wr_lo=wr_lo, r_bias=row(_pad_cols(jnp.concatenate([p["moe_b_grp"], p["moe_b_exp"]], axis=-1), LANE)),
        ln2_g=row(p["ln2_g"]), ln2_b=row(p["ln2_b"]),
    )


def _route(routed, counts, tb):
    t = routed.shape[0]
    a = t * TOP_K
    n_blocks = (a + N_EXPERTS * (tb - 1) + tb - 1) // tb
    n_slots = n_blocks * tb
    expert = routed[:, :TOP_K].astype(jnp.int32)
    rank = routed[:, 2 * TOP_K : 3 * TOP_K].astype(jnp.int32)
    padded = (counts + tb - 1) // tb * tb
    pend = jnp.cumsum(padded)
    pstart = pend - padded
    ids = jnp.arange(N_EXPERTS, dtype=jnp.int32)
    slot_of_assignment = jnp.sum(jnp.where(expert[..., None] == ids, pstart, 0), axis=-1) + rank
    block_start = jnp.arange(n_blocks, dtype=jnp.int32) * tb
    block_expert = jnp.minimum(jnp.sum(block_start[:, None] >= pend, axis=-1), N_EXPERTS - 1).astype(jnp.int32)
    fill_end = jnp.cumsum(padded - counts)
    filler = jnp.arange(n_slots - a, dtype=jnp.int32)
    fill_key = jnp.sum(filler[:, None] >= fill_end, axis=-1).astype(jnp.int32)
    keys = jnp.concatenate([expert.reshape(a), fill_key])
    vals = jnp.concatenate([jnp.arange(a, dtype=jnp.int32) // TOP_K, filler % t])
    _, token_of_slot = lax.sort((keys, vals), num_keys=1, is_stable=True)
    return token_of_slot, block_expert, slot_of_assignment


def _layer(x, xb, cs, sn, w, moe_w, seg_ones, *, batch, seq, alpha):
    u_fox, fox_tail = _inproj(xb, w["w_fox"], LANE)
    (u_mla,) = _inproj(xb, w["w_mla"], 0)
    (u_rwkv,) = _inproj(xb, w["w_rwkv"], 0)
    u_gdn, gdn_tail = _inproj(xb, w["w_gdn"], 2 * LANE)

    vt_fox = u_fox[:, 2 * GROUP_W :].reshape(batch, seq, GROUP_W).transpose(0, 2, 1)
    y_fox = _attention(u_fox, 0, u_fox, 1, vt_fox, (fox_tail, w["fox_b_f"]), w["fox_out_g"], batch=batch, seq=seq,
                       heads=FOX_HEADS, dk=FOX_HD, dv=FOX_HD, chunk=1)

    q_mla, k_mla, vt_mla = _mla_prep(u_mla, cs, sn, w["mla_qg"], w["mla_kvg"], w["mla_wq"], w["mla_wkv"],
                                     batch=batch, seq=seq)
    y_mla = _attention(q_mla, 0, k_mla, 0, vt_mla, None, w["mla_out_g"], batch=batch, seq=seq,
                       heads=MLA_HEADS, dk=2 * LANE, dv=MLA_VD, chunk=CHUNK)

    y_rwkv = _rwkv(u_rwkv, w["rwkv_mu"], w["rwkv_w0"], w["rwkv_w2"], w["rwkv_a0"], w["rwkv_a2"], w["rwkv_g2"],
                   w["rwkv_k_k"], w["rwkv_k_a"], w["rwkv_r_k"], w["rwkv_ln_g"], w["rwkv_ln_b"], seg_ones,
                   batch=batch, seq=seq)
    y_gdn = _gdn(u_gdn, gdn_tail, w["gdn_conv_w"], w["gdn_a_log"], w["gdn_dt_bias"], w["gdn_norm_g"],
                 batch=batch, seq=seq)

    x1, x1b, routed, counts = _outproj((y_fox, y_mla, y_rwkv, y_gdn), w["w_out"], x, w["ln1_g"], w["ln1_b"],
                                       w["wr_hi"], w["wr_lo"], w["r_bias"], alpha)

    token_of_slot, block_expert, slot_of_assignment = _route(routed, counts[0, :N_EXPERTS].astype(jnp.int32), MOE_TILE)
    y_slots = _experts(block_expert + w["layer"] * N_EXPERTS, x1b[token_of_slot], *moe_w)
    return _ln2(x1, y_slots[slot_of_assignment[:, 0]], y_slots[slot_of_assignment[:, 1]], routed,
                w["ln2_g"], w["ln2_b"], alpha)


def kernel(x, positions, w_in, fox_b_f, fox_out_g, mla_q_norm_g, mla_kv_norm_g, mla_w_uq, mla_w_ukv, mla_out_g, rwkv_mu, rwkv_w0, rwkv_w2, rwkv_a0, rwkv_a2, rwkv_g2, rwkv_k_k, rwkv_k_a, rwkv_r_k, rwkv_ln_g, rwkv_ln_b, gdn_conv_w, gdn_a_log, gdn_dt_bias, gdn_norm_g, w_out, ln1_g, ln1_b, moe_w_grp, moe_b_grp, moe_w_exp, moe_b_exp, moe_w_gate, moe_w_up, moe_w_down, ln2_g, ln2_b):
    batch, seq, d = x.shape
    depth = w_in.shape[0]
    alpha = (2 * depth) ** 0.25
    params = dict(
        w_in=w_in, fox_b_f=fox_b_f, fox_out_g=fox_out_g, mla_q_norm_g=mla_q_norm_g, mla_kv_norm_g=mla_kv_norm_g,
        mla_w_uq=mla_w_uq, mla_w_ukv=mla_w_ukv, mla_out_g=mla_out_g, rwkv_mu=rwkv_mu, rwkv_w0=rwkv_w0, rwkv_w2=rwkv_w2,
        rwkv_a0=rwkv_a0, rwkv_a2=rwkv_a2, rwkv_g2=rwkv_g2, rwkv_k_k=rwkv_k_k, rwkv_k_a=rwkv_k_a, rwkv_r_k=rwkv_r_k,
        rwkv_ln_g=rwkv_ln_g, rwkv_ln_b=rwkv_ln_b, gdn_conv_w=gdn_conv_w, gdn_a_log=gdn_a_log, gdn_dt_bias=gdn_dt_bias,
        gdn_norm_g=gdn_norm_g, w_out=w_out, ln1_g=ln1_g, ln1_b=ln1_b, moe_w_grp=moe_w_grp, moe_b_grp=moe_b_grp,
        moe_w_exp=moe_w_exp, moe_b_exp=moe_b_exp, moe_w_gate=moe_w_gate, moe_w_up=moe_w_up, moe_w_down=moe_w_down,
        ln2_g=ln2_g, ln2_b=ln2_b,
    )
    weights = _prep_weights(params)

    half = MLA_ROPE // 2
    inv_freq = ROPE_THETA ** (-jnp.arange(half, dtype=F32) / half)
    ang = positions.astype(F32).reshape(batch * seq, 1) * inv_freq
    zpad = jnp.zeros((batch * seq, LANE - MLA_ROPE), F32)
    cs = jnp.concatenate([jnp.cos(ang), jnp.cos(ang), zpad], axis=-1)
    sn = jnp.concatenate([jnp.sin(ang), jnp.sin(ang), zpad], axis=-1)

    xf = x.reshape(batch * seq, d).astype(F32)

    stack = lambda a: a.reshape((-1,) + a.shape[2:])
    moe_w = (stack(moe_w_gate), stack(moe_w_up), stack(moe_w_down))
    seg = jnp.arange(GROUP_W) // RWKV_HD
    seg_ones = (seg[:, None] == seg[None, :]).astype(BF16)

    def body(carry, w):
        xc, xcb = carry
        return _layer(xc, xcb, cs, sn, w, moe_w, seg_ones, batch=batch, seq=seq, alpha=alpha), None

    (xf, _), _ = lax.scan(body, (xf, xf.astype(BF16)), weights)
    return xf.reshape(batch, seq, d).astype(x.dtype)
```

```python
import functools
import math

import jax
import jax.numpy as jnp
from jax import lax
from jax.experimental import pallas as pl
from jax.experimental.pallas import tpu as pltpu

F32 = jnp.float32
BF16 = jnp.bfloat16

D_MODEL = 2048
GROUP_W = 512
FOX_HD, FOX_HEADS = 64, 8
MLA_HEADS, MLA_NOPE, MLA_ROPE, MLA_VD = 4, 128, 64, 128
MLA_Q_RANK, MLA_KV_RANK = 384, 128
ROPE_THETA = 10000.0
RWKV_HD, RWKV_HEADS = 64, 8
RWKV_GN_EPS = 64e-5
GDN_HD, GDN_HEADS, GDN_CONV = 128, 4, 4
N_GROUPS, EXP_PER_GROUP, TOP_K, D_EXPERT = 4, 8, 2, 512
N_EXPERTS = N_GROUPS * EXP_PER_GROUP
CHUNK = 64
LN_EPS = 1e-5
RMS_EPS = 1e-6
LOG2E = math.log2(math.e)

LANE = 128
VMEM_LIMIT_BYTES = 56 * 1024 * 1024
ROW_TILE = 512
ATTN_TILE = 256
SCAN_TILE = 256
MOE_TILE = 256
ONES_ROWS = 16


def _cparams(*sem):
    return pltpu.CompilerParams(dimension_semantics=sem, vmem_limit_bytes=VMEM_LIMIT_BYTES)


def _bdot(a, b):
    return jnp.dot(a.astype(BF16), b.astype(BF16), preferred_element_type=F32)


def _bdot_nt(a, b):
    return lax.dot_general(a.astype(BF16), b.astype(BF16), (((1,), (1,)), ((), ())), preferred_element_type=F32)


def _bdot_tn(a, b):
    return lax.dot_general(a.astype(BF16), b.astype(BF16), (((0,), (0,)), ((), ())), preferred_element_type=F32)


def _split3(x):
    hi = x.astype(BF16)
    r1 = x - hi.astype(F32)
    mid = r1.astype(BF16)
    lo = (r1 - mid.astype(F32)).astype(BF16)
    return hi, mid, lo


def _dot_exact_lhs(m, x):
    hi, mid, lo = _split3(x)
    d = lambda p: jnp.dot(m, p, preferred_element_type=F32)
    return d(hi) + d(mid) + d(lo)


def _sigmoid(x):
    return 1.0 / (1.0 + jnp.exp(-x))


def _softplus(x):
    return jnp.maximum(x, 0.0) + jnp.log(1.0 + jnp.exp(-jnp.abs(x)))


def _chunk_masks(n, c):
    r = lax.broadcasted_iota(jnp.int32, (n, n), 0)
    col = lax.broadcasted_iota(jnp.int32, (n, n), 1)
    same = (r // c) == (col // c)
    lower = jnp.logical_and(same, col <= r)
    strict = jnp.logical_and(same, col < r)
    return r, col, same, lower, strict


def _tri_inv_masks(r, col, c):
    blk = lambda s: (r // s) == (col // s)
    one = lambda cond: jnp.where(cond, 1.0, 0.0).astype(F32)
    offs = []
    s = 8
    while s < c:
        offs.append(one(jnp.logical_and(blk(2 * s), jnp.logical_not(blk(s)))))
        s *= 2
    return one(r == col), one(blk(8)), offs


def _tri_inv(ms, masks):
    eye, blk8, offs = masks
    bf = lambda a: a.astype(BF16)
    dot = lambda a, b: jnp.dot(a, b, preferred_element_type=F32)
    mdf = [m * blk8 for m in ms]
    mds = [bf(m) for m in mdf]
    xs = [eye + m for m in mdf]
    m2s = [bf(dot(md, md)) for md in mds]
    yield
    xs = [x + dot(m2, bf(x)) for x, m2 in zip(xs, m2s)]
    yield
    m4s = [bf(dot(m2, m2)) for m2 in m2s]
    yield
    xs = [x + dot(m4, bf(x)) for x, m4 in zip(xs, m4s)]
    yield
    for off in offs:
        xbs = [bf(x) for x in xs]
        ts = [bf(dot(xb, bf(m * off))) for xb, m in zip(xbs, ms)]
        yield
        xs = [x + dot(t, xb) for x, t, xb in zip(xs, ts, xbs)]
        yield
    return xs


def _interleave(*stage_generators):
    live = list(stage_generators)
    while live:
        for g in list(live):
            try:
                next(g)
            except StopIteration:
                live.remove(g)


def _inproj_kernel(x_ref, w_ref, o_ref, *tail_refs, tail):
    acc = jnp.dot(x_ref[...], w_ref[...], preferred_element_type=F32)
    n = acc.shape[1]
    if tail:
        o_ref[...] = acc[:, : n - tail].astype(o_ref.dtype)
        tail_refs[0][...] = acc[:, n - tail :]
    else:
        o_ref[...] = acc.astype(o_ref.dtype)


def _inproj(xb, w, tail):
    t, d = xb.shape
    n = w.shape[1]
    tm = min(ROW_TILE, t)
    out_shape = [jax.ShapeDtypeStruct((t, n - tail), BF16)]
    out_specs = [pl.BlockSpec((tm, n - tail), lambda i: (i, 0))]
    if tail:
        out_shape.append(jax.ShapeDtypeStruct((t, tail), F32))
        out_specs.append(pl.BlockSpec((tm, tail), lambda i: (i, 0)))
    return pl.pallas_call(
        functools.partial(_inproj_kernel, tail=tail),
        grid=(t // tm,),
        in_specs=[pl.BlockSpec((tm, d), lambda i: (i, 0)), pl.BlockSpec((d, n), lambda i: (0, 0))],
        out_specs=out_specs,
        out_shape=out_shape,
        compiler_params=_cparams("parallel"),
        name="inproj",
    )(xb, w)


def _attn_kernel(*refs, heads, dk, dv, chunk, tq, seq, use_bias):
    if use_bias:
        q_ref, k_ref, vt_ref, gate_ref, gb_ref, g_ref, o_ref, kb_ref, m_ref, acc_ref = refs
    else:
        q_ref, k_ref, vt_ref, g_ref, o_ref, m_ref, acc_ref = refs
    paired = dk < LANE
    dkp = LANE if paired else dk
    dva = dv + ONES_ROWS
    i = pl.program_id(1)
    lane = lax.broadcasted_iota(jnp.int32, (tq, LANE), 1)

    if use_bias:
        @pl.when(i == 0)
        def _():
            r = lax.broadcasted_iota(jnp.int32, (tq, tq), 0)
            c = lax.broadcasted_iota(jnp.int32, (tq, tq), 1)
            tri = jnp.where(c <= r, 1.0, 0.0).astype(BF16)
            carry = jnp.zeros((1, LANE), F32)
            for j in range(seq // tq):
                rows = slice(j * tq, (j + 1) * tq)
                log_f = -_softplus(-(gate_ref[rows, :] + gb_ref[...])) * LOG2E
                cum = _dot_exact_lhs(tri, log_f) + carry
                carry = cum[tq - 1 : tq, :]
                hi, mid, lo = (jnp.where(lane < heads, p.astype(F32), 0.0) for p in _split3(cum))
                pieces = hi + pltpu.roll(mid, heads, axis=1) + pltpu.roll(lo, 2 * heads, axis=1)
                kb_ref[rows, :] = (-pieces).astype(BF16)

    m_ref[...] = jnp.full(m_ref.shape, -1e30, F32)
    acc_ref[...] = jnp.zeros(acc_ref.shape, F32)

    qs = []
    for h in range(heads):
        if paired:
            slab = q_ref[:, (h // 2) * LANE : (h // 2 + 1) * LANE]
            q = jnp.where(lane // dk == h % 2, slab, jnp.zeros_like(slab))
        else:
            q = q_ref[:, h * dkp : (h + 1) * dkp]
        if use_bias:
            pick = jnp.logical_and(lane % heads == h, lane < 3 * heads)
            q = jnp.concatenate([q, jnp.where(pick, 1.0, 0.0).astype(q.dtype)], axis=1)
        qs.append(q)
    ones_rows = jnp.ones((ONES_ROWS, tq), BF16)

    def step(off, masked):
        if masked:
            kr = lax.broadcasted_iota(jnp.int32, (tq, tq), 0)
            qc = lax.broadcasted_iota(jnp.int32, (tq, tq), 1)
            allowed = (kr // chunk) <= (qc // chunk)
        sts = []
        for h in range(heads):
            slab = h // 2 if paired else h
            k = k_ref[pl.ds(off, tq), slab * dkp : (slab + 1) * dkp]
            if use_bias:
                k = jnp.concatenate([k, kb_ref[pl.ds(off, tq), :]], axis=1)
            sts.append(lax.dot_general(k, qs[h], (((1,), (1,)), ((), ())), preferred_element_type=F32))
        ps, alphas = [], []
        for h in range(heads):
            st = sts[h]
            if masked:
                st = jnp.where(allowed, st, -1e30)
            m_old = m_ref[h, 0:1, :]
            m_new = jnp.maximum(m_old, jnp.max(st, axis=0, keepdims=True))
            m_ref[h, 0:1, :] = m_new
            ps.append(jnp.exp2(st - m_new).astype(BF16))
            alphas.append(jnp.exp2(m_old - m_new))
        for h in range(heads):
            vt_h = jnp.concatenate([vt_ref[h * dv : (h + 1) * dv, pl.ds(off, tq)], ones_rows], axis=0)
            rows = slice(h * dva, (h + 1) * dva)
            acc_ref[rows, :] = alphas[h] * acc_ref[rows, :] + jnp.dot(vt_h, ps[h], preferred_element_type=F32)

    def body(j, carry):
        step(pl.multiple_of(j * tq, tq), False)
        return carry

    lax.fori_loop(0, i, body, 0)
    step(pl.multiple_of(i * tq, tq), True)

    ot = jnp.concatenate(
        [acc_ref[h * dva : h * dva + dv, :] / acc_ref[h * dva + dv : h * dva + dv + 1, :] for h in range(heads)], axis=0
    )
    ot = ot * lax.rsqrt(jnp.mean(ot * ot, axis=0, keepdims=True) + RMS_EPS)
    o_ref[...] = (ot.T * g_ref[...]).astype(o_ref.dtype)


def _attention(q_arr, q_col, k_arr, k_col, vt, gate, gain, *, batch, seq, heads, dk, dv, chunk):
    bias = gate
    assert dk % LANE == 0 or (2 * dk == LANE and heads % 2 == 0)
    tq = min(ATTN_TILE, seq)
    nq = seq // tq
    t = batch * seq
    in_specs = [
        pl.BlockSpec((tq, heads * dk), lambda b, i: (b * nq + i, q_col)),
        pl.BlockSpec((seq, heads * dk), lambda b, i: (b, k_col)),
        pl.BlockSpec((None, heads * dv, seq), lambda b, i: (b, 0, 0)),
    ]
    args = [q_arr, k_arr, vt]
    scratch = []
    if bias is not None:
        in_specs += [pl.BlockSpec((seq, LANE), lambda b, i: (b, 0)), pl.BlockSpec((1, LANE), lambda b, i: (0, 0))]
        args += list(gate)
        assert 3 * heads <= LANE
        scratch.append(pltpu.VMEM((seq, LANE), BF16))
    in_specs.append(pl.BlockSpec((1, heads * dv), lambda b, i: (0, 0)))
    args.append(gain)
    scratch += [pltpu.VMEM((heads, 8, tq), F32), pltpu.VMEM((heads * (dv + ONES_ROWS), tq), F32)]
    return pl.pallas_call(
        functools.partial(_attn_kernel, heads=heads, dk=dk, dv=dv, chunk=chunk, tq=tq, seq=seq, use_bias=bias is not None),
        grid=(batch, nq),
        in_specs=in_specs,
        out_specs=pl.BlockSpec((tq, heads * dv), lambda b, i: (b * nq + i, 0)),
        out_shape=jax.ShapeDtypeStruct((t, heads * dv), BF16),
        scratch_shapes=scratch,
        compiler_params=_cparams("parallel", "arbitrary"),
        name="attention",
    )(*args)


def _mla_prep_kernel(u_ref, cs_ref, sn_ref, qg_ref, kvg_ref, wq_ref, wkv_ref, q_ref, k_ref, v_ref):
    u = u_ref[...].astype(F32)
    cs = cs_ref[...]
    sn = sn_ref[...]

    def rms(x, g):
        return x * lax.rsqrt(jnp.mean(x * x, axis=-1, keepdims=True) + RMS_EPS) * g

    qo = _bdot(rms(u[:, :MLA_Q_RANK], qg_ref[...]), wq_ref[...])
    kvo = _bdot(rms(u[:, MLA_Q_RANK : MLA_Q_RANK + MLA_KV_RANK], kvg_ref[...]), wkv_ref[...])
    c0 = MLA_Q_RANK + MLA_KV_RANK
    kpe = (u[:, c0 : c0 + LANE] * cs + u[:, c0 + LANE : c0 + 2 * LANE] * sn).astype(k_ref.dtype)
    nn = MLA_HEADS * MLA_NOPE
    for h in range(MLA_HEADS):
        a = h * 2 * LANE
        q_ref[:, a : a + LANE] = qo[:, h * LANE : (h + 1) * LANE].astype(q_ref.dtype)
        qpe = qo[:, nn + h * LANE : nn + (h + 1) * LANE] * cs + qo[:, 2 * nn + h * LANE : 2 * nn + (h + 1) * LANE] * sn
        q_ref[:, a + LANE : a + 2 * LANE] = qpe.astype(q_ref.dtype)
        k_ref[:, a : a + LANE] = kvo[:, h * LANE : (h + 1) * LANE].astype(k_ref.dtype)
        k_ref[:, a + LANE : a + 2 * LANE] = kpe
    v_ref[...] = kvo[:, nn:].T.astype(v_ref.dtype)


def _mla_prep(u_mla, cs, sn, qg, kvg, wq, wkv, *, batch, seq):
    t = u_mla.shape[0]
    tm = min(ROW_TILE, seq)
    nt = seq // tm
    row = lambda n: pl.BlockSpec((tm, n), lambda i: (i, 0))
    full = lambda a: pl.BlockSpec(a.shape, lambda i: (0,) * a.ndim)
    wide = MLA_HEADS * 2 * LANE
    dvs = MLA_HEADS * MLA_VD
    return pl.pallas_call(
        _mla_prep_kernel,
        grid=(t // tm,),
        in_specs=[row(u_mla.shape[1]), row(LANE), row(LANE), full(qg), full(kvg), full(wq), full(wkv)],
        out_specs=[row(wide), row(wide), pl.BlockSpec((None, dvs, tm), lambda i: (i // nt, 0, i % nt))],
        out_shape=[
            jax.ShapeDtypeStruct((t, wide), BF16),
            jax.ShapeDtypeStruct((t, wide), BF16),
            jax.ShapeDtypeStruct((batch, dvs, seq), BF16),
        ],
        compiler_params=_cparams("parallel"),
        name="mla_prep",
    )(u_mla, cs, sn, qg, kvg, wq, wkv)


def _gdn_stages(u_ref, t_ref, cw_ref, alog_ref, dtb_ref, ng_ref, o_ref, xs_ref, st_ref, *, tt, c):
    i = pl.program_id(1)
    gw, hd = GROUP_W, GDN_HD

    @pl.when(i == 0)
    def _():
        xs_ref[0:8, :] = jnp.zeros((8, 3 * gw), F32)
        st_ref[...] = jnp.zeros(st_ref.shape, F32)

    xs_ref[8 : 8 + tt, :] = u_ref[:, : 3 * gw].astype(F32)
    cw = cw_ref[...]
    conv = cw[0:1, :] * xs_ref[5 : 5 + tt, :]
    for j in range(1, GDN_CONV):
        conv = conv + cw[j : j + 1, :] * xs_ref[5 + j : 5 + j + tt, :]
    xs_ref[0:8, :] = xs_ref[tt : tt + 8, :]
    qkv = conv * _sigmoid(conv)
    yield

    tail = t_ref[...]
    beta = _sigmoid(tail[:, :LANE])
    g = -jnp.exp(alog_ref[...]) * _softplus(tail[:, LANE:] + dtb_ref[...])

    r, col, same, lower, strict = _chunk_masks(tt, c)
    tri = jnp.where(lower, 1.0, 0.0).astype(BF16)
    ones_blk = jnp.where(same, 1.0, 0.0).astype(BF16)
    gc = _dot_exact_lhs(tri, g)
    gtot = _dot_exact_lhs(ones_blk, g)
    gct = gc.T
    yield

    def l2n(x):
        return x * lax.rsqrt(jnp.sum(x * x, axis=-1, keepdims=True) + 1e-6)

    heads = range(GDN_HEADS)
    bf = lambda a: a.astype(BF16)
    lmats, attns, kbs, vbs, qds, kds, egs = [], [], [], [], [], [], []
    for h in heads:
        qf = l2n(qkv[:, h * hd : (h + 1) * hd]) * (hd**-0.5)
        qh = bf(qf)
        kf = l2n(qkv[:, gw + h * hd : gw + (h + 1) * hd])
        kh = bf(kf)
        vh = qkv[:, 2 * gw + h * hd : 2 * gw + (h + 1) * hd]
        gcol = gc[:, h : h + 1]
        dec = jnp.exp(jnp.minimum(gcol - gct[h : h + 1, :], 0.0))
        bcol = beta[:, h : h + 1]
        kb = kf * bcol
        eg = jnp.exp(gcol)
        lmats.append(jnp.where(strict, _bdot_nt(kb, kh) * dec, 0.0))
        attns.append(bf(jnp.where(lower, _bdot_nt(qh, kh) * dec, 0.0)))
        kbs.append(bf(kb * eg))
        vbs.append(bf(vh * bcol))
        qds.append(bf(qf * eg))
        kds.append(bf(kf * jnp.exp(gtot[:, h : h + 1] - gcol)))
        yield
    tinvs = yield from _tri_inv([-m for m in lmats], _tri_inv_masks(r, col, c))
    tinvs = [bf(t) for t in tinvs]
    uvals = [jnp.dot(t, vb, preferred_element_type=F32) for t, vb in zip(tinvs, vbs)]
    wcums = [bf(jnp.dot(t, kb, preferred_element_type=F32)) for t, kb in zip(tinvs, kbs)]
    yield
    states = [st_ref[h] for h in heads]
    outs = [[] for _ in heads]
    for cc in range(tt // c):
        rs = slice(cc * c, (cc + 1) * c)
        sbs = [bf(s) for s in states]
        vnews = [uvals[h][rs] - jnp.dot(wcums[h][rs], sbs[h], preferred_element_type=F32) for h in heads]
        yield
        for h in heads:
            outs[h].append(jnp.dot(qds[h][rs], sbs[h], preferred_element_type=F32) + _bdot(attns[h][rs, rs], vnews[h]))
            glast = jnp.exp(gtot[cc * c : cc * c + 1, h : h + 1])
            states[h] = states[h] * glast + _bdot_tn(kds[h][rs], vnews[h])
        yield
    for h in heads:
        st_ref[h] = states[h]
        o = jnp.concatenate(outs[h], axis=0)
        o = o * lax.rsqrt(jnp.mean(o * o, axis=-1, keepdims=True) + RMS_EPS) * ng_ref[...]
        z = u_ref[:, 3 * gw + h * hd : 3 * gw + (h + 1) * hd].astype(F32)
        o_ref[:, h * hd : (h + 1) * hd] = (o * (z * _sigmoid(z))).astype(o_ref.dtype)


def _rwkv_stages(u_ref, mu_ref, w0_ref, w2_ref, a0_ref, a2_ref, g2_ref, kk_ref, ka_ref, rk_ref, lng_ref, lnb_ref,
                 e_ref, o_ref, xs_ref, st_ref, *, tt, c):
    i = pl.program_id(1)
    gw, hd = GROUP_W, RWKV_HD

    @pl.when(i == 0)
    def _():
        xs_ref[0:8, :] = jnp.zeros((8, xs_ref.shape[1]), F32)
        st_ref[...] = jnp.zeros(st_ref.shape, F32)

    u = u_ref[...].astype(F32)
    xs_ref[8 : 8 + tt, :] = u
    prev = xs_ref[7 : 7 + tt, :]
    xs_ref[0:8, :] = xs_ref[tt : tt + 8, :]
    x = u + mu_ref[...] * (prev - u)
    rr = x[:, :gw]
    k = x[:, gw : 2 * gw]
    v = x[:, 2 * gw : 3 * gw]
    wa = x[:, 3 * gw : 3 * gw + LANE]
    glo = x[:, 3 * gw + LANE :]
    w = -_softplus(-(w0_ref[...] + _bdot(jnp.tanh(wa), w2_ref[...]))) - 0.5
    ld = -jnp.exp(w)
    gate_a = _sigmoid(a0_ref[...] + _bdot(wa, a2_ref[...]))
    gate_g = _bdot(_sigmoid(glo), g2_ref[...])
    e = e_ref[...]

    def segsum(y):
        hi = y.astype(BF16)
        lo = (y - hi.astype(F32)).astype(BF16)
        return jnp.dot(hi, e, preferred_element_type=F32) + jnp.dot(lo, e, preferred_element_type=F32)

    kkr = k * kk_ref[...]
    kk = kkr * lax.rsqrt(segsum(kkr * kkr) + 1e-6)
    k2 = k * (1.0 + (gate_a - 1.0) * ka_ref[...])
    bonus = segsum(rr * k2 * rk_ref[...]) * v
    yield

    r, col, same, lower, strict = _chunk_masks(tt, c)
    tri = jnp.where(lower, 1.0, 0.0).astype(BF16)
    ones_blk = jnp.where(same, 1.0, 0.0).astype(BF16)
    cum = _dot_exact_lhs(tri, ld)
    ctot = _dot_exact_lhs(ones_blk, ld)
    encum = jnp.exp(-cum)
    edec = jnp.exp(ctot - cum)
    kka = kk * gate_a
    rt = rr * jnp.exp(cum)
    at = -kk * jnp.exp(cum - ld)
    bt = kka * encum
    kt = k2 * encum
    bd = kka * edec
    kd = k2 * edec
    pc = jnp.exp(ctot)
    yield

    heads = range(RWKV_HEADS)
    bf = lambda a: a.astype(BF16)
    dot = lambda a, b: jnp.dot(a, b, preferred_element_type=F32)
    dot_nt = lambda a, b: lax.dot_general(a, b, (((1,), (1,)), ((), ())), preferred_element_type=F32)
    dot_tn = lambda a, b: lax.dot_general(a, b, (((0,), (0,)), ((), ())), preferred_element_type=F32)
    at_b, rt_b, bt_b, kt_b, v_b, bd_b, kd_b = bf(at), bf(rt), bf(bt), bf(kt), bf(v), bf(bd), bf(kd)
    sls = [slice(h * hd, (h + 1) * hd) for h in heads]
    strict_f = jnp.where(strict, 1.0, 0.0).astype(F32)
    lower_f = jnp.where(lower, 1.0, 0.0).astype(F32)
    mabs = [dot_nt(at_b[:, sl], bt_b[:, sl]) * strict_f for sl in sls]
    yield
    maks = [bf(dot_nt(at_b[:, sl], kt_b[:, sl]) * strict_f) for sl in sls]
    yield
    arbs = [bf(dot_nt(rt_b[:, sl], bt_b[:, sl]) * lower_f) for sl in sls]
    yield
    arks = [bf(dot_nt(rt_b[:, sl], kt_b[:, sl]) * lower_f) for sl in sls]
    yield
    tinvs = yield from _tri_inv(mabs, _tri_inv_masks(r, col, c))
    tinvs = [bf(t) for t in tinvs]
    wmats = [bf(dot(t, at_b[:, sl])) for t, sl in zip(tinvs, sls)]
    mkvs = [bf(dot(m, v_b[:, sl])) for m, sl in zip(maks, sls)]
    yield
    umats = [dot(t, mkv) for t, mkv in zip(tinvs, mkvs)]
    yconsts = [dot(m, v_b[:, sl]) for m, sl in zip(arks, sls)]
    yield
    states = [st_ref[h] for h in heads]
    outs = [[] for _ in heads]
    for cc in range(tt // c):
        rs = slice(cc * c, (cc + 1) * c)
        sbs = [bf(s) for s in states]
        sas = [dot_nt(wmats[h][rs], sbs[h]) + umats[h][rs] for h in heads]
        yield
        for h in heads:
            sab = bf(sas[h])
            outs[h].append(dot_nt(rt_b[rs, sls[h]], sbs[h]) + dot(arbs[h][rs, rs], sab) + yconsts[h][rs])
            states[h] = (states[h] * pc[cc * c : cc * c + 1, sls[h]] + dot_tn(sab, bd_b[rs, sls[h]])
                         + dot_tn(v_b[rs, sls[h]], kd_b[rs, sls[h]]))
        yield
    for h in heads:
        st_ref[h] = states[h]
    y = jnp.concatenate([jnp.concatenate(o, axis=0) for o in outs], axis=-1)
    mean = segsum(y) * (1.0 / hd)
    d = y - mean
    var = segsum(d * d) * (1.0 / hd)
    yn = d * lax.rsqrt(var + RWKV_GN_EPS) * lng_ref[...] + lnb_ref[...]
    o_ref[...] = ((yn + bonus) * gate_g).astype(o_ref.dtype)


def _scan_mixers_kernel(*refs, n_rwkv_in, n_gdn_in, tt, c):
    rwkv_in = refs[:n_rwkv_in]
    gdn_in = refs[n_rwkv_in : n_rwkv_in + n_gdn_in]
    o_rwkv, o_gdn, xs_rwkv, st_rwkv, xs_gdn, st_gdn = refs[n_rwkv_in + n_gdn_in :]
    _interleave(
        _rwkv_stages(*rwkv_in, o_rwkv, xs_rwkv, st_rwkv, tt=tt, c=c),
        _gdn_stages(*gdn_in, o_gdn, xs_gdn, st_gdn, tt=tt, c=c),
    )


def _scan_mixers(rwkv_args, gdn_args, *, batch, seq):
    tt = min(SCAN_TILE, seq)
    nt = seq // tt
    t = batch * seq
    tile = lambda a: pl.BlockSpec((tt, a.shape[1]), lambda b, i: (b * nt + i, 0))
    full = lambda a: pl.BlockSpec(a.shape, lambda b, i: (0,) * a.ndim)
    u_rwkv, u_gdn, gdn_tail = rwkv_args[0], gdn_args[0], gdn_args[1]
    in_specs = [tile(u_rwkv)] + [full(p) for p in rwkv_args[1:]] + [tile(u_gdn), tile(gdn_tail)] + [full(p) for p in gdn_args[2:]]
    out = pl.BlockSpec((tt, GROUP_W), lambda b, i: (b * nt + i, 0))
    return pl.pallas_call(
        functools.partial(_scan_mixers_kernel, n_rwkv_in=len(rwkv_args), n_gdn_in=len(gdn_args), tt=tt, c=CHUNK),
        grid=(batch, nt),
        in_specs=in_specs,
        out_specs=[out, out],
        out_shape=[jax.ShapeDtypeStruct((t, GROUP_W), BF16)] * 2,
        scratch_shapes=[
            pltpu.VMEM((tt + 8, u_rwkv.shape[1]), F32),
            pltpu.VMEM((RWKV_HEADS, RWKV_HD, RWKV_HD), F32),
            pltpu.VMEM((tt + 8, 3 * GROUP_W), F32),
            pltpu.VMEM((GDN_HEADS, GDN_HD, GDN_HD), F32),
        ],
        compiler_params=_cparams("parallel", "arbitrary"),
        name="rwkv7_gdn",
    )(*rwkv_args, *gdn_args)


def _layer_norm(h, g, b):
    mu = jnp.mean(h, axis=-1, keepdims=True)
    d = h - mu
    var = jnp.mean(d * d, axis=-1, keepdims=True)
    return d * lax.rsqrt(var + LN_EPS) * g + b


def _route_rows(lg):
    neg = -1e30
    lane = lax.broadcasted_iota(jnp.int32, lg.shape, 1)
    lane_f = lane.astype(F32)
    first = lambda hit: jnp.min(jnp.where(hit, lane_f, float(LANE)), axis=-1, keepdims=True)
    is_g = lane < N_GROUPS
    gl = jnp.where(is_g, lg, neg)
    gmax = jnp.max(gl, axis=-1, keepdims=True)
    p_grp = 1.0 / jnp.sum(jnp.where(is_g, jnp.exp(gl - gmax), 0.0), axis=-1, keepdims=True)
    grp = first(gl == gmax)
    lo = N_GROUPS + grp * EXP_PER_GROUP
    el = jnp.where(jnp.logical_and(lane_f >= lo, lane_f < lo + EXP_PER_GROUP), lg, neg)
    v1 = jnp.max(el, axis=-1, keepdims=True)
    i1 = first(el == v1)
    el2 = jnp.where(lane_f == i1, neg, el)
    v2 = jnp.max(el2, axis=-1, keepdims=True)
    i2 = first(el2 == v2)
    r = jnp.exp(v2 - v1)
    g1 = p_grp / (1.0 + r)
    out = jnp.where(lane == 0, i1 - N_GROUPS, 0.0)
    out = jnp.where(lane == 1, i2 - N_GROUPS, out)
    out = jnp.where(lane == 2, g1, out)
    return jnp.where(lane == 3, g1 * r, out)


def _outproj_kernel(yf_ref, ym_ref, yr_ref, yg_ref, w_ref, x_ref, g_ref, b_ref, wrh_ref, wrl_ref, rb_ref, tril_ref,
                    xo_ref, xb_ref, lg_ref, cnt_ref, *, alpha):
    gw = GROUP_W
    acc = jnp.dot(yf_ref[...], w_ref[0:gw, :], preferred_element_type=F32)
    acc = acc + jnp.dot(ym_ref[...], w_ref[gw : 2 * gw, :], preferred_element_type=F32)
    acc = acc + jnp.dot(yr_ref[...], w_ref[2 * gw : 3 * gw, :], preferred_element_type=F32)
    acc = acc + jnp.dot(yg_ref[...], w_ref[3 * gw : 4 * gw, :], preferred_element_type=F32)
    xn = _layer_norm(alpha * x_ref[...] + acc, g_ref[...], b_ref[...])
    xo_ref[...] = xn
    xh = xn.astype(BF16)
    xb_ref[...] = xh
    xl = (xn - xh.astype(F32)).astype(BF16)
    wrh = wrh_ref[...]
    d = lambda a, b: jnp.dot(a, b, preferred_element_type=F32)
    routed = _route_rows(d(xh, wrh) + d(xl, wrh) + d(xh, wrl_ref[...]) + rb_ref[...])

    @pl.when(pl.program_id(0) == 0)
    def _():
        cnt_ref[...] = jnp.zeros(cnt_ref.shape, F32)

    lane = lax.broadcasted_iota(jnp.int32, routed.shape, 1).astype(F32)
    oh = [jnp.where(lane == routed[:, k : k + 1], 1.0, 0.0) for k in range(TOP_K)]
    both = oh[0] + oh[1]
    before = jnp.dot(tril_ref[...], both.astype(BF16), preferred_element_type=F32) + cnt_ref[0:1, :]
    for k in range(TOP_K):
        rank = jnp.sum(before * oh[k], axis=-1, keepdims=True)
        routed = jnp.where(lane == 2 * TOP_K + k, rank, routed)
    cnt_ref[...] = cnt_ref[...] + jnp.sum(both, axis=0, keepdims=True)
    lg_ref[...] = routed


def _outproj(ys, w_out, x, ln_g, ln_b, wr_hi, wr_lo, r_bias, alpha):
    t, d = x.shape
    tm = min(ROW_TILE, t)
    row = lambda n: pl.BlockSpec((tm, n), lambda i: (i, 0))
    full = lambda a: pl.BlockSpec(a.shape, lambda i: (0,) * a.ndim)
    idx = jnp.arange(tm)
    tril = (idx[None, :] < idx[:, None]).astype(BF16)
    return pl.pallas_call(
        functools.partial(_outproj_kernel, alpha=alpha),
        grid=(t // tm,),
        in_specs=[row(GROUP_W)] * 4
        + [full(w_out), row(d), full(ln_g), full(ln_b), full(wr_hi), full(wr_lo), full(r_bias), full(tril)],
        out_specs=[row(d), row(d), row(LANE), pl.BlockSpec((8, LANE), lambda i: (0, 0))],
        out_shape=[
            jax.ShapeDtypeStruct((t, d), F32),
            jax.ShapeDtypeStruct((t, d), BF16),
            jax.ShapeDtypeStruct((t, LANE), F32),
            jax.ShapeDtypeStruct((8, LANE), F32),
        ],
        compiler_params=_cparams("arbitrary"),
        name="outproj_ln_router",
    )(*ys, w_out, x, ln_g, ln_b, wr_hi, wr_lo, r_bias, tril)


def _expert_kernel(be_ref, x_ref, wg_ref, wu_ref, wd_ref, o_ref, wg_s, wu_s, wd_s):
    i = pl.program_id(0)

    @pl.when(jnp.logical_or(i == 0, be_ref[i] != be_ref[jnp.maximum(i - 1, 0)]))
    def _():
        wg_s[...] = wg_ref[...].astype(BF16)
        wu_s[...] = wu_ref[...].astype(BF16)
        wd_s[...] = wd_ref[...].astype(BF16)

    x = x_ref[...]
    a = jnp.dot(x, wg_s[...], preferred_element_type=F32)
    b = jnp.dot(x, wu_s[...], preferred_element_type=F32)
    hmid = (a * _sigmoid(a) * b).astype(BF16)
    o_ref[...] = jnp.dot(hmid, wd_s[...], preferred_element_type=F32).astype(o_ref.dtype)


def _experts(block_weight, xs, w_gate, w_up, w_down):
    n_slots, d = xs.shape
    tb = MOE_TILE
    de = w_gate.shape[-1]
    grid_spec = pltpu.PrefetchScalarGridSpec(
        num_scalar_prefetch=1,
        grid=(n_slots // tb,),
        in_specs=[
            pl.BlockSpec((tb, d), lambda i, be: (i, 0)),
            pl.BlockSpec((None, d, de), lambda i, be: (be[i], 0, 0)),
            pl.BlockSpec((None, d, de), lambda i, be: (be[i], 0, 0)),
            pl.BlockSpec((None, de, d), lambda i, be: (be[i], 0, 0)),
        ],
        out_specs=pl.BlockSpec((tb, d), lambda i, be: (i, 0)),
        scratch_shapes=[pltpu.VMEM((d, de), BF16), pltpu.VMEM((d, de), BF16), pltpu.VMEM((de, d), BF16)],
    )
    return pl.pallas_call(
        _expert_kernel,
        grid_spec=grid_spec,
        out_shape=jax.ShapeDtypeStruct((n_slots, d), BF16),
        compiler_params=_cparams("arbitrary"),
        name="experts",
    )(block_weight, xs, w_gate, w_up, w_down)


def _ln2_kernel(x_ref, y1_ref, y2_ref, r_ref, g_ref, b_ref, xo_ref, xb_ref, *, alpha):
    routed = r_ref[...]
    moe = routed[:, TOP_K : TOP_K + 1] * y1_ref[...].astype(F32) + routed[:, TOP_K + 1 : TOP_K + 2] * y2_ref[...].astype(F32)
    xn = _layer_norm(alpha * x_ref[...] + moe, g_ref[...], b_ref[...])
    xo_ref[...] = xn
    xb_ref[...] = xn.astype(BF16)


def _ln2(x, y1, y2, routed, g, b, alpha):
    t, d = x.shape
    tm = min(ROW_TILE, t)
    row = lambda n: pl.BlockSpec((tm, n), lambda i: (i, 0))
    full = lambda a: pl.BlockSpec(a.shape, lambda i: (0,) * a.ndim)
    return pl.pallas_call(
        functools.partial(_ln2_kernel, alpha=alpha),
        grid=(t // tm,),
        in_specs=[row(d), row(d), row(d), row(LANE), full(g), full(b)],
        out_specs=[row(d), row(d)],
        out_shape=[jax.ShapeDtypeStruct((t, d), F32), jax.ShapeDtypeStruct((t, d), BF16)],
        compiler_params=_cparams("parallel"),
        name="residual_ln2",
    )(x, y1, y2, routed, g, b)


def _pad_cols(w, n):
    return jnp.pad(w, [(0, 0)] * (w.ndim - 1) + [(0, n - w.shape[-1])])


def _rot_half_cols(w):
    half = w.shape[-1] // 2
    return jnp.concatenate([-w[..., half:], w[..., :half]], axis=-1)


def _prep_weights(p):
    w_in = p["w_in"]
    c_fox = 3 * GROUP_W + FOX_HEADS
    c_mla = MLA_Q_RANK + MLA_KV_RANK + MLA_ROPE
    c_rwkv = 3 * GROUP_W + 2 * 64 + 128
    o_mla = c_fox
    o_rwkv = o_mla + c_mla
    o_gdn = o_rwkv + c_rwkv
    row = lambda a: a[:, None, :].astype(F32)

    w_fox = jnp.concatenate(
        [w_in[..., :GROUP_W] * (FOX_HD**-0.5 * LOG2E), w_in[..., GROUP_W : 3 * GROUP_W], _pad_cols(w_in[..., 3 * GROUP_W : c_fox], LANE)],
        axis=-1,
    )
    kpe_w = w_in[..., o_mla + MLA_Q_RANK + MLA_KV_RANK : o_mla + c_mla]
    w_mla = jnp.concatenate(
        [w_in[..., o_mla : o_mla + MLA_Q_RANK + MLA_KV_RANK], _pad_cols(kpe_w, LANE), _pad_cols(_rot_half_cols(kpe_w), LANE)],
        axis=-1,
    )
    w_rwkv = w_in[..., o_rwkv:o_gdn]
    g0 = o_gdn + 4 * GROUP_W
    w_gdn = jnp.concatenate(
        [w_in[..., o_gdn:g0], _pad_cols(w_in[..., g0 : g0 + GDN_HEADS], LANE), _pad_cols(w_in[..., g0 + GDN_HEADS :], LANE)],
        axis=-1,
    )

    nl = w_in.shape[0]
    scale = (MLA_NOPE + MLA_ROPE) ** -0.5 * LOG2E
    wq = p["mla_w_uq"].reshape(nl, MLA_Q_RANK, MLA_HEADS, MLA_NOPE + MLA_ROPE) * scale
    wq_nope = wq[..., :MLA_NOPE].reshape(nl, MLA_Q_RANK, -1)
    wq_pe = wq[..., MLA_NOPE:]
    wq_p = jnp.concatenate(
        [wq_nope, _pad_cols(wq_pe, LANE).reshape(nl, MLA_Q_RANK, -1), _pad_cols(_rot_half_cols(wq_pe), LANE).reshape(nl, MLA_Q_RANK, -1)],
        axis=-1,
    )
    wkv = p["mla_w_ukv"].reshape(nl, MLA_KV_RANK, MLA_HEADS, MLA_NOPE + MLA_VD)
    wkv_p = jnp.concatenate([wkv[..., :MLA_NOPE].reshape(nl, MLA_KV_RANK, -1), wkv[..., MLA_NOPE:].reshape(nl, MLA_KV_RANK, -1)], axis=-1)

    zeros64 = jnp.zeros((nl, 64, GROUP_W), F32)
    w_router = _pad_cols(jnp.concatenate([p["moe_w_grp"], p["moe_w_exp"]], axis=-1), LANE)
    wr_hi = w_router.astype(BF16)
    wr_lo = (w_router - wr_hi.astype(F32)).astype(BF16)
    return dict(
        w_fox=w_fox.astype(BF16), w_mla=w_mla.astype(BF16), w_rwkv=w_rwkv.astype(BF16), w_gdn=w_gdn.astype(BF16),
        layer=jnp.arange(nl, dtype=jnp.int32),
        fox_b_f=row(_pad_cols(p["fox_b_f"], LANE)), fox_out_g=row(p["fox_out_g"]),
        mla_qg=row(p["mla_q_norm_g"]), mla_kvg=row(p["mla_kv_norm_g"]), mla_wq=wq_p.astype(BF16), mla_wkv=wkv_p.astype(BF16),
        mla_out_g=row(p["mla_out_g"]),
        rwkv_mu=row(p["rwkv_mu"]), rwkv_w0=row(p["rwkv_w0"]),
        rwkv_w2=jnp.concatenate([p["rwkv_w2"], zeros64], axis=1).astype(BF16),
        rwkv_a0=row(p["rwkv_a0"]), rwkv_a2=jnp.concatenate([zeros64, p["rwkv_a2"]], axis=1).astype(BF16),
        rwkv_g2=p["rwkv_g2"].astype(BF16), rwkv_k_k=row(p["rwkv_k_k"]), rwkv_k_a=row(p["rwkv_k_a"]),
        rwkv_r_k=row(p["rwkv_r_k"]), rwkv_ln_g=row(p["rwkv_ln_g"]), rwkv_ln_b=row(p["rwkv_ln_b"]),
        gdn_conv_w=p["gdn_conv_w"].astype(F32), gdn_a_log=row(_pad_cols(p["gdn_a_log"], LANE)),
        gdn_dt_bias=row(_pad_cols(p["gdn_dt_bias"], LANE)), gdn_norm_g=row(p["gdn_norm_g"]),
        w_out=p["w_out"].astype(BF16), ln1_g=row(p["ln1_g"]), ln1_b=row(p["ln1_b"]),
        wr_hi=wr_hi, wr_lo=wr_lo, r_bias=row(_pad_cols(jnp.concatenate([p["moe_b_grp"], p["moe_b_exp"]], axis=-1), LANE)),
        ln2_g=row(p["ln2_g"]), ln2_b=row(p["ln2_b"]),
    )


def _route(routed, counts, tb):
    t = routed.shape[0]
    a = t * TOP_K
    n_blocks = (a + N_EXPERTS * (tb - 1) + tb - 1) // tb
    n_slots = n_blocks * tb
    expert = routed[:, :TOP_K].astype(jnp.int32)
    rank = routed[:, 2 * TOP_K : 3 * TOP_K].astype(jnp.int32)
    padded = (counts + tb - 1) // tb * tb
    pend = jnp.cumsum(padded)
    pstart = pend - padded
    ids = jnp.arange(N_EXPERTS, dtype=jnp.int32)
    slot_of_assignment = jnp.sum(jnp.where(expert[..., None] == ids, pstart, 0), axis=-1) + rank
    block_start = jnp.arange(n_blocks, dtype=jnp.int32) * tb
    block_expert = jnp.minimum(jnp.sum(block_start[:, None] >= pend, axis=-1), N_EXPERTS - 1).astype(jnp.int32)
    fill_end = jnp.cumsum(padded - counts)
    filler = jnp.arange(n_slots - a, dtype=jnp.int32)
    fill_key = jnp.sum(filler[:, None] >= fill_end, axis=-1).astype(jnp.int32)
    keys = jnp.concatenate([expert.reshape(a), fill_key])
    vals = jnp.concatenate([jnp.arange(a, dtype=jnp.int32) // TOP_K, filler % t])
    _, token_of_slot = lax.sort((keys, vals), num_keys=1, is_stable=True)
    return token_of_slot, block_expert, slot_of_assignment


def _layer(x, xb, cs, sn, w, moe_w, seg_ones, *, batch, seq, alpha):
    u_fox, fox_tail = _inproj(xb, w["w_fox"], LANE)
    (u_mla,) = _inproj(xb, w["w_mla"], 0)
    (u_rwkv,) = _inproj(xb, w["w_rwkv"], 0)
    u_gdn, gdn_tail = _inproj(xb, w["w_gdn"], 2 * LANE)

    vt_fox = u_fox[:, 2 * GROUP_W :].reshape(batch, seq, GROUP_W).transpose(0, 2, 1)
    y_fox = _attention(u_fox, 0, u_fox, 1, vt_fox, (fox_tail, w["fox_b_f"]), w["fox_out_g"], batch=batch, seq=seq,
                       heads=FOX_HEADS, dk=FOX_HD, dv=FOX_HD, chunk=1)

    q_mla, k_mla, vt_mla = _mla_prep(u_mla, cs, sn, w["mla_qg"], w["mla_kvg"], w["mla_wq"], w["mla_wkv"],
                                     batch=batch, seq=seq)
    y_mla = _attention(q_mla, 0, k_mla, 0, vt_mla, None, w["mla_out_g"], batch=batch, seq=seq,
                       heads=MLA_HEADS, dk=2 * LANE, dv=MLA_VD, chunk=CHUNK)

    y_rwkv, y_gdn = _scan_mixers(
        (u_rwkv, w["rwkv_mu"], w["rwkv_w0"], w["rwkv_w2"], w["rwkv_a0"], w["rwkv_a2"], w["rwkv_g2"],
         w["rwkv_k_k"], w["rwkv_k_a"], w["rwkv_r_k"], w["rwkv_ln_g"], w["rwkv_ln_b"], seg_ones),
        (u_gdn, gdn_tail, w["gdn_conv_w"], w["gdn_a_log"], w["gdn_dt_bias"], w["gdn_norm_g"]),
        batch=batch, seq=seq)

    x1, x1b, routed, counts = _outproj((y_fox, y_mla, y_rwkv, y_gdn), w["w_out"], x, w["ln1_g"], w["ln1_b"],
                                       w["wr_hi"], w["wr_lo"], w["r_bias"], alpha)

    token_of_slot, block_expert, slot_of_assignment = _route(routed, counts[0, :N_EXPERTS].astype(jnp.int32), MOE_TILE)
    y_slots = _experts(block_expert + w["layer"] * N_EXPERTS, x1b[token_of_slot], *moe_w)
    return _ln2(x1, y_slots[slot_of_assignment[:, 0]], y_slots[slot_of_assignment[:, 1]], routed,
                w["ln2_g"], w["ln2_b"], alpha)


def kernel(x, positions, w_in, fox_b_f, fox_out_g, mla_q_norm_g, mla_kv_norm_g, mla_w_uq, mla_w_ukv, mla_out_g, rwkv_mu, rwkv_w0, rwkv_w2, rwkv_a0, rwkv_a2, rwkv_g2, rwkv_k_k, rwkv_k_a, rwkv_r_k, rwkv_ln_g, rwkv_ln_b, gdn_conv_w, gdn_a_log, gdn_dt_bias, gdn_norm_g, w_out, ln1_g, ln1_b, moe_w_grp, moe_b_grp, moe_w_exp, moe_b_exp, moe_w_gate, moe_w_up, moe_w_down, ln2_g, ln2_b):
    batch, seq, d = x.shape
    depth = w_in.shape[0]
    alpha = (2 * depth) ** 0.25
    params = dict(
        w_in=w_in, fox_b_f=fox_b_f, fox_out_g=fox_out_g, mla_q_norm_g=mla_q_norm_g, mla_kv_norm_g=mla_kv_norm_g,
        mla_w_uq=mla_w_uq, mla_w_ukv=mla_w_ukv, mla_out_g=mla_out_g, rwkv_mu=rwkv_mu, rwkv_w0=rwkv_w0, rwkv_w2=rwkv_w2,
        rwkv_a0=rwkv_a0, rwkv_a2=rwkv_a2, rwkv_g2=rwkv_g2, rwkv_k_k=rwkv_k_k, rwkv_k_a=rwkv_k_a, rwkv_r_k=rwkv_r_k,
        rwkv_ln_g=rwkv_ln_g, rwkv_ln_b=rwkv_ln_b, gdn_conv_w=gdn_conv_w, gdn_a_log=gdn_a_log, gdn_dt_bias=gdn_dt_bias,
        gdn_norm_g=gdn_norm_g, w_out=w_out, ln1_g=ln1_g, ln1_b=ln1_b, moe_w_grp=moe_w_grp, moe_b_grp=moe_b_grp,
        moe_w_exp=moe_w_exp, moe_b_exp=moe_b_exp, moe_w_gate=moe_w_gate, moe_w_up=moe_w_up, moe_w_down=moe_w_down,
        ln2_g=ln2_g, ln2_b=ln2_b,
    )
    weights = _prep_weights(params)

    half = MLA_ROPE // 2
    inv_freq = ROPE_THETA ** (-jnp.arange(half, dtype=F32) / half)
    ang = positions.astype(F32).reshape(batch * seq, 1) * inv_freq
    zpad = jnp.zeros((batch * seq, LANE - MLA_ROPE), F32)
    cs = jnp.concatenate([jnp.cos(ang), jnp.cos(ang), zpad], axis=-1)
    sn = jnp.concatenate([jnp.sin(ang), jnp.sin(ang), zpad], axis=-1)

    xf = x.reshape(batch * seq, d).astype(F32)

    stack = lambda a: a.reshape((-1,) + a.shape[2:])
    moe_w = (stack(moe_w_gate), stack(moe_w_up), stack(moe_w_down))
    seg = jnp.arange(GROUP_W) // RWKV_HD
    seg_ones = (seg[:, None] == seg[None, :]).astype(BF16)

    def body(carry, w):
        xc, xcb = carry
        return _layer(xc, xcb, cs, sn, w, moe_w, seg_ones, batch=batch, seq=seq, alpha=alpha), None

    (xf, _), _ = lax.scan(body, (xf, xf.astype(BF16)), weights)
    return xf.reshape(batch, seq, d).astype(x.dtype)
```

```python
import functools
import math

import jax
import jax.numpy as jnp
from jax import lax
from jax.experimental import pallas as pl
from jax.experimental.pallas import tpu as pltpu

F32 = jnp.float32
BF16 = jnp.bfloat16

D_MODEL = 2048
GROUP_W = 512
FOX_HD, FOX_HEADS = 64, 8
MLA_HEADS, MLA_NOPE, MLA_ROPE, MLA_VD = 4, 128, 64, 128
MLA_Q_RANK, MLA_KV_RANK = 384, 128
ROPE_THETA = 10000.0
RWKV_HD, RWKV_HEADS = 64, 8
RWKV_GN_EPS = 64e-5
GDN_HD, GDN_HEADS, GDN_CONV = 128, 4, 4
N_GROUPS, EXP_PER_GROUP, TOP_K, D_EXPERT = 4, 8, 2, 512
N_EXPERTS = N_GROUPS * EXP_PER_GROUP
CHUNK = 64
LN_EPS = 1e-5
RMS_EPS = 1e-6
LOG2E = math.log2(math.e)

LANE = 128
VMEM_LIMIT_BYTES = 56 * 1024 * 1024
ROW_TILE = 512
ATTN_TILE = 512
ATTN_KV_TILE = 256
SCAN_TILE = 256
MOE_TILE = 256
ONES_ROWS = 16


def _cparams(*sem):
    return pltpu.CompilerParams(dimension_semantics=sem, vmem_limit_bytes=VMEM_LIMIT_BYTES)


def _bdot(a, b):
    return jnp.dot(a.astype(BF16), b.astype(BF16), preferred_element_type=F32)


def _bdot_nt(a, b):
    return lax.dot_general(a.astype(BF16), b.astype(BF16), (((1,), (1,)), ((), ())), preferred_element_type=F32)


def _bdot_tn(a, b):
    return lax.dot_general(a.astype(BF16), b.astype(BF16), (((0,), (0,)), ((), ())), preferred_element_type=F32)


def _split3(x):
    hi = x.astype(BF16)
    r1 = x - hi.astype(F32)
    mid = r1.astype(BF16)
    lo = (r1 - mid.astype(F32)).astype(BF16)
    return hi, mid, lo


def _dot_exact_lhs(m, x):
    hi, mid, lo = _split3(x)
    d = lambda p: jnp.dot(m, p, preferred_element_type=F32)
    return d(hi) + d(mid) + d(lo)


def _sigmoid(x):
    return 1.0 / (1.0 + jnp.exp(-x))


def _softplus(x):
    return jnp.maximum(x, 0.0) + jnp.log(1.0 + jnp.exp(-jnp.abs(x)))


def _chunk_masks(n, c):
    r = lax.broadcasted_iota(jnp.int32, (n, n), 0)
    col = lax.broadcasted_iota(jnp.int32, (n, n), 1)
    same = (r // c) == (col // c)
    lower = jnp.logical_and(same, col <= r)
    strict = jnp.logical_and(same, col < r)
    return r, col, same, lower, strict


def _tri_inv_masks(r, col, c):
    blk = lambda s: (r // s) == (col // s)
    one = lambda cond: jnp.where(cond, 1.0, 0.0).astype(F32)
    offs = []
    s = 8
    while s < c:
        offs.append(one(jnp.logical_and(blk(2 * s), jnp.logical_not(blk(s)))))
        s *= 2
    return one(r == col), one(blk(8)), offs


def _tri_inv(ms, masks):
    eye, blk8, offs = masks
    bf = lambda a: a.astype(BF16)
    dot = lambda a, b: jnp.dot(a, b, preferred_element_type=F32)
    mdf = [m * blk8 for m in ms]
    mds = [bf(m) for m in mdf]
    xs = [eye + m for m in mdf]
    m2s = [bf(dot(md, md)) for md in mds]
    yield
    xs = [x + dot(m2, bf(x)) for x, m2 in zip(xs, m2s)]
    yield
    m4s = [bf(dot(m2, m2)) for m2 in m2s]
    yield
    xs = [x + dot(m4, bf(x)) for x, m4 in zip(xs, m4s)]
    yield
    for off in offs:
        xbs = [bf(x) for x in xs]
        ts = [bf(dot(xb, bf(m * off))) for xb, m in zip(xbs, ms)]
        yield
        xs = [x + dot(t, xb) for x, t, xb in zip(xs, ts, xbs)]
        yield
    return xs


def _interleave(*stage_generators):
    live = list(stage_generators)
    while live:
        for g in list(live):
            try:
                next(g)
            except StopIteration:
                live.remove(g)


def _inproj_kernel(x_ref, w_ref, o_ref, *tail_refs, tail):
    acc = jnp.dot(x_ref[...], w_ref[...], preferred_element_type=F32)
    n = acc.shape[1]
    if tail:
        o_ref[...] = acc[:, : n - tail].astype(o_ref.dtype)
        tail_refs[0][...] = acc[:, n - tail :]
    else:
        o_ref[...] = acc.astype(o_ref.dtype)


def _inproj(xb, w, tail):
    t, d = xb.shape
    n = w.shape[1]
    tm = min(ROW_TILE, t)
    out_shape = [jax.ShapeDtypeStruct((t, n - tail), BF16)]
    out_specs = [pl.BlockSpec((tm, n - tail), lambda i: (i, 0))]
    if tail:
        out_shape.append(jax.ShapeDtypeStruct((t, tail), F32))
        out_specs.append(pl.BlockSpec((tm, tail), lambda i: (i, 0)))
    return pl.pallas_call(
        functools.partial(_inproj_kernel, tail=tail),
        grid=(t // tm,),
        in_specs=[pl.BlockSpec((tm, d), lambda i: (i, 0)), pl.BlockSpec((d, n), lambda i: (0, 0))],
        out_specs=out_specs,
        out_shape=out_shape,
        compiler_params=_cparams("parallel"),
        name="inproj",
    )(xb, w)


def _attn_kernel(*refs, heads, dk, dv, chunk, tq, tk, seq, use_bias):
    if use_bias:
        q_ref, k_ref, vt_ref, gate_ref, gb_ref, g_ref, o_ref, kb_ref, m_ref, acc_ref = refs
    else:
        q_ref, k_ref, vt_ref, g_ref, o_ref, m_ref, acc_ref = refs
    paired = dk < LANE
    dkp = LANE if paired else dk
    dva = dv + ONES_ROWS
    i = pl.program_id(1)
    lane = lax.broadcasted_iota(jnp.int32, (tq, LANE), 1)

    if use_bias:
        @pl.when(i == 0)
        def _():
            r = lax.broadcasted_iota(jnp.int32, (tk, tk), 0)
            c = lax.broadcasted_iota(jnp.int32, (tk, tk), 1)
            tri = jnp.where(c <= r, 1.0, 0.0).astype(BF16)
            head_lane = lax.broadcasted_iota(jnp.int32, (tk, LANE), 1) < heads
            carry = jnp.zeros((1, LANE), F32)
            for j in range(seq // tk):
                rows = slice(j * tk, (j + 1) * tk)
                log_f = -_softplus(-(gate_ref[rows, :] + gb_ref[...])) * LOG2E
                cum = _dot_exact_lhs(tri, log_f) + carry
                carry = cum[tk - 1 : tk, :]
                hi, mid, lo = (jnp.where(head_lane, p.astype(F32), 0.0) for p in _split3(cum))
                pieces = hi + pltpu.roll(mid, heads, axis=1) + pltpu.roll(lo, 2 * heads, axis=1)
                kb_ref[rows, :] = (-pieces).astype(BF16)

    m_ref[...] = jnp.full(m_ref.shape, -1e30, F32)
    acc_ref[...] = jnp.zeros(acc_ref.shape, F32)

    qs = []
    for h in range(heads):
        if paired:
            slab = q_ref[:, (h // 2) * LANE : (h // 2 + 1) * LANE]
            q = jnp.where(lane // dk == h % 2, slab, jnp.zeros_like(slab))
        else:
            q = q_ref[:, h * dkp : (h + 1) * dkp]
        if use_bias:
            pick = jnp.logical_and(lane % heads == h, lane < 3 * heads)
            q = jnp.concatenate([q, jnp.where(pick, 1.0, 0.0).astype(q.dtype)], axis=1)
        qs.append(q)
    ones_rows = jnp.ones((ONES_ROWS, tk), BF16)

    def step(off, key_shift):
        if key_shift is not None:
            kr = lax.broadcasted_iota(jnp.int32, (tk, tq), 0) + key_shift
            qc = lax.broadcasted_iota(jnp.int32, (tk, tq), 1)
            allowed = (kr // chunk) <= (qc // chunk)
        masked = key_shift is not None
        sts = []
        for h in range(heads):
            slab = h // 2 if paired else h
            k = k_ref[pl.ds(off, tk), slab * dkp : (slab + 1) * dkp]
            if use_bias:
                k = jnp.concatenate([k, kb_ref[pl.ds(off, tk), :]], axis=1)
            sts.append(lax.dot_general(k, qs[h], (((1,), (1,)), ((), ())), preferred_element_type=F32))
        ps, alphas = [], []
        for h in range(heads):
            st = sts[h]
            if masked:
                st = jnp.where(allowed, st, -1e30)
            m_old = m_ref[h, 0:1, :]
            m_new = jnp.maximum(m_old, jnp.max(st, axis=0, keepdims=True))
            m_ref[h, 0:1, :] = m_new
            ps.append(jnp.exp2(st - m_new).astype(BF16))
            alphas.append(jnp.exp2(m_old - m_new))
        for h in range(heads):
            vt_h = jnp.concatenate([vt_ref[h * dv : (h + 1) * dv, pl.ds(off, tk)], ones_rows], axis=0)
            rows = slice(h * dva, (h + 1) * dva)
            acc_ref[rows, :] = alphas[h] * acc_ref[rows, :] + jnp.dot(vt_h, ps[h], preferred_element_type=F32)

    def body(j, carry):
        step(pl.multiple_of(j * tk, tk), None)
        return carry

    lax.fori_loop(0, i * (tq // tk), body, 0)
    for d in range(tq // tk):
        step(pl.multiple_of(i * tq + d * tk, tk), d * tk)

    ot = jnp.concatenate(
        [acc_ref[h * dva : h * dva + dv, :] / acc_ref[h * dva + dv : h * dva + dv + 1, :] for h in range(heads)], axis=0
    )
    ot = ot * lax.rsqrt(jnp.mean(ot * ot, axis=0, keepdims=True) + RMS_EPS)
    o_ref[...] = (ot.T * g_ref[...]).astype(o_ref.dtype)


def _attention(q_arr, q_col, k_arr, k_col, vt, gate, gain, *, batch, seq, heads, dk, dv, chunk):
    bias = gate
    assert dk % LANE == 0 or (2 * dk == LANE and heads % 2 == 0)
    tq = min(ATTN_TILE, seq)
    tk = min(ATTN_KV_TILE, tq)
    nq = seq // tq
    t = batch * seq
    in_specs = [
        pl.BlockSpec((tq, heads * dk), lambda b, i: (b * nq + i, q_col)),
        pl.BlockSpec((seq, heads * dk), lambda b, i: (b, k_col)),
        pl.BlockSpec((None, heads * dv, seq), lambda b, i: (b, 0, 0)),
    ]
    args = [q_arr, k_arr, vt]
    scratch = []
    if bias is not None:
        in_specs += [pl.BlockSpec((seq, LANE), lambda b, i: (b, 0)), pl.BlockSpec((1, LANE), lambda b, i: (0, 0))]
        args += list(gate)
        assert 3 * heads <= LANE
        scratch.append(pltpu.VMEM((seq, LANE), BF16))
    in_specs.append(pl.BlockSpec((1, heads * dv), lambda b, i: (0, 0)))
    args.append(gain)
    scratch += [pltpu.VMEM((heads, 8, tq), F32), pltpu.VMEM((heads * (dv + ONES_ROWS), tq), F32)]
    return pl.pallas_call(
        functools.partial(_attn_kernel, heads=heads, dk=dk, dv=dv, chunk=chunk, tq=tq, tk=tk, seq=seq, use_bias=bias is not None),
        grid=(batch, nq),
        in_specs=in_specs,
        out_specs=pl.BlockSpec((tq, heads * dv), lambda b, i: (b * nq + i, 0)),
        out_shape=jax.ShapeDtypeStruct((t, heads * dv), BF16),
        scratch_shapes=scratch,
        compiler_params=_cparams("parallel", "arbitrary"),
        name="attention",
    )(*args)


def _mla_prep_kernel(u_ref, cs_ref, sn_ref, qg_ref, kvg_ref, wq_ref, wkv_ref, q_ref, k_ref, v_ref):
    u = u_ref[...].astype(F32)
    cs = cs_ref[...]
    sn = sn_ref[...]

    def rms(x, g):
        return x * lax.rsqrt(jnp.mean(x * x, axis=-1, keepdims=True) + RMS_EPS) * g

    qo = _bdot(rms(u[:, :MLA_Q_RANK], qg_ref[...]), wq_ref[...])
    kvo = _bdot(rms(u[:, MLA_Q_RANK : MLA_Q_RANK + MLA_KV_RANK], kvg_ref[...]), wkv_ref[...])
    c0 = MLA_Q_RANK + MLA_KV_RANK
    kpe = (u[:, c0 : c0 + LANE] * cs + u[:, c0 + LANE : c0 + 2 * LANE] * sn).astype(k_ref.dtype)
    nn = MLA_HEADS * MLA_NOPE
    for h in range(MLA_HEADS):
        a = h * 2 * LANE
        q_ref[:, a : a + LANE] = qo[:, h * LANE : (h + 1) * LANE].astype(q_ref.dtype)
        qpe = qo[:, nn + h * LANE : nn + (h + 1) * LANE] * cs + qo[:, 2 * nn + h * LANE : 2 * nn + (h + 1) * LANE] * sn
        q_ref[:, a + LANE : a + 2 * LANE] = qpe.astype(q_ref.dtype)
        k_ref[:, a : a + LANE] = kvo[:, h * LANE : (h + 1) * LANE].astype(k_ref.dtype)
        k_ref[:, a + LANE : a + 2 * LANE] = kpe
    v_ref[...] = kvo[:, nn:].T.astype(v_ref.dtype)


def _mla_prep(u_mla, cs, sn, qg, kvg, wq, wkv, *, batch, seq):
    t = u_mla.shape[0]
    tm = min(ROW_TILE, seq)
    nt = seq // tm
    row = lambda n: pl.BlockSpec((tm, n), lambda i: (i, 0))
    full = lambda a: pl.BlockSpec(a.shape, lambda i: (0,) * a.ndim)
    wide = MLA_HEADS * 2 * LANE
    dvs = MLA_HEADS * MLA_VD
    return pl.pallas_call(
        _mla_prep_kernel,
        grid=(t // tm,),
        in_specs=[row(u_mla.shape[1]), row(LANE), row(LANE), full(qg), full(kvg), full(wq), full(wkv)],
        out_specs=[row(wide), row(wide), pl.BlockSpec((None, dvs, tm), lambda i: (i // nt, 0, i % nt))],
        out_shape=[
            jax.ShapeDtypeStruct((t, wide), BF16),
            jax.ShapeDtypeStruct((t, wide), BF16),
            jax.ShapeDtypeStruct((batch, dvs, seq), BF16),
        ],
        compiler_params=_cparams("parallel"),
        name="mla_prep",
    )(u_mla, cs, sn, qg, kvg, wq, wkv)


def _gdn_stages(u_ref, t_ref, cw_ref, alog_ref, dtb_ref, ng_ref, o_ref, xs_ref, st_ref, *, tt, c):
    i = pl.program_id(1)
    gw, hd = GROUP_W, GDN_HD

    @pl.when(i == 0)
    def _():
        xs_ref[0:8, :] = jnp.zeros((8, 3 * gw), F32)
        st_ref[...] = jnp.zeros(st_ref.shape, F32)

    xs_ref[8 : 8 + tt, :] = u_ref[:, : 3 * gw].astype(F32)
    cw = cw_ref[...]
    conv = cw[0:1, :] * xs_ref[5 : 5 + tt, :]
    for j in range(1, GDN_CONV):
        conv = conv + cw[j : j + 1, :] * xs_ref[5 + j : 5 + j + tt, :]
    xs_ref[0:8, :] = xs_ref[tt : tt + 8, :]
    qkv = conv * _sigmoid(conv)
    yield

    tail = t_ref[...]
    beta = _sigmoid(tail[:, :LANE])
    g = -jnp.exp(alog_ref[...]) * _softplus(tail[:, LANE:] + dtb_ref[...])

    r, col, same, lower, strict = _chunk_masks(tt, c)
    tri = jnp.where(lower, 1.0, 0.0).astype(BF16)
    ones_blk = jnp.where(same, 1.0, 0.0).astype(BF16)
    gc = _dot_exact_lhs(tri, g)
    gtot = _dot_exact_lhs(ones_blk, g)
    gct = gc.T
    yield

    def l2n(x):
        return x * lax.rsqrt(jnp.sum(x * x, axis=-1, keepdims=True) + 1e-6)

    heads = range(GDN_HEADS)
    bf = lambda a: a.astype(BF16)
    lmats, attns, kbs, vbs, qds, kds, egs = [], [], [], [], [], [], []
    for h in heads:
        qf = l2n(qkv[:, h * hd : (h + 1) * hd]) * (hd**-0.5)
        qh = bf(qf)
        kf = l2n(qkv[:, gw + h * hd : gw + (h + 1) * hd])
        kh = bf(kf)
        vh = qkv[:, 2 * gw + h * hd : 2 * gw + (h + 1) * hd]
        gcol = gc[:, h : h + 1]
        dec = jnp.exp(jnp.minimum(gcol - gct[h : h + 1, :], 0.0))
        bcol = beta[:, h : h + 1]
        kb = kf * bcol
        eg = jnp.exp(gcol)
        lmats.append(jnp.where(strict, _bdot_nt(kb, kh) * dec, 0.0))
        attns.append(bf(jnp.where(lower, _bdot_nt(qh, kh) * dec, 0.0)))
        kbs.append(bf(kb * eg))
        vbs.append(bf(vh * bcol))
        qds.append(bf(qf * eg))
        kds.append(bf(kf * jnp.exp(gtot[:, h : h + 1] - gcol)))
        yield
    tinvs = yield from _tri_inv([-m for m in lmats], _tri_inv_masks(r, col, c))
    tinvs = [bf(t) for t in tinvs]
    uvals = [jnp.dot(t, vb, preferred_element_type=F32) for t, vb in zip(tinvs, vbs)]
    wcums = [bf(jnp.dot(t, kb, preferred_element_type=F32)) for t, kb in zip(tinvs, kbs)]
    yield
    states = [st_ref[h] for h in heads]
    outs = [[] for _ in heads]
    for cc in range(tt // c):
        rs = slice(cc * c, (cc + 1) * c)
        sbs = [bf(s) for s in states]
        vnews = [uvals[h][rs] - jnp.dot(wcums[h][rs], sbs[h], preferred_element_type=F32) for h in heads]
        yield
        for h in heads:
            outs[h].append(jnp.dot(qds[h][rs], sbs[h], preferred_element_type=F32) + _bdot(attns[h][rs, rs], vnews[h]))
            glast = jnp.exp(gtot[cc * c : cc * c + 1, h : h + 1])
            states[h] = states[h] * glast + _bdot_tn(kds[h][rs], vnews[h])
        yield
    for h in heads:
        st_ref[h] = states[h]
        o = jnp.concatenate(outs[h], axis=0)
        o = o * lax.rsqrt(jnp.mean(o * o, axis=-1, keepdims=True) + RMS_EPS) * ng_ref[...]
        z = u_ref[:, 3 * gw + h * hd : 3 * gw + (h + 1) * hd].astype(F32)
        o_ref[:, h * hd : (h + 1) * hd] = (o * (z * _sigmoid(z))).astype(o_ref.dtype)


def _rwkv_stages(u_ref, mu_ref, w0_ref, w2_ref, a0_ref, a2_ref, g2_ref, kk_ref, ka_ref, rk_ref, lng_ref, lnb_ref,
                 e_ref, o_ref, xs_ref, st_ref, *, tt, c):
    i = pl.program_id(1)
    gw, hd = GROUP_W, RWKV_HD

    @pl.when(i == 0)
    def _():
        xs_ref[0:8, :] = jnp.zeros((8, xs_ref.shape[1]), F32)
        st_ref[...] = jnp.zeros(st_ref.shape, F32)

    u = u_ref[...].astype(F32)
    xs_ref[8 : 8 + tt, :] = u
    prev = xs_ref[7 : 7 + tt, :]
    xs_ref[0:8, :] = xs_ref[tt : tt + 8, :]
    x = u + mu_ref[...] * (prev - u)
    rr = x[:, :gw]
    k = x[:, gw : 2 * gw]
    v = x[:, 2 * gw : 3 * gw]
    wa = x[:, 3 * gw : 3 * gw + LANE]
    glo = x[:, 3 * gw + LANE :]
    w = -_softplus(-(w0_ref[...] + _bdot(jnp.tanh(wa), w2_ref[...]))) - 0.5
    ld = -jnp.exp(w)
    gate_a = _sigmoid(a0_ref[...] + _bdot(wa, a2_ref[...]))
    gate_g = _bdot(_sigmoid(glo), g2_ref[...])
    e = e_ref[...]

    def segsum(y):
        hi = y.astype(BF16)
        lo = (y - hi.astype(F32)).astype(BF16)
        return jnp.dot(hi, e, preferred_element_type=F32) + jnp.dot(lo, e, preferred_element_type=F32)

    kkr = k * kk_ref[...]
    kk = kkr * lax.rsqrt(segsum(kkr * kkr) + 1e-6)
    k2 = k * (1.0 + (gate_a - 1.0) * ka_ref[...])
    bonus = segsum(rr * k2 * rk_ref[...]) * v
    yield

    r, col, same, lower, strict = _chunk_masks(tt, c)
    tri = jnp.where(lower, 1.0, 0.0).astype(BF16)
    ones_blk = jnp.where(same, 1.0, 0.0).astype(BF16)
    cum = _dot_exact_lhs(tri, ld)
    ctot = _dot_exact_lhs(ones_blk, ld)
    encum = jnp.exp(-cum)
    edec = jnp.exp(ctot - cum)
    kka = kk * gate_a
    rt = rr * jnp.exp(cum)
    at = -kk * jnp.exp(cum - ld)
    bt = kka * encum
    kt = k2 * encum
    bd = kka * edec
    kd = k2 * edec
    pc = jnp.exp(ctot)
    yield

    heads = range(RWKV_HEADS)
    bf = lambda a: a.astype(BF16)
    dot = lambda a, b: jnp.dot(a, b, preferred_element_type=F32)
    dot_nt = lambda a, b: lax.dot_general(a, b, (((1,), (1,)), ((), ())), preferred_element_type=F32)
    dot_tn = lambda a, b: lax.dot_general(a, b, (((0,), (0,)), ((), ())), preferred_element_type=F32)
    at_b, rt_b, bt_b, kt_b, v_b, bd_b, kd_b = bf(at), bf(rt), bf(bt), bf(kt), bf(v), bf(bd), bf(kd)
    sls = [slice(h * hd, (h + 1) * hd) for h in heads]
    strict_f = jnp.where(strict, 1.0, 0.0).astype(F32)
    lower_f = jnp.where(lower, 1.0, 0.0).astype(F32)
    mabs = [dot_nt(at_b[:, sl], bt_b[:, sl]) * strict_f for sl in sls]
    yield
    maks = [bf(dot_nt(at_b[:, sl], kt_b[:, sl]) * strict_f) for sl in sls]
    yield
    arbs = [bf(dot_nt(rt_b[:, sl], bt_b[:, sl]) * lower_f) for sl in sls]
    yield
    arks = [bf(dot_nt(rt_b[:, sl], kt_b[:, sl]) * lower_f) for sl in sls]
    yield
    tinvs = yield from _tri_inv(mabs, _tri_inv_masks(r, col, c))
    tinvs = [bf(t) for t in tinvs]
    wmats = [bf(dot(t, at_b[:, sl])) for t, sl in zip(tinvs, sls)]
    mkvs = [bf(dot(m, v_b[:, sl])) for m, sl in zip(maks, sls)]
    yield
    umats = [dot(t, mkv) for t, mkv in zip(tinvs, mkvs)]
    yconsts = [dot(m, v_b[:, sl]) for m, sl in zip(arks, sls)]
    yield
    states = [st_ref[h] for h in heads]
    outs = [[] for _ in heads]
    for cc in range(tt // c):
        rs = slice(cc * c, (cc + 1) * c)
        sbs = [bf(s) for s in states]
        sas = [dot_nt(wmats[h][rs], sbs[h]) + umats[h][rs] for h in heads]
        yield
        for h in heads:
            sab = bf(sas[h])
            outs[h].append(dot_nt(rt_b[rs, sls[h]], sbs[h]) + dot(arbs[h][rs, rs], sab) + yconsts[h][rs])
            states[h] = (states[h] * pc[cc * c : cc * c + 1, sls[h]] + dot_tn(sab, bd_b[rs, sls[h]])
                         + dot_tn(v_b[rs, sls[h]], kd_b[rs, sls[h]]))
        yield
    for h in heads:
        st_ref[h] = states[h]
    y = jnp.concatenate([jnp.concatenate(o, axis=0) for o in outs], axis=-1)
    mean = segsum(y) * (1.0 / hd)
    d = y - mean
    var = segsum(d * d) * (1.0 / hd)
    yn = d * lax.rsqrt(var + RWKV_GN_EPS) * lng_ref[...] + lnb_ref[...]
    o_ref[...] = ((yn + bonus) * gate_g).astype(o_ref.dtype)


def _scan_mixers_kernel(*refs, n_rwkv_in, n_gdn_in, tt, c):
    rwkv_in = refs[:n_rwkv_in]
    gdn_in = refs[n_rwkv_in : n_rwkv_in + n_gdn_in]
    o_rwkv, o_gdn, xs_rwkv, st_rwkv, xs_gdn, st_gdn = refs[n_rwkv_in + n_gdn_in :]
    _interleave(
        _rwkv_stages(*rwkv_in, o_rwkv, xs_rwkv, st_rwkv, tt=tt, c=c),
        _gdn_stages(*gdn_in, o_gdn, xs_gdn, st_gdn, tt=tt, c=c),
    )


def _scan_mixers(rwkv_args, gdn_args, *, batch, seq):
    tt = min(SCAN_TILE, seq)
    nt = seq // tt
    t = batch * seq
    tile = lambda a: pl.BlockSpec((tt, a.shape[1]), lambda b, i: (b * nt + i, 0))
    full = lambda a: pl.BlockSpec(a.shape, lambda b, i: (0,) * a.ndim)
    u_rwkv, u_gdn, gdn_tail = rwkv_args[0], gdn_args[0], gdn_args[1]
    in_specs = [tile(u_rwkv)] + [full(p) for p in rwkv_args[1:]] + [tile(u_gdn), tile(gdn_tail)] + [full(p) for p in gdn_args[2:]]
    out = pl.BlockSpec((tt, GROUP_W), lambda b, i: (b * nt + i, 0))
    return pl.pallas_call(
        functools.partial(_scan_mixers_kernel, n_rwkv_in=len(rwkv_args), n_gdn_in=len(gdn_args), tt=tt, c=CHUNK),
        grid=(batch, nt),
        in_specs=in_specs,
        out_specs=[out, out],
        out_shape=[jax.ShapeDtypeStruct((t, GROUP_W), BF16)] * 2,
        scratch_shapes=[
            pltpu.VMEM((tt + 8, u_rwkv.shape[1]), F32),
            pltpu.VMEM((RWKV_HEADS, RWKV_HD, RWKV_HD), F32),
            pltpu.VMEM((tt + 8, 3 * GROUP_W), F32),
            pltpu.VMEM((GDN_HEADS, GDN_HD, GDN_HD), F32),
        ],
        compiler_params=_cparams("parallel", "arbitrary"),
        name="rwkv7_gdn",
    )(*rwkv_args, *gdn_args)


def _layer_norm(h, g, b):
    mu = jnp.mean(h, axis=-1, keepdims=True)
    d = h - mu
    var = jnp.mean(d * d, axis=-1, keepdims=True)
    return d * lax.rsqrt(var + LN_EPS) * g + b


def _route_rows(lg):
    neg = -1e30
    lane = lax.broadcasted_iota(jnp.int32, lg.shape, 1)
    lane_f = lane.astype(F32)
    first = lambda hit: jnp.min(jnp.where(hit, lane_f, float(LANE)), axis=-1, keepdims=True)
    is_g = lane < N_GROUPS
    gl = jnp.where(is_g, lg, neg)
    gmax = jnp.max(gl, axis=-1, keepdims=True)
    p_grp = 1.0 / jnp.sum(jnp.where(is_g, jnp.exp(gl - gmax), 0.0), axis=-1, keepdims=True)
    grp = first(gl == gmax)
    lo = N_GROUPS + grp * EXP_PER_GROUP
    el = jnp.where(jnp.logical_and(lane_f >= lo, lane_f < lo + EXP_PER_GROUP), lg, neg)
    v1 = jnp.max(el, axis=-1, keepdims=True)
    i1 = first(el == v1)
    el2 = jnp.where(lane_f == i1, neg, el)
    v2 = jnp.max(el2, axis=-1, keepdims=True)
    i2 = first(el2 == v2)
    r = jnp.exp(v2 - v1)
    g1 = p_grp / (1.0 + r)
    out = jnp.where(lane == 0, i1 - N_GROUPS, 0.0)
    out = jnp.where(lane == 1, i2 - N_GROUPS, out)
    out = jnp.where(lane == 2, g1, out)
    return jnp.where(lane == 3, g1 * r, out)


def _outproj_kernel(yf_ref, ym_ref, yr_ref, yg_ref, w_ref, x_ref, g_ref, b_ref, wrh_ref, wrl_ref, rb_ref, tril_ref,
                    xo_ref, xb_ref, lg_ref, cnt_ref, *, alpha):
    gw = GROUP_W
    acc = jnp.dot(yf_ref[...], w_ref[0:gw, :], preferred_element_type=F32)
    acc = acc + jnp.dot(ym_ref[...], w_ref[gw : 2 * gw, :], preferred_element_type=F32)
    acc = acc + jnp.dot(yr_ref[...], w_ref[2 * gw : 3 * gw, :], preferred_element_type=F32)
    acc = acc + jnp.dot(yg_ref[...], w_ref[3 * gw : 4 * gw, :], preferred_element_type=F32)
    xn = _layer_norm(alpha * x_ref[...] + acc, g_ref[...], b_ref[...])
    xo_ref[...] = xn
    xh = xn.astype(BF16)
    xb_ref[...] = xh
    xl = (xn - xh.astype(F32)).astype(BF16)
    wrh = wrh_ref[...]
    d = lambda a, b: jnp.dot(a, b, preferred_element_type=F32)
    routed = _route_rows(d(xh, wrh) + d(xl, wrh) + d(xh, wrl_ref[...]) + rb_ref[...])

    @pl.when(pl.program_id(0) == 0)
    def _():
        cnt_ref[...] = jnp.zeros(cnt_ref.shape, F32)

    lane = lax.broadcasted_iota(jnp.int32, routed.shape, 1).astype(F32)
    oh = [jnp.where(lane == routed[:, k : k + 1], 1.0, 0.0) for k in range(TOP_K)]
    both = oh[0] + oh[1]
    before = jnp.dot(tril_ref[...], both.astype(BF16), preferred_element_type=F32) + cnt_ref[0:1, :]
    for k in range(TOP_K):
        rank = jnp.sum(before * oh[k], axis=-1, keepdims=True)
        routed = jnp.where(lane == 2 * TOP_K + k, rank, routed)
    cnt_ref[...] = cnt_ref[...] + jnp.sum(both, axis=0, keepdims=True)
    lg_ref[...] = routed


def _outproj(ys, w_out, x, ln_g, ln_b, wr_hi, wr_lo, r_bias, alpha):
    t, d = x.shape
    tm = min(ROW_TILE, t)
    row = lambda n: pl.BlockSpec((tm, n), lambda i: (i, 0))
    full = lambda a: pl.BlockSpec(a.shape, lambda i: (0,) * a.ndim)
    idx = jnp.arange(tm)
    tril = (idx[None, :] < idx[:, None]).astype(BF16)
    return pl.pallas_call(
        functools.partial(_outproj_kernel, alpha=alpha),
        grid=(t // tm,),
        in_specs=[row(GROUP_W)] * 4
        + [full(w_out), row(d), full(ln_g), full(ln_b), full(wr_hi), full(wr_lo), full(r_bias), full(tril)],
        out_specs=[row(d), row(d), row(LANE), pl.BlockSpec((8, LANE), lambda i: (0, 0))],
        out_shape=[
            jax.ShapeDtypeStruct((t, d), F32),
            jax.ShapeDtypeStruct((t, d), BF16),
            jax.ShapeDtypeStruct((t, LANE), F32),
            jax.ShapeDtypeStruct((8, LANE), F32),
        ],
        compiler_params=_cparams("arbitrary"),
        name="outproj_ln_router",
    )(*ys, w_out, x, ln_g, ln_b, wr_hi, wr_lo, r_bias, tril)


def _expert_kernel(be_ref, x_ref, wg_ref, wu_ref, wd_ref, o_ref, wg_s, wu_s, wd_s):
    i = pl.program_id(0)

    @pl.when(jnp.logical_or(i == 0, be_ref[i] != be_ref[jnp.maximum(i - 1, 0)]))
    def _():
        wg_s[...] = wg_ref[...].astype(BF16)
        wu_s[...] = wu_ref[...].astype(BF16)
        wd_s[...] = wd_ref[...].astype(BF16)

    x = x_ref[...]
    a = jnp.dot(x, wg_s[...], preferred_element_type=F32)
    b = jnp.dot(x, wu_s[...], preferred_element_type=F32)
    hmid = (a * _sigmoid(a) * b).astype(BF16)
    o_ref[...] = jnp.dot(hmid, wd_s[...], preferred_element_type=F32).astype(o_ref.dtype)


def _experts(block_weight, xs, w_gate, w_up, w_down):
    n_slots, d = xs.shape
    tb = MOE_TILE
    de = w_gate.shape[-1]
    grid_spec = pltpu.PrefetchScalarGridSpec(
        num_scalar_prefetch=1,
        grid=(n_slots // tb,),
        in_specs=[
            pl.BlockSpec((tb, d), lambda i, be: (i, 0)),
            pl.BlockSpec((None, d, de), lambda i, be: (be[i], 0, 0)),
            pl.BlockSpec((None, d, de), lambda i, be: (be[i], 0, 0)),
            pl.BlockSpec((None, de, d), lambda i, be: (be[i], 0, 0)),
        ],
        out_specs=pl.BlockSpec((tb, d), lambda i, be: (i, 0)),
        scratch_shapes=[pltpu.VMEM((d, de), BF16), pltpu.VMEM((d, de), BF16), pltpu.VMEM((de, d), BF16)],
    )
    return pl.pallas_call(
        _expert_kernel,
        grid_spec=grid_spec,
        out_shape=jax.ShapeDtypeStruct((n_slots, d), BF16),
        compiler_params=_cparams("arbitrary"),
        name="experts",
    )(block_weight, xs, w_gate, w_up, w_down)


def _ln2_kernel(x_ref, y1_ref, y2_ref, r_ref, g_ref, b_ref, xo_ref, xb_ref, *, alpha):
    routed = r_ref[...]
    moe = routed[:, TOP_K : TOP_K + 1] * y1_ref[...].astype(F32) + routed[:, TOP_K + 1 : TOP_K + 2] * y2_ref[...].astype(F32)
    xn = _layer_norm(alpha * x_ref[...] + moe, g_ref[...], b_ref[...])
    xo_ref[...] = xn
    xb_ref[...] = xn.astype(BF16)


def _ln2(x, y1, y2, routed, g, b, alpha):
    t, d = x.shape
    tm = min(ROW_TILE, t)
    row = lambda n: pl.BlockSpec((tm, n), lambda i: (i, 0))
    full = lambda a: pl.BlockSpec(a.shape, lambda i: (0,) * a.ndim)
    return pl.pallas_call(
        functools.partial(_ln2_kernel, alpha=alpha),
        grid=(t // tm,),
        in_specs=[row(d), row(d), row(d), row(LANE), full(g), full(b)],
        out_specs=[row(d), row(d)],
        out_shape=[jax.ShapeDtypeStruct((t, d), F32), jax.ShapeDtypeStruct((t, d), BF16)],
        compiler_params=_cparams("parallel"),
        name="residual_ln2",
    )(x, y1, y2, routed, g, b)


def _pad_cols(w, n):
    return jnp.pad(w, [(0, 0)] * (w.ndim - 1) + [(0, n - w.shape[-1])])


def _rot_half_cols(w):
    half = w.shape[-1] // 2
    return jnp.concatenate([-w[..., half:], w[..., :half]], axis=-1)


def _prep_weights(p):
    w_in = p["w_in"]
    c_fox = 3 * GROUP_W + FOX_HEADS
    c_mla = MLA_Q_RANK + MLA_KV_RANK + MLA_ROPE
    c_rwkv = 3 * GROUP_W + 2 * 64 + 128
    o_mla = c_fox
    o_rwkv = o_mla + c_mla
    o_gdn = o_rwkv + c_rwkv
    row = lambda a: a[:, None, :].astype(F32)

    w_fox = jnp.concatenate(
        [w_in[..., :GROUP_W] * (FOX_HD**-0.5 * LOG2E), w_in[..., GROUP_W : 3 * GROUP_W], _pad_cols(w_in[..., 3 * GROUP_W : c_fox], LANE)],
        axis=-1,
    )
    kpe_w = w_in[..., o_mla + MLA_Q_RANK + MLA_KV_RANK : o_mla + c_mla]
    w_mla = jnp.concatenate(
        [w_in[..., o_mla : o_mla + MLA_Q_RANK + MLA_KV_RANK], _pad_cols(kpe_w, LANE), _pad_cols(_rot_half_cols(kpe_w), LANE)],
        axis=-1,
    )
    w_rwkv = w_in[..., o_rwkv:o_gdn]
    g0 = o_gdn + 4 * GROUP_W
    w_gdn = jnp.concatenate(
        [w_in[..., o_gdn:g0], _pad_cols(w_in[..., g0 : g0 + GDN_HEADS], LANE), _pad_cols(w_in[..., g0 + GDN_HEADS :], LANE)],
        axis=-1,
    )

    nl = w_in.shape[0]
    scale = (MLA_NOPE + MLA_ROPE) ** -0.5 * LOG2E
    wq = p["mla_w_uq"].reshape(nl, MLA_Q_RANK, MLA_HEADS, MLA_NOPE + MLA_ROPE) * scale
    wq_nope = wq[..., :MLA_NOPE].reshape(nl, MLA_Q_RANK, -1)
    wq_pe = wq[..., MLA_NOPE:]
    wq_p = jnp.concatenate(
        [wq_nope, _pad_cols(wq_pe, LANE).reshape(nl, MLA_Q_RANK, -1), _pad_cols(_rot_half_cols(wq_pe), LANE).reshape(nl, MLA_Q_RANK, -1)],
        axis=-1,
    )
    wkv = p["mla_w_ukv"].reshape(nl, MLA_KV_RANK, MLA_HEADS, MLA_NOPE + MLA_VD)
    wkv_p = jnp.concatenate([wkv[..., :MLA_NOPE].reshape(nl, MLA_KV_RANK, -1), wkv[..., MLA_NOPE:].reshape(nl, MLA_KV_RANK, -1)], axis=-1)

    zeros64 = jnp.zeros((nl, 64, GROUP_W), F32)
    w_router = _pad_cols(jnp.concatenate([p["moe_w_grp"], p["moe_w_exp"]], axis=-1), LANE)
    wr_hi = w_router.astype(BF16)
    wr_lo = (w_router - wr_hi.astype(F32)).astype(BF16)
    return dict(
        w_fox=w_fox.astype(BF16), w_mla=w_mla.astype(BF16), w_rwkv=w_rwkv.astype(BF16), w_gdn=w_gdn.astype(BF16),
        layer=jnp.arange(nl, dtype=jnp.int32),
        fox_b_f=row(_pad_cols(p["fox_b_f"], LANE)), fox_out_g=row(p["fox_out_g"]),
        mla_qg=row(p["mla_q_norm_g"]), mla_kvg=row(p["mla_kv_norm_g"]), mla_wq=wq_p.astype(BF16), mla_wkv=wkv_p.astype(BF16),
        mla_out_g=row(p["mla_out_g"]),
        rwkv_mu=row(p["rwkv_mu"]), rwkv_w0=row(p["rwkv_w0"]),
        rwkv_w2=jnp.concatenate([p["rwkv_w2"], zeros64], axis=1).astype(BF16),
        rwkv_a0=row(p["rwkv_a0"]), rwkv_a2=jnp.concatenate([zeros64, p["rwkv_a2"]], axis=1).astype(BF16),
        rwkv_g2=p["rwkv_g2"].astype(BF16), rwkv_k_k=row(p["rwkv_k_k"]), rwkv_k_a=row(p["rwkv_k_a"]),
        rwkv_r_k=row(p["rwkv_r_k"]), rwkv_ln_g=row(p["rwkv_ln_g"]), rwkv_ln_b=row(p["rwkv_ln_b"]),
        gdn_conv_w=p["gdn_conv_w"].astype(F32), gdn_a_log=row(_pad_cols(p["gdn_a_log"], LANE)),
        gdn_dt_bias=row(_pad_cols(p["gdn_dt_bias"], LANE)), gdn_norm_g=row(p["gdn_norm_g"]),
        w_out=p["w_out"].astype(BF16), ln1_g=row(p["ln1_g"]), ln1_b=row(p["ln1_b"]),
        wr_hi=wr_hi, wr_lo=wr_lo, r_bias=row(_pad_cols(jnp.concatenate([p["moe_b_grp"], p["moe_b_exp"]], axis=-1), LANE)),
        ln2_g=row(p["ln2_g"]), ln2_b=row(p["ln2_b"]),
    )


def _route(routed, counts, tb):
    t = routed.shape[0]
    a = t * TOP_K
    n_blocks = (a + N_EXPERTS * (tb - 1) + tb - 1) // tb
    n_slots = n_blocks * tb
    expert = routed[:, :TOP_K].astype(jnp.int32)
    rank = routed[:, 2 * TOP_K : 3 * TOP_K].astype(jnp.int32)
    padded = (counts + tb - 1) // tb * tb
    pend = jnp.cumsum(padded)
    pstart = pend - padded
    ids = jnp.arange(N_EXPERTS, dtype=jnp.int32)
    slot_of_assignment = jnp.sum(jnp.where(expert[..., None] == ids, pstart, 0), axis=-1) + rank
    block_start = jnp.arange(n_blocks, dtype=jnp.int32) * tb
    block_expert = jnp.minimum(jnp.sum(block_start[:, None] >= pend, axis=-1), N_EXPERTS - 1).astype(jnp.int32)
    fill_end = jnp.cumsum(padded - counts)
    filler = jnp.arange(n_slots - a, dtype=jnp.int32)
    fill_key = jnp.sum(filler[:, None] >= fill_end, axis=-1).astype(jnp.int32)
    keys = jnp.concatenate([expert.reshape(a), fill_key])
    vals = jnp.concatenate([jnp.arange(a, dtype=jnp.int32) // TOP_K, filler % t])
    _, token_of_slot = lax.sort((keys, vals), num_keys=1, is_stable=True)
    return token_of_slot, block_expert, slot_of_assignment


def _layer(x, xb, cs, sn, w, moe_w, seg_ones, *, batch, seq, alpha):
    u_fox, fox_tail = _inproj(xb, w["w_fox"], LANE)
    (u_mla,) = _inproj(xb, w["w_mla"], 0)
    (u_rwkv,) = _inproj(xb, w["w_rwkv"], 0)
    u_gdn, gdn_tail = _inproj(xb, w["w_gdn"], 2 * LANE)

    vt_fox = u_fox[:, 2 * GROUP_W :].reshape(batch, seq, GROUP_W).transpose(0, 2, 1)
    y_fox = _attention(u_fox, 0, u_fox, 1, vt_fox, (fox_tail, w["fox_b_f"]), w["fox_out_g"], batch=batch, seq=seq,
                       heads=FOX_HEADS, dk=FOX_HD, dv=FOX_HD, chunk=1)

    q_mla, k_mla, vt_mla = _mla_prep(u_mla, cs, sn, w["mla_qg"], w["mla_kvg"], w["mla_wq"], w["mla_wkv"],
                                     batch=batch, seq=seq)
    y_mla = _attention(q_mla, 0, k_mla, 0, vt_mla, None, w["mla_out_g"], batch=batch, seq=seq,
                       heads=MLA_HEADS, dk=2 * LANE, dv=MLA_VD, chunk=CHUNK)

    y_rwkv, y_gdn = _scan_mixers(
        (u_rwkv, w["rwkv_mu"], w["rwkv_w0"], w["rwkv_w2"], w["rwkv_a0"], w["rwkv_a2"], w["rwkv_g2"],
         w["rwkv_k_k"], w["rwkv_k_a"], w["rwkv_r_k"], w["rwkv_ln_g"], w["rwkv_ln_b"], seg_ones),
        (u_gdn, gdn_tail, w["gdn_conv_w"], w["gdn_a_log"], w["gdn_dt_bias"], w["gdn_norm_g"]),
        batch=batch, seq=seq)

    x1, x1b, routed, counts = _outproj((y_fox, y_mla, y_rwkv, y_gdn), w["w_out"], x, w["ln1_g"], w["ln1_b"],
                                       w["wr_hi"], w["wr_lo"], w["r_bias"], alpha)

    token_of_slot, block_expert, slot_of_assignment = _route(routed, counts[0, :N_EXPERTS].astype(jnp.int32), MOE_TILE)
    y_slots = _experts(block_expert + w["layer"] * N_EXPERTS, x1b[token_of_slot], *moe_w)
    return _ln2(x1, y_slots[slot_of_assignment[:, 0]], y_slots[slot_of_assignment[:, 1]], routed,
                w["ln2_g"], w["ln2_b"], alpha)


def kernel(x, positions, w_in, fox_b_f, fox_out_g, mla_q_norm_g, mla_kv_norm_g, mla_w_uq, mla_w_ukv, mla_out_g, rwkv_mu, rwkv_w0, rwkv_w2, rwkv_a0, rwkv_a2, rwkv_g2, rwkv_k_k, rwkv_k_a, rwkv_r_k, rwkv_ln_g, rwkv_ln_b, gdn_conv_w, gdn_a_log, gdn_dt_bias, gdn_norm_g, w_out, ln1_g, ln1_b, moe_w_grp, moe_b_grp, moe_w_exp, moe_b_exp, moe_w_gate, moe_w_up, moe_w_down, ln2_g, ln2_b):
    batch, seq, d = x.shape
    depth = w_in.shape[0]
    alpha = (2 * depth) ** 0.25
    params = dict(
        w_in=w_in, fox_b_f=fox_b_f, fox_out_g=fox_out_g, mla_q_norm_g=mla_q_norm_g, mla_kv_norm_g=mla_kv_norm_g,
        mla_w_uq=mla_w_uq, mla_w_ukv=mla_w_ukv, mla_out_g=mla_out_g, rwkv_mu=rwkv_mu, rwkv_w0=rwkv_w0, rwkv_w2=rwkv_w2,
        rwkv_a0=rwkv_a0, rwkv_a2=rwkv_a2, rwkv_g2=rwkv_g2, rwkv_k_k=rwkv_k_k, rwkv_k_a=rwkv_k_a, rwkv_r_k=rwkv_r_k,
        rwkv_ln_g=rwkv_ln_g, rwkv_ln_b=rwkv_ln_b, gdn_conv_w=gdn_conv_w, gdn_a_log=gdn_a_log, gdn_dt_bias=gdn_dt_bias,
        gdn_norm_g=gdn_norm_g, w_out=w_out, ln1_g=ln1_g, ln1_b=ln1_b, moe_w_grp=moe_w_grp, moe_b_grp=moe_b_grp,
        moe_w_exp=moe_w_exp, moe_b_exp=moe_b_exp, moe_w_gate=moe_w_gate, moe_w_up=moe_w_up, moe_w_down=moe_w_down,
        ln2_g=ln2_g, ln2_b=ln2_b,
    )
    weights = _prep_weights(params)

    half = MLA_ROPE // 2
    inv_freq = ROPE_THETA ** (-jnp.arange(half, dtype=F32) / half)
    ang = positions.astype(F32).reshape(batch * seq, 1) * inv_freq
    zpad = jnp.zeros((batch * seq, LANE - MLA_ROPE), F32)
    cs = jnp.concatenate([jnp.cos(ang), jnp.cos(ang), zpad], axis=-1)
    sn = jnp.concatenate([jnp.sin(ang), jnp.sin(ang), zpad], axis=-1)

    xf = x.reshape(batch * seq, d).astype(F32)

    stack = lambda a: a.reshape((-1,) + a.shape[2:])
    moe_w = (stack(moe_w_gate), stack(moe_w_up), stack(moe_w_down))
    seg = jnp.arange(GROUP_W) // RWKV_HD
    seg_ones = (seg[:, None] == seg[None, :]).astype(BF16)

    def body(carry, w):
        xc, xcb = carry
        return _layer(xc, xcb, cs, sn, w, moe_w, seg_ones, batch=batch, seq=seq, alpha=alpha), None

    (xf, _), _ = lax.scan(body, (xf, xf.astype(BF16)), weights)
    return xf.reshape(batch, seq, d).astype(x.dtype)
```

```python
import functools
import math

import jax
import jax.numpy as jnp
from jax import lax
from jax.experimental import pallas as pl
from jax.experimental.pallas import tpu as pltpu

F32 = jnp.float32
BF16 = jnp.bfloat16

D_MODEL = 2048
GROUP_W = 512
FOX_HD, FOX_HEADS = 64, 8
MLA_HEADS, MLA_NOPE, MLA_ROPE, MLA_VD = 4, 128, 64, 128
MLA_Q_RANK, MLA_KV_RANK = 384, 128
ROPE_THETA = 10000.0
RWKV_HD, RWKV_HEADS = 64, 8
RWKV_GN_EPS = 64e-5
GDN_HD, GDN_HEADS, GDN_CONV = 128, 4, 4
N_GROUPS, EXP_PER_GROUP, TOP_K, D_EXPERT = 4, 8, 2, 512
N_EXPERTS = N_GROUPS * EXP_PER_GROUP
CHUNK = 64
LN_EPS = 1e-5
RMS_EPS = 1e-6
LOG2E = math.log2(math.e)

LANE = 128
VMEM_LIMIT_BYTES = 56 * 1024 * 1024
ROW_TILE = 512
ATTN_TILE = 512
ATTN_KV_TILE = 256
SCAN_TILE = 256
MOE_TILE = 512
ONES_ROWS = 16


def _cparams(*sem):
    return pltpu.CompilerParams(dimension_semantics=sem, vmem_limit_bytes=VMEM_LIMIT_BYTES)


def _bdot(a, b):
    return jnp.dot(a.astype(BF16), b.astype(BF16), preferred_element_type=F32)


def _bdot_nt(a, b):
    return lax.dot_general(a.astype(BF16), b.astype(BF16), (((1,), (1,)), ((), ())), preferred_element_type=F32)


def _bdot_tn(a, b):
    return lax.dot_general(a.astype(BF16), b.astype(BF16), (((0,), (0,)), ((), ())), preferred_element_type=F32)


def _split3(x):
    hi = x.astype(BF16)
    r1 = x - hi.astype(F32)
    mid = r1.astype(BF16)
    lo = (r1 - mid.astype(F32)).astype(BF16)
    return hi, mid, lo


def _dot_exact_lhs(m, x):
    hi, mid, lo = _split3(x)
    d = lambda p: jnp.dot(m, p, preferred_element_type=F32)
    return d(hi) + d(mid) + d(lo)


def _sigmoid(x):
    return 1.0 / (1.0 + jnp.exp(-x))


def _softplus(x):
    return jnp.maximum(x, 0.0) + jnp.log(1.0 + jnp.exp(-jnp.abs(x)))


def _chunk_masks(n, c):
    r = lax.broadcasted_iota(jnp.int32, (n, n), 0)
    col = lax.broadcasted_iota(jnp.int32, (n, n), 1)
    same = (r // c) == (col // c)
    lower = jnp.logical_and(same, col <= r)
    strict = jnp.logical_and(same, col < r)
    return r, col, same, lower, strict


def _tri_inv_masks(r, col, c):
    blk = lambda s: (r // s) == (col // s)
    one = lambda cond: jnp.where(cond, 1.0, 0.0).astype(F32)
    offs = []
    s = 8
    while s < c:
        offs.append(one(jnp.logical_and(blk(2 * s), jnp.logical_not(blk(s)))))
        s *= 2
    return one(r == col), one(blk(8)), offs


def _tri_inv(ms, masks):
    eye, blk8, offs = masks
    bf = lambda a: a.astype(BF16)
    dot = lambda a, b: jnp.dot(a, b, preferred_element_type=F32)
    mdf = [m * blk8 for m in ms]
    mds = [bf(m) for m in mdf]
    xs = [eye + m for m in mdf]
    m2s = [bf(dot(md, md)) for md in mds]
    yield
    xs = [x + dot(m2, bf(x)) for x, m2 in zip(xs, m2s)]
    yield
    m4s = [bf(dot(m2, m2)) for m2 in m2s]
    yield
    xs = [x + dot(m4, bf(x)) for x, m4 in zip(xs, m4s)]
    yield
    n = ms[0].shape[0]
    s = 8
    for off in offs:
        xbs = [bf(x) for x in xs]
        if s % 16:
            ts = [bf(dot(xb, bf(m * off))) for xb, m in zip(xbs, ms)]
            yield
            xs = [x + dot(t, xb) for x, t, xb in zip(xs, ts, xbs)]
        else:
            low = lambda a: jnp.concatenate([a[j + s : j + 2 * s] for j in range(0, n, 2 * s)], axis=0)
            ts = [bf(dot(low(xb), bf(m * off))) for xb, m in zip(xbs, ms)]
            yield
            us = [dot(t, xb) for t, xb in zip(ts, xbs)]
            xs = [
                jnp.concatenate(
                    [p for j in range(0, n, 2 * s) for p in (x[j : j + s], x[j + s : j + 2 * s] + u[j // 2 : j // 2 + s])],
                    axis=0,
                )
                for x, u in zip(xs, us)
            ]
        yield
        s *= 2
    return xs


def _interleave(*stage_generators):
    live = list(stage_generators)
    while live:
        for g in list(live):
            try:
                next(g)
            except StopIteration:
                live.remove(g)


def _inproj_kernel(x_ref, w_ref, o_ref, *tail_refs, tail):
    acc = jnp.dot(x_ref[...], w_ref[...], preferred_element_type=F32)
    n = acc.shape[1]
    if tail:
        o_ref[...] = acc[:, : n - tail].astype(o_ref.dtype)
        tail_refs[0][...] = acc[:, n - tail :]
    else:
        o_ref[...] = acc.astype(o_ref.dtype)


def _inproj(xb, w, tail):
    t, d = xb.shape
    n = w.shape[1]
    tm = min(ROW_TILE, t)
    out_shape = [jax.ShapeDtypeStruct((t, n - tail), BF16)]
    out_specs = [pl.BlockSpec((tm, n - tail), lambda i: (i, 0))]
    if tail:
        out_shape.append(jax.ShapeDtypeStruct((t, tail), F32))
        out_specs.append(pl.BlockSpec((tm, tail), lambda i: (i, 0)))
    return pl.pallas_call(
        functools.partial(_inproj_kernel, tail=tail),
        grid=(t // tm,),
        in_specs=[pl.BlockSpec((tm, d), lambda i: (i, 0)), pl.BlockSpec((d, n), lambda i: (0, 0))],
        out_specs=out_specs,
        out_shape=out_shape,
        compiler_params=_cparams("parallel"),
        name="inproj",
    )(xb, w)


def _attn_kernel(*refs, heads, dk, dv, chunk, tq, tk, seq, use_bias):
    if use_bias:
        q_ref, k_ref, vt_ref, gate_ref, gb_ref, g_ref, o_ref, kb_ref, m_ref, acc_ref = refs
    else:
        q_ref, k_ref, vt_ref, g_ref, o_ref, m_ref, acc_ref = refs
    paired = dk < LANE
    dkp = LANE if paired else dk
    dva = dv + ONES_ROWS
    i = pl.program_id(1)
    lane = lax.broadcasted_iota(jnp.int32, (tq, LANE), 1)

    if use_bias:
        @pl.when(i == 0)
        def _():
            r = lax.broadcasted_iota(jnp.int32, (tk, tk), 0)
            c = lax.broadcasted_iota(jnp.int32, (tk, tk), 1)
            tri = jnp.where(c <= r, 1.0, 0.0).astype(BF16)
            head_lane = lax.broadcasted_iota(jnp.int32, (tk, LANE), 1) < heads
            carry = jnp.zeros((1, LANE), F32)
            for j in range(seq // tk):
                rows = slice(j * tk, (j + 1) * tk)
                log_f = -_softplus(-(gate_ref[rows, :] + gb_ref[...])) * LOG2E
                cum = _dot_exact_lhs(tri, log_f) + carry
                carry = cum[tk - 1 : tk, :]
                hi, mid, lo = (jnp.where(head_lane, p.astype(F32), 0.0) for p in _split3(cum))
                pieces = hi + pltpu.roll(mid, heads, axis=1) + pltpu.roll(lo, 2 * heads, axis=1)
                kb_ref[rows, :] = (-pieces).astype(BF16)

    m_ref[...] = jnp.full(m_ref.shape, -1e30, F32)
    acc_ref[...] = jnp.zeros(acc_ref.shape, F32)

    qs = []
    for h in range(heads):
        if paired:
            slab = q_ref[:, (h // 2) * LANE : (h // 2 + 1) * LANE]
            q = jnp.where(lane // dk == h % 2, slab, jnp.zeros_like(slab))
        else:
            q = q_ref[:, h * dkp : (h + 1) * dkp]
        if use_bias:
            pick = jnp.logical_and(lane % heads == h, lane < 3 * heads)
            q = jnp.concatenate([q, jnp.where(pick, 1.0, 0.0).astype(q.dtype)], axis=1)
        qs.append(q)
    ones_rows = jnp.ones((ONES_ROWS, tk), BF16)

    def step(off, key_shift):
        if key_shift is not None:
            kr = lax.broadcasted_iota(jnp.int32, (tk, tq), 0) + key_shift
            qc = lax.broadcasted_iota(jnp.int32, (tk, tq), 1)
            allowed = (kr // chunk) <= (qc // chunk)
        masked = key_shift is not None
        sts = []
        for h in range(heads):
            slab = h // 2 if paired else h
            k = k_ref[pl.ds(off, tk), slab * dkp : (slab + 1) * dkp]
            if use_bias:
                k = jnp.concatenate([k, kb_ref[pl.ds(off, tk), :]], axis=1)
            sts.append(lax.dot_general(k, qs[h], (((1,), (1,)), ((), ())), preferred_element_type=F32))
        ps, alphas = [], []
        for h in range(heads):
            st = sts[h]
            if masked:
                st = jnp.where(allowed, st, -1e30)
            m_old = m_ref[h, 0:1, :]
            m_new = jnp.maximum(m_old, jnp.max(st, axis=0, keepdims=True))
            m_ref[h, 0:1, :] = m_new
            ps.append(jnp.exp2(st - m_new).astype(BF16))
            alphas.append(jnp.exp2(m_old - m_new))
        for h in range(heads):
            vt_h = jnp.concatenate([vt_ref[h * dv : (h + 1) * dv, pl.ds(off, tk)], ones_rows], axis=0)
            rows = slice(h * dva, (h + 1) * dva)
            acc_ref[rows, :] = alphas[h] * acc_ref[rows, :] + jnp.dot(vt_h, ps[h], preferred_element_type=F32)

    def body(j, carry):
        step(pl.multiple_of(j * tk, tk), None)
        return carry

    lax.fori_loop(0, i * (tq // tk), body, 0)
    for d in range(tq // tk):
        step(pl.multiple_of(i * tq + d * tk, tk), d * tk)

    ot = jnp.concatenate(
        [acc_ref[h * dva : h * dva + dv, :] / acc_ref[h * dva + dv : h * dva + dv + 1, :] for h in range(heads)], axis=0
    )
    ot = ot * lax.rsqrt(jnp.mean(ot * ot, axis=0, keepdims=True) + RMS_EPS)
    o_ref[...] = (ot.T * g_ref[...]).astype(o_ref.dtype)


def _attention(q_arr, q_col, k_arr, k_col, vt, gate, gain, *, batch, seq, heads, dk, dv, chunk):
    bias = gate
    assert dk % LANE == 0 or (2 * dk == LANE and heads % 2 == 0)
    tq = min(ATTN_TILE, seq)
    tk = min(ATTN_KV_TILE, tq)
    nq = seq // tq
    t = batch * seq
    in_specs = [
        pl.BlockSpec((tq, heads * dk), lambda b, i: (b * nq + i, q_col)),
        pl.BlockSpec((seq, heads * dk), lambda b, i: (b, k_col)),
        pl.BlockSpec((None, heads * dv, seq), lambda b, i: (b, 0, 0)),
    ]
    args = [q_arr, k_arr, vt]
    scratch = []
    if bias is not None:
        in_specs += [pl.BlockSpec((seq, LANE), lambda b, i: (b, 0)), pl.BlockSpec((1, LANE), lambda b, i: (0, 0))]
        args += list(gate)
        assert 3 * heads <= LANE
        scratch.append(pltpu.VMEM((seq, LANE), BF16))
    in_specs.append(pl.BlockSpec((1, heads * dv), lambda b, i: (0, 0)))
    args.append(gain)
    scratch += [pltpu.VMEM((heads, 8, tq), F32), pltpu.VMEM((heads * (dv + ONES_ROWS), tq), F32)]
    return pl.pallas_call(
        functools.partial(_attn_kernel, heads=heads, dk=dk, dv=dv, chunk=chunk, tq=tq, tk=tk, seq=seq, use_bias=bias is not None),
        grid=(batch, nq),
        in_specs=in_specs,
        out_specs=pl.BlockSpec((tq, heads * dv), lambda b, i: (b * nq + i, 0)),
        out_shape=jax.ShapeDtypeStruct((t, heads * dv), BF16),
        scratch_shapes=scratch,
        compiler_params=_cparams("parallel", "arbitrary"),
        name="attention",
    )(*args)


def _mla_prep_kernel(u_ref, cs_ref, sn_ref, qg_ref, kvg_ref, wq_ref, wkv_ref, q_ref, k_ref, v_ref):
    u = u_ref[...].astype(F32)
    cs = cs_ref[...]
    sn = sn_ref[...]

    def rms(x, g):
        return x * lax.rsqrt(jnp.mean(x * x, axis=-1, keepdims=True) + RMS_EPS) * g

    qo = _bdot(rms(u[:, :MLA_Q_RANK], qg_ref[...]), wq_ref[...])
    kvo = _bdot(rms(u[:, MLA_Q_RANK : MLA_Q_RANK + MLA_KV_RANK], kvg_ref[...]), wkv_ref[...])
    c0 = MLA_Q_RANK + MLA_KV_RANK
    kpe = (u[:, c0 : c0 + LANE] * cs + u[:, c0 + LANE : c0 + 2 * LANE] * sn).astype(k_ref.dtype)
    nn = MLA_HEADS * MLA_NOPE
    for h in range(MLA_HEADS):
        a = h * 2 * LANE
        q_ref[:, a : a + LANE] = qo[:, h * LANE : (h + 1) * LANE].astype(q_ref.dtype)
        qpe = qo[:, nn + h * LANE : nn + (h + 1) * LANE] * cs + qo[:, 2 * nn + h * LANE : 2 * nn + (h + 1) * LANE] * sn
        q_ref[:, a + LANE : a + 2 * LANE] = qpe.astype(q_ref.dtype)
        k_ref[:, a : a + LANE] = kvo[:, h * LANE : (h + 1) * LANE].astype(k_ref.dtype)
        k_ref[:, a + LANE : a + 2 * LANE] = kpe
    v_ref[...] = kvo[:, nn:].T.astype(v_ref.dtype)


def _mla_prep(u_mla, cs, sn, qg, kvg, wq, wkv, *, batch, seq):
    t = u_mla.shape[0]
    tm = min(ROW_TILE, seq)
    nt = seq // tm
    row = lambda n: pl.BlockSpec((tm, n), lambda i: (i, 0))
    full = lambda a: pl.BlockSpec(a.shape, lambda i: (0,) * a.ndim)
    wide = MLA_HEADS * 2 * LANE
    dvs = MLA_HEADS * MLA_VD
    return pl.pallas_call(
        _mla_prep_kernel,
        grid=(t // tm,),
        in_specs=[row(u_mla.shape[1]), row(LANE), row(LANE), full(qg), full(kvg), full(wq), full(wkv)],
        out_specs=[row(wide), row(wide), pl.BlockSpec((None, dvs, tm), lambda i: (i // nt, 0, i % nt))],
        out_shape=[
            jax.ShapeDtypeStruct((t, wide), BF16),
            jax.ShapeDtypeStruct((t, wide), BF16),
            jax.ShapeDtypeStruct((batch, dvs, seq), BF16),
        ],
        compiler_params=_cparams("parallel"),
        name="mla_prep",
    )(u_mla, cs, sn, qg, kvg, wq, wkv)


def _gdn_stages(u_ref, t_ref, cw_ref, alog_ref, dtb_ref, ng_ref, o_ref, xs_ref, st_ref, *, tt, c):
    i = pl.program_id(1)
    gw, hd = GROUP_W, GDN_HD

    @pl.when(i == 0)
    def _():
        xs_ref[0:8, :] = jnp.zeros((8, 3 * gw), F32)
        st_ref[...] = jnp.zeros(st_ref.shape, F32)

    xs_ref[8 : 8 + tt, :] = u_ref[:, : 3 * gw].astype(F32)
    cw = cw_ref[...]
    conv = cw[0:1, :] * xs_ref[5 : 5 + tt, :]
    for j in range(1, GDN_CONV):
        conv = conv + cw[j : j + 1, :] * xs_ref[5 + j : 5 + j + tt, :]
    xs_ref[0:8, :] = xs_ref[tt : tt + 8, :]
    qkv = conv * _sigmoid(conv)
    yield

    tail = t_ref[...]
    beta = _sigmoid(tail[:, :LANE])
    g = -jnp.exp(alog_ref[...]) * _softplus(tail[:, LANE:] + dtb_ref[...])

    r, col, same, lower, strict = _chunk_masks(tt, c)
    tri = jnp.where(lower, 1.0, 0.0).astype(BF16)
    ones_blk = jnp.where(same, 1.0, 0.0).astype(BF16)
    gc = _dot_exact_lhs(tri, g)
    gtot = _dot_exact_lhs(ones_blk, g)
    gct = gc.T
    yield

    def l2n(x):
        return x * lax.rsqrt(jnp.sum(x * x, axis=-1, keepdims=True) + 1e-6)

    heads = range(GDN_HEADS)
    bf = lambda a: a.astype(BF16)
    lmats, attns, kbs, vbs, qds, kds, egs = [], [], [], [], [], [], []
    for h in heads:
        qf = l2n(qkv[:, h * hd : (h + 1) * hd]) * (hd**-0.5)
        qh = bf(qf)
        kf = l2n(qkv[:, gw + h * hd : gw + (h + 1) * hd])
        kh = bf(kf)
        vh = qkv[:, 2 * gw + h * hd : 2 * gw + (h + 1) * hd]
        gcol = gc[:, h : h + 1]
        dec = jnp.exp(jnp.minimum(gcol - gct[h : h + 1, :], 0.0))
        bcol = beta[:, h : h + 1]
        kb = kf * bcol
        eg = jnp.exp(gcol)
        lmats.append(jnp.where(strict, _bdot_nt(kb, kh) * dec, 0.0))
        attns.append(bf(jnp.where(lower, _bdot_nt(qh, kh) * dec, 0.0)))
        kbs.append(bf(kb * eg))
        vbs.append(bf(vh * bcol))
        qds.append(bf(qf * eg))
        kds.append(bf(kf * jnp.exp(gtot[:, h : h + 1] - gcol)))
        yield
    tinvs = yield from _tri_inv([-m for m in lmats], _tri_inv_masks(r, col, c))
    tinvs = [bf(t) for t in tinvs]
    uws = [jnp.dot(t, jnp.concatenate([vb, kb], axis=1), preferred_element_type=F32) for t, vb, kb in zip(tinvs, vbs, kbs)]
    uvals = [uw[:, :hd] for uw in uws]
    wcums = [bf(uw[:, hd:]) for uw in uws]
    yield
    states = [st_ref[h] for h in heads]
    outs = [[] for _ in heads]
    for cc in range(tt // c):
        rs = slice(cc * c, (cc + 1) * c)
        sbs = [bf(s) for s in states]
        vnews = [uvals[h][rs] - jnp.dot(wcums[h][rs], sbs[h], preferred_element_type=F32) for h in heads]
        yield
        for h in heads:
            outs[h].append(jnp.dot(qds[h][rs], sbs[h], preferred_element_type=F32) + _bdot(attns[h][rs, rs], vnews[h]))
            glast = jnp.exp(gtot[cc * c : cc * c + 1, h : h + 1])
            states[h] = states[h] * glast + _bdot_tn(kds[h][rs], vnews[h])
        yield
    for h in heads:
        st_ref[h] = states[h]
        o = jnp.concatenate(outs[h], axis=0)
        o = o * lax.rsqrt(jnp.mean(o * o, axis=-1, keepdims=True) + RMS_EPS) * ng_ref[...]
        z = u_ref[:, 3 * gw + h * hd : 3 * gw + (h + 1) * hd].astype(F32)
        o_ref[:, h * hd : (h + 1) * hd] = (o * (z * _sigmoid(z))).astype(o_ref.dtype)


def _rwkv_stages(u_ref, mu_ref, w0_ref, w2_ref, a0_ref, a2_ref, g2_ref, kk_ref, ka_ref, rk_ref, lng_ref, lnb_ref,
                 e_ref, o_ref, xs_ref, st_ref, *, tt, c):
    i = pl.program_id(1)
    gw, hd = GROUP_W, RWKV_HD

    @pl.when(i == 0)
    def _():
        xs_ref[0:8, :] = jnp.zeros((8, xs_ref.shape[1]), F32)
        st_ref[...] = jnp.zeros(st_ref.shape, F32)

    u = u_ref[...].astype(F32)
    xs_ref[8 : 8 + tt, :] = u
    prev = xs_ref[7 : 7 + tt, :]
    xs_ref[0:8, :] = xs_ref[tt : tt + 8, :]
    x = u + mu_ref[...] * (prev - u)
    rr = x[:, :gw]
    k = x[:, gw : 2 * gw]
    v = x[:, 2 * gw : 3 * gw]
    wa = x[:, 3 * gw : 3 * gw + LANE]
    glo = x[:, 3 * gw + LANE :]
    w = -_softplus(-(w0_ref[...] + _bdot(jnp.tanh(wa), w2_ref[...]))) - 0.5
    ld = -jnp.exp(w)
    gate_a = _sigmoid(a0_ref[...] + _bdot(wa, a2_ref[...]))
    gate_g = _bdot(_sigmoid(glo), g2_ref[...])
    e = e_ref[...]

    def segsum(y):
        hi = y.astype(BF16)
        lo = (y - hi.astype(F32)).astype(BF16)
        return jnp.dot(hi, e, preferred_element_type=F32) + jnp.dot(lo, e, preferred_element_type=F32)

    kkr = k * kk_ref[...]
    kk = kkr * lax.rsqrt(segsum(kkr * kkr) + 1e-6)
    k2 = k * (1.0 + (gate_a - 1.0) * ka_ref[...])
    bonus = segsum(rr * k2 * rk_ref[...]) * v
    yield

    r, col, same, lower, strict = _chunk_masks(tt, c)
    tri = jnp.where(lower, 1.0, 0.0).astype(BF16)
    ones_blk = jnp.where(same, 1.0, 0.0).astype(BF16)
    cum = _dot_exact_lhs(tri, ld)
    ctot = _dot_exact_lhs(ones_blk, ld)
    encum = jnp.exp(-cum)
    edec = jnp.exp(ctot - cum)
    kka = kk * gate_a
    rt = rr * jnp.exp(cum)
    at = -kk * jnp.exp(cum - ld)
    bt = kka * encum
    kt = k2 * encum
    bd = kka * edec
    kd = k2 * edec
    pc = jnp.exp(ctot)
    yield

    heads = range(RWKV_HEADS)
    bf = lambda a: a.astype(BF16)
    dot = lambda a, b: jnp.dot(a, b, preferred_element_type=F32)
    dot_nt = lambda a, b: lax.dot_general(a, b, (((1,), (1,)), ((), ())), preferred_element_type=F32)
    dot_tn = lambda a, b: lax.dot_general(a, b, (((0,), (0,)), ((), ())), preferred_element_type=F32)
    at_b, rt_b, bt_b, kt_b, v_b, bd_b, kd_b = bf(at), bf(rt), bf(bt), bf(kt), bf(v), bf(bd), bf(kd)
    sls = [slice(h * hd, (h + 1) * hd) for h in heads]
    strict_f = jnp.where(strict, 1.0, 0.0).astype(F32)
    lower_f = jnp.where(lower, 1.0, 0.0).astype(F32)
    mabs = [dot_nt(at_b[:, sl], bt_b[:, sl]) * strict_f for sl in sls]
    yield
    maks = [bf(dot_nt(at_b[:, sl], kt_b[:, sl]) * strict_f) for sl in sls]
    yield
    arbs = [bf(dot_nt(rt_b[:, sl], bt_b[:, sl]) * lower_f) for sl in sls]
    yield
    arks = [bf(dot_nt(rt_b[:, sl], kt_b[:, sl]) * lower_f) for sl in sls]
    yield
    tinvs = yield from _tri_inv(mabs, _tri_inv_masks(r, col, c))
    tinvs = [bf(t) for t in tinvs]
    wmats = [bf(dot(t, at_b[:, sl])) for t, sl in zip(tinvs, sls)]
    mkvs = [bf(dot(m, v_b[:, sl])) for m, sl in zip(maks, sls)]
    yield
    umats = [dot(t, mkv) for t, mkv in zip(tinvs, mkvs)]
    yconsts = [dot(m, v_b[:, sl]) for m, sl in zip(arks, sls)]
    yield
    states = [st_ref[h] for h in heads]
    outs = [[] for _ in heads]
    for cc in range(tt // c):
        rs = slice(cc * c, (cc + 1) * c)
        sbs = [bf(s) for s in states]
        sas = [dot_nt(wmats[h][rs], sbs[h]) + umats[h][rs] for h in heads]
        yield
        for h in heads:
            sab = bf(sas[h])
            outs[h].append(dot_nt(rt_b[rs, sls[h]], sbs[h]) + dot(arbs[h][rs, rs], sab) + yconsts[h][rs])
            states[h] = (states[h] * pc[cc * c : cc * c + 1, sls[h]] + dot_tn(sab, bd_b[rs, sls[h]])
                         + dot_tn(v_b[rs, sls[h]], kd_b[rs, sls[h]]))
        yield
    for h in heads:
        st_ref[h] = states[h]
    y = jnp.concatenate([jnp.concatenate(o, axis=0) for o in outs], axis=-1)
    mean = segsum(y) * (1.0 / hd)
    d = y - mean
    var = segsum(d * d) * (1.0 / hd)
    yn = d * lax.rsqrt(var + RWKV_GN_EPS) * lng_ref[...] + lnb_ref[...]
    o_ref[...] = ((yn + bonus) * gate_g).astype(o_ref.dtype)


def _scan_mixers_kernel(*refs, n_rwkv_in, n_gdn_in, tt, c):
    rwkv_in = refs[:n_rwkv_in]
    gdn_in = refs[n_rwkv_in : n_rwkv_in + n_gdn_in]
    o_rwkv, o_gdn, xs_rwkv, st_rwkv, xs_gdn, st_gdn = refs[n_rwkv_in + n_gdn_in :]
    _interleave(
        _rwkv_stages(*rwkv_in, o_rwkv, xs_rwkv, st_rwkv, tt=tt, c=c),
        _gdn_stages(*gdn_in, o_gdn, xs_gdn, st_gdn, tt=tt, c=c),
    )


def _scan_mixers(rwkv_args, gdn_args, *, batch, seq):
    tt = min(SCAN_TILE, seq)
    nt = seq // tt
    t = batch * seq
    tile = lambda a: pl.BlockSpec((tt, a.shape[1]), lambda b, i: (b * nt + i, 0))
    full = lambda a: pl.BlockSpec(a.shape, lambda b, i: (0,) * a.ndim)
    u_rwkv, u_gdn, gdn_tail = rwkv_args[0], gdn_args[0], gdn_args[1]
    in_specs = [tile(u_rwkv)] + [full(p) for p in rwkv_args[1:]] + [tile(u_gdn), tile(gdn_tail)] + [full(p) for p in gdn_args[2:]]
    out = pl.BlockSpec((tt, GROUP_W), lambda b, i: (b * nt + i, 0))
    return pl.pallas_call(
        functools.partial(_scan_mixers_kernel, n_rwkv_in=len(rwkv_args), n_gdn_in=len(gdn_args), tt=tt, c=CHUNK),
        grid=(batch, nt),
        in_specs=in_specs,
        out_specs=[out, out],
        out_shape=[jax.ShapeDtypeStruct((t, GROUP_W), BF16)] * 2,
        scratch_shapes=[
            pltpu.VMEM((tt + 8, u_rwkv.shape[1]), F32),
            pltpu.VMEM((RWKV_HEADS, RWKV_HD, RWKV_HD), F32),
            pltpu.VMEM((tt + 8, 3 * GROUP_W), F32),
            pltpu.VMEM((GDN_HEADS, GDN_HD, GDN_HD), F32),
        ],
        compiler_params=_cparams("parallel", "arbitrary"),
        name="rwkv7_gdn",
    )(*rwkv_args, *gdn_args)


def _layer_norm(h, g, b):
    mu = jnp.mean(h, axis=-1, keepdims=True)
    d = h - mu
    var = jnp.mean(d * d, axis=-1, keepdims=True)
    return d * lax.rsqrt(var + LN_EPS) * g + b


def _route_rows(lg):
    neg = -1e30
    lane = lax.broadcasted_iota(jnp.int32, lg.shape, 1)
    lane_f = lane.astype(F32)
    first = lambda hit: jnp.min(jnp.where(hit, lane_f, float(LANE)), axis=-1, keepdims=True)
    is_g = lane < N_GROUPS
    gl = jnp.where(is_g, lg, neg)
    gmax = jnp.max(gl, axis=-1, keepdims=True)
    p_grp = 1.0 / jnp.sum(jnp.where(is_g, jnp.exp(gl - gmax), 0.0), axis=-1, keepdims=True)
    grp = first(gl == gmax)
    lo = N_GROUPS + grp * EXP_PER_GROUP
    el = jnp.where(jnp.logical_and(lane_f >= lo, lane_f < lo + EXP_PER_GROUP), lg, neg)
    v1 = jnp.max(el, axis=-1, keepdims=True)
    i1 = first(el == v1)
    el2 = jnp.where(lane_f == i1, neg, el)
    v2 = jnp.max(el2, axis=-1, keepdims=True)
    i2 = first(el2 == v2)
    r = jnp.exp(v2 - v1)
    g1 = p_grp / (1.0 + r)
    out = jnp.where(lane == 0, i1 - N_GROUPS, 0.0)
    out = jnp.where(lane == 1, i2 - N_GROUPS, out)
    out = jnp.where(lane == 2, g1, out)
    return jnp.where(lane == 3, g1 * r, out)


def _outproj_kernel(yf_ref, ym_ref, yr_ref, yg_ref, w_ref, x_ref, g_ref, b_ref, wr_ref, rb_ref, tril_ref,
                    xo_ref, xb_ref, lg_ref, cnt_ref, *, alpha):
    mixed = jnp.concatenate([yf_ref[...], ym_ref[...], yr_ref[...], yg_ref[...]], axis=1)
    acc = jnp.dot(mixed, w_ref[...], preferred_element_type=F32)
    xn = _layer_norm(alpha * x_ref[...] + acc, g_ref[...], b_ref[...])
    xo_ref[...] = xn
    xh = xn.astype(BF16)
    xb_ref[...] = xh
    xl = (xn - xh.astype(F32)).astype(BF16)
    d = lambda a, b: jnp.dot(a, b, preferred_element_type=F32)
    both_w = d(xh, wr_ref[...])
    routed = _route_rows(both_w[:, :LANE] + both_w[:, LANE:] + d(xl, wr_ref[:, :LANE]) + rb_ref[...])

    @pl.when(pl.program_id(0) == 0)
    def _():
        cnt_ref[...] = jnp.zeros(cnt_ref.shape, F32)

    lane = lax.broadcasted_iota(jnp.int32, routed.shape, 1).astype(F32)
    oh = [jnp.where(lane == routed[:, k : k + 1], 1.0, 0.0) for k in range(TOP_K)]
    both = oh[0] + oh[1]
    before = jnp.dot(tril_ref[...], both.astype(BF16), preferred_element_type=F32) + cnt_ref[0:1, :]
    for k in range(TOP_K):
        rank = jnp.sum(before * oh[k], axis=-1, keepdims=True)
        routed = jnp.where(lane == 2 * TOP_K + k, rank, routed)
    cnt_ref[...] = cnt_ref[...] + jnp.sum(both, axis=0, keepdims=True)
    lg_ref[...] = routed


def _outproj(ys, w_out, x, ln_g, ln_b, w_router, r_bias, alpha):
    t, d = x.shape
    tm = min(ROW_TILE, t)
    row = lambda n: pl.BlockSpec((tm, n), lambda i: (i, 0))
    full = lambda a: pl.BlockSpec(a.shape, lambda i: (0,) * a.ndim)
    idx = jnp.arange(tm)
    tril = (idx[None, :] < idx[:, None]).astype(BF16)
    return pl.pallas_call(
        functools.partial(_outproj_kernel, alpha=alpha),
        grid=(t // tm,),
        in_specs=[row(GROUP_W)] * 4
        + [full(w_out), row(d), full(ln_g), full(ln_b), full(w_router), full(r_bias), full(tril)],
        out_specs=[row(d), row(d), row(LANE), pl.BlockSpec((8, LANE), lambda i: (0, 0))],
        out_shape=[
            jax.ShapeDtypeStruct((t, d), F32),
            jax.ShapeDtypeStruct((t, d), BF16),
            jax.ShapeDtypeStruct((t, LANE), F32),
            jax.ShapeDtypeStruct((8, LANE), F32),
        ],
        compiler_params=_cparams("arbitrary"),
        name="outproj_ln_router",
    )(*ys, w_out, x, ln_g, ln_b, w_router, r_bias, tril)


def _expert_kernel(be_ref, x_ref, wg_ref, wu_ref, wd_ref, o_ref, wg_s, wu_s, wd_s):
    i = pl.program_id(0)

    @pl.when(jnp.logical_or(i == 0, be_ref[i] != be_ref[jnp.maximum(i - 1, 0)]))
    def _():
        wg_s[...] = wg_ref[...].astype(BF16)
        wu_s[...] = wu_ref[...].astype(BF16)
        wd_s[...] = wd_ref[...].astype(BF16)

    x = x_ref[...]
    a = jnp.dot(x, wg_s[...], preferred_element_type=F32)
    b = jnp.dot(x, wu_s[...], preferred_element_type=F32)
    hmid = (a * _sigmoid(a) * b).astype(BF16)
    o_ref[...] = jnp.dot(hmid, wd_s[...], preferred_element_type=F32).astype(o_ref.dtype)


def _experts(block_weight, xs, w_gate, w_up, w_down):
    n_slots, d = xs.shape
    tb = MOE_TILE
    de = w_gate.shape[-1]
    grid_spec = pltpu.PrefetchScalarGridSpec(
        num_scalar_prefetch=1,
        grid=(n_slots // tb,),
        in_specs=[
            pl.BlockSpec((tb, d), lambda i, be: (i, 0)),
            pl.BlockSpec((None, d, de), lambda i, be: (be[i], 0, 0)),
            pl.BlockSpec((None, d, de), lambda i, be: (be[i], 0, 0)),
            pl.BlockSpec((None, de, d), lambda i, be: (be[i], 0, 0)),
        ],
        out_specs=pl.BlockSpec((tb, d), lambda i, be: (i, 0)),
        scratch_shapes=[pltpu.VMEM((d, de), BF16), pltpu.VMEM((d, de), BF16), pltpu.VMEM((de, d), BF16)],
    )
    return pl.pallas_call(
        _expert_kernel,
        grid_spec=grid_spec,
        out_shape=jax.ShapeDtypeStruct((n_slots, d), BF16),
        compiler_params=_cparams("arbitrary"),
        name="experts",
    )(block_weight, xs, w_gate, w_up, w_down)


def _ln2_kernel(x_ref, y1_ref, y2_ref, r_ref, g_ref, b_ref, xo_ref, xb_ref, *, alpha):
    routed = r_ref[...]
    moe = routed[:, TOP_K : TOP_K + 1] * y1_ref[...].astype(F32) + routed[:, TOP_K + 1 : TOP_K + 2] * y2_ref[...].astype(F32)
    xn = _layer_norm(alpha * x_ref[...] + moe, g_ref[...], b_ref[...])
    xo_ref[...] = xn
    xb_ref[...] = xn.astype(BF16)


def _ln2(x, y1, y2, routed, g, b, alpha):
    t, d = x.shape
    tm = min(ROW_TILE, t)
    row = lambda n: pl.BlockSpec((tm, n), lambda i: (i, 0))
    full = lambda a: pl.BlockSpec(a.shape, lambda i: (0,) * a.ndim)
    return pl.pallas_call(
        functools.partial(_ln2_kernel, alpha=alpha),
        grid=(t // tm,),
        in_specs=[row(d), row(d), row(d), row(LANE), full(g), full(b)],
        out_specs=[row(d), row(d)],
        out_shape=[jax.ShapeDtypeStruct((t, d), F32), jax.ShapeDtypeStruct((t, d), BF16)],
        compiler_params=_cparams("parallel"),
        name="residual_ln2",
    )(x, y1, y2, routed, g, b)


def _pad_cols(w, n):
    return jnp.pad(w, [(0, 0)] * (w.ndim - 1) + [(0, n - w.shape[-1])])


def _rot_half_cols(w):
    half = w.shape[-1] // 2
    return jnp.concatenate([-w[..., half:], w[..., :half]], axis=-1)


def _prep_weights(p):
    w_in = p["w_in"]
    c_fox = 3 * GROUP_W + FOX_HEADS
    c_mla = MLA_Q_RANK + MLA_KV_RANK + MLA_ROPE
    c_rwkv = 3 * GROUP_W + 2 * 64 + 128
    o_mla = c_fox
    o_rwkv = o_mla + c_mla
    o_gdn = o_rwkv + c_rwkv
    row = lambda a: a[:, None, :].astype(F32)

    w_fox = jnp.concatenate(
        [w_in[..., :GROUP_W] * (FOX_HD**-0.5 * LOG2E), w_in[..., GROUP_W : 3 * GROUP_W], _pad_cols(w_in[..., 3 * GROUP_W : c_fox], LANE)],
        axis=-1,
    )
    kpe_w = w_in[..., o_mla + MLA_Q_RANK + MLA_KV_RANK : o_mla + c_mla]
    w_mla = jnp.concatenate(
        [w_in[..., o_mla : o_mla + MLA_Q_RANK + MLA_KV_RANK], _pad_cols(kpe_w, LANE), _pad_cols(_rot_half_cols(kpe_w), LANE)],
        axis=-1,
    )
    w_rwkv = w_in[..., o_rwkv:o_gdn]
    g0 = o_gdn + 4 * GROUP_W
    w_gdn = jnp.concatenate(
        [w_in[..., o_gdn:g0], _pad_cols(w_in[..., g0 : g0 + GDN_HEADS], LANE), _pad_cols(w_in[..., g0 + GDN_HEADS :], LANE)],
        axis=-1,
    )

    nl = w_in.shape[0]
    scale = (MLA_NOPE + MLA_ROPE) ** -0.5 * LOG2E
    wq = p["mla_w_uq"].reshape(nl, MLA_Q_RANK, MLA_HEADS, MLA_NOPE + MLA_ROPE) * scale
    wq_nope = wq[..., :MLA_NOPE].reshape(nl, MLA_Q_RANK, -1)
    wq_pe = wq[..., MLA_NOPE:]
    wq_p = jnp.concatenate(
        [wq_nope, _pad_cols(wq_pe, LANE).reshape(nl, MLA_Q_RANK, -1), _pad_cols(_rot_half_cols(wq_pe), LANE).reshape(nl, MLA_Q_RANK, -1)],
        axis=-1,
    )
    wkv = p["mla_w_ukv"].reshape(nl, MLA_KV_RANK, MLA_HEADS, MLA_NOPE + MLA_VD)
    wkv_p = jnp.concatenate([wkv[..., :MLA_NOPE].reshape(nl, MLA_KV_RANK, -1), wkv[..., MLA_NOPE:].reshape(nl, MLA_KV_RANK, -1)], axis=-1)

    zeros64 = jnp.zeros((nl, 64, GROUP_W), F32)
    w_router = _pad_cols(jnp.concatenate([p["moe_w_grp"], p["moe_w_exp"]], axis=-1), LANE)
    wr_hi = w_router.astype(BF16)
    wr_lo = (w_router - wr_hi.astype(F32)).astype(BF16)
    return dict(
        w_fox=w_fox.astype(BF16), w_mla=w_mla.astype(BF16), w_rwkv=w_rwkv.astype(BF16), w_gdn=w_gdn.astype(BF16),
        layer=jnp.arange(nl, dtype=jnp.int32),
        fox_b_f=row(_pad_cols(p["fox_b_f"], LANE)), fox_out_g=row(p["fox_out_g"]),
        mla_qg=row(p["mla_q_norm_g"]), mla_kvg=row(p["mla_kv_norm_g"]), mla_wq=wq_p.astype(BF16), mla_wkv=wkv_p.astype(BF16),
        mla_out_g=row(p["mla_out_g"]),
        rwkv_mu=row(p["rwkv_mu"]), rwkv_w0=row(p["rwkv_w0"]),
        rwkv_w2=jnp.concatenate([p["rwkv_w2"], zeros64], axis=1).astype(BF16),
        rwkv_a0=row(p["rwkv_a0"]), rwkv_a2=jnp.concatenate([zeros64, p["rwkv_a2"]], axis=1).astype(BF16),
        rwkv_g2=p["rwkv_g2"].astype(BF16), rwkv_k_k=row(p["rwkv_k_k"]), rwkv_k_a=row(p["rwkv_k_a"]),
        rwkv_r_k=row(p["rwkv_r_k"]), rwkv_ln_g=row(p["rwkv_ln_g"]), rwkv_ln_b=row(p["rwkv_ln_b"]),
        gdn_conv_w=p["gdn_conv_w"].astype(F32), gdn_a_log=row(_pad_cols(p["gdn_a_log"], LANE)),
        gdn_dt_bias=row(_pad_cols(p["gdn_dt_bias"], LANE)), gdn_norm_g=row(p["gdn_norm_g"]),
        w_out=p["w_out"].astype(BF16), ln1_g=row(p["ln1_g"]), ln1_b=row(p["ln1_b"]),
        w_router=jnp.concatenate([wr_hi, wr_lo], axis=-1), r_bias=row(_pad_cols(jnp.concatenate([p["moe_b_grp"], p["moe_b_exp"]], axis=-1), LANE)),
        ln2_g=row(p["ln2_g"]), ln2_b=row(p["ln2_b"]),
    )


def _route(routed, counts, tb):
    t = routed.shape[0]
    a = t * TOP_K
    n_blocks = (a + N_EXPERTS * (tb - 1) + tb - 1) // tb
    n_slots = n_blocks * tb
    expert = routed[:, :TOP_K].astype(jnp.int32)
    rank = routed[:, 2 * TOP_K : 3 * TOP_K].astype(jnp.int32)
    padded = (counts + tb - 1) // tb * tb
    pend = jnp.cumsum(padded)
    pstart = pend - padded
    ids = jnp.arange(N_EXPERTS, dtype=jnp.int32)
    slot_of_assignment = jnp.sum(jnp.where(expert[..., None] == ids, pstart, 0), axis=-1) + rank
    block_start = jnp.arange(n_blocks, dtype=jnp.int32) * tb
    block_expert = jnp.minimum(jnp.sum(block_start[:, None] >= pend, axis=-1), N_EXPERTS - 1).astype(jnp.int32)
    fill_end = jnp.cumsum(padded - counts)
    filler = jnp.arange(n_slots - a, dtype=jnp.int32)
    fill_key = jnp.sum(filler[:, None] >= fill_end, axis=-1).astype(jnp.int32)
    keys = jnp.concatenate([expert.reshape(a), fill_key])
    vals = jnp.concatenate([jnp.arange(a, dtype=jnp.int32) // TOP_K, filler % t])
    _, token_of_slot = lax.sort((keys, vals), num_keys=1, is_stable=True)
    return token_of_slot, block_expert, slot_of_assignment


def _layer(x, xb, cs, sn, w, moe_w, seg_ones, *, batch, seq, alpha):
    u_fox, fox_tail = _inproj(xb, w["w_fox"], LANE)
    (u_mla,) = _inproj(xb, w["w_mla"], 0)
    (u_rwkv,) = _inproj(xb, w["w_rwkv"], 0)
    u_gdn, gdn_tail = _inproj(xb, w["w_gdn"], 2 * LANE)

    vt_fox = u_fox[:, 2 * GROUP_W :].reshape(batch, seq, GROUP_W).transpose(0, 2, 1)
    y_fox = _attention(u_fox, 0, u_fox, 1, vt_fox, (fox_tail, w["fox_b_f"]), w["fox_out_g"], batch=batch, seq=seq,
                       heads=FOX_HEADS, dk=FOX_HD, dv=FOX_HD, chunk=1)

    q_mla, k_mla, vt_mla = _mla_prep(u_mla, cs, sn, w["mla_qg"], w["mla_kvg"], w["mla_wq"], w["mla_wkv"],
                                     batch=batch, seq=seq)
    y_mla = _attention(q_mla, 0, k_mla, 0, vt_mla, None, w["mla_out_g"], batch=batch, seq=seq,
                       heads=MLA_HEADS, dk=2 * LANE, dv=MLA_VD, chunk=CHUNK)

    y_rwkv, y_gdn = _scan_mixers(
        (u_rwkv, w["rwkv_mu"], w["rwkv_w0"], w["rwkv_w2"], w["rwkv_a0"], w["rwkv_a2"], w["rwkv_g2"],
         w["rwkv_k_k"], w["rwkv_k_a"], w["rwkv_r_k"], w["rwkv_ln_g"], w["rwkv_ln_b"], seg_ones),
        (u_gdn, gdn_tail, w["gdn_conv_w"], w["gdn_a_log"], w["gdn_dt_bias"], w["gdn_norm_g"]),
        batch=batch, seq=seq)

    x1, x1b, routed, counts = _outproj((y_fox, y_mla, y_rwkv, y_gdn), w["w_out"], x, w["ln1_g"], w["ln1_b"],
                                       w["w_router"], w["r_bias"], alpha)

    token_of_slot, block_expert, slot_of_assignment = _route(routed, counts[0, :N_EXPERTS].astype(jnp.int32), MOE_TILE)
    y_slots = _experts(block_expert + w["layer"] * N_EXPERTS, x1b[token_of_slot], *moe_w)
    return _ln2(x1, y_slots[slot_of_assignment[:, 0]], y_slots[slot_of_assignment[:, 1]], routed,
                w["ln2_g"], w["ln2_b"], alpha)


def kernel(x, positions, w_in, fox_b_f, fox_out_g, mla_q_norm_g, mla_kv_norm_g, mla_w_uq, mla_w_ukv, mla_out_g, rwkv_mu, rwkv_w0, rwkv_w2, rwkv_a0, rwkv_a2, rwkv_g2, rwkv_k_k, rwkv_k_a, rwkv_r_k, rwkv_ln_g, rwkv_ln_b, gdn_conv_w, gdn_a_log, gdn_dt_bias, gdn_norm_g, w_out, ln1_g, ln1_b, moe_w_grp, moe_b_grp, moe_w_exp, moe_b_exp, moe_w_gate, moe_w_up, moe_w_down, ln2_g, ln2_b):
    batch, seq, d = x.shape
    depth = w_in.shape[0]
    alpha = (2 * depth) ** 0.25
    params = dict(
        w_in=w_in, fox_b_f=fox_b_f, fox_out_g=fox_out_g, mla_q_norm_g=mla_q_norm_g, mla_kv_norm_g=mla_kv_norm_g,
        mla_w_uq=mla_w_uq, mla_w_ukv=mla_w_ukv, mla_out_g=mla_out_g, rwkv_mu=rwkv_mu, rwkv_w0=rwkv_w0, rwkv_w2=rwkv_w2,
        rwkv_a0=rwkv_a0, rwkv_a2=rwkv_a2, rwkv_g2=rwkv_g2, rwkv_k_k=rwkv_k_k, rwkv_k_a=rwkv_k_a, rwkv_r_k=rwkv_r_k,
        rwkv_ln_g=rwkv_ln_g, rwkv_ln_b=rwkv_ln_b, gdn_conv_w=gdn_conv_w, gdn_a_log=gdn_a_log, gdn_dt_bias=gdn_dt_bias,
        gdn_norm_g=gdn_norm_g, w_out=w_out, ln1_g=ln1_g, ln1_b=ln1_b, moe_w_grp=moe_w_grp, moe_b_grp=moe_b_grp,
        moe_w_exp=moe_w_exp, moe_b_exp=moe_b_exp, moe_w_gate=moe_w_gate, moe_w_up=moe_w_up, moe_w_down=moe_w_down,
        ln2_g=ln2_g, ln2_b=ln2_b,
    )
    weights = _prep_weights(params)

    half = MLA_ROPE // 2
    inv_freq = ROPE_THETA ** (-jnp.arange(half, dtype=F32) / half)
    ang = positions.astype(F32).reshape(batch * seq, 1) * inv_freq
    zpad = jnp.zeros((batch * seq, LANE - MLA_ROPE), F32)
    cs = jnp.concatenate([jnp.cos(ang), jnp.cos(ang), zpad], axis=-1)
    sn = jnp.concatenate([jnp.sin(ang), jnp.sin(ang), zpad], axis=-1)

    xf = x.reshape(batch * seq, d).astype(F32)

    stack = lambda a: a.reshape((-1,) + a.shape[2:])
    moe_w = (stack(moe_w_gate), stack(moe_w_up), stack(moe_w_down))
    seg = jnp.arange(GROUP_W) // RWKV_HD
    seg_ones = (seg[:, None] == seg[None, :]).astype(BF16)

    def body(carry, w):
        xc, xcb = carry
        return _layer(xc, xcb, cs, sn, w, moe_w, seg_ones, batch=batch, seq=seq, alpha=alpha), None

    (xf, _), _ = lax.scan(body, (xf, xf.astype(BF16)), weights)
    return xf.reshape(batch, seq, d).astype(x.dtype)
```

```python
import functools
import math

import jax
import jax.numpy as jnp
from jax import lax
from jax.experimental import pallas as pl
from jax.experimental.pallas import tpu as pltpu

F32 = jnp.float32
BF16 = jnp.bfloat16

D_MODEL = 2048
GROUP_W = 512
FOX_HD, FOX_HEADS = 64, 8
MLA_HEADS, MLA_NOPE, MLA_ROPE, MLA_VD = 4, 128, 64, 128
MLA_Q_RANK, MLA_KV_RANK = 384, 128
ROPE_THETA = 10000.0
RWKV_HD, RWKV_HEADS = 64, 8
RWKV_GN_EPS = 64e-5
GDN_HD, GDN_HEADS, GDN_CONV = 128, 4, 4
N_GROUPS, EXP_PER_GROUP, TOP_K, D_EXPERT = 4, 8, 2, 512
N_EXPERTS = N_GROUPS * EXP_PER_GROUP
CHUNK = 64
LN_EPS = 1e-5
RMS_EPS = 1e-6
LOG2E = math.log2(math.e)

LANE = 128
VMEM_LIMIT_BYTES = 56 * 1024 * 1024
ROW_TILE = 512
ATTN_TILE = 512
ATTN_KV_TILE = 256
SCAN_TILE = 256
MOE_TILE = 512
MOE_PARTS = 2
ONES_ROWS = 16


def _cparams(*sem):
    return pltpu.CompilerParams(dimension_semantics=sem, vmem_limit_bytes=VMEM_LIMIT_BYTES)


def _bdot(a, b):
    return jnp.dot(a.astype(BF16), b.astype(BF16), preferred_element_type=F32)


def _bdot_nt(a, b):
    return lax.dot_general(a.astype(BF16), b.astype(BF16), (((1,), (1,)), ((), ())), preferred_element_type=F32)


def _bdot_tn(a, b):
    return lax.dot_general(a.astype(BF16), b.astype(BF16), (((0,), (0,)), ((), ())), preferred_element_type=F32)


def _split3(x):
    hi = x.astype(BF16)
    r1 = x - hi.astype(F32)
    mid = r1.astype(BF16)
    lo = (r1 - mid.astype(F32)).astype(BF16)
    return hi, mid, lo


def _dot_exact_lhs(m, x):
    hi, mid, lo = _split3(x)
    d = lambda p: jnp.dot(m, p, preferred_element_type=F32)
    return d(hi) + d(mid) + d(lo)


def _sigmoid(x):
    return 1.0 / (1.0 + jnp.exp(-x))


def _softplus(x):
    return jnp.maximum(x, 0.0) + jnp.log(1.0 + jnp.exp(-jnp.abs(x)))


def _chunk_masks(n, c):
    r = lax.broadcasted_iota(jnp.int32, (n, n), 0)
    col = lax.broadcasted_iota(jnp.int32, (n, n), 1)
    same = (r // c) == (col // c)
    lower = jnp.logical_and(same, col <= r)
    strict = jnp.logical_and(same, col < r)
    return r, col, same, lower, strict


def _tri_inv_masks(r, col, c):
    blk = lambda s: (r // s) == (col // s)
    one = lambda cond: jnp.where(cond, 1.0, 0.0).astype(F32)
    offs = []
    s = 8
    while s < c:
        offs.append(one(jnp.logical_and(blk(2 * s), jnp.logical_not(blk(s)))))
        s *= 2
    return one(r == col), one(blk(8)), offs


def _tri_inv(ms, masks):
    eye, blk8, offs = masks
    bf = lambda a: a.astype(BF16)
    dot = lambda a, b: jnp.dot(a, b, preferred_element_type=F32)
    mdf = [m * blk8 for m in ms]
    mds = [bf(m) for m in mdf]
    xs = [eye + m for m in mdf]
    m2s = [bf(dot(md, md)) for md in mds]
    yield
    xs = [x + dot(m2, bf(x)) for x, m2 in zip(xs, m2s)]
    yield
    m4s = [bf(dot(m2, m2)) for m2 in m2s]
    yield
    xs = [x + dot(m4, bf(x)) for x, m4 in zip(xs, m4s)]
    yield
    n = ms[0].shape[0]
    s = 8
    for off in offs:
        xbs = [bf(x) for x in xs]
        if s % 16:
            ts = [bf(dot(xb, bf(m * off))) for xb, m in zip(xbs, ms)]
            yield
            xs = [x + dot(t, xb) for x, t, xb in zip(xs, ts, xbs)]
        else:
            low = lambda a: jnp.concatenate([a[j + s : j + 2 * s] for j in range(0, n, 2 * s)], axis=0)
            ts = [bf(dot(low(xb), bf(m * off))) for xb, m in zip(xbs, ms)]
            yield
            us = [dot(t, xb) for t, xb in zip(ts, xbs)]
            xs = [
                jnp.concatenate(
                    [p for j in range(0, n, 2 * s) for p in (x[j : j + s], x[j + s : j + 2 * s] + u[j // 2 : j // 2 + s])],
                    axis=0,
                )
                for x, u in zip(xs, us)
            ]
        yield
        s *= 2
    return xs


def _interleave(*stage_generators):
    live = list(stage_generators)
    while live:
        for g in list(live):
            try:
                next(g)
            except StopIteration:
                live.remove(g)


def _inproj_kernel(x_ref, w_ref, o_ref, *tail_refs, tail):
    acc = jnp.dot(x_ref[...], w_ref[...], preferred_element_type=F32)
    n = acc.shape[1]
    if tail:
        o_ref[...] = acc[:, : n - tail].astype(o_ref.dtype)
        tail_refs[0][...] = acc[:, n - tail :]
    else:
        o_ref[...] = acc.astype(o_ref.dtype)


def _inproj(xb, w, tail):
    t, d = xb.shape
    n = w.shape[1]
    tm = min(ROW_TILE, t)
    out_shape = [jax.ShapeDtypeStruct((t, n - tail), BF16)]
    out_specs = [pl.BlockSpec((tm, n - tail), lambda i: (i, 0))]
    if tail:
        out_shape.append(jax.ShapeDtypeStruct((t, tail), F32))
        out_specs.append(pl.BlockSpec((tm, tail), lambda i: (i, 0)))
    return pl.pallas_call(
        functools.partial(_inproj_kernel, tail=tail),
        grid=(t // tm,),
        in_specs=[pl.BlockSpec((tm, d), lambda i: (i, 0)), pl.BlockSpec((d, n), lambda i: (0, 0))],
        out_specs=out_specs,
        out_shape=out_shape,
        compiler_params=_cparams("parallel"),
        name="inproj",
    )(xb, w)


def _attn_kernel(*refs, heads, dk, dv, chunk, tq, tk, seq, use_bias):
    if use_bias:
        q_ref, k_ref, vt_ref, gate_ref, gb_ref, g_ref, o_ref, kb_ref, m_ref, acc_ref = refs
    else:
        q_ref, k_ref, vt_ref, g_ref, o_ref, m_ref, acc_ref = refs
    paired = dk < LANE
    dkp = LANE if paired else dk
    dva = dv + ONES_ROWS
    i = pl.program_id(1)
    lane = lax.broadcasted_iota(jnp.int32, (tq, LANE), 1)

    if use_bias:
        @pl.when(i == 0)
        def _():
            r = lax.broadcasted_iota(jnp.int32, (tk, tk), 0)
            c = lax.broadcasted_iota(jnp.int32, (tk, tk), 1)
            tri = jnp.where(c <= r, 1.0, 0.0).astype(BF16)
            head_lane = lax.broadcasted_iota(jnp.int32, (tk, LANE), 1) < heads
            carry = jnp.zeros((1, LANE), F32)
            for j in range(seq // tk):
                rows = slice(j * tk, (j + 1) * tk)
                log_f = -_softplus(-(gate_ref[rows, :] + gb_ref[...])) * LOG2E
                cum = _dot_exact_lhs(tri, log_f) + carry
                carry = cum[tk - 1 : tk, :]
                hi, mid, lo = (jnp.where(head_lane, p.astype(F32), 0.0) for p in _split3(cum))
                pieces = hi + pltpu.roll(mid, heads, axis=1) + pltpu.roll(lo, 2 * heads, axis=1)
                kb_ref[rows, :] = (-pieces).astype(BF16)

    m_ref[...] = jnp.full(m_ref.shape, -1e30, F32)
    acc_ref[...] = jnp.zeros(acc_ref.shape, F32)

    qs = []
    for h in range(heads):
        if paired:
            slab = q_ref[:, (h // 2) * LANE : (h // 2 + 1) * LANE]
            q = jnp.where(lane // dk == h % 2, slab, jnp.zeros_like(slab))
        else:
            q = q_ref[:, h * dkp : (h + 1) * dkp]
        if use_bias:
            pick = jnp.logical_and(lane % heads == h, lane < 3 * heads)
            q = jnp.concatenate([q, jnp.where(pick, 1.0, 0.0).astype(q.dtype)], axis=1)
        qs.append(q)
    ones_rows = jnp.ones((ONES_ROWS, tk), BF16)

    def step(off, key_shift):
        if key_shift is not None:
            kr = lax.broadcasted_iota(jnp.int32, (tk, tq), 0) + key_shift
            qc = lax.broadcasted_iota(jnp.int32, (tk, tq), 1)
            allowed = (kr // chunk) <= (qc // chunk)
        masked = key_shift is not None
        sts = []
        for h in range(heads):
            slab = h // 2 if paired else h
            k = k_ref[pl.ds(off, tk), slab * dkp : (slab + 1) * dkp]
            if use_bias:
                k = jnp.concatenate([k, kb_ref[pl.ds(off, tk), :]], axis=1)
            sts.append(lax.dot_general(k, qs[h], (((1,), (1,)), ((), ())), preferred_element_type=F32))
        ps, alphas = [], []
        for h in range(heads):
            st = sts[h]
            if masked:
                st = jnp.where(allowed, st, -1e30)
            m_old = m_ref[h, 0:1, :]
            m_new = jnp.maximum(m_old, jnp.max(st, axis=0, keepdims=True))
            m_ref[h, 0:1, :] = m_new
            ps.append(jnp.exp2(st - m_new).astype(BF16))
            alphas.append(jnp.exp2(m_old - m_new))
        for h in range(heads):
            vt_h = jnp.concatenate([vt_ref[h * dv : (h + 1) * dv, pl.ds(off, tk)], ones_rows], axis=0)
            rows = slice(h * dva, (h + 1) * dva)
            acc_ref[rows, :] = alphas[h] * acc_ref[rows, :] + jnp.dot(vt_h, ps[h], preferred_element_type=F32)

    def body(j, carry):
        step(pl.multiple_of(j * tk, tk), None)
        return carry

    lax.fori_loop(0, i * (tq // tk), body, 0)
    for d in range(tq // tk):
        step(pl.multiple_of(i * tq + d * tk, tk), d * tk)

    ot = jnp.concatenate(
        [acc_ref[h * dva : h * dva + dv, :] / acc_ref[h * dva + dv : h * dva + dv + 1, :] for h in range(heads)], axis=0
    )
    ot = ot * lax.rsqrt(jnp.mean(ot * ot, axis=0, keepdims=True) + RMS_EPS)
    o_ref[...] = (ot.T * g_ref[...]).astype(o_ref.dtype)


def _attention(q_arr, q_col, k_arr, k_col, vt, gate, gain, *, batch, seq, heads, dk, dv, chunk):
    bias = gate
    assert dk % LANE == 0 or (2 * dk == LANE and heads % 2 == 0)
    tq = min(ATTN_TILE, seq)
    tk = min(ATTN_KV_TILE, tq)
    nq = seq // tq
    t = batch * seq
    in_specs = [
        pl.BlockSpec((tq, heads * dk), lambda b, i: (b * nq + i, q_col)),
        pl.BlockSpec((seq, heads * dk), lambda b, i: (b, k_col)),
        pl.BlockSpec((None, heads * dv, seq), lambda b, i: (b, 0, 0)),
    ]
    args = [q_arr, k_arr, vt]
    scratch = []
    if bias is not None:
        in_specs += [pl.BlockSpec((seq, LANE), lambda b, i: (b, 0)), pl.BlockSpec((1, LANE), lambda b, i: (0, 0))]
        args += list(gate)
        assert 3 * heads <= LANE
        scratch.append(pltpu.VMEM((seq, LANE), BF16))
    in_specs.append(pl.BlockSpec((1, heads * dv), lambda b, i: (0, 0)))
    args.append(gain)
    scratch += [pltpu.VMEM((heads, 8, tq), F32), pltpu.VMEM((heads * (dv + ONES_ROWS), tq), F32)]
    return pl.pallas_call(
        functools.partial(_attn_kernel, heads=heads, dk=dk, dv=dv, chunk=chunk, tq=tq, tk=tk, seq=seq, use_bias=bias is not None),
        grid=(batch, nq),
        in_specs=in_specs,
        out_specs=pl.BlockSpec((tq, heads * dv), lambda b, i: (b * nq + i, 0)),
        out_shape=jax.ShapeDtypeStruct((t, heads * dv), BF16),
        scratch_shapes=scratch,
        compiler_params=_cparams("parallel", "arbitrary"),
        name="attention",
    )(*args)


def _mla_prep_kernel(u_ref, cs_ref, sn_ref, qg_ref, kvg_ref, wq_ref, wkv_ref, q_ref, k_ref, v_ref):
    u = u_ref[...].astype(F32)
    cs = cs_ref[...]
    sn = sn_ref[...]

    def rms(x, g):
        return x * lax.rsqrt(jnp.mean(x * x, axis=-1, keepdims=True) + RMS_EPS) * g

    qo = _bdot(rms(u[:, :MLA_Q_RANK], qg_ref[...]), wq_ref[...])
    kvo = _bdot(rms(u[:, MLA_Q_RANK : MLA_Q_RANK + MLA_KV_RANK], kvg_ref[...]), wkv_ref[...])
    c0 = MLA_Q_RANK + MLA_KV_RANK
    kpe = (u[:, c0 : c0 + LANE] * cs + u[:, c0 + LANE : c0 + 2 * LANE] * sn).astype(k_ref.dtype)
    nn = MLA_HEADS * MLA_NOPE
    for h in range(MLA_HEADS):
        a = h * 2 * LANE
        q_ref[:, a : a + LANE] = qo[:, h * LANE : (h + 1) * LANE].astype(q_ref.dtype)
        qpe = qo[:, nn + h * LANE : nn + (h + 1) * LANE] * cs + qo[:, 2 * nn + h * LANE : 2 * nn + (h + 1) * LANE] * sn
        q_ref[:, a + LANE : a + 2 * LANE] = qpe.astype(q_ref.dtype)
        k_ref[:, a : a + LANE] = kvo[:, h * LANE : (h + 1) * LANE].astype(k_ref.dtype)
        k_ref[:, a + LANE : a + 2 * LANE] = kpe
    v_ref[...] = kvo[:, nn:].T.astype(v_ref.dtype)


def _mla_prep(u_mla, cs, sn, qg, kvg, wq, wkv, *, batch, seq):
    t = u_mla.shape[0]
    tm = min(ROW_TILE, seq)
    nt = seq // tm
    row = lambda n: pl.BlockSpec((tm, n), lambda i: (i, 0))
    full = lambda a: pl.BlockSpec(a.shape, lambda i: (0,) * a.ndim)
    wide = MLA_HEADS * 2 * LANE
    dvs = MLA_HEADS * MLA_VD
    return pl.pallas_call(
        _mla_prep_kernel,
        grid=(t // tm,),
        in_specs=[row(u_mla.shape[1]), row(LANE), row(LANE), full(qg), full(kvg), full(wq), full(wkv)],
        out_specs=[row(wide), row(wide), pl.BlockSpec((None, dvs, tm), lambda i: (i // nt, 0, i % nt))],
        out_shape=[
            jax.ShapeDtypeStruct((t, wide), BF16),
            jax.ShapeDtypeStruct((t, wide), BF16),
            jax.ShapeDtypeStruct((batch, dvs, seq), BF16),
        ],
        compiler_params=_cparams("parallel"),
        name="mla_prep",
    )(u_mla, cs, sn, qg, kvg, wq, wkv)


def _gdn_stages(u_ref, t_ref, cw_ref, alog_ref, dtb_ref, ng_ref, o_ref, xs_ref, st_ref, *, tt, c):
    i = pl.program_id(1)
    gw, hd = GROUP_W, GDN_HD

    @pl.when(i == 0)
    def _():
        xs_ref[0:8, :] = jnp.zeros((8, 3 * gw), F32)
        st_ref[...] = jnp.zeros(st_ref.shape, F32)

    xs_ref[8 : 8 + tt, :] = u_ref[:, : 3 * gw].astype(F32)
    cw = cw_ref[...]
    conv = cw[0:1, :] * xs_ref[5 : 5 + tt, :]
    for j in range(1, GDN_CONV):
        conv = conv + cw[j : j + 1, :] * xs_ref[5 + j : 5 + j + tt, :]
    xs_ref[0:8, :] = xs_ref[tt : tt + 8, :]
    qkv = conv * _sigmoid(conv)
    yield

    tail = t_ref[...]
    beta = _sigmoid(tail[:, :LANE])
    g = -jnp.exp(alog_ref[...]) * _softplus(tail[:, LANE:] + dtb_ref[...])

    r, col, same, lower, strict = _chunk_masks(tt, c)
    tri = jnp.where(lower, 1.0, 0.0).astype(BF16)
    ones_blk = jnp.where(same, 1.0, 0.0).astype(BF16)
    gc = _dot_exact_lhs(tri, g)
    gtot = _dot_exact_lhs(ones_blk, g)
    gct = gc.T
    yield

    def l2n(x):
        return x * lax.rsqrt(jnp.sum(x * x, axis=-1, keepdims=True) + 1e-6)

    heads = range(GDN_HEADS)
    bf = lambda a: a.astype(BF16)
    lmats, attns, kbs, vbs, qds, kds, egs = [], [], [], [], [], [], []
    for h in heads:
        qf = l2n(qkv[:, h * hd : (h + 1) * hd]) * (hd**-0.5)
        qh = bf(qf)
        kf = l2n(qkv[:, gw + h * hd : gw + (h + 1) * hd])
        kh = bf(kf)
        vh = qkv[:, 2 * gw + h * hd : 2 * gw + (h + 1) * hd]
        gcol = gc[:, h : h + 1]
        dec = jnp.exp(jnp.minimum(gcol - gct[h : h + 1, :], 0.0))
        bcol = beta[:, h : h + 1]
        kb = kf * bcol
        eg = jnp.exp(gcol)
        lmats.append(jnp.where(strict, _bdot_nt(kb, kh) * dec, 0.0))
        attns.append(bf(jnp.where(lower, _bdot_nt(qh, kh) * dec, 0.0)))
        kbs.append(bf(kb * eg))
        vbs.append(bf(vh * bcol))
        qds.append(bf(qf * eg))
        kds.append(bf(kf * jnp.exp(gtot[:, h : h + 1] - gcol)))
        yield
    tinvs = yield from _tri_inv([-m for m in lmats], _tri_inv_masks(r, col, c))
    tinvs = [bf(t) for t in tinvs]
    uws = [jnp.dot(t, jnp.concatenate([vb, kb], axis=1), preferred_element_type=F32) for t, vb, kb in zip(tinvs, vbs, kbs)]
    uvals = [uw[:, :hd] for uw in uws]
    wcums = [bf(uw[:, hd:]) for uw in uws]
    yield
    states = [st_ref[h] for h in heads]
    outs = [[] for _ in heads]
    for cc in range(tt // c):
        rs = slice(cc * c, (cc + 1) * c)
        sbs = [bf(s) for s in states]
        vnews = [uvals[h][rs] - jnp.dot(wcums[h][rs], sbs[h], preferred_element_type=F32) for h in heads]
        yield
        for h in heads:
            outs[h].append(jnp.dot(qds[h][rs], sbs[h], preferred_element_type=F32) + _bdot(attns[h][rs, rs], vnews[h]))
            glast = jnp.exp(gtot[cc * c : cc * c + 1, h : h + 1])
            states[h] = states[h] * glast + _bdot_tn(kds[h][rs], vnews[h])
        yield
    for h in heads:
        st_ref[h] = states[h]
        o = jnp.concatenate(outs[h], axis=0)
        o = o * lax.rsqrt(jnp.mean(o * o, axis=-1, keepdims=True) + RMS_EPS) * ng_ref[...]
        z = u_ref[:, 3 * gw + h * hd : 3 * gw + (h + 1) * hd].astype(F32)
        o_ref[:, h * hd : (h + 1) * hd] = (o * (z * _sigmoid(z))).astype(o_ref.dtype)


def _rwkv_stages(u_ref, mu_ref, w0_ref, w2_ref, a0_ref, a2_ref, g2_ref, kk_ref, ka_ref, rk_ref, lng_ref, lnb_ref,
                 e_ref, o_ref, xs_ref, st_ref, *, tt, c):
    i = pl.program_id(1)
    gw, hd = GROUP_W, RWKV_HD

    @pl.when(i == 0)
    def _():
        xs_ref[0:8, :] = jnp.zeros((8, xs_ref.shape[1]), F32)
        st_ref[...] = jnp.zeros(st_ref.shape, F32)

    u = u_ref[...].astype(F32)
    xs_ref[8 : 8 + tt, :] = u
    prev = xs_ref[7 : 7 + tt, :]
    xs_ref[0:8, :] = xs_ref[tt : tt + 8, :]
    x = u + mu_ref[...] * (prev - u)
    rr = x[:, :gw]
    k = x[:, gw : 2 * gw]
    v = x[:, 2 * gw : 3 * gw]
    wa = x[:, 3 * gw : 3 * gw + LANE]
    glo = x[:, 3 * gw + LANE :]
    w = -_softplus(-(w0_ref[...] + _bdot(jnp.tanh(wa), w2_ref[...]))) - 0.5
    ld = -jnp.exp(w)
    gate_a = _sigmoid(a0_ref[...] + _bdot(wa, a2_ref[...]))
    gate_g = _bdot(_sigmoid(glo), g2_ref[...])
    e = e_ref[...]

    def segsum(y):
        hi = y.astype(BF16)
        lo = (y - hi.astype(F32)).astype(BF16)
        return jnp.dot(hi, e, preferred_element_type=F32) + jnp.dot(lo, e, preferred_element_type=F32)

    kkr = k * kk_ref[...]
    kk = kkr * lax.rsqrt(segsum(kkr * kkr) + 1e-6)
    k2 = k * (1.0 + (gate_a - 1.0) * ka_ref[...])
    bonus = segsum(rr * k2 * rk_ref[...]) * v
    yield

    r, col, same, lower, strict = _chunk_masks(tt, c)
    tri = jnp.where(lower, 1.0, 0.0).astype(BF16)
    ones_blk = jnp.where(same, 1.0, 0.0).astype(BF16)
    cum = _dot_exact_lhs(tri, ld)
    ctot = _dot_exact_lhs(ones_blk, ld)
    encum = jnp.exp(-cum)
    edec = jnp.exp(ctot - cum)
    kka = kk * gate_a
    rt = rr * jnp.exp(cum)
    at = -kk * jnp.exp(cum - ld)
    bt = kka * encum
    kt = k2 * encum
    bd = kka * edec
    kd = k2 * edec
    pc = jnp.exp(ctot)
    yield

    heads = range(RWKV_HEADS)
    bf = lambda a: a.astype(BF16)
    dot = lambda a, b: jnp.dot(a, b, preferred_element_type=F32)
    dot_nt = lambda a, b: lax.dot_general(a, b, (((1,), (1,)), ((), ())), preferred_element_type=F32)
    dot_tn = lambda a, b: lax.dot_general(a, b, (((0,), (0,)), ((), ())), preferred_element_type=F32)
    at_b, rt_b, bt_b, kt_b, v_b, bd_b, kd_b = bf(at), bf(rt), bf(bt), bf(kt), bf(v), bf(bd), bf(kd)
    sls = [slice(h * hd, (h + 1) * hd) for h in heads]
    strict_f = jnp.where(strict, 1.0, 0.0).astype(F32)
    lower_f = jnp.where(lower, 1.0, 0.0).astype(F32)
    mabs = [dot_nt(at_b[:, sl], bt_b[:, sl]) * strict_f for sl in sls]
    yield
    maks = [bf(dot_nt(at_b[:, sl], kt_b[:, sl]) * strict_f) for sl in sls]
    yield
    arbs = [bf(dot_nt(rt_b[:, sl], bt_b[:, sl]) * lower_f) for sl in sls]
    yield
    arks = [bf(dot_nt(rt_b[:, sl], kt_b[:, sl]) * lower_f) for sl in sls]
    yield
    tinvs = yield from _tri_inv(mabs, _tri_inv_masks(r, col, c))
    tinvs = [bf(t) for t in tinvs]
    wmats = [bf(dot(t, at_b[:, sl])) for t, sl in zip(tinvs, sls)]
    mkvs = [bf(dot(m, v_b[:, sl])) for m, sl in zip(maks, sls)]
    yield
    umats = [dot(t, mkv) for t, mkv in zip(tinvs, mkvs)]
    yconsts = [dot(m, v_b[:, sl]) for m, sl in zip(arks, sls)]
    yield
    states = [st_ref[h] for h in heads]
    outs = [[] for _ in heads]
    for cc in range(tt // c):
        rs = slice(cc * c, (cc + 1) * c)
        sbs = [bf(s) for s in states]
        sas = [dot_nt(wmats[h][rs], sbs[h]) + umats[h][rs] for h in heads]
        yield
        for h in heads:
            sab = bf(sas[h])
            outs[h].append(dot_nt(rt_b[rs, sls[h]], sbs[h]) + dot(arbs[h][rs, rs], sab) + yconsts[h][rs])
            states[h] = (states[h] * pc[cc * c : cc * c + 1, sls[h]] + dot_tn(sab, bd_b[rs, sls[h]])
                         + dot_tn(v_b[rs, sls[h]], kd_b[rs, sls[h]]))
        yield
    for h in heads:
        st_ref[h] = states[h]
    y = jnp.concatenate([jnp.concatenate(o, axis=0) for o in outs], axis=-1)
    mean = segsum(y) * (1.0 / hd)
    d = y - mean
    var = segsum(d * d) * (1.0 / hd)
    yn = d * lax.rsqrt(var + RWKV_GN_EPS) * lng_ref[...] + lnb_ref[...]
    o_ref[...] = ((yn + bonus) * gate_g).astype(o_ref.dtype)


def _scan_mixers_kernel(*refs, n_rwkv_in, n_gdn_in, tt, c):
    rwkv_in = refs[:n_rwkv_in]
    gdn_in = refs[n_rwkv_in : n_rwkv_in + n_gdn_in]
    o_rwkv, o_gdn, xs_rwkv, st_rwkv, xs_gdn, st_gdn = refs[n_rwkv_in + n_gdn_in :]
    _interleave(
        _rwkv_stages(*rwkv_in, o_rwkv, xs_rwkv, st_rwkv, tt=tt, c=c),
        _gdn_stages(*gdn_in, o_gdn, xs_gdn, st_gdn, tt=tt, c=c),
    )


def _scan_mixers(rwkv_args, gdn_args, *, batch, seq):
    tt = min(SCAN_TILE, seq)
    nt = seq // tt
    t = batch * seq
    tile = lambda a: pl.BlockSpec((tt, a.shape[1]), lambda b, i: (b * nt + i, 0))
    full = lambda a: pl.BlockSpec(a.shape, lambda b, i: (0,) * a.ndim)
    u_rwkv, u_gdn, gdn_tail = rwkv_args[0], gdn_args[0], gdn_args[1]
    in_specs = [tile(u_rwkv)] + [full(p) for p in rwkv_args[1:]] + [tile(u_gdn), tile(gdn_tail)] + [full(p) for p in gdn_args[2:]]
    out = pl.BlockSpec((tt, GROUP_W), lambda b, i: (b * nt + i, 0))
    return pl.pallas_call(
        functools.partial(_scan_mixers_kernel, n_rwkv_in=len(rwkv_args), n_gdn_in=len(gdn_args), tt=tt, c=CHUNK),
        grid=(batch, nt),
        in_specs=in_specs,
        out_specs=[out, out],
        out_shape=[jax.ShapeDtypeStruct((t, GROUP_W), BF16)] * 2,
        scratch_shapes=[
            pltpu.VMEM((tt + 8, u_rwkv.shape[1]), F32),
            pltpu.VMEM((RWKV_HEADS, RWKV_HD, RWKV_HD), F32),
            pltpu.VMEM((tt + 8, 3 * GROUP_W), F32),
            pltpu.VMEM((GDN_HEADS, GDN_HD, GDN_HD), F32),
        ],
        compiler_params=_cparams("parallel", "arbitrary"),
        name="rwkv7_gdn",
    )(*rwkv_args, *gdn_args)


def _layer_norm(h, g, b):
    mu = jnp.mean(h, axis=-1, keepdims=True)
    d = h - mu
    var = jnp.mean(d * d, axis=-1, keepdims=True)
    return d * lax.rsqrt(var + LN_EPS) * g + b


def _route_rows(lg):
    neg = -1e30
    lane = lax.broadcasted_iota(jnp.int32, lg.shape, 1)
    lane_f = lane.astype(F32)
    first = lambda hit: jnp.min(jnp.where(hit, lane_f, float(LANE)), axis=-1, keepdims=True)
    is_g = lane < N_GROUPS
    gl = jnp.where(is_g, lg, neg)
    gmax = jnp.max(gl, axis=-1, keepdims=True)
    p_grp = 1.0 / jnp.sum(jnp.where(is_g, jnp.exp(gl - gmax), 0.0), axis=-1, keepdims=True)
    grp = first(gl == gmax)
    lo = N_GROUPS + grp * EXP_PER_GROUP
    el = jnp.where(jnp.logical_and(lane_f >= lo, lane_f < lo + EXP_PER_GROUP), lg, neg)
    v1 = jnp.max(el, axis=-1, keepdims=True)
    i1 = first(el == v1)
    el2 = jnp.where(lane_f == i1, neg, el)
    v2 = jnp.max(el2, axis=-1, keepdims=True)
    i2 = first(el2 == v2)
    r = jnp.exp(v2 - v1)
    g1 = p_grp / (1.0 + r)
    out = jnp.where(lane == 0, i1 - N_GROUPS, 0.0)
    out = jnp.where(lane == 1, i2 - N_GROUPS, out)
    out = jnp.where(lane == 2, g1, out)
    return jnp.where(lane == 3, g1 * r, out)


def _outproj_kernel(yf_ref, ym_ref, yr_ref, yg_ref, w_ref, x_ref, g_ref, b_ref, wr_ref, rb_ref, tril_ref,
                    xo_ref, xb_ref, lg_ref, cnt_ref, *, alpha):
    mixed = jnp.concatenate([yf_ref[...], ym_ref[...], yr_ref[...], yg_ref[...]], axis=1)
    acc = jnp.dot(mixed, w_ref[...], preferred_element_type=F32)
    xn = _layer_norm(alpha * x_ref[...] + acc, g_ref[...], b_ref[...])
    xo_ref[...] = xn
    xh = xn.astype(BF16)
    xb_ref[...] = xh
    xl = (xn - xh.astype(F32)).astype(BF16)
    d = lambda a, b: jnp.dot(a, b, preferred_element_type=F32)
    both_w = d(xh, wr_ref[...])
    routed = _route_rows(both_w[:, :LANE] + both_w[:, LANE:] + d(xl, wr_ref[:, :LANE]) + rb_ref[...])

    @pl.when(pl.program_id(0) == 0)
    def _():
        cnt_ref[...] = jnp.zeros(cnt_ref.shape, F32)

    lane = lax.broadcasted_iota(jnp.int32, routed.shape, 1).astype(F32)
    oh = [jnp.where(lane == routed[:, k : k + 1], 1.0, 0.0) for k in range(TOP_K)]
    both = oh[0] + oh[1]
    before = jnp.dot(tril_ref[...], both.astype(BF16), preferred_element_type=F32) + cnt_ref[0:1, :]
    for k in range(TOP_K):
        rank = jnp.sum(before * oh[k], axis=-1, keepdims=True)
        routed = jnp.where(lane == 2 * TOP_K + k, rank, routed)
    cnt_ref[...] = cnt_ref[...] + jnp.sum(both, axis=0, keepdims=True)
    lg_ref[...] = routed


def _outproj(ys, w_out, x, ln_g, ln_b, w_router, r_bias, alpha):
    t, d = x.shape
    tm = min(ROW_TILE, t)
    row = lambda n: pl.BlockSpec((tm, n), lambda i: (i, 0))
    full = lambda a: pl.BlockSpec(a.shape, lambda i: (0,) * a.ndim)
    idx = jnp.arange(tm)
    tril = (idx[None, :] < idx[:, None]).astype(BF16)
    return pl.pallas_call(
        functools.partial(_outproj_kernel, alpha=alpha),
        grid=(t // tm,),
        in_specs=[row(GROUP_W)] * 4
        + [full(w_out), row(d), full(ln_g), full(ln_b), full(w_router), full(r_bias), full(tril)],
        out_specs=[row(d), row(d), row(LANE), pl.BlockSpec((8, LANE), lambda i: (0, 0))],
        out_shape=[
            jax.ShapeDtypeStruct((t, d), F32),
            jax.ShapeDtypeStruct((t, d), BF16),
            jax.ShapeDtypeStruct((t, LANE), F32),
            jax.ShapeDtypeStruct((8, LANE), F32),
        ],
        compiler_params=_cparams("arbitrary"),
        name="outproj_ln_router",
    )(*ys, w_out, x, ln_g, ln_b, w_router, r_bias, tril)


def _expert_kernel(be_ref, nu_ref, x_ref, wg_ref, wu_ref, wd_ref, *rest, first_block):
    o_ref, wg_s, wu_s, wd_s = rest[-4:]
    i = pl.program_id(0)
    blk = i + first_block
    used = blk < nu_ref[0]

    @pl.when(jnp.logical_and(used, jnp.logical_or(i == 0, be_ref[blk] != be_ref[jnp.maximum(blk - 1, 0)])))
    def _():
        wg_s[...] = wg_ref[...].astype(BF16)
        wu_s[...] = wu_ref[...].astype(BF16)
        wd_s[...] = wd_ref[...].astype(BF16)

    @pl.when(used)
    def _():
        x = x_ref[...]
        a = jnp.dot(x, wg_s[...], preferred_element_type=F32)
        b = jnp.dot(x, wu_s[...], preferred_element_type=F32)
        hmid = (a * _sigmoid(a) * b).astype(BF16)
        o_ref[...] = jnp.dot(hmid, wd_s[...], preferred_element_type=F32).astype(o_ref.dtype)

    @pl.when(jnp.logical_not(used))
    def _():
        o_ref[...] = jnp.zeros(o_ref.shape, o_ref.dtype)


def _experts(block_weight, n_used, xs_part, w_gate, w_up, w_down, *, first_block, n_slots, y_prev):
    d = xs_part.shape[1]
    tb = MOE_TILE
    de = w_gate.shape[-1]
    weight = lambda shape: pl.BlockSpec((None,) + shape, lambda i, be, nu: (be[i + first_block], 0, 0))
    in_specs = [pl.BlockSpec((tb, d), lambda i, be, nu: (i, 0)), weight((d, de)), weight((d, de)), weight((de, d))]
    args = [block_weight, n_used, xs_part, w_gate, w_up, w_down]
    aliases = {}
    if y_prev is not None:
        in_specs.append(pl.BlockSpec(memory_space=pl.ANY))
        aliases = {len(args): 0}
        args.append(y_prev)
    grid_spec = pltpu.PrefetchScalarGridSpec(
        num_scalar_prefetch=2,
        grid=(xs_part.shape[0] // tb,),
        in_specs=in_specs,
        out_specs=pl.BlockSpec((tb, d), lambda i, be, nu: (i + first_block, 0)),
        scratch_shapes=[pltpu.VMEM((d, de), BF16), pltpu.VMEM((d, de), BF16), pltpu.VMEM((de, d), BF16)],
    )
    return pl.pallas_call(
        functools.partial(_expert_kernel, first_block=first_block),
        grid_spec=grid_spec,
        out_shape=jax.ShapeDtypeStruct((n_slots, d), BF16),
        input_output_aliases=aliases,
        compiler_params=_cparams("arbitrary"),
        name="experts",
    )(*args)


def _ln2_kernel(x_ref, y1_ref, y2_ref, r_ref, g_ref, b_ref, xo_ref, xb_ref, *, alpha):
    routed = r_ref[...]
    moe = routed[:, TOP_K : TOP_K + 1] * y1_ref[...].astype(F32) + routed[:, TOP_K + 1 : TOP_K + 2] * y2_ref[...].astype(F32)
    xn = _layer_norm(alpha * x_ref[...] + moe, g_ref[...], b_ref[...])
    xo_ref[...] = xn
    xb_ref[...] = xn.astype(BF16)


def _ln2(x, y1, y2, routed, g, b, alpha):
    t, d = x.shape
    tm = min(ROW_TILE, t)
    row = lambda n: pl.BlockSpec((tm, n), lambda i: (i, 0))
    full = lambda a: pl.BlockSpec(a.shape, lambda i: (0,) * a.ndim)
    return pl.pallas_call(
        functools.partial(_ln2_kernel, alpha=alpha),
        grid=(t // tm,),
        in_specs=[row(d), row(d), row(d), row(LANE), full(g), full(b)],
        out_specs=[row(d), row(d)],
        out_shape=[jax.ShapeDtypeStruct((t, d), F32), jax.ShapeDtypeStruct((t, d), BF16)],
        compiler_params=_cparams("parallel"),
        name="residual_ln2",
    )(x, y1, y2, routed, g, b)


def _pad_cols(w, n):
    return jnp.pad(w, [(0, 0)] * (w.ndim - 1) + [(0, n - w.shape[-1])])


def _rot_half_cols(w):
    half = w.shape[-1] // 2
    return jnp.concatenate([-w[..., half:], w[..., :half]], axis=-1)


def _prep_weights(p):
    w_in = p["w_in"]
    c_fox = 3 * GROUP_W + FOX_HEADS
    c_mla = MLA_Q_RANK + MLA_KV_RANK + MLA_ROPE
    c_rwkv = 3 * GROUP_W + 2 * 64 + 128
    o_mla = c_fox
    o_rwkv = o_mla + c_mla
    o_gdn = o_rwkv + c_rwkv
    row = lambda a: a[:, None, :].astype(F32)

    w_fox = jnp.concatenate(
        [w_in[..., :GROUP_W] * (FOX_HD**-0.5 * LOG2E), w_in[..., GROUP_W : 3 * GROUP_W], _pad_cols(w_in[..., 3 * GROUP_W : c_fox], LANE)],
        axis=-1,
    )
    kpe_w = w_in[..., o_mla + MLA_Q_RANK + MLA_KV_RANK : o_mla + c_mla]
    w_mla = jnp.concatenate(
        [w_in[..., o_mla : o_mla + MLA_Q_RANK + MLA_KV_RANK], _pad_cols(kpe_w, LANE), _pad_cols(_rot_half_cols(kpe_w), LANE)],
        axis=-1,
    )
    w_rwkv = w_in[..., o_rwkv:o_gdn]
    g0 = o_gdn + 4 * GROUP_W
    w_gdn = jnp.concatenate(
        [w_in[..., o_gdn:g0], _pad_cols(w_in[..., g0 : g0 + GDN_HEADS], LANE), _pad_cols(w_in[..., g0 + GDN_HEADS :], LANE)],
        axis=-1,
    )

    nl = w_in.shape[0]
    scale = (MLA_NOPE + MLA_ROPE) ** -0.5 * LOG2E
    wq = p["mla_w_uq"].reshape(nl, MLA_Q_RANK, MLA_HEADS, MLA_NOPE + MLA_ROPE) * scale
    wq_nope = wq[..., :MLA_NOPE].reshape(nl, MLA_Q_RANK, -1)
    wq_pe = wq[..., MLA_NOPE:]
    wq_p = jnp.concatenate(
        [wq_nope, _pad_cols(wq_pe, LANE).reshape(nl, MLA_Q_RANK, -1), _pad_cols(_rot_half_cols(wq_pe), LANE).reshape(nl, MLA_Q_RANK, -1)],
        axis=-1,
    )
    wkv = p["mla_w_ukv"].reshape(nl, MLA_KV_RANK, MLA_HEADS, MLA_NOPE + MLA_VD)
    wkv_p = jnp.concatenate([wkv[..., :MLA_NOPE].reshape(nl, MLA_KV_RANK, -1), wkv[..., MLA_NOPE:].reshape(nl, MLA_KV_RANK, -1)], axis=-1)

    zeros64 = jnp.zeros((nl, 64, GROUP_W), F32)
    w_router = _pad_cols(jnp.concatenate([p["moe_w_grp"], p["moe_w_exp"]], axis=-1), LANE)
    wr_hi = w_router.astype(BF16)
    wr_lo = (w_router - wr_hi.astype(F32)).astype(BF16)
    return dict(
        w_fox=w_fox.astype(BF16), w_mla=w_mla.astype(BF16), w_rwkv=w_rwkv.astype(BF16), w_gdn=w_gdn.astype(BF16),
        layer=jnp.arange(nl, dtype=jnp.int32),
        fox_b_f=row(_pad_cols(p["fox_b_f"], LANE)), fox_out_g=row(p["fox_out_g"]),
        mla_qg=row(p["mla_q_norm_g"]), mla_kvg=row(p["mla_kv_norm_g"]), mla_wq=wq_p.astype(BF16), mla_wkv=wkv_p.astype(BF16),
        mla_out_g=row(p["mla_out_g"]),
        rwkv_mu=row(p["rwkv_mu"]), rwkv_w0=row(p["rwkv_w0"]),
        rwkv_w2=jnp.concatenate([p["rwkv_w2"], zeros64], axis=1).astype(BF16),
        rwkv_a0=row(p["rwkv_a0"]), rwkv_a2=jnp.concatenate([zeros64, p["rwkv_a2"]], axis=1).astype(BF16),
        rwkv_g2=p["rwkv_g2"].astype(BF16), rwkv_k_k=row(p["rwkv_k_k"]), rwkv_k_a=row(p["rwkv_k_a"]),
        rwkv_r_k=row(p["rwkv_r_k"]), rwkv_ln_g=row(p["rwkv_ln_g"]), rwkv_ln_b=row(p["rwkv_ln_b"]),
        gdn_conv_w=p["gdn_conv_w"].astype(F32), gdn_a_log=row(_pad_cols(p["gdn_a_log"], LANE)),
        gdn_dt_bias=row(_pad_cols(p["gdn_dt_bias"], LANE)), gdn_norm_g=row(p["gdn_norm_g"]),
        w_out=p["w_out"].astype(BF16), ln1_g=row(p["ln1_g"]), ln1_b=row(p["ln1_b"]),
        w_router=jnp.concatenate([wr_hi, wr_lo], axis=-1), r_bias=row(_pad_cols(jnp.concatenate([p["moe_b_grp"], p["moe_b_exp"]], axis=-1), LANE)),
        ln2_g=row(p["ln2_g"]), ln2_b=row(p["ln2_b"]),
    )


def _route(routed, counts, tb):
    t = routed.shape[0]
    a = t * TOP_K
    n_blocks = (a + N_EXPERTS * (tb - 1) + tb - 1) // tb
    n_slots = n_blocks * tb
    expert = routed[:, :TOP_K].astype(jnp.int32)
    rank = routed[:, 2 * TOP_K : 3 * TOP_K].astype(jnp.int32)
    padded = (counts + tb - 1) // tb * tb
    pend = jnp.cumsum(padded)
    pstart = pend - padded
    ids = jnp.arange(N_EXPERTS, dtype=jnp.int32)
    slot_of_assignment = jnp.sum(jnp.where(expert[..., None] == ids, pstart, 0), axis=-1) + rank
    block_start = jnp.arange(n_blocks, dtype=jnp.int32) * tb
    block_expert = jnp.minimum(jnp.sum(block_start[:, None] >= pend, axis=-1), N_EXPERTS - 1).astype(jnp.int32)
    fill_end = jnp.cumsum(padded - counts)
    filler = jnp.arange(n_slots - a, dtype=jnp.int32)
    fill_key = jnp.sum(filler[:, None] >= fill_end, axis=-1).astype(jnp.int32)
    keys = jnp.concatenate([expert.reshape(a), fill_key])
    vals = jnp.concatenate([jnp.arange(a, dtype=jnp.int32) // TOP_K, filler % t])
    _, token_of_slot = lax.sort((keys, vals), num_keys=1, is_stable=True)
    n_used = (pend[-1:] // tb).astype(jnp.int32)
    return token_of_slot, block_expert, n_used, slot_of_assignment


def _layer(x, xb, cs, sn, w, moe_w, seg_ones, *, batch, seq, alpha):
    u_fox, fox_tail = _inproj(xb, w["w_fox"], LANE)
    (u_mla,) = _inproj(xb, w["w_mla"], 0)
    (u_rwkv,) = _inproj(xb, w["w_rwkv"], 0)
    u_gdn, gdn_tail = _inproj(xb, w["w_gdn"], 2 * LANE)

    vt_fox = u_fox[:, 2 * GROUP_W :].reshape(batch, seq, GROUP_W).transpose(0, 2, 1)
    y_fox = _attention(u_fox, 0, u_fox, 1, vt_fox, (fox_tail, w["fox_b_f"]), w["fox_out_g"], batch=batch, seq=seq,
                       heads=FOX_HEADS, dk=FOX_HD, dv=FOX_HD, chunk=1)

    q_mla, k_mla, vt_mla = _mla_prep(u_mla, cs, sn, w["mla_qg"], w["mla_kvg"], w["mla_wq"], w["mla_wkv"],
                                     batch=batch, seq=seq)
    y_mla = _attention(q_mla, 0, k_mla, 0, vt_mla, None, w["mla_out_g"], batch=batch, seq=seq,
                       heads=MLA_HEADS, dk=2 * LANE, dv=MLA_VD, chunk=CHUNK)

    y_rwkv, y_gdn = _scan_mixers(
        (u_rwkv, w["rwkv_mu"], w["rwkv_w0"], w["rwkv_w2"], w["rwkv_a0"], w["rwkv_a2"], w["rwkv_g2"],
         w["rwkv_k_k"], w["rwkv_k_a"], w["rwkv_r_k"], w["rwkv_ln_g"], w["rwkv_ln_b"], seg_ones),
        (u_gdn, gdn_tail, w["gdn_conv_w"], w["gdn_a_log"], w["gdn_dt_bias"], w["gdn_norm_g"]),
        batch=batch, seq=seq)

    x1, x1b, routed, counts = _outproj((y_fox, y_mla, y_rwkv, y_gdn), w["w_out"], x, w["ln1_g"], w["ln1_b"],
                                       w["w_router"], w["r_bias"], alpha)

    token_of_slot, block_expert, n_used, slot_of_assignment = _route(
        routed, counts[0, :N_EXPERTS].astype(jnp.int32), MOE_TILE)
    block_weight = block_expert + w["layer"] * N_EXPERTS
    n_slots = token_of_slot.shape[0]
    part = pl.cdiv(n_slots // MOE_TILE, MOE_PARTS) * MOE_TILE
    y_slots = None
    for lo in range(0, n_slots, part):
        y_slots = _experts(block_weight, n_used, x1b[token_of_slot[lo : lo + part]], *moe_w,
                           first_block=lo // MOE_TILE, n_slots=n_slots, y_prev=y_slots)
    return _ln2(x1, y_slots[slot_of_assignment[:, 0]], y_slots[slot_of_assignment[:, 1]], routed,
                w["ln2_g"], w["ln2_b"], alpha)


def kernel(x, positions, w_in, fox_b_f, fox_out_g, mla_q_norm_g, mla_kv_norm_g, mla_w_uq, mla_w_ukv, mla_out_g, rwkv_mu, rwkv_w0, rwkv_w2, rwkv_a0, rwkv_a2, rwkv_g2, rwkv_k_k, rwkv_k_a, rwkv_r_k, rwkv_ln_g, rwkv_ln_b, gdn_conv_w, gdn_a_log, gdn_dt_bias, gdn_norm_g, w_out, ln1_g, ln1_b, moe_w_grp, moe_b_grp, moe_w_exp, moe_b_exp, moe_w_gate, moe_w_up, moe_w_down, ln2_g, ln2_b):
    batch, seq, d = x.shape
    depth = w_in.shape[0]
    alpha = (2 * depth) ** 0.25
    params = dict(
        w_in=w_in, fox_b_f=fox_b_f, fox_out_g=fox_out_g, mla_q_norm_g=mla_q_norm_g, mla_kv_norm_g=mla_kv_norm_g,
        mla_w_uq=mla_w_uq, mla_w_ukv=mla_w_ukv, mla_out_g=mla_out_g, rwkv_mu=rwkv_mu, rwkv_w0=rwkv_w0, rwkv_w2=rwkv_w2,
        rwkv_a0=rwkv_a0, rwkv_a2=rwkv_a2, rwkv_g2=rwkv_g2, rwkv_k_k=rwkv_k_k, rwkv_k_a=rwkv_k_a, rwkv_r_k=rwkv_r_k,
        rwkv_ln_g=rwkv_ln_g, rwkv_ln_b=rwkv_ln_b, gdn_conv_w=gdn_conv_w, gdn_a_log=gdn_a_log, gdn_dt_bias=gdn_dt_bias,
        gdn_norm_g=gdn_norm_g, w_out=w_out, ln1_g=ln1_g, ln1_b=ln1_b, moe_w_grp=moe_w_grp, moe_b_grp=moe_b_grp,
        moe_w_exp=moe_w_exp, moe_b_exp=moe_b_exp, moe_w_gate=moe_w_gate, moe_w_up=moe_w_up, moe_w_down=moe_w_down,
        ln2_g=ln2_g, ln2_b=ln2_b,
    )
    weights = _prep_weights(params)

    half = MLA_ROPE // 2
    inv_freq = ROPE_THETA ** (-jnp.arange(half, dtype=F32) / half)
    ang = positions.astype(F32).reshape(batch * seq, 1) * inv_freq
    zpad = jnp.zeros((batch * seq, LANE - MLA_ROPE), F32)
    cs = jnp.concatenate([jnp.cos(ang), jnp.cos(ang), zpad], axis=-1)
    sn = jnp.concatenate([jnp.sin(ang), jnp.sin(ang), zpad], axis=-1)

    xf = x.reshape(batch * seq, d).astype(F32)

    stack = lambda a: a.reshape((-1,) + a.shape[2:])
    moe_w = (stack(moe_w_gate), stack(moe_w_up), stack(moe_w_down))
    seg = jnp.arange(GROUP_W) // RWKV_HD
    seg_ones = (seg[:, None] == seg[None, :]).astype(BF16)

    def body(carry, w):
        xc, xcb = carry
        return _layer(xc, xcb, cs, sn, w, moe_w, seg_ones, batch=batch, seq=seq, alpha=alpha), None

    (xf, _), _ = lax.scan(body, (xf, xf.astype(BF16)), weights)
    return xf.reshape(batch, seq, d).astype(x.dtype)
```

```python
import functools
import math

import jax
import jax.numpy as jnp
from jax import lax
from jax.experimental import pallas as pl
from jax.experimental.pallas import tpu as pltpu

F32 = jnp.float32
BF16 = jnp.bfloat16

D_MODEL = 2048
GROUP_W = 512
FOX_HD, FOX_HEADS = 64, 8
MLA_HEADS, MLA_NOPE, MLA_ROPE, MLA_VD = 4, 128, 64, 128
MLA_Q_RANK, MLA_KV_RANK = 384, 128
ROPE_THETA = 10000.0
RWKV_HD, RWKV_HEADS = 64, 8
RWKV_GN_EPS = 64e-5
GDN_HD, GDN_HEADS, GDN_CONV = 128, 4, 4
N_GROUPS, EXP_PER_GROUP, TOP_K, D_EXPERT = 4, 8, 2, 512
N_EXPERTS = N_GROUPS * EXP_PER_GROUP
CHUNK = 64
LN_EPS = 1e-5
RMS_EPS = 1e-6
LOG2E = math.log2(math.e)

LANE = 128
VMEM_LIMIT_BYTES = 56 * 1024 * 1024
ROW_TILE = 512
ATTN_TILE = 512
ATTN_KV_TILE = 256
SCAN_TILE = 256
MOE_TILE = 512
MOE_PARTS = 4
LN2_PARTS = 2
ONES_ROWS = 16


def _cparams(*sem):
    return pltpu.CompilerParams(dimension_semantics=sem, vmem_limit_bytes=VMEM_LIMIT_BYTES)


def _bdot(a, b):
    return jnp.dot(a.astype(BF16), b.astype(BF16), preferred_element_type=F32)


def _bdot_nt(a, b):
    return lax.dot_general(a.astype(BF16), b.astype(BF16), (((1,), (1,)), ((), ())), preferred_element_type=F32)


def _bdot_tn(a, b):
    return lax.dot_general(a.astype(BF16), b.astype(BF16), (((0,), (0,)), ((), ())), preferred_element_type=F32)


def _split3(x):
    hi = x.astype(BF16)
    r1 = x - hi.astype(F32)
    mid = r1.astype(BF16)
    lo = (r1 - mid.astype(F32)).astype(BF16)
    return hi, mid, lo


def _dot_exact_lhs(m, x):
    hi, mid, lo = _split3(x)
    d = lambda p: jnp.dot(m, p, preferred_element_type=F32)
    return d(hi) + d(mid) + d(lo)


def _sigmoid(x):
    return 1.0 / (1.0 + jnp.exp(-x))


def _softplus(x):
    return jnp.maximum(x, 0.0) + jnp.log(1.0 + jnp.exp(-jnp.abs(x)))


def _chunk_masks(n, c):
    r = lax.broadcasted_iota(jnp.int32, (n, n), 0)
    col = lax.broadcasted_iota(jnp.int32, (n, n), 1)
    same = (r // c) == (col // c)
    lower = jnp.logical_and(same, col <= r)
    strict = jnp.logical_and(same, col < r)
    return r, col, same, lower, strict


def _tri_inv_masks(r, col, c):
    blk = lambda s: (r // s) == (col // s)
    one = lambda cond: jnp.where(cond, 1.0, 0.0).astype(F32)
    offs = []
    s = 8
    while s < c:
        offs.append(one(jnp.logical_and(blk(2 * s), jnp.logical_not(blk(s)))))
        s *= 2
    return one(r == col), one(blk(8)), offs


def _tri_inv(ms, masks):
    eye, blk8, offs = masks
    bf = lambda a: a.astype(BF16)
    dot = lambda a, b: jnp.dot(a, b, preferred_element_type=F32)
    mdf = [m * blk8 for m in ms]
    mds = [bf(m) for m in mdf]
    xs = [eye + m for m in mdf]
    m2s = [bf(dot(md, md)) for md in mds]
    yield
    xs = [x + dot(m2, bf(x)) for x, m2 in zip(xs, m2s)]
    yield
    m4s = [bf(dot(m2, m2)) for m2 in m2s]
    yield
    xs = [x + dot(m4, bf(x)) for x, m4 in zip(xs, m4s)]
    yield
    n = ms[0].shape[0]
    s = 8
    for off in offs:
        xbs = [bf(x) for x in xs]
        if s % 16:
            ts = [bf(dot(xb, bf(m * off))) for xb, m in zip(xbs, ms)]
            yield
            xs = [x + dot(t, xb) for x, t, xb in zip(xs, ts, xbs)]
        else:
            low = lambda a: jnp.concatenate([a[j + s : j + 2 * s] for j in range(0, n, 2 * s)], axis=0)
            ts = [bf(dot(low(xb), bf(m * off))) for xb, m in zip(xbs, ms)]
            yield
            us = [dot(t, xb) for t, xb in zip(ts, xbs)]
            xs = [
                jnp.concatenate(
                    [p for j in range(0, n, 2 * s) for p in (x[j : j + s], x[j + s : j + 2 * s] + u[j // 2 : j // 2 + s])],
                    axis=0,
                )
                for x, u in zip(xs, us)
            ]
        yield
        s *= 2
    return xs


def _interleave(*stage_generators):
    live = list(stage_generators)
    while live:
        for g in list(live):
            try:
                next(g)
            except StopIteration:
                live.remove(g)


def _inproj_kernel(x_ref, w_ref, o_ref, *tail_refs, tail):
    acc = jnp.dot(x_ref[...], w_ref[...], preferred_element_type=F32)
    n = acc.shape[1]
    if tail:
        o_ref[...] = acc[:, : n - tail].astype(o_ref.dtype)
        tail_refs[0][...] = acc[:, n - tail :]
    else:
        o_ref[...] = acc.astype(o_ref.dtype)


def _inproj(xb, w, tail):
    t, d = xb.shape
    n = w.shape[1]
    tm = min(ROW_TILE, t)
    out_shape = [jax.ShapeDtypeStruct((t, n - tail), BF16)]
    out_specs = [pl.BlockSpec((tm, n - tail), lambda i: (i, 0))]
    if tail:
        out_shape.append(jax.ShapeDtypeStruct((t, tail), F32))
        out_specs.append(pl.BlockSpec((tm, tail), lambda i: (i, 0)))
    return pl.pallas_call(
        functools.partial(_inproj_kernel, tail=tail),
        grid=(t // tm,),
        in_specs=[pl.BlockSpec((tm, d), lambda i: (i, 0)), pl.BlockSpec((d, n), lambda i: (0, 0))],
        out_specs=out_specs,
        out_shape=out_shape,
        compiler_params=_cparams("parallel"),
        name="inproj",
    )(xb, w)


def _attn_kernel(*refs, heads, dk, dv, chunk, tq, tk, seq, use_bias):
    if use_bias:
        q_ref, k_ref, vt_ref, gate_ref, gb_ref, g_ref, o_ref, kb_ref, m_ref, acc_ref = refs
    else:
        q_ref, k_ref, vt_ref, g_ref, o_ref, m_ref, acc_ref = refs
    paired = dk < LANE
    dkp = LANE if paired else dk
    dva = dv + ONES_ROWS
    i = pl.program_id(1)
    lane = lax.broadcasted_iota(jnp.int32, (tq, LANE), 1)

    if use_bias:
        @pl.when(i == 0)
        def _():
            r = lax.broadcasted_iota(jnp.int32, (tk, tk), 0)
            c = lax.broadcasted_iota(jnp.int32, (tk, tk), 1)
            tri = jnp.where(c <= r, 1.0, 0.0).astype(BF16)
            head_lane = lax.broadcasted_iota(jnp.int32, (tk, LANE), 1) < heads
            carry = jnp.zeros((1, LANE), F32)
            for j in range(seq // tk):
                rows = slice(j * tk, (j + 1) * tk)
                log_f = -_softplus(-(gate_ref[rows, :] + gb_ref[...])) * LOG2E
                cum = _dot_exact_lhs(tri, log_f) + carry
                carry = cum[tk - 1 : tk, :]
                hi, mid, lo = (jnp.where(head_lane, p.astype(F32), 0.0) for p in _split3(cum))
                pieces = hi + pltpu.roll(mid, heads, axis=1) + pltpu.roll(lo, 2 * heads, axis=1)
                kb_ref[rows, :] = (-pieces).astype(BF16)

    m_ref[...] = jnp.full(m_ref.shape, -1e30, F32)
    acc_ref[...] = jnp.zeros(acc_ref.shape, F32)

    qs = []
    for h in range(heads):
        if paired:
            slab = q_ref[:, (h // 2) * LANE : (h // 2 + 1) * LANE]
            q = jnp.where(lane // dk == h % 2, slab, jnp.zeros_like(slab))
        else:
            q = q_ref[:, h * dkp : (h + 1) * dkp]
        if use_bias:
            pick = jnp.logical_and(lane % heads == h, lane < 3 * heads)
            q = jnp.concatenate([q, jnp.where(pick, 1.0, 0.0).astype(q.dtype)], axis=1)
        qs.append(q)
    ones_rows = jnp.ones((ONES_ROWS, tk), BF16)

    def step(off, key_shift):
        if key_shift is not None:
            kr = lax.broadcasted_iota(jnp.int32, (tk, tq), 0) + key_shift
            qc = lax.broadcasted_iota(jnp.int32, (tk, tq), 1)
            allowed = (kr // chunk) <= (qc // chunk)
        masked = key_shift is not None
        sts = []
        for h in range(heads):
            slab = h // 2 if paired else h
            k = k_ref[pl.ds(off, tk), slab * dkp : (slab + 1) * dkp]
            if use_bias:
                k = jnp.concatenate([k, kb_ref[pl.ds(off, tk), :]], axis=1)
            sts.append(lax.dot_general(k, qs[h], (((1,), (1,)), ((), ())), preferred_element_type=F32))
        ps, alphas = [], []
        for h in range(heads):
            st = sts[h]
            if masked:
                st = jnp.where(allowed, st, -1e30)
            m_old = m_ref[h, 0:1, :]
            m_new = jnp.maximum(m_old, jnp.max(st, axis=0, keepdims=True))
            m_ref[h, 0:1, :] = m_new
            ps.append(jnp.exp2(st - m_new).astype(BF16))
            alphas.append(jnp.exp2(m_old - m_new))
        for h in range(heads):
            vt_h = jnp.concatenate([vt_ref[h * dv : (h + 1) * dv, pl.ds(off, tk)], ones_rows], axis=0)
            rows = slice(h * dva, (h + 1) * dva)
            acc_ref[rows, :] = alphas[h] * acc_ref[rows, :] + jnp.dot(vt_h, ps[h], preferred_element_type=F32)

    def body(j, carry):
        step(pl.multiple_of(j * tk, tk), None)
        return carry

    lax.fori_loop(0, i * (tq // tk), body, 0)
    for d in range(tq // tk):
        step(pl.multiple_of(i * tq + d * tk, tk), d * tk)

    ot = jnp.concatenate(
        [acc_ref[h * dva : h * dva + dv, :] / acc_ref[h * dva + dv : h * dva + dv + 1, :] for h in range(heads)], axis=0
    )
    ot = ot * lax.rsqrt(jnp.mean(ot * ot, axis=0, keepdims=True) + RMS_EPS)
    o_ref[...] = (ot.T * g_ref[...]).astype(o_ref.dtype)


def _attention(q_arr, q_col, k_arr, k_col, vt, gate, gain, *, batch, seq, heads, dk, dv, chunk):
    bias = gate
    assert dk % LANE == 0 or (2 * dk == LANE and heads % 2 == 0)
    tq = min(ATTN_TILE, seq)
    tk = min(ATTN_KV_TILE, tq)
    nq = seq // tq
    t = batch * seq
    in_specs = [
        pl.BlockSpec((tq, heads * dk), lambda b, i: (b * nq + i, q_col)),
        pl.BlockSpec((seq, heads * dk), lambda b, i: (b, k_col)),
        pl.BlockSpec((None, heads * dv, seq), lambda b, i: (b, 0, 0)),
    ]
    args = [q_arr, k_arr, vt]
    scratch = []
    if bias is not None:
        in_specs += [pl.BlockSpec((seq, LANE), lambda b, i: (b, 0)), pl.BlockSpec((1, LANE), lambda b, i: (0, 0))]
        args += list(gate)
        assert 3 * heads <= LANE
        scratch.append(pltpu.VMEM((seq, LANE), BF16))
    in_specs.append(pl.BlockSpec((1, heads * dv), lambda b, i: (0, 0)))
    args.append(gain)
    scratch += [pltpu.VMEM((heads, 8, tq), F32), pltpu.VMEM((heads * (dv + ONES_ROWS), tq), F32)]
    return pl.pallas_call(
        functools.partial(_attn_kernel, heads=heads, dk=dk, dv=dv, chunk=chunk, tq=tq, tk=tk, seq=seq, use_bias=bias is not None),
        grid=(batch, nq),
        in_specs=in_specs,
        out_specs=pl.BlockSpec((tq, heads * dv), lambda b, i: (b * nq + i, 0)),
        out_shape=jax.ShapeDtypeStruct((t, heads * dv), BF16),
        scratch_shapes=scratch,
        compiler_params=_cparams("parallel", "arbitrary"),
        name="attention",
    )(*args)


def _mla_prep_kernel(u_ref, cs_ref, sn_ref, qg_ref, kvg_ref, wq_ref, wkv_ref, q_ref, k_ref, v_ref):
    u = u_ref[...].astype(F32)
    cs = cs_ref[...]
    sn = sn_ref[...]

    def rms(x, g):
        return x * lax.rsqrt(jnp.mean(x * x, axis=-1, keepdims=True) + RMS_EPS) * g

    qo = _bdot(rms(u[:, :MLA_Q_RANK], qg_ref[...]), wq_ref[...])
    kvo = _bdot(rms(u[:, MLA_Q_RANK : MLA_Q_RANK + MLA_KV_RANK], kvg_ref[...]), wkv_ref[...])
    c0 = MLA_Q_RANK + MLA_KV_RANK
    kpe = (u[:, c0 : c0 + LANE] * cs + u[:, c0 + LANE : c0 + 2 * LANE] * sn).astype(k_ref.dtype)
    nn = MLA_HEADS * MLA_NOPE
    for h in range(MLA_HEADS):
        a = h * 2 * LANE
        q_ref[:, a : a + LANE] = qo[:, h * LANE : (h + 1) * LANE].astype(q_ref.dtype)
        qpe = qo[:, nn + h * LANE : nn + (h + 1) * LANE] * cs + qo[:, 2 * nn + h * LANE : 2 * nn + (h + 1) * LANE] * sn
        q_ref[:, a + LANE : a + 2 * LANE] = qpe.astype(q_ref.dtype)
        k_ref[:, a : a + LANE] = kvo[:, h * LANE : (h + 1) * LANE].astype(k_ref.dtype)
        k_ref[:, a + LANE : a + 2 * LANE] = kpe
    v_ref[...] = kvo[:, nn:].T.astype(v_ref.dtype)


def _mla_prep(u_mla, cs, sn, qg, kvg, wq, wkv, *, batch, seq):
    t = u_mla.shape[0]
    tm = min(ROW_TILE, seq)
    nt = seq // tm
    row = lambda n: pl.BlockSpec((tm, n), lambda i: (i, 0))
    full = lambda a: pl.BlockSpec(a.shape, lambda i: (0,) * a.ndim)
    wide = MLA_HEADS * 2 * LANE
    dvs = MLA_HEADS * MLA_VD
    return pl.pallas_call(
        _mla_prep_kernel,
        grid=(t // tm,),
        in_specs=[row(u_mla.shape[1]), row(LANE), row(LANE), full(qg), full(kvg), full(wq), full(wkv)],
        out_specs=[row(wide), row(wide), pl.BlockSpec((None, dvs, tm), lambda i: (i // nt, 0, i % nt))],
        out_shape=[
            jax.ShapeDtypeStruct((t, wide), BF16),
            jax.ShapeDtypeStruct((t, wide), BF16),
            jax.ShapeDtypeStruct((batch, dvs, seq), BF16),
        ],
        compiler_params=_cparams("parallel"),
        name="mla_prep",
    )(u_mla, cs, sn, qg, kvg, wq, wkv)


def _gdn_stages(u_ref, t_ref, cw_ref, alog_ref, dtb_ref, ng_ref, o_ref, xs_ref, st_ref, *, tt, c):
    i = pl.program_id(1)
    gw, hd = GROUP_W, GDN_HD

    @pl.when(i == 0)
    def _():
        xs_ref[0:8, :] = jnp.zeros((8, 3 * gw), F32)
        st_ref[...] = jnp.zeros(st_ref.shape, F32)

    xs_ref[8 : 8 + tt, :] = u_ref[:, : 3 * gw].astype(F32)
    cw = cw_ref[...]
    conv = cw[0:1, :] * xs_ref[5 : 5 + tt, :]
    for j in range(1, GDN_CONV):
        conv = conv + cw[j : j + 1, :] * xs_ref[5 + j : 5 + j + tt, :]
    xs_ref[0:8, :] = xs_ref[tt : tt + 8, :]
    qkv = conv * _sigmoid(conv)
    yield

    tail = t_ref[...]
    beta = _sigmoid(tail[:, :LANE])
    g = -jnp.exp(alog_ref[...]) * _softplus(tail[:, LANE:] + dtb_ref[...])

    r, col, same, lower, strict = _chunk_masks(tt, c)
    tri = jnp.where(lower, 1.0, 0.0).astype(BF16)
    ones_blk = jnp.where(same, 1.0, 0.0).astype(BF16)
    gc = _dot_exact_lhs(tri, g)
    gtot = _dot_exact_lhs(ones_blk, g)
    gct = gc.T
    yield

    def l2n(x):
        return x * lax.rsqrt(jnp.sum(x * x, axis=-1, keepdims=True) + 1e-6)

    heads = range(GDN_HEADS)
    bf = lambda a: a.astype(BF16)
    lmats, attns, kbs, vbs, qds, kds, egs = [], [], [], [], [], [], []
    for h in heads:
        qf = l2n(qkv[:, h * hd : (h + 1) * hd]) * (hd**-0.5)
        qh = bf(qf)
        kf = l2n(qkv[:, gw + h * hd : gw + (h + 1) * hd])
        kh = bf(kf)
        vh = qkv[:, 2 * gw + h * hd : 2 * gw + (h + 1) * hd]
        gcol = gc[:, h : h + 1]
        dec = jnp.exp(jnp.minimum(gcol - gct[h : h + 1, :], 0.0))
        bcol = beta[:, h : h + 1]
        kb = kf * bcol
        eg = jnp.exp(gcol)
        lmats.append(jnp.where(strict, _bdot_nt(kb, kh) * dec, 0.0))
        attns.append(bf(jnp.where(lower, _bdot_nt(qh, kh) * dec, 0.0)))
        kbs.append(bf(kb * eg))
        vbs.append(bf(vh * bcol))
        qds.append(bf(qf * eg))
        kds.append(bf(kf * jnp.exp(gtot[:, h : h + 1] - gcol)))
        yield
    tinvs = yield from _tri_inv([-m for m in lmats], _tri_inv_masks(r, col, c))
    tinvs = [bf(t) for t in tinvs]
    uws = [jnp.dot(t, jnp.concatenate([vb, kb], axis=1), preferred_element_type=F32) for t, vb, kb in zip(tinvs, vbs, kbs)]
    uvals = [uw[:, :hd] for uw in uws]
    wcums = [bf(uw[:, hd:]) for uw in uws]
    yield
    states = [st_ref[h] for h in heads]
    outs = [[] for _ in heads]
    for cc in range(tt // c):
        rs = slice(cc * c, (cc + 1) * c)
        sbs = [bf(s) for s in states]
        vnews = [uvals[h][rs] - jnp.dot(wcums[h][rs], sbs[h], preferred_element_type=F32) for h in heads]
        yield
        for h in heads:
            outs[h].append(jnp.dot(qds[h][rs], sbs[h], preferred_element_type=F32) + _bdot(attns[h][rs, rs], vnews[h]))
            glast = jnp.exp(gtot[cc * c : cc * c + 1, h : h + 1])
            states[h] = states[h] * glast + _bdot_tn(kds[h][rs], vnews[h])
        yield
    for h in heads:
        st_ref[h] = states[h]
        o = jnp.concatenate(outs[h], axis=0)
        o = o * lax.rsqrt(jnp.mean(o * o, axis=-1, keepdims=True) + RMS_EPS) * ng_ref[...]
        z = u_ref[:, 3 * gw + h * hd : 3 * gw + (h + 1) * hd].astype(F32)
        o_ref[:, h * hd : (h + 1) * hd] = (o * (z * _sigmoid(z))).astype(o_ref.dtype)


def _rwkv_stages(u_ref, mu_ref, w0_ref, w2_ref, a0_ref, a2_ref, g2_ref, kk_ref, ka_ref, rk_ref, lng_ref, lnb_ref,
                 e_ref, o_ref, xs_ref, st_ref, *, tt, c):
    i = pl.program_id(1)
    gw, hd = GROUP_W, RWKV_HD

    @pl.when(i == 0)
    def _():
        xs_ref[0:8, :] = jnp.zeros((8, xs_ref.shape[1]), F32)
        st_ref[...] = jnp.zeros(st_ref.shape, F32)

    u = u_ref[...].astype(F32)
    xs_ref[8 : 8 + tt, :] = u
    prev = xs_ref[7 : 7 + tt, :]
    xs_ref[0:8, :] = xs_ref[tt : tt + 8, :]
    x = u + mu_ref[...] * (prev - u)
    rr = x[:, :gw]
    k = x[:, gw : 2 * gw]
    v = x[:, 2 * gw : 3 * gw]
    wa = x[:, 3 * gw : 3 * gw + LANE]
    glo = x[:, 3 * gw + LANE :]
    w = -_softplus(-(w0_ref[...] + _bdot(jnp.tanh(wa), w2_ref[...]))) - 0.5
    ld = -jnp.exp(w)
    gate_a = _sigmoid(a0_ref[...] + _bdot(wa, a2_ref[...]))
    gate_g = _bdot(_sigmoid(glo), g2_ref[...])
    e = e_ref[...]

    def segsum(y):
        hi = y.astype(BF16)
        lo = (y - hi.astype(F32)).astype(BF16)
        return jnp.dot(hi, e, preferred_element_type=F32) + jnp.dot(lo, e, preferred_element_type=F32)

    kkr = k * kk_ref[...]
    kk = kkr * lax.rsqrt(segsum(kkr * kkr) + 1e-6)
    k2 = k * (1.0 + (gate_a - 1.0) * ka_ref[...])
    bonus = segsum(rr * k2 * rk_ref[...]) * v
    yield

    r, col, same, lower, strict = _chunk_masks(tt, c)
    tri = jnp.where(lower, 1.0, 0.0).astype(BF16)
    ones_blk = jnp.where(same, 1.0, 0.0).astype(BF16)
    cum = _dot_exact_lhs(tri, ld)
    ctot = _dot_exact_lhs(ones_blk, ld)
    encum = jnp.exp(-cum)
    edec = jnp.exp(ctot - cum)
    kka = kk * gate_a
    rt = rr * jnp.exp(cum)
    at = -kk * jnp.exp(cum - ld)
    bt = kka * encum
    kt = k2 * encum
    bd = kka * edec
    kd = k2 * edec
    pc = jnp.exp(ctot)
    yield

    heads = range(RWKV_HEADS)
    bf = lambda a: a.astype(BF16)
    dot = lambda a, b: jnp.dot(a, b, preferred_element_type=F32)
    dot_nt = lambda a, b: lax.dot_general(a, b, (((1,), (1,)), ((), ())), preferred_element_type=F32)
    dot_tn = lambda a, b: lax.dot_general(a, b, (((0,), (0,)), ((), ())), preferred_element_type=F32)
    at_b, rt_b, bt_b, kt_b, v_b, bd_b, kd_b = bf(at), bf(rt), bf(bt), bf(kt), bf(v), bf(bd), bf(kd)
    sls = [slice(h * hd, (h + 1) * hd) for h in heads]
    strict_f = jnp.where(strict, 1.0, 0.0).astype(F32)
    lower_f = jnp.where(lower, 1.0, 0.0).astype(F32)
    mabs = [dot_nt(at_b[:, sl], bt_b[:, sl]) * strict_f for sl in sls]
    yield
    maks = [bf(dot_nt(at_b[:, sl], kt_b[:, sl]) * strict_f) for sl in sls]
    yield
    arbs = [bf(dot_nt(rt_b[:, sl], bt_b[:, sl]) * lower_f) for sl in sls]
    yield
    arks = [bf(dot_nt(rt_b[:, sl], kt_b[:, sl]) * lower_f) for sl in sls]
    yield
    tinvs = yield from _tri_inv(mabs, _tri_inv_masks(r, col, c))
    tinvs = [bf(t) for t in tinvs]
    wmats = [bf(dot(t, at_b[:, sl])) for t, sl in zip(tinvs, sls)]
    mkvs = [bf(dot(m, v_b[:, sl])) for m, sl in zip(maks, sls)]
    yield
    umats = [dot(t, mkv) for t, mkv in zip(tinvs, mkvs)]
    yconsts = [dot(m, v_b[:, sl]) for m, sl in zip(arks, sls)]
    yield
    states = [st_ref[h] for h in heads]
    outs = [[] for _ in heads]
    for cc in range(tt // c):
        rs = slice(cc * c, (cc + 1) * c)
        sbs = [bf(s) for s in states]
        sas = [dot_nt(wmats[h][rs], sbs[h]) + umats[h][rs] for h in heads]
        yield
        for h in heads:
            sab = bf(sas[h])
            outs[h].append(dot_nt(rt_b[rs, sls[h]], sbs[h]) + dot(arbs[h][rs, rs], sab) + yconsts[h][rs])
            states[h] = (states[h] * pc[cc * c : cc * c + 1, sls[h]] + dot_tn(sab, bd_b[rs, sls[h]])
                         + dot_tn(v_b[rs, sls[h]], kd_b[rs, sls[h]]))
        yield
    for h in heads:
        st_ref[h] = states[h]
    y = jnp.concatenate([jnp.concatenate(o, axis=0) for o in outs], axis=-1)
    mean = segsum(y) * (1.0 / hd)
    d = y - mean
    var = segsum(d * d) * (1.0 / hd)
    yn = d * lax.rsqrt(var + RWKV_GN_EPS) * lng_ref[...] + lnb_ref[...]
    o_ref[...] = ((yn + bonus) * gate_g).astype(o_ref.dtype)


def _scan_mixers_kernel(*refs, n_rwkv_in, n_gdn_in, tt, c):
    rwkv_in = refs[:n_rwkv_in]
    gdn_in = refs[n_rwkv_in : n_rwkv_in + n_gdn_in]
    o_rwkv, o_gdn, xs_rwkv, st_rwkv, xs_gdn, st_gdn = refs[n_rwkv_in + n_gdn_in :]
    _interleave(
        _rwkv_stages(*rwkv_in, o_rwkv, xs_rwkv, st_rwkv, tt=tt, c=c),
        _gdn_stages(*gdn_in, o_gdn, xs_gdn, st_gdn, tt=tt, c=c),
    )


def _scan_mixers(rwkv_args, gdn_args, *, batch, seq):
    tt = min(SCAN_TILE, seq)
    nt = seq // tt
    t = batch * seq
    tile = lambda a: pl.BlockSpec((tt, a.shape[1]), lambda b, i: (b * nt + i, 0))
    full = lambda a: pl.BlockSpec(a.shape, lambda b, i: (0,) * a.ndim)
    u_rwkv, u_gdn, gdn_tail = rwkv_args[0], gdn_args[0], gdn_args[1]
    in_specs = [tile(u_rwkv)] + [full(p) for p in rwkv_args[1:]] + [tile(u_gdn), tile(gdn_tail)] + [full(p) for p in gdn_args[2:]]
    out = pl.BlockSpec((tt, GROUP_W), lambda b, i: (b * nt + i, 0))
    return pl.pallas_call(
        functools.partial(_scan_mixers_kernel, n_rwkv_in=len(rwkv_args), n_gdn_in=len(gdn_args), tt=tt, c=CHUNK),
        grid=(batch, nt),
        in_specs=in_specs,
        out_specs=[out, out],
        out_shape=[jax.ShapeDtypeStruct((t, GROUP_W), BF16)] * 2,
        scratch_shapes=[
            pltpu.VMEM((tt + 8, u_rwkv.shape[1]), F32),
            pltpu.VMEM((RWKV_HEADS, RWKV_HD, RWKV_HD), F32),
            pltpu.VMEM((tt + 8, 3 * GROUP_W), F32),
            pltpu.VMEM((GDN_HEADS, GDN_HD, GDN_HD), F32),
        ],
        compiler_params=_cparams("parallel", "arbitrary"),
        name="rwkv7_gdn",
    )(*rwkv_args, *gdn_args)


def _layer_norm(h, g, b):
    mu = jnp.mean(h, axis=-1, keepdims=True)
    d = h - mu
    var = jnp.mean(d * d, axis=-1, keepdims=True)
    return d * lax.rsqrt(var + LN_EPS) * g + b


def _route_rows(lg):
    neg = -1e30
    lane = lax.broadcasted_iota(jnp.int32, lg.shape, 1)
    lane_f = lane.astype(F32)
    first = lambda hit: jnp.min(jnp.where(hit, lane_f, float(LANE)), axis=-1, keepdims=True)
    is_g = lane < N_GROUPS
    gl = jnp.where(is_g, lg, neg)
    gmax = jnp.max(gl, axis=-1, keepdims=True)
    p_grp = 1.0 / jnp.sum(jnp.where(is_g, jnp.exp(gl - gmax), 0.0), axis=-1, keepdims=True)
    grp = first(gl == gmax)
    lo = N_GROUPS + grp * EXP_PER_GROUP
    el = jnp.where(jnp.logical_and(lane_f >= lo, lane_f < lo + EXP_PER_GROUP), lg, neg)
    v1 = jnp.max(el, axis=-1, keepdims=True)
    i1 = first(el == v1)
    el2 = jnp.where(lane_f == i1, neg, el)
    v2 = jnp.max(el2, axis=-1, keepdims=True)
    i2 = first(el2 == v2)
    r = jnp.exp(v2 - v1)
    g1 = p_grp / (1.0 + r)
    out = jnp.where(lane == 0, i1 - N_GROUPS, 0.0)
    out = jnp.where(lane == 1, i2 - N_GROUPS, out)
    out = jnp.where(lane == 2, g1, out)
    return jnp.where(lane == 3, g1 * r, out)


def _outproj_kernel(yf_ref, ym_ref, yr_ref, yg_ref, w_ref, x_ref, g_ref, b_ref, wr_ref, rb_ref, tril_ref,
                    xo_ref, xb_ref, lg_ref, cnt_ref, *, alpha):
    mixed = jnp.concatenate([yf_ref[...], ym_ref[...], yr_ref[...], yg_ref[...]], axis=1)
    acc = jnp.dot(mixed, w_ref[...], preferred_element_type=F32)
    xn = _layer_norm(alpha * x_ref[...] + acc, g_ref[...], b_ref[...])
    xo_ref[...] = xn
    xh = xn.astype(BF16)
    xb_ref[...] = xh
    xl = (xn - xh.astype(F32)).astype(BF16)
    d = lambda a, b: jnp.dot(a, b, preferred_element_type=F32)
    both_w = d(xh, wr_ref[...])
    routed = _route_rows(both_w[:, :LANE] + both_w[:, LANE:] + d(xl, wr_ref[:, :LANE]) + rb_ref[...])

    @pl.when(pl.program_id(0) == 0)
    def _():
        cnt_ref[...] = jnp.zeros(cnt_ref.shape, F32)

    lane = lax.broadcasted_iota(jnp.int32, routed.shape, 1).astype(F32)
    oh = [jnp.where(lane == routed[:, k : k + 1], 1.0, 0.0) for k in range(TOP_K)]
    both = oh[0] + oh[1]
    before = jnp.dot(tril_ref[...], both.astype(BF16), preferred_element_type=F32) + cnt_ref[0:1, :]
    for k in range(TOP_K):
        rank = jnp.sum(before * oh[k], axis=-1, keepdims=True)
        routed = jnp.where(lane == 2 * TOP_K + k, rank, routed)
    cnt_ref[...] = cnt_ref[...] + jnp.sum(both, axis=0, keepdims=True)
    lg_ref[...] = routed


def _outproj(ys, w_out, x, ln_g, ln_b, w_router, r_bias, alpha):
    t, d = x.shape
    tm = min(ROW_TILE, t)
    row = lambda n: pl.BlockSpec((tm, n), lambda i: (i, 0))
    full = lambda a: pl.BlockSpec(a.shape, lambda i: (0,) * a.ndim)
    idx = jnp.arange(tm)
    tril = (idx[None, :] < idx[:, None]).astype(BF16)
    return pl.pallas_call(
        functools.partial(_outproj_kernel, alpha=alpha),
        grid=(t // tm,),
        in_specs=[row(GROUP_W)] * 4
        + [full(w_out), row(d), full(ln_g), full(ln_b), full(w_router), full(r_bias), full(tril)],
        out_specs=[row(d), row(d), row(LANE), pl.BlockSpec((8, LANE), lambda i: (0, 0))],
        out_shape=[
            jax.ShapeDtypeStruct((t, d), F32),
            jax.ShapeDtypeStruct((t, d), BF16),
            jax.ShapeDtypeStruct((t, LANE), F32),
            jax.ShapeDtypeStruct((8, LANE), F32),
        ],
        compiler_params=_cparams("arbitrary"),
        name="outproj_ln_router",
    )(*ys, w_out, x, ln_g, ln_b, w_router, r_bias, tril)


def _expert_kernel(be_ref, nu_ref, x_ref, wg_ref, wu_ref, wd_ref, *rest, first_block):
    o_ref, wg_s, wu_s, wd_s = rest[-4:]
    i = pl.program_id(0)
    blk = i + first_block
    used = blk < nu_ref[0]

    @pl.when(jnp.logical_and(used, jnp.logical_or(i == 0, be_ref[blk] != be_ref[jnp.maximum(blk - 1, 0)])))
    def _():
        wg_s[...] = wg_ref[...].astype(BF16)
        wu_s[...] = wu_ref[...].astype(BF16)
        wd_s[...] = wd_ref[...].astype(BF16)

    @pl.when(used)
    def _():
        x = x_ref[...]
        a = jnp.dot(x, wg_s[...], preferred_element_type=F32)
        b = jnp.dot(x, wu_s[...], preferred_element_type=F32)
        hmid = (a * _sigmoid(a) * b).astype(BF16)
        o_ref[...] = jnp.dot(hmid, wd_s[...], preferred_element_type=F32).astype(o_ref.dtype)

    @pl.when(jnp.logical_not(used))
    def _():
        o_ref[...] = jnp.zeros(o_ref.shape, o_ref.dtype)


def _experts(block_weight, n_used, xs_part, w_gate, w_up, w_down, *, first_block, n_slots, y_prev):
    d = xs_part.shape[1]
    tb = MOE_TILE
    de = w_gate.shape[-1]
    weight = lambda shape: pl.BlockSpec((None,) + shape, lambda i, be, nu: (be[i + first_block], 0, 0))
    in_specs = [pl.BlockSpec((tb, d), lambda i, be, nu: (i, 0)), weight((d, de)), weight((d, de)), weight((de, d))]
    args = [block_weight, n_used, xs_part, w_gate, w_up, w_down]
    aliases = {}
    if y_prev is not None:
        in_specs.append(pl.BlockSpec(memory_space=pl.ANY))
        aliases = {len(args): 0}
        args.append(y_prev)
    grid_spec = pltpu.PrefetchScalarGridSpec(
        num_scalar_prefetch=2,
        grid=(xs_part.shape[0] // tb,),
        in_specs=in_specs,
        out_specs=pl.BlockSpec((tb, d), lambda i, be, nu: (i + first_block, 0)),
        scratch_shapes=[pltpu.VMEM((d, de), BF16), pltpu.VMEM((d, de), BF16), pltpu.VMEM((de, d), BF16)],
    )
    return pl.pallas_call(
        functools.partial(_expert_kernel, first_block=first_block),
        grid_spec=grid_spec,
        out_shape=jax.ShapeDtypeStruct((n_slots, d), BF16),
        input_output_aliases=aliases,
        compiler_params=_cparams("arbitrary"),
        name="experts",
    )(*args)


def _ln2_kernel(x_ref, y1_ref, y2_ref, r_ref, g_ref, b_ref, *rest, alpha):
    xo_ref, xb_ref = rest[-2:]
    routed = r_ref[...]
    moe = routed[:, TOP_K : TOP_K + 1] * y1_ref[...].astype(F32) + routed[:, TOP_K + 1 : TOP_K + 2] * y2_ref[...].astype(F32)
    xn = _layer_norm(alpha * x_ref[...] + moe, g_ref[...], b_ref[...])
    xo_ref[...] = xn
    xb_ref[...] = xn.astype(BF16)


def _ln2(x, y1, y2, routed, g, b, alpha, *, row0, prev):
    t, d = x.shape
    tm = min(ROW_TILE, y1.shape[0])
    b0 = row0 // tm
    row = lambda n: pl.BlockSpec((tm, n), lambda i: (i + b0, 0))
    part = pl.BlockSpec((tm, d), lambda i: (i, 0))
    full = lambda a: pl.BlockSpec(a.shape, lambda i: (0,) * a.ndim)
    in_specs = [row(d), part, part, row(LANE), full(g), full(b)]
    args = [x, y1, y2, routed, g, b]
    aliases = {}
    if prev is not None:
        in_specs += [pl.BlockSpec(memory_space=pl.ANY)] * 2
        aliases = {len(args): 0, len(args) + 1: 1}
        args += list(prev)
    return pl.pallas_call(
        functools.partial(_ln2_kernel, alpha=alpha),
        grid=(y1.shape[0] // tm,),
        in_specs=in_specs,
        out_specs=[row(d), row(d)],
        out_shape=[jax.ShapeDtypeStruct((t, d), F32), jax.ShapeDtypeStruct((t, d), BF16)],
        input_output_aliases=aliases,
        compiler_params=_cparams("parallel"),
        name="residual_ln2",
    )(*args)


def _pad_cols(w, n):
    return jnp.pad(w, [(0, 0)] * (w.ndim - 1) + [(0, n - w.shape[-1])])


def _rot_half_cols(w):
    half = w.shape[-1] // 2
    return jnp.concatenate([-w[..., half:], w[..., :half]], axis=-1)


def _prep_weights(p):
    w_in = p["w_in"]
    c_fox = 3 * GROUP_W + FOX_HEADS
    c_mla = MLA_Q_RANK + MLA_KV_RANK + MLA_ROPE
    c_rwkv = 3 * GROUP_W + 2 * 64 + 128
    o_mla = c_fox
    o_rwkv = o_mla + c_mla
    o_gdn = o_rwkv + c_rwkv
    row = lambda a: a[:, None, :].astype(F32)

    w_fox = jnp.concatenate(
        [w_in[..., :GROUP_W] * (FOX_HD**-0.5 * LOG2E), w_in[..., GROUP_W : 3 * GROUP_W], _pad_cols(w_in[..., 3 * GROUP_W : c_fox], LANE)],
        axis=-1,
    )
    kpe_w = w_in[..., o_mla + MLA_Q_RANK + MLA_KV_RANK : o_mla + c_mla]
    w_mla = jnp.concatenate(
        [w_in[..., o_mla : o_mla + MLA_Q_RANK + MLA_KV_RANK], _pad_cols(kpe_w, LANE), _pad_cols(_rot_half_cols(kpe_w), LANE)],
        axis=-1,
    )
    w_rwkv = w_in[..., o_rwkv:o_gdn]
    g0 = o_gdn + 4 * GROUP_W
    w_gdn = jnp.concatenate(
        [w_in[..., o_gdn:g0], _pad_cols(w_in[..., g0 : g0 + GDN_HEADS], LANE), _pad_cols(w_in[..., g0 + GDN_HEADS :], LANE)],
        axis=-1,
    )

    nl = w_in.shape[0]
    scale = (MLA_NOPE + MLA_ROPE) ** -0.5 * LOG2E
    wq = p["mla_w_uq"].reshape(nl, MLA_Q_RANK, MLA_HEADS, MLA_NOPE + MLA_ROPE) * scale
    wq_nope = wq[..., :MLA_NOPE].reshape(nl, MLA_Q_RANK, -1)
    wq_pe = wq[..., MLA_NOPE:]
    wq_p = jnp.concatenate(
        [wq_nope, _pad_cols(wq_pe, LANE).reshape(nl, MLA_Q_RANK, -1), _pad_cols(_rot_half_cols(wq_pe), LANE).reshape(nl, MLA_Q_RANK, -1)],
        axis=-1,
    )
    wkv = p["mla_w_ukv"].reshape(nl, MLA_KV_RANK, MLA_HEADS, MLA_NOPE + MLA_VD)
    wkv_p = jnp.concatenate([wkv[..., :MLA_NOPE].reshape(nl, MLA_KV_RANK, -1), wkv[..., MLA_NOPE:].reshape(nl, MLA_KV_RANK, -1)], axis=-1)

    zeros64 = jnp.zeros((nl, 64, GROUP_W), F32)
    w_router = _pad_cols(jnp.concatenate([p["moe_w_grp"], p["moe_w_exp"]], axis=-1), LANE)
    wr_hi = w_router.astype(BF16)
    wr_lo = (w_router - wr_hi.astype(F32)).astype(BF16)
    return dict(
        w_fox=w_fox.astype(BF16), w_mla=w_mla.astype(BF16), w_rwkv=w_rwkv.astype(BF16), w_gdn=w_gdn.astype(BF16),
        layer=jnp.arange(nl, dtype=jnp.int32),
        fox_b_f=row(_pad_cols(p["fox_b_f"], LANE)), fox_out_g=row(p["fox_out_g"]),
        mla_qg=row(p["mla_q_norm_g"]), mla_kvg=row(p["mla_kv_norm_g"]), mla_wq=wq_p.astype(BF16), mla_wkv=wkv_p.astype(BF16),
        mla_out_g=row(p["mla_out_g"]),
        rwkv_mu=row(p["rwkv_mu"]), rwkv_w0=row(p["rwkv_w0"]),
        rwkv_w2=jnp.concatenate([p["rwkv_w2"], zeros64], axis=1).astype(BF16),
        rwkv_a0=row(p["rwkv_a0"]), rwkv_a2=jnp.concatenate([zeros64, p["rwkv_a2"]], axis=1).astype(BF16),
        rwkv_g2=p["rwkv_g2"].astype(BF16), rwkv_k_k=row(p["rwkv_k_k"]), rwkv_k_a=row(p["rwkv_k_a"]),
        rwkv_r_k=row(p["rwkv_r_k"]), rwkv_ln_g=row(p["rwkv_ln_g"]), rwkv_ln_b=row(p["rwkv_ln_b"]),
        gdn_conv_w=p["gdn_conv_w"].astype(F32), gdn_a_log=row(_pad_cols(p["gdn_a_log"], LANE)),
        gdn_dt_bias=row(_pad_cols(p["gdn_dt_bias"], LANE)), gdn_norm_g=row(p["gdn_norm_g"]),
        w_out=p["w_out"].astype(BF16), ln1_g=row(p["ln1_g"]), ln1_b=row(p["ln1_b"]),
        w_router=jnp.concatenate([wr_hi, wr_lo], axis=-1), r_bias=row(_pad_cols(jnp.concatenate([p["moe_b_grp"], p["moe_b_exp"]], axis=-1), LANE)),
        ln2_g=row(p["ln2_g"]), ln2_b=row(p["ln2_b"]),
    )


def _route(routed, counts, tb):
    t = routed.shape[0]
    a = t * TOP_K
    n_blocks = (a + N_EXPERTS * (tb - 1) + tb - 1) // tb
    n_slots = n_blocks * tb
    expert = routed[:, :TOP_K].astype(jnp.int32)
    rank = routed[:, 2 * TOP_K : 3 * TOP_K].astype(jnp.int32)
    padded = (counts + tb - 1) // tb * tb
    pend = jnp.cumsum(padded)
    pstart = pend - padded
    ids = jnp.arange(N_EXPERTS, dtype=jnp.int32)
    slot_of_assignment = jnp.sum(jnp.where(expert[..., None] == ids, pstart, 0), axis=-1) + rank
    block_start = jnp.arange(n_blocks, dtype=jnp.int32) * tb
    block_expert = jnp.minimum(jnp.sum(block_start[:, None] >= pend, axis=-1), N_EXPERTS - 1).astype(jnp.int32)
    fill_end = jnp.cumsum(padded - counts)
    filler = jnp.arange(n_slots - a, dtype=jnp.int32)
    fill_key = jnp.sum(filler[:, None] >= fill_end, axis=-1).astype(jnp.int32)
    keys = jnp.concatenate([expert.reshape(a), fill_key])
    vals = jnp.concatenate([jnp.arange(a, dtype=jnp.int32) // TOP_K, filler % t])
    _, token_of_slot = lax.sort((keys, vals), num_keys=1, is_stable=True)
    n_used = (pend[-1:] // tb).astype(jnp.int32)
    return token_of_slot, block_expert, n_used, slot_of_assignment


def _layer(x, xb, cs, sn, w, moe_w, seg_ones, *, batch, seq, alpha):
    u_fox, fox_tail = _inproj(xb, w["w_fox"], LANE)
    (u_mla,) = _inproj(xb, w["w_mla"], 0)
    (u_rwkv,) = _inproj(xb, w["w_rwkv"], 0)
    u_gdn, gdn_tail = _inproj(xb, w["w_gdn"], 2 * LANE)

    vt_fox = u_fox[:, 2 * GROUP_W :].reshape(batch, seq, GROUP_W).transpose(0, 2, 1)
    y_fox = _attention(u_fox, 0, u_fox, 1, vt_fox, (fox_tail, w["fox_b_f"]), w["fox_out_g"], batch=batch, seq=seq,
                       heads=FOX_HEADS, dk=FOX_HD, dv=FOX_HD, chunk=1)

    q_mla, k_mla, vt_mla = _mla_prep(u_mla, cs, sn, w["mla_qg"], w["mla_kvg"], w["mla_wq"], w["mla_wkv"],
                                     batch=batch, seq=seq)
    y_mla = _attention(q_mla, 0, k_mla, 0, vt_mla, None, w["mla_out_g"], batch=batch, seq=seq,
                       heads=MLA_HEADS, dk=2 * LANE, dv=MLA_VD, chunk=CHUNK)

    y_rwkv, y_gdn = _scan_mixers(
        (u_rwkv, w["rwkv_mu"], w["rwkv_w0"], w["rwkv_w2"], w["rwkv_a0"], w["rwkv_a2"], w["rwkv_g2"],
         w["rwkv_k_k"], w["rwkv_k_a"], w["rwkv_r_k"], w["rwkv_ln_g"], w["rwkv_ln_b"], seg_ones),
        (u_gdn, gdn_tail, w["gdn_conv_w"], w["gdn_a_log"], w["gdn_dt_bias"], w["gdn_norm_g"]),
        batch=batch, seq=seq)

    x1, x1b, routed, counts = _outproj((y_fox, y_mla, y_rwkv, y_gdn), w["w_out"], x, w["ln1_g"], w["ln1_b"],
                                       w["w_router"], w["r_bias"], alpha)

    token_of_slot, block_expert, n_used, slot_of_assignment = _route(
        routed, counts[0, :N_EXPERTS].astype(jnp.int32), MOE_TILE)
    block_weight = block_expert + w["layer"] * N_EXPERTS
    n_slots = token_of_slot.shape[0]
    part = pl.cdiv(n_slots // MOE_TILE, MOE_PARTS) * MOE_TILE
    y_slots = None
    for lo in range(0, n_slots, part):
        y_slots = _experts(block_weight, n_used, x1b[token_of_slot[lo : lo + part]], *moe_w,
                           first_block=lo // MOE_TILE, n_slots=n_slots, y_prev=y_slots)
    out = None
    rows = x1.shape[0] // LN2_PARTS
    for r0 in range(0, x1.shape[0], rows):
        slots = slot_of_assignment[r0 : r0 + rows]
        out = _ln2(x1, y_slots[slots[:, 0]], y_slots[slots[:, 1]], routed, w["ln2_g"], w["ln2_b"], alpha,
                   row0=r0, prev=out)
    return tuple(out)


def kernel(x, positions, w_in, fox_b_f, fox_out_g, mla_q_norm_g, mla_kv_norm_g, mla_w_uq, mla_w_ukv, mla_out_g, rwkv_mu, rwkv_w0, rwkv_w2, rwkv_a0, rwkv_a2, rwkv_g2, rwkv_k_k, rwkv_k_a, rwkv_r_k, rwkv_ln_g, rwkv_ln_b, gdn_conv_w, gdn_a_log, gdn_dt_bias, gdn_norm_g, w_out, ln1_g, ln1_b, moe_w_grp, moe_b_grp, moe_w_exp, moe_b_exp, moe_w_gate, moe_w_up, moe_w_down, ln2_g, ln2_b):
    batch, seq, d = x.shape
    depth = w_in.shape[0]
    alpha = (2 * depth) ** 0.25
    params = dict(
        w_in=w_in, fox_b_f=fox_b_f, fox_out_g=fox_out_g, mla_q_norm_g=mla_q_norm_g, mla_kv_norm_g=mla_kv_norm_g,
        mla_w_uq=mla_w_uq, mla_w_ukv=mla_w_ukv, mla_out_g=mla_out_g, rwkv_mu=rwkv_mu, rwkv_w0=rwkv_w0, rwkv_w2=rwkv_w2,
        rwkv_a0=rwkv_a0, rwkv_a2=rwkv_a2, rwkv_g2=rwkv_g2, rwkv_k_k=rwkv_k_k, rwkv_k_a=rwkv_k_a, rwkv_r_k=rwkv_r_k,
        rwkv_ln_g=rwkv_ln_g, rwkv_ln_b=rwkv_ln_b, gdn_conv_w=gdn_conv_w, gdn_a_log=gdn_a_log, gdn_dt_bias=gdn_dt_bias,
        gdn_norm_g=gdn_norm_g, w_out=w_out, ln1_g=ln1_g, ln1_b=ln1_b, moe_w_grp=moe_w_grp, moe_b_grp=moe_b_grp,
        moe_w_exp=moe_w_exp, moe_b_exp=moe_b_exp, moe_w_gate=moe_w_gate, moe_w_up=moe_w_up, moe_w_down=moe_w_down,
        ln2_g=ln2_g, ln2_b=ln2_b,
    )
    weights = _prep_weights(params)

    half = MLA_ROPE // 2
    inv_freq = ROPE_THETA ** (-jnp.arange(half, dtype=F32) / half)
    ang = positions.astype(F32).reshape(batch * seq, 1) * inv_freq
    zpad = jnp.zeros((batch * seq, LANE - MLA_ROPE), F32)
    cs = jnp.concatenate([jnp.cos(ang), jnp.cos(ang), zpad], axis=-1)
    sn = jnp.concatenate([jnp.sin(ang), jnp.sin(ang), zpad], axis=-1)

    xf = x.reshape(batch * seq, d).astype(F32)

    stack = lambda a: a.reshape((-1,) + a.shape[2:])
    moe_w = (stack(moe_w_gate), stack(moe_w_up), stack(moe_w_down))
    seg = jnp.arange(GROUP_W) // RWKV_HD
    seg_ones = (seg[:, None] == seg[None, :]).astype(BF16)

    def body(carry, w):
        xc, xcb = carry
        return _layer(xc, xcb, cs, sn, w, moe_w, seg_ones, batch=batch, seq=seq, alpha=alpha), None

    (xf, _), _ = lax.scan(body, (xf, xf.astype(BF16)), weights)
    return xf.reshape(batch, seq, d).astype(x.dtype)
```

```python
import functools
import math

import jax
import jax.numpy as jnp
from jax import lax
from jax.experimental import pallas as pl
from jax.experimental.pallas import tpu as pltpu

F32 = jnp.float32
BF16 = jnp.bfloat16

D_MODEL = 2048
GROUP_W = 512
FOX_HD, FOX_HEADS = 64, 8
MLA_HEADS, MLA_NOPE, MLA_ROPE, MLA_VD = 4, 128, 64, 128
MLA_Q_RANK, MLA_KV_RANK = 384, 128
ROPE_THETA = 10000.0
RWKV_HD, RWKV_HEADS = 64, 8
RWKV_GN_EPS = 64e-5
GDN_HD, GDN_HEADS, GDN_CONV = 128, 4, 4
N_GROUPS, EXP_PER_GROUP, TOP_K, D_EXPERT = 4, 8, 2, 512
N_EXPERTS = N_GROUPS * EXP_PER_GROUP
CHUNK = 64
LN_EPS = 1e-5
RMS_EPS = 1e-6
LOG2E = math.log2(math.e)

LANE = 128
VMEM_LIMIT_BYTES = 56 * 1024 * 1024
ROW_TILE = 512
ATTN_TILE = 512
ATTN_KV_TILE = 256
SCAN_TILE = 128
SCAN_BATCH_ROWS = 1
MOE_TILE = 512
MOE_PARTS = 4
LN2_PARTS = 2
ONES_ROWS = 16


def _cparams(*sem):
    return pltpu.CompilerParams(dimension_semantics=sem, vmem_limit_bytes=VMEM_LIMIT_BYTES)


def _bdot(a, b):
    return jnp.dot(a.astype(BF16), b.astype(BF16), preferred_element_type=F32)


def _bdot_nt(a, b):
    return lax.dot_general(a.astype(BF16), b.astype(BF16), (((1,), (1,)), ((), ())), preferred_element_type=F32)


def _bdot_tn(a, b):
    return lax.dot_general(a.astype(BF16), b.astype(BF16), (((0,), (0,)), ((), ())), preferred_element_type=F32)


def _split3(x):
    hi = x.astype(BF16)
    r1 = x - hi.astype(F32)
    mid = r1.astype(BF16)
    lo = (r1 - mid.astype(F32)).astype(BF16)
    return hi, mid, lo


def _dot_exact_lhs(m, x):
    hi, mid, lo = _split3(x)
    d = lambda p: jnp.dot(m, p, preferred_element_type=F32)
    return d(hi) + d(mid) + d(lo)


def _sigmoid(x):
    return 1.0 / (1.0 + jnp.exp(-x))


def _softplus(x):
    return jnp.maximum(x, 0.0) + jnp.log(1.0 + jnp.exp(-jnp.abs(x)))


def _chunk_masks(n, c):
    r = lax.broadcasted_iota(jnp.int32, (n, n), 0)
    col = lax.broadcasted_iota(jnp.int32, (n, n), 1)
    same = (r // c) == (col // c)
    lower = jnp.logical_and(same, col <= r)
    strict = jnp.logical_and(same, col < r)
    return r, col, same, lower, strict


def _tri_inv_masks(r, col, c):
    blk = lambda s: (r // s) == (col // s)
    one = lambda cond: jnp.where(cond, 1.0, 0.0).astype(F32)
    offs = []
    s = 8
    while s < c:
        offs.append(one(jnp.logical_and(blk(2 * s), jnp.logical_not(blk(s)))))
        s *= 2
    return one(r == col), one(blk(8)), offs


def _tri_inv(ms, masks):
    eye, blk8, offs = masks
    bf = lambda a: a.astype(BF16)
    dot = lambda a, b: jnp.dot(a, b, preferred_element_type=F32)
    mdf = [m * blk8 for m in ms]
    mds = [bf(m) for m in mdf]
    xs = [eye + m for m in mdf]
    m2s = [bf(dot(md, md)) for md in mds]
    yield
    xs = [x + dot(m2, bf(x)) for x, m2 in zip(xs, m2s)]
    yield
    m4s = [bf(dot(m2, m2)) for m2 in m2s]
    yield
    xs = [x + dot(m4, bf(x)) for x, m4 in zip(xs, m4s)]
    yield
    n = ms[0].shape[0]
    s = 8
    for off in offs:
        xbs = [bf(x) for x in xs]
        if s % 16:
            ts = [bf(dot(xb, bf(m * off))) for xb, m in zip(xbs, ms)]
            yield
            xs = [x + dot(t, xb) for x, t, xb in zip(xs, ts, xbs)]
        else:
            low = lambda a: jnp.concatenate([a[j + s : j + 2 * s] for j in range(0, n, 2 * s)], axis=0)
            ts = [bf(dot(low(xb), bf(m * off))) for xb, m in zip(xbs, ms)]
            yield
            us = [dot(t, xb) for t, xb in zip(ts, xbs)]
            xs = [
                jnp.concatenate(
                    [p for j in range(0, n, 2 * s) for p in (x[j : j + s], x[j + s : j + 2 * s] + u[j // 2 : j // 2 + s])],
                    axis=0,
                )
                for x, u in zip(xs, us)
            ]
        yield
        s *= 2
    return xs


def _interleave(*stage_generators):
    live = list(stage_generators)
    while live:
        for g in list(live):
            try:
                next(g)
            except StopIteration:
                live.remove(g)


def _inproj_kernel(x_ref, w_ref, o_ref, *tail_refs, tail):
    acc = jnp.dot(x_ref[...], w_ref[...], preferred_element_type=F32)
    n = acc.shape[1]
    if tail:
        o_ref[...] = acc[:, : n - tail].astype(o_ref.dtype)
        tail_refs[0][...] = acc[:, n - tail :]
    else:
        o_ref[...] = acc.astype(o_ref.dtype)


def _inproj(xb, w, tail):
    t, d = xb.shape
    n = w.shape[1]
    tm = min(ROW_TILE, t)
    out_shape = [jax.ShapeDtypeStruct((t, n - tail), BF16)]
    out_specs = [pl.BlockSpec((tm, n - tail), lambda i: (i, 0))]
    if tail:
        out_shape.append(jax.ShapeDtypeStruct((t, tail), F32))
        out_specs.append(pl.BlockSpec((tm, tail), lambda i: (i, 0)))
    return pl.pallas_call(
        functools.partial(_inproj_kernel, tail=tail),
        grid=(t // tm,),
        in_specs=[pl.BlockSpec((tm, d), lambda i: (i, 0)), pl.BlockSpec((d, n), lambda i: (0, 0))],
        out_specs=out_specs,
        out_shape=out_shape,
        compiler_params=_cparams("parallel"),
        name="inproj",
    )(xb, w)


def _attn_kernel(*refs, heads, dk, dv, chunk, tq, tk, seq, use_bias):
    if use_bias:
        q_ref, k_ref, vt_ref, gate_ref, gb_ref, g_ref, o_ref, kb_ref, m_ref, acc_ref = refs
    else:
        q_ref, k_ref, vt_ref, g_ref, o_ref, m_ref, acc_ref = refs
    paired = dk < LANE
    dkp = LANE if paired else dk
    dva = dv + ONES_ROWS
    i = pl.program_id(1)
    lane = lax.broadcasted_iota(jnp.int32, (tq, LANE), 1)

    if use_bias:
        @pl.when(i == 0)
        def _():
            r = lax.broadcasted_iota(jnp.int32, (tk, tk), 0)
            c = lax.broadcasted_iota(jnp.int32, (tk, tk), 1)
            tri = jnp.where(c <= r, 1.0, 0.0).astype(BF16)
            head_lane = lax.broadcasted_iota(jnp.int32, (tk, LANE), 1) < heads
            carry = jnp.zeros((1, LANE), F32)
            for j in range(seq // tk):
                rows = slice(j * tk, (j + 1) * tk)
                log_f = -_softplus(-(gate_ref[rows, :] + gb_ref[...])) * LOG2E
                cum = _dot_exact_lhs(tri, log_f) + carry
                carry = cum[tk - 1 : tk, :]
                hi, mid, lo = (jnp.where(head_lane, p.astype(F32), 0.0) for p in _split3(cum))
                pieces = hi + pltpu.roll(mid, heads, axis=1) + pltpu.roll(lo, 2 * heads, axis=1)
                kb_ref[rows, :] = (-pieces).astype(BF16)

    m_ref[...] = jnp.full(m_ref.shape, -1e30, F32)
    acc_ref[...] = jnp.zeros(acc_ref.shape, F32)

    qs = []
    for h in range(heads):
        if paired:
            slab = q_ref[:, (h // 2) * LANE : (h // 2 + 1) * LANE]
            q = jnp.where(lane // dk == h % 2, slab, jnp.zeros_like(slab))
        else:
            q = q_ref[:, h * dkp : (h + 1) * dkp]
        if use_bias:
            pick = jnp.logical_and(lane % heads == h, lane < 3 * heads)
            q = jnp.concatenate([q, jnp.where(pick, 1.0, 0.0).astype(q.dtype)], axis=1)
        qs.append(q)
    ones_rows = jnp.ones((ONES_ROWS, tk), BF16)

    def step(off, key_shift):
        if key_shift is not None:
            kr = lax.broadcasted_iota(jnp.int32, (tk, tq), 0) + key_shift
            qc = lax.broadcasted_iota(jnp.int32, (tk, tq), 1)
            allowed = (kr // chunk) <= (qc // chunk)
        masked = key_shift is not None
        sts = []
        for h in range(heads):
            slab = h // 2 if paired else h
            k = k_ref[pl.ds(off, tk), slab * dkp : (slab + 1) * dkp]
            if use_bias:
                k = jnp.concatenate([k, kb_ref[pl.ds(off, tk), :]], axis=1)
            sts.append(lax.dot_general(k, qs[h], (((1,), (1,)), ((), ())), preferred_element_type=F32))
        ps, alphas = [], []
        for h in range(heads):
            st = sts[h]
            if masked:
                st = jnp.where(allowed, st, -1e30)
            m_old = m_ref[h, 0:1, :]
            m_new = jnp.maximum(m_old, jnp.max(st, axis=0, keepdims=True))
            m_ref[h, 0:1, :] = m_new
            ps.append(jnp.exp2(st - m_new).astype(BF16))
            alphas.append(jnp.exp2(m_old - m_new))
        for h in range(heads):
            vt_h = jnp.concatenate([vt_ref[h * dv : (h + 1) * dv, pl.ds(off, tk)], ones_rows], axis=0)
            rows = slice(h * dva, (h + 1) * dva)
            acc_ref[rows, :] = alphas[h] * acc_ref[rows, :] + jnp.dot(vt_h, ps[h], preferred_element_type=F32)

    def body(j, carry):
        step(pl.multiple_of(j * tk, tk), None)
        return carry

    lax.fori_loop(0, i * (tq // tk), body, 0)
    for d in range(tq // tk):
        step(pl.multiple_of(i * tq + d * tk, tk), d * tk)

    ot = jnp.concatenate(
        [acc_ref[h * dva : h * dva + dv, :] / acc_ref[h * dva + dv : h * dva + dv + 1, :] for h in range(heads)], axis=0
    )
    ot = ot * lax.rsqrt(jnp.mean(ot * ot, axis=0, keepdims=True) + RMS_EPS)
    o_ref[...] = (ot.T * g_ref[...]).astype(o_ref.dtype)


def _attention(q_arr, q_col, k_arr, k_col, vt, gate, gain, *, batch, seq, heads, dk, dv, chunk):
    bias = gate
    assert dk % LANE == 0 or (2 * dk == LANE and heads % 2 == 0)
    tq = min(ATTN_TILE, seq)
    tk = min(ATTN_KV_TILE, tq)
    nq = seq // tq
    t = batch * seq
    in_specs = [
        pl.BlockSpec((tq, heads * dk), lambda b, i: (b * nq + i, q_col)),
        pl.BlockSpec((seq, heads * dk), lambda b, i: (b, k_col)),
        pl.BlockSpec((None, heads * dv, seq), lambda b, i: (b, 0, 0)),
    ]
    args = [q_arr, k_arr, vt]
    scratch = []
    if bias is not None:
        in_specs += [pl.BlockSpec((seq, LANE), lambda b, i: (b, 0)), pl.BlockSpec((1, LANE), lambda b, i: (0, 0))]
        args += list(gate)
        assert 3 * heads <= LANE
        scratch.append(pltpu.VMEM((seq, LANE), BF16))
    in_specs.append(pl.BlockSpec((1, heads * dv), lambda b, i: (0, 0)))
    args.append(gain)
    scratch += [pltpu.VMEM((heads, 8, tq), F32), pltpu.VMEM((heads * (dv + ONES_ROWS), tq), F32)]
    return pl.pallas_call(
        functools.partial(_attn_kernel, heads=heads, dk=dk, dv=dv, chunk=chunk, tq=tq, tk=tk, seq=seq, use_bias=bias is not None),
        grid=(batch, nq),
        in_specs=in_specs,
        out_specs=pl.BlockSpec((tq, heads * dv), lambda b, i: (b * nq + i, 0)),
        out_shape=jax.ShapeDtypeStruct((t, heads * dv), BF16),
        scratch_shapes=scratch,
        compiler_params=_cparams("parallel", "arbitrary"),
        name="attention",
    )(*args)


def _mla_prep_kernel(u_ref, cs_ref, sn_ref, qg_ref, kvg_ref, wq_ref, wkv_ref, q_ref, k_ref, v_ref):
    u = u_ref[...].astype(F32)
    cs = cs_ref[...]
    sn = sn_ref[...]

    def rms(x, g):
        return x * lax.rsqrt(jnp.mean(x * x, axis=-1, keepdims=True) + RMS_EPS) * g

    qo = _bdot(rms(u[:, :MLA_Q_RANK], qg_ref[...]), wq_ref[...])
    kvo = _bdot(rms(u[:, MLA_Q_RANK : MLA_Q_RANK + MLA_KV_RANK], kvg_ref[...]), wkv_ref[...])
    c0 = MLA_Q_RANK + MLA_KV_RANK
    kpe = (u[:, c0 : c0 + LANE] * cs + u[:, c0 + LANE : c0 + 2 * LANE] * sn).astype(k_ref.dtype)
    nn = MLA_HEADS * MLA_NOPE
    for h in range(MLA_HEADS):
        a = h * 2 * LANE
        q_ref[:, a : a + LANE] = qo[:, h * LANE : (h + 1) * LANE].astype(q_ref.dtype)
        qpe = qo[:, nn + h * LANE : nn + (h + 1) * LANE] * cs + qo[:, 2 * nn + h * LANE : 2 * nn + (h + 1) * LANE] * sn
        q_ref[:, a + LANE : a + 2 * LANE] = qpe.astype(q_ref.dtype)
        k_ref[:, a : a + LANE] = kvo[:, h * LANE : (h + 1) * LANE].astype(k_ref.dtype)
        k_ref[:, a + LANE : a + 2 * LANE] = kpe
    v_ref[...] = kvo[:, nn:].T.astype(v_ref.dtype)


def _mla_prep(u_mla, cs, sn, qg, kvg, wq, wkv, *, batch, seq):
    t = u_mla.shape[0]
    tm = min(ROW_TILE, seq)
    nt = seq // tm
    row = lambda n: pl.BlockSpec((tm, n), lambda i: (i, 0))
    full = lambda a: pl.BlockSpec(a.shape, lambda i: (0,) * a.ndim)
    wide = MLA_HEADS * 2 * LANE
    dvs = MLA_HEADS * MLA_VD
    return pl.pallas_call(
        _mla_prep_kernel,
        grid=(t // tm,),
        in_specs=[row(u_mla.shape[1]), row(LANE), row(LANE), full(qg), full(kvg), full(wq), full(wkv)],
        out_specs=[row(wide), row(wide), pl.BlockSpec((None, dvs, tm), lambda i: (i // nt, 0, i % nt))],
        out_shape=[
            jax.ShapeDtypeStruct((t, wide), BF16),
            jax.ShapeDtypeStruct((t, wide), BF16),
            jax.ShapeDtypeStruct((batch, dvs, seq), BF16),
        ],
        compiler_params=_cparams("parallel"),
        name="mla_prep",
    )(u_mla, cs, sn, qg, kvg, wq, wkv)


def _gdn_stages(u_ref, t_ref, cw_ref, alog_ref, dtb_ref, ng_ref, o_ref, xs_ref, st_ref, *, tt, c):
    i = pl.program_id(1)
    gw, hd = GROUP_W, GDN_HD

    @pl.when(i == 0)
    def _():
        xs_ref[0:8, :] = jnp.zeros((8, 3 * gw), F32)
        st_ref[...] = jnp.zeros(st_ref.shape, F32)

    xs_ref[8 : 8 + tt, :] = u_ref[:, : 3 * gw].astype(F32)
    cw = cw_ref[...]
    conv = cw[0:1, :] * xs_ref[5 : 5 + tt, :]
    for j in range(1, GDN_CONV):
        conv = conv + cw[j : j + 1, :] * xs_ref[5 + j : 5 + j + tt, :]
    xs_ref[0:8, :] = xs_ref[tt : tt + 8, :]
    qkv = conv * _sigmoid(conv)
    yield

    tail = t_ref[...]
    beta = _sigmoid(tail[:, :LANE])
    g = -jnp.exp(alog_ref[...]) * _softplus(tail[:, LANE:] + dtb_ref[...])

    r, col, same, lower, strict = _chunk_masks(tt, c)
    tri = jnp.where(lower, 1.0, 0.0).astype(BF16)
    ones_blk = jnp.where(same, 1.0, 0.0).astype(BF16)
    gc = _dot_exact_lhs(tri, g)
    gtot = _dot_exact_lhs(ones_blk, g)
    gct = gc.T
    yield

    def l2n(x):
        return x * lax.rsqrt(jnp.sum(x * x, axis=-1, keepdims=True) + 1e-6)

    heads = range(GDN_HEADS)
    bf = lambda a: a.astype(BF16)
    lmats, attns, kbs, vbs, qds, kds, egs = [], [], [], [], [], [], []
    for h in heads:
        qf = l2n(qkv[:, h * hd : (h + 1) * hd]) * (hd**-0.5)
        qh = bf(qf)
        kf = l2n(qkv[:, gw + h * hd : gw + (h + 1) * hd])
        kh = bf(kf)
        vh = qkv[:, 2 * gw + h * hd : 2 * gw + (h + 1) * hd]
        gcol = gc[:, h : h + 1]
        dec = jnp.exp(jnp.minimum(gcol - gct[h : h + 1, :], 0.0))
        bcol = beta[:, h : h + 1]
        kb = kf * bcol
        eg = jnp.exp(gcol)
        lmats.append(jnp.where(strict, _bdot_nt(kb, kh) * dec, 0.0))
        attns.append(bf(jnp.where(lower, _bdot_nt(qh, kh) * dec, 0.0)))
        kbs.append(bf(kb * eg))
        vbs.append(bf(vh * bcol))
        qds.append(bf(qf * eg))
        kds.append(bf(kf * jnp.exp(gtot[:, h : h + 1] - gcol)))
        yield
    tinvs = yield from _tri_inv([-m for m in lmats], _tri_inv_masks(r, col, c))
    tinvs = [bf(t) for t in tinvs]
    uws = [jnp.dot(t, jnp.concatenate([vb, kb], axis=1), preferred_element_type=F32) for t, vb, kb in zip(tinvs, vbs, kbs)]
    uvals = [uw[:, :hd] for uw in uws]
    wcums = [bf(uw[:, hd:]) for uw in uws]
    yield
    states = [st_ref[h] for h in heads]
    outs = [[] for _ in heads]
    for cc in range(tt // c):
        rs = slice(cc * c, (cc + 1) * c)
        sbs = [bf(s) for s in states]
        vnews = [uvals[h][rs] - jnp.dot(wcums[h][rs], sbs[h], preferred_element_type=F32) for h in heads]
        yield
        for h in heads:
            outs[h].append(jnp.dot(qds[h][rs], sbs[h], preferred_element_type=F32) + _bdot(attns[h][rs, rs], vnews[h]))
            glast = jnp.exp(gtot[cc * c : cc * c + 1, h : h + 1])
            states[h] = states[h] * glast + _bdot_tn(kds[h][rs], vnews[h])
        yield
    for h in heads:
        st_ref[h] = states[h]
        o = jnp.concatenate(outs[h], axis=0)
        o = o * lax.rsqrt(jnp.mean(o * o, axis=-1, keepdims=True) + RMS_EPS) * ng_ref[...]
        z = u_ref[:, 3 * gw + h * hd : 3 * gw + (h + 1) * hd].astype(F32)
        o_ref[:, h * hd : (h + 1) * hd] = (o * (z * _sigmoid(z))).astype(o_ref.dtype)


def _rwkv_stages(u_ref, mu_ref, w0_ref, w2_ref, a0_ref, a2_ref, g2_ref, kk_ref, ka_ref, rk_ref, lng_ref, lnb_ref,
                 e_ref, o_ref, xs_ref, st_ref, *, tt, c):
    i = pl.program_id(1)
    gw, hd = GROUP_W, RWKV_HD

    @pl.when(i == 0)
    def _():
        xs_ref[0:8, :] = jnp.zeros((8, xs_ref.shape[1]), F32)
        st_ref[...] = jnp.zeros(st_ref.shape, F32)

    u = u_ref[...].astype(F32)
    xs_ref[8 : 8 + tt, :] = u
    prev = xs_ref[7 : 7 + tt, :]
    xs_ref[0:8, :] = xs_ref[tt : tt + 8, :]
    x = u + mu_ref[...] * (prev - u)
    rr = x[:, :gw]
    k = x[:, gw : 2 * gw]
    v = x[:, 2 * gw : 3 * gw]
    wa = x[:, 3 * gw : 3 * gw + LANE]
    glo = x[:, 3 * gw + LANE :]
    w = -_softplus(-(w0_ref[...] + _bdot(jnp.tanh(wa), w2_ref[...]))) - 0.5
    ld = -jnp.exp(w)
    gate_a = _sigmoid(a0_ref[...] + _bdot(wa, a2_ref[...]))
    gate_g = _bdot(_sigmoid(glo), g2_ref[...])
    e = e_ref[...]

    def segsum(y):
        hi = y.astype(BF16)
        lo = (y - hi.astype(F32)).astype(BF16)
        return jnp.dot(hi, e, preferred_element_type=F32) + jnp.dot(lo, e, preferred_element_type=F32)

    kkr = k * kk_ref[...]
    kk = kkr * lax.rsqrt(segsum(kkr * kkr) + 1e-6)
    k2 = k * (1.0 + (gate_a - 1.0) * ka_ref[...])
    bonus = segsum(rr * k2 * rk_ref[...]) * v
    yield

    r, col, same, lower, strict = _chunk_masks(tt, c)
    tri = jnp.where(lower, 1.0, 0.0).astype(BF16)
    ones_blk = jnp.where(same, 1.0, 0.0).astype(BF16)
    cum = _dot_exact_lhs(tri, ld)
    ctot = _dot_exact_lhs(ones_blk, ld)
    encum = jnp.exp(-cum)
    edec = jnp.exp(ctot - cum)
    kka = kk * gate_a
    rt = rr * jnp.exp(cum)
    at = -kk * jnp.exp(cum - ld)
    bt = kka * encum
    kt = k2 * encum
    bd = kka * edec
    kd = k2 * edec
    pc = jnp.exp(ctot)
    yield

    heads = range(RWKV_HEADS)
    bf = lambda a: a.astype(BF16)
    dot = lambda a, b: jnp.dot(a, b, preferred_element_type=F32)
    dot_nt = lambda a, b: lax.dot_general(a, b, (((1,), (1,)), ((), ())), preferred_element_type=F32)
    dot_tn = lambda a, b: lax.dot_general(a, b, (((0,), (0,)), ((), ())), preferred_element_type=F32)
    at_b, rt_b, bt_b, kt_b, v_b, bd_b, kd_b = bf(at), bf(rt), bf(bt), bf(kt), bf(v), bf(bd), bf(kd)
    sls = [slice(h * hd, (h + 1) * hd) for h in heads]
    strict_f = jnp.where(strict, 1.0, 0.0).astype(F32)
    lower_f = jnp.where(lower, 1.0, 0.0).astype(F32)
    mabs = [dot_nt(at_b[:, sl], bt_b[:, sl]) * strict_f for sl in sls]
    yield
    maks = [bf(dot_nt(at_b[:, sl], kt_b[:, sl]) * strict_f) for sl in sls]
    yield
    arbs = [bf(dot_nt(rt_b[:, sl], bt_b[:, sl]) * lower_f) for sl in sls]
    yield
    arks = [bf(dot_nt(rt_b[:, sl], kt_b[:, sl]) * lower_f) for sl in sls]
    yield
    tinvs = yield from _tri_inv(mabs, _tri_inv_masks(r, col, c))
    tinvs = [bf(t) for t in tinvs]
    wmats = [bf(dot(t, at_b[:, sl])) for t, sl in zip(tinvs, sls)]
    mkvs = [bf(dot(m, v_b[:, sl])) for m, sl in zip(maks, sls)]
    yield
    umats = [dot(t, mkv) for t, mkv in zip(tinvs, mkvs)]
    yconsts = [dot(m, v_b[:, sl]) for m, sl in zip(arks, sls)]
    yield
    states = [st_ref[h] for h in heads]
    outs = [[] for _ in heads]
    for cc in range(tt // c):
        rs = slice(cc * c, (cc + 1) * c)
        sbs = [bf(s) for s in states]
        sas = [dot_nt(wmats[h][rs], sbs[h]) + umats[h][rs] for h in heads]
        yield
        for h in heads:
            sab = bf(sas[h])
            outs[h].append(dot_nt(rt_b[rs, sls[h]], sbs[h]) + dot(arbs[h][rs, rs], sab) + yconsts[h][rs])
            states[h] = (states[h] * pc[cc * c : cc * c + 1, sls[h]] + dot_tn(sab, bd_b[rs, sls[h]])
                         + dot_tn(v_b[rs, sls[h]], kd_b[rs, sls[h]]))
        yield
    for h in heads:
        st_ref[h] = states[h]
    y = jnp.concatenate([jnp.concatenate(o, axis=0) for o in outs], axis=-1)
    mean = segsum(y) * (1.0 / hd)
    d = y - mean
    var = segsum(d * d) * (1.0 / hd)
    yn = d * lax.rsqrt(var + RWKV_GN_EPS) * lng_ref[...] + lnb_ref[...]
    o_ref[...] = ((yn + bonus) * gate_g).astype(o_ref.dtype)


def _scan_mixers_kernel(*refs, n_rwkv_in, n_gdn_in, nb, tt, c):
    rwkv_in = refs[:n_rwkv_in]
    gdn_in = refs[n_rwkv_in : n_rwkv_in + n_gdn_in]
    o_rwkv, o_gdn, xs_rwkv, st_rwkv, xs_gdn, st_gdn = refs[n_rwkv_in + n_gdn_in :]
    gens = []
    for s in range(nb):
        gens.append(_rwkv_stages(rwkv_in[0].at[s], *rwkv_in[1:], o_rwkv.at[s], xs_rwkv.at[s], st_rwkv.at[s], tt=tt, c=c))
        gens.append(_gdn_stages(gdn_in[0].at[s], gdn_in[1].at[s], *gdn_in[2:], o_gdn.at[s], xs_gdn.at[s], st_gdn.at[s],
                                tt=tt, c=c))
    _interleave(*gens)


def _scan_mixers(rwkv_args, gdn_args, *, batch, seq):
    tt = min(SCAN_TILE, seq)
    nt = seq // tt
    t = batch * seq
    nb = SCAN_BATCH_ROWS if batch % SCAN_BATCH_ROWS == 0 else 1
    by_batch = lambda a: a.reshape(batch // nb, nb, seq, a.shape[1])
    tile = lambda a: pl.BlockSpec((None, nb, tt, a.shape[1]), lambda b, i: (b, 0, i, 0))
    full = lambda a: pl.BlockSpec(a.shape, lambda b, i: (0,) * a.ndim)
    u_rwkv, u_gdn, gdn_tail = rwkv_args[0], gdn_args[0], gdn_args[1]
    in_specs = [tile(u_rwkv)] + [full(p) for p in rwkv_args[1:]] + [tile(u_gdn), tile(gdn_tail)] + [full(p) for p in gdn_args[2:]]
    out = pl.BlockSpec((None, nb, tt, GROUP_W), lambda b, i: (b, 0, i, 0))
    y_rwkv, y_gdn = pl.pallas_call(
        functools.partial(_scan_mixers_kernel, n_rwkv_in=len(rwkv_args), n_gdn_in=len(gdn_args), nb=nb, tt=tt, c=CHUNK),
        grid=(batch // nb, nt),
        in_specs=in_specs,
        out_specs=[out, out],
        out_shape=[jax.ShapeDtypeStruct((batch // nb, nb, seq, GROUP_W), BF16)] * 2,
        scratch_shapes=[
            pltpu.VMEM((nb, tt + 8, u_rwkv.shape[1]), F32),
            pltpu.VMEM((nb, RWKV_HEADS, RWKV_HD, RWKV_HD), F32),
            pltpu.VMEM((nb, tt + 8, 3 * GROUP_W), F32),
            pltpu.VMEM((nb, GDN_HEADS, GDN_HD, GDN_HD), F32),
        ],
        compiler_params=_cparams("parallel", "arbitrary"),
        name="rwkv7_gdn",
    )(by_batch(u_rwkv), *rwkv_args[1:], by_batch(u_gdn), by_batch(gdn_tail), *gdn_args[2:])
    return y_rwkv.reshape(t, GROUP_W), y_gdn.reshape(t, GROUP_W)


def _layer_norm(h, g, b):
    mu = jnp.mean(h, axis=-1, keepdims=True)
    d = h - mu
    var = jnp.mean(d * d, axis=-1, keepdims=True)
    return d * lax.rsqrt(var + LN_EPS) * g + b


def _route_rows(lg):
    neg = -1e30
    lane = lax.broadcasted_iota(jnp.int32, lg.shape, 1)
    lane_f = lane.astype(F32)
    first = lambda hit: jnp.min(jnp.where(hit, lane_f, float(LANE)), axis=-1, keepdims=True)
    is_g = lane < N_GROUPS
    gl = jnp.where(is_g, lg, neg)
    gmax = jnp.max(gl, axis=-1, keepdims=True)
    p_grp = 1.0 / jnp.sum(jnp.where(is_g, jnp.exp(gl - gmax), 0.0), axis=-1, keepdims=True)
    grp = first(gl == gmax)
    lo = N_GROUPS + grp * EXP_PER_GROUP
    el = jnp.where(jnp.logical_and(lane_f >= lo, lane_f < lo + EXP_PER_GROUP), lg, neg)
    v1 = jnp.max(el, axis=-1, keepdims=True)
    i1 = first(el == v1)
    el2 = jnp.where(lane_f == i1, neg, el)
    v2 = jnp.max(el2, axis=-1, keepdims=True)
    i2 = first(el2 == v2)
    r = jnp.exp(v2 - v1)
    g1 = p_grp / (1.0 + r)
    out = jnp.where(lane == 0, i1 - N_GROUPS, 0.0)
    out = jnp.where(lane == 1, i2 - N_GROUPS, out)
    out = jnp.where(lane == 2, g1, out)
    return jnp.where(lane == 3, g1 * r, out)


def _outproj_kernel(yf_ref, ym_ref, yr_ref, yg_ref, w_ref, x_ref, g_ref, b_ref, wr_ref, rb_ref, tril_ref,
                    xo_ref, xb_ref, lg_ref, cnt_ref, *, alpha):
    mixed = jnp.concatenate([yf_ref[...], ym_ref[...], yr_ref[...], yg_ref[...]], axis=1)
    acc = jnp.dot(mixed, w_ref[...], preferred_element_type=F32)
    xn = _layer_norm(alpha * x_ref[...] + acc, g_ref[...], b_ref[...])
    xo_ref[...] = xn
    xh = xn.astype(BF16)
    xb_ref[...] = xh
    xl = (xn - xh.astype(F32)).astype(BF16)
    d = lambda a, b: jnp.dot(a, b, preferred_element_type=F32)
    both_w = d(xh, wr_ref[...])
    routed = _route_rows(both_w[:, :LANE] + both_w[:, LANE:] + d(xl, wr_ref[:, :LANE]) + rb_ref[...])

    @pl.when(pl.program_id(0) == 0)
    def _():
        cnt_ref[...] = jnp.zeros(cnt_ref.shape, F32)

    lane = lax.broadcasted_iota(jnp.int32, routed.shape, 1).astype(F32)
    oh = [jnp.where(lane == routed[:, k : k + 1], 1.0, 0.0) for k in range(TOP_K)]
    both = oh[0] + oh[1]
    before = jnp.dot(tril_ref[...], both.astype(BF16), preferred_element_type=F32) + cnt_ref[0:1, :]
    for k in range(TOP_K):
        rank = jnp.sum(before * oh[k], axis=-1, keepdims=True)
        routed = jnp.where(lane == 2 * TOP_K + k, rank, routed)
    cnt_ref[...] = cnt_ref[...] + jnp.sum(both, axis=0, keepdims=True)
    lg_ref[...] = routed


def _outproj(ys, w_out, x, ln_g, ln_b, w_router, r_bias, alpha):
    t, d = x.shape
    tm = min(ROW_TILE, t)
    row = lambda n: pl.BlockSpec((tm, n), lambda i: (i, 0))
    full = lambda a: pl.BlockSpec(a.shape, lambda i: (0,) * a.ndim)
    idx = jnp.arange(tm)
    tril = (idx[None, :] < idx[:, None]).astype(BF16)
    return pl.pallas_call(
        functools.partial(_outproj_kernel, alpha=alpha),
        grid=(t // tm,),
        in_specs=[row(GROUP_W)] * 4
        + [full(w_out), row(d), full(ln_g), full(ln_b), full(w_router), full(r_bias), full(tril)],
        out_specs=[row(d), row(d), row(LANE), pl.BlockSpec((8, LANE), lambda i: (0, 0))],
        out_shape=[
            jax.ShapeDtypeStruct((t, d), F32),
            jax.ShapeDtypeStruct((t, d), BF16),
            jax.ShapeDtypeStruct((t, LANE), F32),
            jax.ShapeDtypeStruct((8, LANE), F32),
        ],
        compiler_params=_cparams("arbitrary"),
        name="outproj_ln_router",
    )(*ys, w_out, x, ln_g, ln_b, w_router, r_bias, tril)


def _expert_kernel(be_ref, nu_ref, x_ref, wg_ref, wu_ref, wd_ref, *rest, first_block):
    o_ref, wg_s, wu_s, wd_s = rest[-4:]
    i = pl.program_id(0)
    blk = i + first_block
    used = blk < nu_ref[0]

    @pl.when(jnp.logical_and(used, jnp.logical_or(i == 0, be_ref[blk] != be_ref[jnp.maximum(blk - 1, 0)])))
    def _():
        wg_s[...] = wg_ref[...].astype(BF16)
        wu_s[...] = wu_ref[...].astype(BF16)
        wd_s[...] = wd_ref[...].astype(BF16)

    @pl.when(used)
    def _():
        x = x_ref[...]
        a = jnp.dot(x, wg_s[...], preferred_element_type=F32)
        b = jnp.dot(x, wu_s[...], preferred_element_type=F32)
        hmid = (a * _sigmoid(a) * b).astype(BF16)
        o_ref[...] = jnp.dot(hmid, wd_s[...], preferred_element_type=F32).astype(o_ref.dtype)

    @pl.when(jnp.logical_not(used))
    def _():
        o_ref[...] = jnp.zeros(o_ref.shape, o_ref.dtype)


def _experts(block_weight, n_used, xs_part, w_gate, w_up, w_down, *, first_block, n_slots, y_prev):
    d = xs_part.shape[1]
    tb = MOE_TILE
    de = w_gate.shape[-1]
    weight = lambda shape: pl.BlockSpec((None,) + shape, lambda i, be, nu: (be[i + first_block], 0, 0))
    in_specs = [pl.BlockSpec((tb, d), lambda i, be, nu: (i, 0)), weight((d, de)), weight((d, de)), weight((de, d))]
    args = [block_weight, n_used, xs_part, w_gate, w_up, w_down]
    aliases = {}
    if y_prev is not None:
        in_specs.append(pl.BlockSpec(memory_space=pl.ANY))
        aliases = {len(args): 0}
        args.append(y_prev)
    grid_spec = pltpu.PrefetchScalarGridSpec(
        num_scalar_prefetch=2,
        grid=(xs_part.shape[0] // tb,),
        in_specs=in_specs,
        out_specs=pl.BlockSpec((tb, d), lambda i, be, nu: (i + first_block, 0)),
        scratch_shapes=[pltpu.VMEM((d, de), BF16), pltpu.VMEM((d, de), BF16), pltpu.VMEM((de, d), BF16)],
    )
    return pl.pallas_call(
        functools.partial(_expert_kernel, first_block=first_block),
        grid_spec=grid_spec,
        out_shape=jax.ShapeDtypeStruct((n_slots, d), BF16),
        input_output_aliases=aliases,
        compiler_params=_cparams("arbitrary"),
        name="experts",
    )(*args)


def _ln2_kernel(x_ref, y1_ref, y2_ref, r_ref, g_ref, b_ref, *rest, alpha):
    xo_ref, xb_ref = rest[-2:]
    routed = r_ref[...]
    moe = routed[:, TOP_K : TOP_K + 1] * y1_ref[...].astype(F32) + routed[:, TOP_K + 1 : TOP_K + 2] * y2_ref[...].astype(F32)
    xn = _layer_norm(alpha * x_ref[...] + moe, g_ref[...], b_ref[...])
    xo_ref[...] = xn
    xb_ref[...] = xn.astype(BF16)


def _ln2(x, y1, y2, routed, g, b, alpha, *, row0, prev):
    t, d = x.shape
    tm = min(ROW_TILE, y1.shape[0])
    b0 = row0 // tm
    row = lambda n: pl.BlockSpec((tm, n), lambda i: (i + b0, 0))
    part = pl.BlockSpec((tm, d), lambda i: (i, 0))
    full = lambda a: pl.BlockSpec(a.shape, lambda i: (0,) * a.ndim)
    in_specs = [row(d), part, part, row(LANE), full(g), full(b)]
    args = [x, y1, y2, routed, g, b]
    aliases = {}
    if prev is not None:
        in_specs += [pl.BlockSpec(memory_space=pl.ANY)] * 2
        aliases = {len(args): 0, len(args) + 1: 1}
        args += list(prev)
    return pl.pallas_call(
        functools.partial(_ln2_kernel, alpha=alpha),
        grid=(y1.shape[0] // tm,),
        in_specs=in_specs,
        out_specs=[row(d), row(d)],
        out_shape=[jax.ShapeDtypeStruct((t, d), F32), jax.ShapeDtypeStruct((t, d), BF16)],
        input_output_aliases=aliases,
        compiler_params=_cparams("parallel"),
        name="residual_ln2",
    )(*args)


def _pad_cols(w, n):
    return jnp.pad(w, [(0, 0)] * (w.ndim - 1) + [(0, n - w.shape[-1])])


def _rot_half_cols(w):
    half = w.shape[-1] // 2
    return jnp.concatenate([-w[..., half:], w[..., :half]], axis=-1)


def _prep_weights(p):
    w_in = p["w_in"]
    c_fox = 3 * GROUP_W + FOX_HEADS
    c_mla = MLA_Q_RANK + MLA_KV_RANK + MLA_ROPE
    c_rwkv = 3 * GROUP_W + 2 * 64 + 128
    o_mla = c_fox
    o_rwkv = o_mla + c_mla
    o_gdn = o_rwkv + c_rwkv
    row = lambda a: a[:, None, :].astype(F32)

    w_fox = jnp.concatenate(
        [w_in[..., :GROUP_W] * (FOX_HD**-0.5 * LOG2E), w_in[..., GROUP_W : 3 * GROUP_W], _pad_cols(w_in[..., 3 * GROUP_W : c_fox], LANE)],
        axis=-1,
    )
    kpe_w = w_in[..., o_mla + MLA_Q_RANK + MLA_KV_RANK : o_mla + c_mla]
    w_mla = jnp.concatenate(
        [w_in[..., o_mla : o_mla + MLA_Q_RANK + MLA_KV_RANK], _pad_cols(kpe_w, LANE), _pad_cols(_rot_half_cols(kpe_w), LANE)],
        axis=-1,
    )
    w_rwkv = w_in[..., o_rwkv:o_gdn]
    g0 = o_gdn + 4 * GROUP_W
    w_gdn = jnp.concatenate(
        [w_in[..., o_gdn:g0], _pad_cols(w_in[..., g0 : g0 + GDN_HEADS], LANE), _pad_cols(w_in[..., g0 + GDN_HEADS :], LANE)],
        axis=-1,
    )

    nl = w_in.shape[0]
    scale = (MLA_NOPE + MLA_ROPE) ** -0.5 * LOG2E
    wq = p["mla_w_uq"].reshape(nl, MLA_Q_RANK, MLA_HEADS, MLA_NOPE + MLA_ROPE) * scale
    wq_nope = wq[..., :MLA_NOPE].reshape(nl, MLA_Q_RANK, -1)
    wq_pe = wq[..., MLA_NOPE:]
    wq_p = jnp.concatenate(
        [wq_nope, _pad_cols(wq_pe, LANE).reshape(nl, MLA_Q_RANK, -1), _pad_cols(_rot_half_cols(wq_pe), LANE).reshape(nl, MLA_Q_RANK, -1)],
        axis=-1,
    )
    wkv = p["mla_w_ukv"].reshape(nl, MLA_KV_RANK, MLA_HEADS, MLA_NOPE + MLA_VD)
    wkv_p = jnp.concatenate([wkv[..., :MLA_NOPE].reshape(nl, MLA_KV_RANK, -1), wkv[..., MLA_NOPE:].reshape(nl, MLA_KV_RANK, -1)], axis=-1)

    zeros64 = jnp.zeros((nl, 64, GROUP_W), F32)
    w_router = _pad_cols(jnp.concatenate([p["moe_w_grp"], p["moe_w_exp"]], axis=-1), LANE)
    wr_hi = w_router.astype(BF16)
    wr_lo = (w_router - wr_hi.astype(F32)).astype(BF16)
    return dict(
        w_fox=w_fox.astype(BF16), w_mla=w_mla.astype(BF16), w_rwkv=w_rwkv.astype(BF16), w_gdn=w_gdn.astype(BF16),
        layer=jnp.arange(nl, dtype=jnp.int32),
        fox_b_f=row(_pad_cols(p["fox_b_f"], LANE)), fox_out_g=row(p["fox_out_g"]),
        mla_qg=row(p["mla_q_norm_g"]), mla_kvg=row(p["mla_kv_norm_g"]), mla_wq=wq_p.astype(BF16), mla_wkv=wkv_p.astype(BF16),
        mla_out_g=row(p["mla_out_g"]),
        rwkv_mu=row(p["rwkv_mu"]), rwkv_w0=row(p["rwkv_w0"]),
        rwkv_w2=jnp.concatenate([p["rwkv_w2"], zeros64], axis=1).astype(BF16),
        rwkv_a0=row(p["rwkv_a0"]), rwkv_a2=jnp.concatenate([zeros64, p["rwkv_a2"]], axis=1).astype(BF16),
        rwkv_g2=p["rwkv_g2"].astype(BF16), rwkv_k_k=row(p["rwkv_k_k"]), rwkv_k_a=row(p["rwkv_k_a"]),
        rwkv_r_k=row(p["rwkv_r_k"]), rwkv_ln_g=row(p["rwkv_ln_g"]), rwkv_ln_b=row(p["rwkv_ln_b"]),
        gdn_conv_w=p["gdn_conv_w"].astype(F32), gdn_a_log=row(_pad_cols(p["gdn_a_log"], LANE)),
        gdn_dt_bias=row(_pad_cols(p["gdn_dt_bias"], LANE)), gdn_norm_g=row(p["gdn_norm_g"]),
        w_out=p["w_out"].astype(BF16), ln1_g=row(p["ln1_g"]), ln1_b=row(p["ln1_b"]),
        w_router=jnp.concatenate([wr_hi, wr_lo], axis=-1), r_bias=row(_pad_cols(jnp.concatenate([p["moe_b_grp"], p["moe_b_exp"]], axis=-1), LANE)),
        ln2_g=row(p["ln2_g"]), ln2_b=row(p["ln2_b"]),
    )


def _route(routed, counts, tb):
    t = routed.shape[0]
    a = t * TOP_K
    n_blocks = (a + N_EXPERTS * (tb - 1) + tb - 1) // tb
    n_slots = n_blocks * tb
    expert = routed[:, :TOP_K].astype(jnp.int32)
    rank = routed[:, 2 * TOP_K : 3 * TOP_K].astype(jnp.int32)
    padded = (counts + tb - 1) // tb * tb
    pend = jnp.cumsum(padded)
    pstart = pend - padded
    ids = jnp.arange(N_EXPERTS, dtype=jnp.int32)
    slot_of_assignment = jnp.sum(jnp.where(expert[..., None] == ids, pstart, 0), axis=-1) + rank
    block_start = jnp.arange(n_blocks, dtype=jnp.int32) * tb
    block_expert = jnp.minimum(jnp.sum(block_start[:, None] >= pend, axis=-1), N_EXPERTS - 1).astype(jnp.int32)
    fill_end = jnp.cumsum(padded - counts)
    filler = jnp.arange(n_slots - a, dtype=jnp.int32)
    fill_key = jnp.sum(filler[:, None] >= fill_end, axis=-1).astype(jnp.int32)
    keys = jnp.concatenate([expert.reshape(a), fill_key])
    vals = jnp.concatenate([jnp.arange(a, dtype=jnp.int32) // TOP_K, filler % t])
    _, token_of_slot = lax.sort((keys, vals), num_keys=1, is_stable=True)
    n_used = (pend[-1:] // tb).astype(jnp.int32)
    return token_of_slot, block_expert, n_used, slot_of_assignment


def _layer(x, xb, cs, sn, w, moe_w, seg_ones, *, batch, seq, alpha):
    u_fox, fox_tail = _inproj(xb, w["w_fox"], LANE)
    (u_mla,) = _inproj(xb, w["w_mla"], 0)
    (u_rwkv,) = _inproj(xb, w["w_rwkv"], 0)
    u_gdn, gdn_tail = _inproj(xb, w["w_gdn"], 2 * LANE)

    vt_fox = u_fox[:, 2 * GROUP_W :].reshape(batch, seq, GROUP_W).transpose(0, 2, 1)
    y_fox = _attention(u_fox, 0, u_fox, 1, vt_fox, (fox_tail, w["fox_b_f"]), w["fox_out_g"], batch=batch, seq=seq,
                       heads=FOX_HEADS, dk=FOX_HD, dv=FOX_HD, chunk=1)

    q_mla, k_mla, vt_mla = _mla_prep(u_mla, cs, sn, w["mla_qg"], w["mla_kvg"], w["mla_wq"], w["mla_wkv"],
                                     batch=batch, seq=seq)
    y_mla = _attention(q_mla, 0, k_mla, 0, vt_mla, None, w["mla_out_g"], batch=batch, seq=seq,
                       heads=MLA_HEADS, dk=2 * LANE, dv=MLA_VD, chunk=CHUNK)

    y_rwkv, y_gdn = _scan_mixers(
        (u_rwkv, w["rwkv_mu"], w["rwkv_w0"], w["rwkv_w2"], w["rwkv_a0"], w["rwkv_a2"], w["rwkv_g2"],
         w["rwkv_k_k"], w["rwkv_k_a"], w["rwkv_r_k"], w["rwkv_ln_g"], w["rwkv_ln_b"], seg_ones),
        (u_gdn, gdn_tail, w["gdn_conv_w"], w["gdn_a_log"], w["gdn_dt_bias"], w["gdn_norm_g"]),
        batch=batch, seq=seq)

    x1, x1b, routed, counts = _outproj((y_fox, y_mla, y_rwkv, y_gdn), w["w_out"], x, w["ln1_g"], w["ln1_b"],
                                       w["w_router"], w["r_bias"], alpha)

    token_of_slot, block_expert, n_used, slot_of_assignment = _route(
        routed, counts[0, :N_EXPERTS].astype(jnp.int32), MOE_TILE)
    block_weight = block_expert + w["layer"] * N_EXPERTS
    n_slots = token_of_slot.shape[0]
    part = pl.cdiv(n_slots // MOE_TILE, MOE_PARTS) * MOE_TILE
    y_slots = None
    for lo in range(0, n_slots, part):
        y_slots = _experts(block_weight, n_used, x1b[token_of_slot[lo : lo + part]], *moe_w,
                           first_block=lo // MOE_TILE, n_slots=n_slots, y_prev=y_slots)
    out = None
    rows = x1.shape[0] // LN2_PARTS
    for r0 in range(0, x1.shape[0], rows):
        slots = slot_of_assignment[r0 : r0 + rows]
        out = _ln2(x1, y_slots[slots[:, 0]], y_slots[slots[:, 1]], routed, w["ln2_g"], w["ln2_b"], alpha,
                   row0=r0, prev=out)
    return tuple(out)


def kernel(x, positions, w_in, fox_b_f, fox_out_g, mla_q_norm_g, mla_kv_norm_g, mla_w_uq, mla_w_ukv, mla_out_g, rwkv_mu, rwkv_w0, rwkv_w2, rwkv_a0, rwkv_a2, rwkv_g2, rwkv_k_k, rwkv_k_a, rwkv_r_k, rwkv_ln_g, rwkv_ln_b, gdn_conv_w, gdn_a_log, gdn_dt_bias, gdn_norm_g, w_out, ln1_g, ln1_b, moe_w_grp, moe_b_grp, moe_w_exp, moe_b_exp, moe_w_gate, moe_w_up, moe_w_down, ln2_g, ln2_b):
    batch, seq, d = x.shape
    depth = w_in.shape[0]
    alpha = (2 * depth) ** 0.25
    params = dict(
        w_in=w_in, fox_b_f=fox_b_f, fox_out_g=fox_out_g, mla_q_norm_g=mla_q_norm_g, mla_kv_norm_g=mla_kv_norm_g,
        mla_w_uq=mla_w_uq, mla_w_ukv=mla_w_ukv, mla_out_g=mla_out_g, rwkv_mu=rwkv_mu, rwkv_w0=rwkv_w0, rwkv_w2=rwkv_w2,
        rwkv_a0=rwkv_a0, rwkv_a2=rwkv_a2, rwkv_g2=rwkv_g2, rwkv_k_k=rwkv_k_k, rwkv_k_a=rwkv_k_a, rwkv_r_k=rwkv_r_k,
        rwkv_ln_g=rwkv_ln_g, rwkv_ln_b=rwkv_ln_b, gdn_conv_w=gdn_conv_w, gdn_a_log=gdn_a_log, gdn_dt_bias=gdn_dt_bias,
        gdn_norm_g=gdn_norm_g, w_out=w_out, ln1_g=ln1_g, ln1_b=ln1_b, moe_w_grp=moe_w_grp, moe_b_grp=moe_b_grp,
        moe_w_exp=moe_w_exp, moe_b_exp=moe_b_exp, moe_w_gate=moe_w_gate, moe_w_up=moe_w_up, moe_w_down=moe_w_down,
        ln2_g=ln2_g, ln2_b=ln2_b,
    )
    weights = _prep_weights(params)

    half = MLA_ROPE // 2
    inv_freq = ROPE_THETA ** (-jnp.arange(half, dtype=F32) / half)
    ang = positions.astype(F32).reshape(batch * seq, 1) * inv_freq
    zpad = jnp.zeros((batch * seq, LANE - MLA_ROPE), F32)
    cs = jnp.concatenate([jnp.cos(ang), jnp.cos(ang), zpad], axis=-1)
    sn = jnp.concatenate([jnp.sin(ang), jnp.sin(ang), zpad], axis=-1)

    xf = x.reshape(batch * seq, d).astype(F32)

    stack = lambda a: a.reshape((-1,) + a.shape[2:])
    moe_w = (stack(moe_w_gate), stack(moe_w_up), stack(moe_w_down))
    seg = jnp.arange(GROUP_W) // RWKV_HD
    seg_ones = (seg[:, None] == seg[None, :]).astype(BF16)

    def body(carry, w):
        xc, xcb = carry
        return _layer(xc, xcb, cs, sn, w, moe_w, seg_ones, batch=batch, seq=seq, alpha=alpha), None

    (xf, _), _ = lax.scan(body, (xf, xf.astype(BF16)), weights)
    return xf.reshape(batch, seq, d).astype(x.dtype)
```

```python
import functools
import math

import jax
import jax.numpy as jnp
from jax import lax
from jax.experimental import pallas as pl
from jax.experimental.pallas import tpu as pltpu

F32 = jnp.float32
BF16 = jnp.bfloat16

D_MODEL = 2048
GROUP_W = 512
FOX_HD, FOX_HEADS = 64, 8
MLA_HEADS, MLA_NOPE, MLA_ROPE, MLA_VD = 4, 128, 64, 128
MLA_Q_RANK, MLA_KV_RANK = 384, 128
ROPE_THETA = 10000.0
RWKV_HD, RWKV_HEADS = 64, 8
RWKV_GN_EPS = 64e-5
GDN_HD, GDN_HEADS, GDN_CONV = 128, 4, 4
N_GROUPS, EXP_PER_GROUP, TOP_K, D_EXPERT = 4, 8, 2, 512
N_EXPERTS = N_GROUPS * EXP_PER_GROUP
CHUNK = 64
LN_EPS = 1e-5
RMS_EPS = 1e-6
LOG2E = math.log2(math.e)

LANE = 128
VMEM_LIMIT_BYTES = 56 * 1024 * 1024
ROW_TILE = 512
ATTN_TILE = 512
ATTN_KV_TILE = 256
SCAN_TILE = 128
SCAN_BATCH_ROWS = 1
MOE_TILE = 512
MOE_PARTS = 4
LN2_PARTS = 2
ONES_ROWS = 16


def _cparams(*sem):
    return pltpu.CompilerParams(dimension_semantics=sem, vmem_limit_bytes=VMEM_LIMIT_BYTES)


def _bdot(a, b):
    return jnp.dot(a.astype(BF16), b.astype(BF16), preferred_element_type=F32)


def _bdot_nt(a, b):
    return lax.dot_general(a.astype(BF16), b.astype(BF16), (((1,), (1,)), ((), ())), preferred_element_type=F32)


def _bdot_tn(a, b):
    return lax.dot_general(a.astype(BF16), b.astype(BF16), (((0,), (0,)), ((), ())), preferred_element_type=F32)


def _split3(x):
    hi = x.astype(BF16)
    r1 = x - hi.astype(F32)
    mid = r1.astype(BF16)
    lo = (r1 - mid.astype(F32)).astype(BF16)
    return hi, mid, lo


def _dot_exact_lhs(m, x):
    hi, mid, lo = _split3(x)
    d = lambda p: jnp.dot(m, p, preferred_element_type=F32)
    return d(hi) + d(mid) + d(lo)


def _sigmoid(x):
    return 1.0 / (1.0 + jnp.exp(-x))


def _softplus(x):
    return jnp.maximum(x, 0.0) + jnp.log(1.0 + jnp.exp(-jnp.abs(x)))


def _chunk_masks(n, c):
    r = lax.broadcasted_iota(jnp.int32, (n, n), 0)
    col = lax.broadcasted_iota(jnp.int32, (n, n), 1)
    same = (r // c) == (col // c)
    lower = jnp.logical_and(same, col <= r)
    strict = jnp.logical_and(same, col < r)
    return r, col, same, lower, strict


def _tri_inv_masks(r, col, c):
    blk = lambda s: (r // s) == (col // s)
    one = lambda cond: jnp.where(cond, 1.0, 0.0).astype(F32)
    offs = []
    s = 8
    while s < c:
        offs.append(one(jnp.logical_and(blk(2 * s), jnp.logical_not(blk(s)))))
        s *= 2
    return one(r == col), one(blk(8)), offs


def _tri_inv(ms, masks):
    eye, blk8, offs = masks
    bf = lambda a: a.astype(BF16)
    dot = lambda a, b: jnp.dot(a, b, preferred_element_type=F32)
    mdf = [m * blk8 for m in ms]
    mds = [bf(m) for m in mdf]
    xs = [eye + m for m in mdf]
    m2s = [bf(dot(md, md)) for md in mds]
    yield
    xs = [x + dot(m2, bf(x)) for x, m2 in zip(xs, m2s)]
    yield
    m4s = [bf(dot(m2, m2)) for m2 in m2s]
    yield
    xs = [x + dot(m4, bf(x)) for x, m4 in zip(xs, m4s)]
    yield
    n = ms[0].shape[0]
    s = 8
    for off in offs:
        xbs = [bf(x) for x in xs]
        if s % 16:
            ts = [bf(dot(xb, bf(m * off))) for xb, m in zip(xbs, ms)]
            yield
            xs = [x + dot(t, xb) for x, t, xb in zip(xs, ts, xbs)]
        else:
            low = lambda a: jnp.concatenate([a[j + s : j + 2 * s] for j in range(0, n, 2 * s)], axis=0)
            ts = [bf(dot(low(xb), bf(m * off))) for xb, m in zip(xbs, ms)]
            yield
            us = [dot(t, xb) for t, xb in zip(ts, xbs)]
            xs = [
                jnp.concatenate(
                    [p for j in range(0, n, 2 * s) for p in (x[j : j + s], x[j + s : j + 2 * s] + u[j // 2 : j // 2 + s])],
                    axis=0,
                )
                for x, u in zip(xs, us)
            ]
        yield
        s *= 2
    return xs


def _interleave(*stage_generators):
    live = list(stage_generators)
    while live:
        for g in list(live):
            try:
                next(g)
            except StopIteration:
                live.remove(g)


def _inproj_kernel(x_ref, w_ref, o_ref, *tail_refs, tail):
    acc = jnp.dot(x_ref[...], w_ref[...], preferred_element_type=F32)
    n = acc.shape[1]
    if tail:
        o_ref[...] = acc[:, : n - tail].astype(o_ref.dtype)
        tail_refs[0][...] = acc[:, n - tail :]
    else:
        o_ref[...] = acc.astype(o_ref.dtype)


def _inproj(xb, w, tail):
    t, d = xb.shape
    n = w.shape[1]
    tm = min(ROW_TILE, t)
    out_shape = [jax.ShapeDtypeStruct((t, n - tail), BF16)]
    out_specs = [pl.BlockSpec((tm, n - tail), lambda i: (i, 0))]
    if tail:
        out_shape.append(jax.ShapeDtypeStruct((t, tail), F32))
        out_specs.append(pl.BlockSpec((tm, tail), lambda i: (i, 0)))
    return pl.pallas_call(
        functools.partial(_inproj_kernel, tail=tail),
        grid=(t // tm,),
        in_specs=[pl.BlockSpec((tm, d), lambda i: (i, 0)), pl.BlockSpec((d, n), lambda i: (0, 0))],
        out_specs=out_specs,
        out_shape=out_shape,
        compiler_params=_cparams("parallel"),
        name="inproj",
    )(xb, w)


def _attn_kernel(*refs, heads, dk, dv, chunk, tq, tk, seq, use_bias):
    if use_bias:
        q_ref, k_ref, vt_ref, gate_ref, gb_ref, g_ref, o_ref, kb_ref, m_ref, acc_ref = refs
    else:
        q_ref, k_ref, vt_ref, g_ref, o_ref, m_ref, acc_ref = refs
    paired = dk < LANE
    dkp = LANE if paired else dk
    dva = dv + ONES_ROWS
    i = pl.program_id(1)
    lane = lax.broadcasted_iota(jnp.int32, (tq, LANE), 1)

    if use_bias:
        @pl.when(i == 0)
        def _():
            r = lax.broadcasted_iota(jnp.int32, (tk, tk), 0)
            c = lax.broadcasted_iota(jnp.int32, (tk, tk), 1)
            tri = jnp.where(c <= r, 1.0, 0.0).astype(BF16)
            head_lane = lax.broadcasted_iota(jnp.int32, (tk, LANE), 1) < heads
            carry = jnp.zeros((1, LANE), F32)
            for j in range(seq // tk):
                rows = slice(j * tk, (j + 1) * tk)
                log_f = -_softplus(-(gate_ref[rows, :] + gb_ref[...])) * LOG2E
                cum = _dot_exact_lhs(tri, log_f) + carry
                carry = cum[tk - 1 : tk, :]
                hi, mid, lo = (jnp.where(head_lane, p.astype(F32), 0.0) for p in _split3(cum))
                pieces = hi + pltpu.roll(mid, heads, axis=1) + pltpu.roll(lo, 2 * heads, axis=1)
                kb_ref[rows, :] = (-pieces).astype(BF16)

    m_ref[...] = jnp.full(m_ref.shape, -1e30, F32)
    acc_ref[...] = jnp.zeros(acc_ref.shape, F32)

    qs = []
    for h in range(heads):
        if paired:
            slab = q_ref[:, (h // 2) * LANE : (h // 2 + 1) * LANE]
            q = jnp.where(lane // dk == h % 2, slab, jnp.zeros_like(slab))
        else:
            q = q_ref[:, h * dkp : (h + 1) * dkp]
        if use_bias:
            pick = jnp.logical_and(lane % heads == h, lane < 3 * heads)
            q = jnp.concatenate([q, jnp.where(pick, 1.0, 0.0).astype(q.dtype)], axis=1)
        qs.append(q)
    ones_rows = jnp.ones((ONES_ROWS, tk), BF16)

    def step(off, key_shift):
        masked = key_shift is not None
        ql = key_shift if masked else 0
        if masked:
            kr = lax.broadcasted_iota(jnp.int32, (tk, tq - ql), 0) + key_shift
            qc = lax.broadcasted_iota(jnp.int32, (tk, tq - ql), 1) + ql
            allowed = (kr // chunk) <= (qc // chunk)
        sts = []
        for h in range(heads):
            slab = h // 2 if paired else h
            k = k_ref[pl.ds(off, tk), slab * dkp : (slab + 1) * dkp]
            if use_bias:
                k = jnp.concatenate([k, kb_ref[pl.ds(off, tk), :]], axis=1)
            q = qs[h][ql:, :]
            sts.append(lax.dot_general(k, q, (((1,), (1,)), ((), ())), preferred_element_type=F32))
        ps, alphas = [], []
        for h in range(heads):
            st = sts[h]
            if masked:
                st = jnp.where(allowed, st, -1e30)
            m_old = m_ref[h, 0:1, ql:]
            m_new = jnp.maximum(m_old, jnp.max(st, axis=0, keepdims=True))
            m_ref[h, 0:1, ql:] = m_new
            ps.append(jnp.exp2(st - m_new).astype(BF16))
            alphas.append(jnp.exp2(m_old - m_new))
        for h in range(heads):
            vt_h = jnp.concatenate([vt_ref[h * dv : (h + 1) * dv, pl.ds(off, tk)], ones_rows], axis=0)
            rows = slice(h * dva, (h + 1) * dva)
            acc_ref[rows, ql:] = alphas[h] * acc_ref[rows, ql:] + jnp.dot(vt_h, ps[h], preferred_element_type=F32)

    def body(j, carry):
        step(pl.multiple_of(j * tk, tk), None)
        return carry

    lax.fori_loop(0, i * (tq // tk), body, 0)
    for d in range(tq // tk):
        step(pl.multiple_of(i * tq + d * tk, tk), d * tk)

    ot = jnp.concatenate(
        [acc_ref[h * dva : h * dva + dv, :] / acc_ref[h * dva + dv : h * dva + dv + 1, :] for h in range(heads)], axis=0
    )
    ot = ot * lax.rsqrt(jnp.mean(ot * ot, axis=0, keepdims=True) + RMS_EPS)
    o_ref[...] = (ot.T * g_ref[...]).astype(o_ref.dtype)


def _attention(q_arr, q_col, k_arr, k_col, vt, gate, gain, *, batch, seq, heads, dk, dv, chunk):
    bias = gate
    assert dk % LANE == 0 or (2 * dk == LANE and heads % 2 == 0)
    tq = min(ATTN_TILE, seq)
    tk = min(ATTN_KV_TILE, tq)
    nq = seq // tq
    t = batch * seq
    in_specs = [
        pl.BlockSpec((tq, heads * dk), lambda b, i: (b * nq + i, q_col)),
        pl.BlockSpec((seq, heads * dk), lambda b, i: (b, k_col)),
        pl.BlockSpec((None, heads * dv, seq), lambda b, i: (b, 0, 0)),
    ]
    args = [q_arr, k_arr, vt]
    scratch = []
    if bias is not None:
        in_specs += [pl.BlockSpec((seq, LANE), lambda b, i: (b, 0)), pl.BlockSpec((1, LANE), lambda b, i: (0, 0))]
        args += list(gate)
        assert 3 * heads <= LANE
        scratch.append(pltpu.VMEM((seq, LANE), BF16))
    in_specs.append(pl.BlockSpec((1, heads * dv), lambda b, i: (0, 0)))
    args.append(gain)
    scratch += [pltpu.VMEM((heads, 8, tq), F32), pltpu.VMEM((heads * (dv + ONES_ROWS), tq), F32)]
    return pl.pallas_call(
        functools.partial(_attn_kernel, heads=heads, dk=dk, dv=dv, chunk=chunk, tq=tq, tk=tk, seq=seq, use_bias=bias is not None),
        grid=(batch, nq),
        in_specs=in_specs,
        out_specs=pl.BlockSpec((tq, heads * dv), lambda b, i: (b * nq + i, 0)),
        out_shape=jax.ShapeDtypeStruct((t, heads * dv), BF16),
        scratch_shapes=scratch,
        compiler_params=_cparams("parallel", "arbitrary"),
        name="attention",
    )(*args)


def _mla_prep_kernel(u_ref, cs_ref, sn_ref, qg_ref, kvg_ref, wq_ref, wkv_ref, q_ref, k_ref, v_ref):
    u = u_ref[...].astype(F32)
    cs = cs_ref[...]
    sn = sn_ref[...]

    def rms(x, g):
        return x * lax.rsqrt(jnp.mean(x * x, axis=-1, keepdims=True) + RMS_EPS) * g

    qo = _bdot(rms(u[:, :MLA_Q_RANK], qg_ref[...]), wq_ref[...])
    kvo = _bdot(rms(u[:, MLA_Q_RANK : MLA_Q_RANK + MLA_KV_RANK], kvg_ref[...]), wkv_ref[...])
    c0 = MLA_Q_RANK + MLA_KV_RANK
    kpe = (u[:, c0 : c0 + LANE] * cs + u[:, c0 + LANE : c0 + 2 * LANE] * sn).astype(k_ref.dtype)
    nn = MLA_HEADS * MLA_NOPE
    for h in range(MLA_HEADS):
        a = h * 2 * LANE
        q_ref[:, a : a + LANE] = qo[:, h * LANE : (h + 1) * LANE].astype(q_ref.dtype)
        qpe = qo[:, nn + h * LANE : nn + (h + 1) * LANE] * cs + qo[:, 2 * nn + h * LANE : 2 * nn + (h + 1) * LANE] * sn
        q_ref[:, a + LANE : a + 2 * LANE] = qpe.astype(q_ref.dtype)
        k_ref[:, a : a + LANE] = kvo[:, h * LANE : (h + 1) * LANE].astype(k_ref.dtype)
        k_ref[:, a + LANE : a + 2 * LANE] = kpe
    v_ref[...] = kvo[:, nn:].T.astype(v_ref.dtype)


def _mla_prep(u_mla, cs, sn, qg, kvg, wq, wkv, *, batch, seq):
    t = u_mla.shape[0]
    tm = min(ROW_TILE, seq)
    nt = seq // tm
    row = lambda n: pl.BlockSpec((tm, n), lambda i: (i, 0))
    full = lambda a: pl.BlockSpec(a.shape, lambda i: (0,) * a.ndim)
    wide = MLA_HEADS * 2 * LANE
    dvs = MLA_HEADS * MLA_VD
    return pl.pallas_call(
        _mla_prep_kernel,
        grid=(t // tm,),
        in_specs=[row(u_mla.shape[1]), row(LANE), row(LANE), full(qg), full(kvg), full(wq), full(wkv)],
        out_specs=[row(wide), row(wide), pl.BlockSpec((None, dvs, tm), lambda i: (i // nt, 0, i % nt))],
        out_shape=[
            jax.ShapeDtypeStruct((t, wide), BF16),
            jax.ShapeDtypeStruct((t, wide), BF16),
            jax.ShapeDtypeStruct((batch, dvs, seq), BF16),
        ],
        compiler_params=_cparams("parallel"),
        name="mla_prep",
    )(u_mla, cs, sn, qg, kvg, wq, wkv)


def _gdn_stages(u_ref, t_ref, cw_ref, alog_ref, dtb_ref, ng_ref, o_ref, xs_ref, st_ref, *, tt, c):
    i = pl.program_id(1)
    gw, hd = GROUP_W, GDN_HD

    @pl.when(i == 0)
    def _():
        xs_ref[0:8, :] = jnp.zeros((8, 3 * gw), F32)
        st_ref[...] = jnp.zeros(st_ref.shape, F32)

    xs_ref[8 : 8 + tt, :] = u_ref[:, : 3 * gw].astype(F32)
    cw = cw_ref[...]
    conv = cw[0:1, :] * xs_ref[5 : 5 + tt, :]
    for j in range(1, GDN_CONV):
        conv = conv + cw[j : j + 1, :] * xs_ref[5 + j : 5 + j + tt, :]
    xs_ref[0:8, :] = xs_ref[tt : tt + 8, :]
    qkv = conv * _sigmoid(conv)
    yield

    tail = t_ref[...]
    beta = _sigmoid(tail[:, :LANE])
    g = -jnp.exp(alog_ref[...]) * _softplus(tail[:, LANE:] + dtb_ref[...])

    r, col, same, lower, strict = _chunk_masks(tt, c)
    tri = jnp.where(lower, 1.0, 0.0).astype(BF16)
    ones_blk = jnp.where(same, 1.0, 0.0).astype(BF16)
    gc = _dot_exact_lhs(tri, g)
    gtot = _dot_exact_lhs(ones_blk, g)
    gct = gc.T
    yield

    def l2n(x):
        return x * lax.rsqrt(jnp.sum(x * x, axis=-1, keepdims=True) + 1e-6)

    heads = range(GDN_HEADS)
    bf = lambda a: a.astype(BF16)
    lmats, attns, kbs, vbs, qds, kds, egs = [], [], [], [], [], [], []
    for h in heads:
        qf = l2n(qkv[:, h * hd : (h + 1) * hd]) * (hd**-0.5)
        qh = bf(qf)
        kf = l2n(qkv[:, gw + h * hd : gw + (h + 1) * hd])
        kh = bf(kf)
        vh = qkv[:, 2 * gw + h * hd : 2 * gw + (h + 1) * hd]
        gcol = gc[:, h : h + 1]
        dec = jnp.exp(jnp.minimum(gcol - gct[h : h + 1, :], 0.0))
        bcol = beta[:, h : h + 1]
        kb = kf * bcol
        eg = jnp.exp(gcol)
        lmats.append(jnp.where(strict, _bdot_nt(kb, kh) * dec, 0.0))
        attns.append(bf(jnp.where(lower, _bdot_nt(qh, kh) * dec, 0.0)))
        kbs.append(bf(kb * eg))
        vbs.append(bf(vh * bcol))
        qds.append(bf(qf * eg))
        kds.append(bf(kf * jnp.exp(gtot[:, h : h + 1] - gcol)))
        yield
    tinvs = yield from _tri_inv([-m for m in lmats], _tri_inv_masks(r, col, c))
    tinvs = [bf(t) for t in tinvs]
    uws = [jnp.dot(t, jnp.concatenate([vb, kb], axis=1), preferred_element_type=F32) for t, vb, kb in zip(tinvs, vbs, kbs)]
    uvals = [uw[:, :hd] for uw in uws]
    wcums = [bf(uw[:, hd:]) for uw in uws]
    yield
    states = [st_ref[h] for h in heads]
    outs = [[] for _ in heads]
    for cc in range(tt // c):
        rs = slice(cc * c, (cc + 1) * c)
        sbs = [bf(s) for s in states]
        vnews = [uvals[h][rs] - jnp.dot(wcums[h][rs], sbs[h], preferred_element_type=F32) for h in heads]
        yield
        for h in heads:
            outs[h].append(jnp.dot(qds[h][rs], sbs[h], preferred_element_type=F32) + _bdot(attns[h][rs, rs], vnews[h]))
            glast = jnp.exp(gtot[cc * c : cc * c + 1, h : h + 1])
            states[h] = states[h] * glast + _bdot_tn(kds[h][rs], vnews[h])
        yield
    for h in heads:
        st_ref[h] = states[h]
        o = jnp.concatenate(outs[h], axis=0)
        o = o * lax.rsqrt(jnp.mean(o * o, axis=-1, keepdims=True) + RMS_EPS) * ng_ref[...]
        z = u_ref[:, 3 * gw + h * hd : 3 * gw + (h + 1) * hd].astype(F32)
        o_ref[:, h * hd : (h + 1) * hd] = (o * (z * _sigmoid(z))).astype(o_ref.dtype)


def _rwkv_stages(u_ref, mu_ref, w0_ref, w2_ref, a0_ref, a2_ref, g2_ref, kk_ref, ka_ref, rk_ref, lng_ref, lnb_ref,
                 e_ref, o_ref, xs_ref, st_ref, *, tt, c):
    i = pl.program_id(1)
    gw, hd = GROUP_W, RWKV_HD

    @pl.when(i == 0)
    def _():
        xs_ref[0:8, :] = jnp.zeros((8, xs_ref.shape[1]), F32)
        st_ref[...] = jnp.zeros(st_ref.shape, F32)

    u = u_ref[...].astype(F32)
    xs_ref[8 : 8 + tt, :] = u
    prev = xs_ref[7 : 7 + tt, :]
    xs_ref[0:8, :] = xs_ref[tt : tt + 8, :]
    x = u + mu_ref[...] * (prev - u)
    rr = x[:, :gw]
    k = x[:, gw : 2 * gw]
    v = x[:, 2 * gw : 3 * gw]
    wa = x[:, 3 * gw : 3 * gw + LANE]
    glo = x[:, 3 * gw + LANE :]
    w = -_softplus(-(w0_ref[...] + _bdot(jnp.tanh(wa), w2_ref[...]))) - 0.5
    ld = -jnp.exp(w)
    gate_a = _sigmoid(a0_ref[...] + _bdot(wa, a2_ref[...]))
    gate_g = _bdot(_sigmoid(glo), g2_ref[...])
    e = e_ref[...]

    def segsum(y):
        hi = y.astype(BF16)
        lo = (y - hi.astype(F32)).astype(BF16)
        return jnp.dot(hi, e, preferred_element_type=F32) + jnp.dot(lo, e, preferred_element_type=F32)

    kkr = k * kk_ref[...]
    kk = kkr * lax.rsqrt(segsum(kkr * kkr) + 1e-6)
    k2 = k * (1.0 + (gate_a - 1.0) * ka_ref[...])
    bonus = segsum(rr * k2 * rk_ref[...]) * v
    yield

    r, col, same, lower, strict = _chunk_masks(tt, c)
    tri = jnp.where(lower, 1.0, 0.0).astype(BF16)
    ones_blk = jnp.where(same, 1.0, 0.0).astype(BF16)
    cum = _dot_exact_lhs(tri, ld)
    ctot = _dot_exact_lhs(ones_blk, ld)
    encum = jnp.exp(-cum)
    edec = jnp.exp(ctot - cum)
    kka = kk * gate_a
    rt = rr * jnp.exp(cum)
    at = -kk * jnp.exp(cum - ld)
    bt = kka * encum
    kt = k2 * encum
    bd = kka * edec
    kd = k2 * edec
    pc = jnp.exp(ctot)
    yield

    heads = range(RWKV_HEADS)
    bf = lambda a: a.astype(BF16)
    dot = lambda a, b: jnp.dot(a, b, preferred_element_type=F32)
    dot_nt = lambda a, b: lax.dot_general(a, b, (((1,), (1,)), ((), ())), preferred_element_type=F32)
    dot_tn = lambda a, b: lax.dot_general(a, b, (((0,), (0,)), ((), ())), preferred_element_type=F32)
    at_b, rt_b, bt_b, kt_b, v_b, bd_b, kd_b = bf(at), bf(rt), bf(bt), bf(kt), bf(v), bf(bd), bf(kd)
    sls = [slice(h * hd, (h + 1) * hd) for h in heads]
    strict_f = jnp.where(strict, 1.0, 0.0).astype(F32)
    lower_f = jnp.where(lower, 1.0, 0.0).astype(F32)
    mabs = [dot_nt(at_b[:, sl], bt_b[:, sl]) * strict_f for sl in sls]
    yield
    maks = [bf(dot_nt(at_b[:, sl], kt_b[:, sl]) * strict_f) for sl in sls]
    yield
    arbs = [bf(dot_nt(rt_b[:, sl], bt_b[:, sl]) * lower_f) for sl in sls]
    yield
    arks = [bf(dot_nt(rt_b[:, sl], kt_b[:, sl]) * lower_f) for sl in sls]
    yield
    tinvs = yield from _tri_inv(mabs, _tri_inv_masks(r, col, c))
    tinvs = [bf(t) for t in tinvs]
    wmats = [bf(dot(t, at_b[:, sl])) for t, sl in zip(tinvs, sls)]
    mkvs = [bf(dot(m, v_b[:, sl])) for m, sl in zip(maks, sls)]
    yield
    umats = [dot(t, mkv) for t, mkv in zip(tinvs, mkvs)]
    yconsts = [dot(m, v_b[:, sl]) for m, sl in zip(arks, sls)]
    yield
    states = [st_ref[h] for h in heads]
    outs = [[] for _ in heads]
    for cc in range(tt // c):
        rs = slice(cc * c, (cc + 1) * c)
        sbs = [bf(s) for s in states]
        sas = [dot_nt(wmats[h][rs], sbs[h]) + umats[h][rs] for h in heads]
        yield
        for h in heads:
            sab = bf(sas[h])
            outs[h].append(dot_nt(rt_b[rs, sls[h]], sbs[h]) + dot(arbs[h][rs, rs], sab) + yconsts[h][rs])
            states[h] = (states[h] * pc[cc * c : cc * c + 1, sls[h]] + dot_tn(sab, bd_b[rs, sls[h]])
                         + dot_tn(v_b[rs, sls[h]], kd_b[rs, sls[h]]))
        yield
    for h in heads:
        st_ref[h] = states[h]
    y = jnp.concatenate([jnp.concatenate(o, axis=0) for o in outs], axis=-1)
    mean = segsum(y) * (1.0 / hd)
    d = y - mean
    var = segsum(d * d) * (1.0 / hd)
    yn = d * lax.rsqrt(var + RWKV_GN_EPS) * lng_ref[...] + lnb_ref[...]
    o_ref[...] = ((yn + bonus) * gate_g).astype(o_ref.dtype)


def _scan_mixers_kernel(*refs, n_rwkv_in, n_gdn_in, nb, tt, c):
    rwkv_in = refs[:n_rwkv_in]
    gdn_in = refs[n_rwkv_in : n_rwkv_in + n_gdn_in]
    o_rwkv, o_gdn, xs_rwkv, st_rwkv, xs_gdn, st_gdn = refs[n_rwkv_in + n_gdn_in :]
    gens = []
    for s in range(nb):
        gens.append(_rwkv_stages(rwkv_in[0].at[s], *rwkv_in[1:], o_rwkv.at[s], xs_rwkv.at[s], st_rwkv.at[s], tt=tt, c=c))
        gens.append(_gdn_stages(gdn_in[0].at[s], gdn_in[1].at[s], *gdn_in[2:], o_gdn.at[s], xs_gdn.at[s], st_gdn.at[s],
                                tt=tt, c=c))
    _interleave(*gens)


def _scan_mixers(rwkv_args, gdn_args, *, batch, seq):
    tt = min(SCAN_TILE, seq)
    nt = seq // tt
    t = batch * seq
    nb = SCAN_BATCH_ROWS if batch % SCAN_BATCH_ROWS == 0 else 1
    by_batch = lambda a: a.reshape(batch // nb, nb, seq, a.shape[1])
    tile = lambda a: pl.BlockSpec((None, nb, tt, a.shape[1]), lambda b, i: (b, 0, i, 0))
    full = lambda a: pl.BlockSpec(a.shape, lambda b, i: (0,) * a.ndim)
    u_rwkv, u_gdn, gdn_tail = rwkv_args[0], gdn_args[0], gdn_args[1]
    in_specs = [tile(u_rwkv)] + [full(p) for p in rwkv_args[1:]] + [tile(u_gdn), tile(gdn_tail)] + [full(p) for p in gdn_args[2:]]
    out = pl.BlockSpec((None, nb, tt, GROUP_W), lambda b, i: (b, 0, i, 0))
    y_rwkv, y_gdn = pl.pallas_call(
        functools.partial(_scan_mixers_kernel, n_rwkv_in=len(rwkv_args), n_gdn_in=len(gdn_args), nb=nb, tt=tt, c=CHUNK),
        grid=(batch // nb, nt),
        in_specs=in_specs,
        out_specs=[out, out],
        out_shape=[jax.ShapeDtypeStruct((batch // nb, nb, seq, GROUP_W), BF16)] * 2,
        scratch_shapes=[
            pltpu.VMEM((nb, tt + 8, u_rwkv.shape[1]), F32),
            pltpu.VMEM((nb, RWKV_HEADS, RWKV_HD, RWKV_HD), F32),
            pltpu.VMEM((nb, tt + 8, 3 * GROUP_W), F32),
            pltpu.VMEM((nb, GDN_HEADS, GDN_HD, GDN_HD), F32),
        ],
        compiler_params=_cparams("parallel", "arbitrary"),
        name="rwkv7_gdn",
    )(by_batch(u_rwkv), *rwkv_args[1:], by_batch(u_gdn), by_batch(gdn_tail), *gdn_args[2:])
    return y_rwkv.reshape(t, GROUP_W), y_gdn.reshape(t, GROUP_W)


def _layer_norm(h, g, b):
    mu = jnp.mean(h, axis=-1, keepdims=True)
    d = h - mu
    var = jnp.mean(d * d, axis=-1, keepdims=True)
    return d * lax.rsqrt(var + LN_EPS) * g + b


def _route_rows(lg):
    neg = -1e30
    lane = lax.broadcasted_iota(jnp.int32, lg.shape, 1)
    lane_f = lane.astype(F32)
    first = lambda hit: jnp.min(jnp.where(hit, lane_f, float(LANE)), axis=-1, keepdims=True)
    is_g = lane < N_GROUPS
    gl = jnp.where(is_g, lg, neg)
    gmax = jnp.max(gl, axis=-1, keepdims=True)
    p_grp = 1.0 / jnp.sum(jnp.where(is_g, jnp.exp(gl - gmax), 0.0), axis=-1, keepdims=True)
    grp = first(gl == gmax)
    lo = N_GROUPS + grp * EXP_PER_GROUP
    el = jnp.where(jnp.logical_and(lane_f >= lo, lane_f < lo + EXP_PER_GROUP), lg, neg)
    v1 = jnp.max(el, axis=-1, keepdims=True)
    i1 = first(el == v1)
    el2 = jnp.where(lane_f == i1, neg, el)
    v2 = jnp.max(el2, axis=-1, keepdims=True)
    i2 = first(el2 == v2)
    r = jnp.exp(v2 - v1)
    g1 = p_grp / (1.0 + r)
    out = jnp.where(lane == 0, i1 - N_GROUPS, 0.0)
    out = jnp.where(lane == 1, i2 - N_GROUPS, out)
    out = jnp.where(lane == 2, g1, out)
    return jnp.where(lane == 3, g1 * r, out)


def _outproj_kernel(yf_ref, ym_ref, yr_ref, yg_ref, w_ref, x_ref, g_ref, b_ref, wr_ref, rb_ref, tril_ref,
                    xo_ref, xb_ref, lg_ref, cnt_ref, *, alpha):
    mixed = jnp.concatenate([yf_ref[...], ym_ref[...], yr_ref[...], yg_ref[...]], axis=1)
    acc = jnp.dot(mixed, w_ref[...], preferred_element_type=F32)
    xn = _layer_norm(alpha * x_ref[...] + acc, g_ref[...], b_ref[...])
    xo_ref[...] = xn
    xh = xn.astype(BF16)
    xb_ref[...] = xh
    xl = (xn - xh.astype(F32)).astype(BF16)
    d = lambda a, b: jnp.dot(a, b, preferred_element_type=F32)
    both_w = d(xh, wr_ref[...])
    routed = _route_rows(both_w[:, :LANE] + both_w[:, LANE:] + d(xl, wr_ref[:, :LANE]) + rb_ref[...])

    @pl.when(pl.program_id(0) == 0)
    def _():
        cnt_ref[...] = jnp.zeros(cnt_ref.shape, F32)

    lane = lax.broadcasted_iota(jnp.int32, routed.shape, 1).astype(F32)
    oh = [jnp.where(lane == routed[:, k : k + 1], 1.0, 0.0) for k in range(TOP_K)]
    both = oh[0] + oh[1]
    before = jnp.dot(tril_ref[...], both.astype(BF16), preferred_element_type=F32) + cnt_ref[0:1, :]
    for k in range(TOP_K):
        rank = jnp.sum(before * oh[k], axis=-1, keepdims=True)
        routed = jnp.where(lane == 2 * TOP_K + k, rank, routed)
    cnt_ref[...] = cnt_ref[...] + jnp.sum(both, axis=0, keepdims=True)
    lg_ref[...] = routed


def _outproj(ys, w_out, x, ln_g, ln_b, w_router, r_bias, alpha):
    t, d = x.shape
    tm = min(ROW_TILE, t)
    row = lambda n: pl.BlockSpec((tm, n), lambda i: (i, 0))
    full = lambda a: pl.BlockSpec(a.shape, lambda i: (0,) * a.ndim)
    idx = jnp.arange(tm)
    tril = (idx[None, :] < idx[:, None]).astype(BF16)
    return pl.pallas_call(
        functools.partial(_outproj_kernel, alpha=alpha),
        grid=(t // tm,),
        in_specs=[row(GROUP_W)] * 4
        + [full(w_out), row(d), full(ln_g), full(ln_b), full(w_router), full(r_bias), full(tril)],
        out_specs=[row(d), row(d), row(LANE), pl.BlockSpec((8, LANE), lambda i: (0, 0))],
        out_shape=[
            jax.ShapeDtypeStruct((t, d), F32),
            jax.ShapeDtypeStruct((t, d), BF16),
            jax.ShapeDtypeStruct((t, LANE), F32),
            jax.ShapeDtypeStruct((8, LANE), F32),
        ],
        compiler_params=_cparams("arbitrary"),
        name="outproj_ln_router",
    )(*ys, w_out, x, ln_g, ln_b, w_router, r_bias, tril)


def _expert_kernel(be_ref, nu_ref, x_ref, wg_ref, wu_ref, wd_ref, *rest, first_block):
    o_ref, wg_s, wu_s, wd_s = rest[-4:]
    i = pl.program_id(0)
    blk = i + first_block
    used = blk < nu_ref[0]

    @pl.when(jnp.logical_and(used, jnp.logical_or(i == 0, be_ref[blk] != be_ref[jnp.maximum(blk - 1, 0)])))
    def _():
        wg_s[...] = wg_ref[...].astype(BF16)
        wu_s[...] = wu_ref[...].astype(BF16)
        wd_s[...] = wd_ref[...].astype(BF16)

    @pl.when(used)
    def _():
        x = x_ref[...]
        a = jnp.dot(x, wg_s[...], preferred_element_type=F32)
        b = jnp.dot(x, wu_s[...], preferred_element_type=F32)
        hmid = (a * _sigmoid(a) * b).astype(BF16)
        o_ref[...] = jnp.dot(hmid, wd_s[...], preferred_element_type=F32).astype(o_ref.dtype)

    @pl.when(jnp.logical_not(used))
    def _():
        o_ref[...] = jnp.zeros(o_ref.shape, o_ref.dtype)


def _experts(block_weight, n_used, xs_part, w_gate, w_up, w_down, *, first_block, n_slots, y_prev):
    d = xs_part.shape[1]
    tb = MOE_TILE
    de = w_gate.shape[-1]
    weight = lambda shape: pl.BlockSpec((None,) + shape, lambda i, be, nu: (be[i + first_block], 0, 0))
    in_specs = [pl.BlockSpec((tb, d), lambda i, be, nu: (i, 0)), weight((d, de)), weight((d, de)), weight((de, d))]
    args = [block_weight, n_used, xs_part, w_gate, w_up, w_down]
    aliases = {}
    if y_prev is not None:
        in_specs.append(pl.BlockSpec(memory_space=pl.ANY))
        aliases = {len(args): 0}
        args.append(y_prev)
    grid_spec = pltpu.PrefetchScalarGridSpec(
        num_scalar_prefetch=2,
        grid=(xs_part.shape[0] // tb,),
        in_specs=in_specs,
        out_specs=pl.BlockSpec((tb, d), lambda i, be, nu: (i + first_block, 0)),
        scratch_shapes=[pltpu.VMEM((d, de), BF16), pltpu.VMEM((d, de), BF16), pltpu.VMEM((de, d), BF16)],
    )
    return pl.pallas_call(
        functools.partial(_expert_kernel, first_block=first_block),
        grid_spec=grid_spec,
        out_shape=jax.ShapeDtypeStruct((n_slots, d), BF16),
        input_output_aliases=aliases,
        compiler_params=_cparams("arbitrary"),
        name="experts",
    )(*args)


def _ln2_kernel(x_ref, y1_ref, y2_ref, r_ref, g_ref, b_ref, *rest, alpha):
    xo_ref, xb_ref = rest[-2:]
    routed = r_ref[...]
    moe = routed[:, TOP_K : TOP_K + 1] * y1_ref[...].astype(F32) + routed[:, TOP_K + 1 : TOP_K + 2] * y2_ref[...].astype(F32)
    xn = _layer_norm(alpha * x_ref[...] + moe, g_ref[...], b_ref[...])
    xo_ref[...] = xn
    xb_ref[...] = xn.astype(BF16)


def _ln2(x, y1, y2, routed, g, b, alpha, *, row0, prev):
    t, d = x.shape
    tm = min(ROW_TILE, y1.shape[0])
    b0 = row0 // tm
    row = lambda n: pl.BlockSpec((tm, n), lambda i: (i + b0, 0))
    part = pl.BlockSpec((tm, d), lambda i: (i, 0))
    full = lambda a: pl.BlockSpec(a.shape, lambda i: (0,) * a.ndim)
    in_specs = [row(d), part, part, row(LANE), full(g), full(b)]
    args = [x, y1, y2, routed, g, b]
    aliases = {}
    if prev is not None:
        in_specs += [pl.BlockSpec(memory_space=pl.ANY)] * 2
        aliases = {len(args): 0, len(args) + 1: 1}
        args += list(prev)
    return pl.pallas_call(
        functools.partial(_ln2_kernel, alpha=alpha),
        grid=(y1.shape[0] // tm,),
        in_specs=in_specs,
        out_specs=[row(d), row(d)],
        out_shape=[jax.ShapeDtypeStruct((t, d), F32), jax.ShapeDtypeStruct((t, d), BF16)],
        input_output_aliases=aliases,
        compiler_params=_cparams("parallel"),
        name="residual_ln2",
    )(*args)


def _pad_cols(w, n):
    return jnp.pad(w, [(0, 0)] * (w.ndim - 1) + [(0, n - w.shape[-1])])


def _rot_half_cols(w):
    half = w.shape[-1] // 2
    return jnp.concatenate([-w[..., half:], w[..., :half]], axis=-1)


def _prep_weights(p):
    w_in = p["w_in"]
    c_fox = 3 * GROUP_W + FOX_HEADS
    c_mla = MLA_Q_RANK + MLA_KV_RANK + MLA_ROPE
    c_rwkv = 3 * GROUP_W + 2 * 64 + 128
    o_mla = c_fox
    o_rwkv = o_mla + c_mla
    o_gdn = o_rwkv + c_rwkv
    row = lambda a: a[:, None, :].astype(F32)

    w_fox = jnp.concatenate(
        [w_in[..., :GROUP_W] * (FOX_HD**-0.5 * LOG2E), w_in[..., GROUP_W : 3 * GROUP_W], _pad_cols(w_in[..., 3 * GROUP_W : c_fox], LANE)],
        axis=-1,
    )
    kpe_w = w_in[..., o_mla + MLA_Q_RANK + MLA_KV_RANK : o_mla + c_mla]
    w_mla = jnp.concatenate(
        [w_in[..., o_mla : o_mla + MLA_Q_RANK + MLA_KV_RANK], _pad_cols(kpe_w, LANE), _pad_cols(_rot_half_cols(kpe_w), LANE)],
        axis=-1,
    )
    w_rwkv = w_in[..., o_rwkv:o_gdn]
    g0 = o_gdn + 4 * GROUP_W
    w_gdn = jnp.concatenate(
        [w_in[..., o_gdn:g0], _pad_cols(w_in[..., g0 : g0 + GDN_HEADS], LANE), _pad_cols(w_in[..., g0 + GDN_HEADS :], LANE)],
        axis=-1,
    )

    nl = w_in.shape[0]
    scale = (MLA_NOPE + MLA_ROPE) ** -0.5 * LOG2E
    wq = p["mla_w_uq"].reshape(nl, MLA_Q_RANK, MLA_HEADS, MLA_NOPE + MLA_ROPE) * scale
    wq_nope = wq[..., :MLA_NOPE].reshape(nl, MLA_Q_RANK, -1)
    wq_pe = wq[..., MLA_NOPE:]
    wq_p = jnp.concatenate(
        [wq_nope, _pad_cols(wq_pe, LANE).reshape(nl, MLA_Q_RANK, -1), _pad_cols(_rot_half_cols(wq_pe), LANE).reshape(nl, MLA_Q_RANK, -1)],
        axis=-1,
    )
    wkv = p["mla_w_ukv"].reshape(nl, MLA_KV_RANK, MLA_HEADS, MLA_NOPE + MLA_VD)
    wkv_p = jnp.concatenate([wkv[..., :MLA_NOPE].reshape(nl, MLA_KV_RANK, -1), wkv[..., MLA_NOPE:].reshape(nl, MLA_KV_RANK, -1)], axis=-1)

    zeros64 = jnp.zeros((nl, 64, GROUP_W), F32)
    w_router = _pad_cols(jnp.concatenate([p["moe_w_grp"], p["moe_w_exp"]], axis=-1), LANE)
    wr_hi = w_router.astype(BF16)
    wr_lo = (w_router - wr_hi.astype(F32)).astype(BF16)
    return dict(
        w_fox=w_fox.astype(BF16), w_mla=w_mla.astype(BF16), w_rwkv=w_rwkv.astype(BF16), w_gdn=w_gdn.astype(BF16),
        layer=jnp.arange(nl, dtype=jnp.int32),
        fox_b_f=row(_pad_cols(p["fox_b_f"], LANE)), fox_out_g=row(p["fox_out_g"]),
        mla_qg=row(p["mla_q_norm_g"]), mla_kvg=row(p["mla_kv_norm_g"]), mla_wq=wq_p.astype(BF16), mla_wkv=wkv_p.astype(BF16),
        mla_out_g=row(p["mla_out_g"]),
        rwkv_mu=row(p["rwkv_mu"]), rwkv_w0=row(p["rwkv_w0"]),
        rwkv_w2=jnp.concatenate([p["rwkv_w2"], zeros64], axis=1).astype(BF16),
        rwkv_a0=row(p["rwkv_a0"]), rwkv_a2=jnp.concatenate([zeros64, p["rwkv_a2"]], axis=1).astype(BF16),
        rwkv_g2=p["rwkv_g2"].astype(BF16), rwkv_k_k=row(p["rwkv_k_k"]), rwkv_k_a=row(p["rwkv_k_a"]),
        rwkv_r_k=row(p["rwkv_r_k"]), rwkv_ln_g=row(p["rwkv_ln_g"]), rwkv_ln_b=row(p["rwkv_ln_b"]),
        gdn_conv_w=p["gdn_conv_w"].astype(F32), gdn_a_log=row(_pad_cols(p["gdn_a_log"], LANE)),
        gdn_dt_bias=row(_pad_cols(p["gdn_dt_bias"], LANE)), gdn_norm_g=row(p["gdn_norm_g"]),
        w_out=p["w_out"].astype(BF16), ln1_g=row(p["ln1_g"]), ln1_b=row(p["ln1_b"]),
        w_router=jnp.concatenate([wr_hi, wr_lo], axis=-1), r_bias=row(_pad_cols(jnp.concatenate([p["moe_b_grp"], p["moe_b_exp"]], axis=-1), LANE)),
        ln2_g=row(p["ln2_g"]), ln2_b=row(p["ln2_b"]),
    )


def _route(routed, counts, tb):
    t = routed.shape[0]
    a = t * TOP_K
    n_blocks = (a + N_EXPERTS * (tb - 1) + tb - 1) // tb
    n_slots = n_blocks * tb
    expert = routed[:, :TOP_K].astype(jnp.int32)
    rank = routed[:, 2 * TOP_K : 3 * TOP_K].astype(jnp.int32)
    padded = (counts + tb - 1) // tb * tb
    pend = jnp.cumsum(padded)
    pstart = pend - padded
    ids = jnp.arange(N_EXPERTS, dtype=jnp.int32)
    slot_of_assignment = jnp.sum(jnp.where(expert[..., None] == ids, pstart, 0), axis=-1) + rank
    block_start = jnp.arange(n_blocks, dtype=jnp.int32) * tb
    block_expert = jnp.minimum(jnp.sum(block_start[:, None] >= pend, axis=-1), N_EXPERTS - 1).astype(jnp.int32)
    fill_end = jnp.cumsum(padded - counts)
    filler = jnp.arange(n_slots - a, dtype=jnp.int32)
    fill_key = jnp.sum(filler[:, None] >= fill_end, axis=-1).astype(jnp.int32)
    keys = jnp.concatenate([expert.reshape(a), fill_key])
    vals = jnp.concatenate([jnp.arange(a, dtype=jnp.int32) // TOP_K, filler % t])
    _, token_of_slot = lax.sort((keys, vals), num_keys=1, is_stable=True)
    n_used = (pend[-1:] // tb).astype(jnp.int32)
    return token_of_slot, block_expert, n_used, slot_of_assignment


def _layer(x, xb, cs, sn, w, moe_w, seg_ones, *, batch, seq, alpha):
    u_fox, fox_tail = _inproj(xb, w["w_fox"], LANE)
    (u_mla,) = _inproj(xb, w["w_mla"], 0)
    (u_rwkv,) = _inproj(xb, w["w_rwkv"], 0)
    u_gdn, gdn_tail = _inproj(xb, w["w_gdn"], 2 * LANE)

    vt_fox = u_fox[:, 2 * GROUP_W :].reshape(batch, seq, GROUP_W).transpose(0, 2, 1)
    y_fox = _attention(u_fox, 0, u_fox, 1, vt_fox, (fox_tail, w["fox_b_f"]), w["fox_out_g"], batch=batch, seq=seq,
                       heads=FOX_HEADS, dk=FOX_HD, dv=FOX_HD, chunk=1)

    q_mla, k_mla, vt_mla = _mla_prep(u_mla, cs, sn, w["mla_qg"], w["mla_kvg"], w["mla_wq"], w["mla_wkv"],
                                     batch=batch, seq=seq)
    y_mla = _attention(q_mla, 0, k_mla, 0, vt_mla, None, w["mla_out_g"], batch=batch, seq=seq,
                       heads=MLA_HEADS, dk=2 * LANE, dv=MLA_VD, chunk=CHUNK)

    y_rwkv, y_gdn = _scan_mixers(
        (u_rwkv, w["rwkv_mu"], w["rwkv_w0"], w["rwkv_w2"], w["rwkv_a0"], w["rwkv_a2"], w["rwkv_g2"],
         w["rwkv_k_k"], w["rwkv_k_a"], w["rwkv_r_k"], w["rwkv_ln_g"], w["rwkv_ln_b"], seg_ones),
        (u_gdn, gdn_tail, w["gdn_conv_w"], w["gdn_a_log"], w["gdn_dt_bias"], w["gdn_norm_g"]),
        batch=batch, seq=seq)

    x1, x1b, routed, counts = _outproj((y_fox, y_mla, y_rwkv, y_gdn), w["w_out"], x, w["ln1_g"], w["ln1_b"],
                                       w["w_router"], w["r_bias"], alpha)

    token_of_slot, block_expert, n_used, slot_of_assignment = _route(
        routed, counts[0, :N_EXPERTS].astype(jnp.int32), MOE_TILE)
    block_weight = block_expert + w["layer"] * N_EXPERTS
    n_slots = token_of_slot.shape[0]
    part = pl.cdiv(n_slots // MOE_TILE, MOE_PARTS) * MOE_TILE
    y_slots = None
    for lo in range(0, n_slots, part):
        y_slots = _experts(block_weight, n_used, x1b[token_of_slot[lo : lo + part]], *moe_w,
                           first_block=lo // MOE_TILE, n_slots=n_slots, y_prev=y_slots)
    out = None
    rows = x1.shape[0] // LN2_PARTS
    for r0 in range(0, x1.shape[0], rows):
        slots = slot_of_assignment[r0 : r0 + rows]
        out = _ln2(x1, y_slots[slots[:, 0]], y_slots[slots[:, 1]], routed, w["ln2_g"], w["ln2_b"], alpha,
                   row0=r0, prev=out)
    return tuple(out)


def kernel(x, positions, w_in, fox_b_f, fox_out_g, mla_q_norm_g, mla_kv_norm_g, mla_w_uq, mla_w_ukv, mla_out_g, rwkv_mu, rwkv_w0, rwkv_w2, rwkv_a0, rwkv_a2, rwkv_g2, rwkv_k_k, rwkv_k_a, rwkv_r_k, rwkv_ln_g, rwkv_ln_b, gdn_conv_w, gdn_a_log, gdn_dt_bias, gdn_norm_g, w_out, ln1_g, ln1_b, moe_w_grp, moe_b_grp, moe_w_exp, moe_b_exp, moe_w_gate, moe_w_up, moe_w_down, ln2_g, ln2_b):
    batch, seq, d = x.shape
    depth = w_in.shape[0]
    alpha = (2 * depth) ** 0.25
    params = dict(
        w_in=w_in, fox_b_f=fox_b_f, fox_out_g=fox_out_g, mla_q_norm_g=mla_q_norm_g, mla_kv_norm_g=mla_kv_norm_g,
        mla_w_uq=mla_w_uq, mla_w_ukv=mla_w_ukv, mla_out_g=mla_out_g, rwkv_mu=rwkv_mu, rwkv_w0=rwkv_w0, rwkv_w2=rwkv_w2,
        rwkv_a0=rwkv_a0, rwkv_a2=rwkv_a2, rwkv_g2=rwkv_g2, rwkv_k_k=rwkv_k_k, rwkv_k_a=rwkv_k_a, rwkv_r_k=rwkv_r_k,
        rwkv_ln_g=rwkv_ln_g, rwkv_ln_b=rwkv_ln_b, gdn_conv_w=gdn_conv_w, gdn_a_log=gdn_a_log, gdn_dt_bias=gdn_dt_bias,
        gdn_norm_g=gdn_norm_g, w_out=w_out, ln1_g=ln1_g, ln1_b=ln1_b, moe_w_grp=moe_w_grp, moe_b_grp=moe_b_grp,
        moe_w_exp=moe_w_exp, moe_b_exp=moe_b_exp, moe_w_gate=moe_w_gate, moe_w_up=moe_w_up, moe_w_down=moe_w_down,
        ln2_g=ln2_g, ln2_b=ln2_b,
    )
    weights = _prep_weights(params)

    half = MLA_ROPE // 2
    inv_freq = ROPE_THETA ** (-jnp.arange(half, dtype=F32) / half)
    ang = positions.astype(F32).reshape(batch * seq, 1) * inv_freq
    zpad = jnp.zeros((batch * seq, LANE - MLA_ROPE), F32)
    cs = jnp.concatenate([jnp.cos(ang), jnp.cos(ang), zpad], axis=-1)
    sn = jnp.concatenate([jnp.sin(ang), jnp.sin(ang), zpad], axis=-1)

    xf = x.reshape(batch * seq, d).astype(F32)

    stack = lambda a: a.reshape((-1,) + a.shape[2:])
    moe_w = (stack(moe_w_gate), stack(moe_w_up), stack(moe_w_down))
    seg = jnp.arange(GROUP_W) // RWKV_HD
    seg_ones = (seg[:, None] == seg[None, :]).astype(BF16)

    def body(carry, w):
        xc, xcb = carry
        return _layer(xc, xcb, cs, sn, w, moe_w, seg_ones, batch=batch, seq=seq, alpha=alpha), None

    (xf, _), _ = lax.scan(body, (xf, xf.astype(BF16)), weights)
    return xf.reshape(batch, seq, d).astype(x.dtype)
```

```python
import functools
import math

import jax
import jax.numpy as jnp
from jax import lax
from jax.experimental import pallas as pl
from jax.experimental.pallas import tpu as pltpu

F32 = jnp.float32
BF16 = jnp.bfloat16

D_MODEL = 2048
GROUP_W = 512
FOX_HD, FOX_HEADS = 64, 8
MLA_HEADS, MLA_NOPE, MLA_ROPE, MLA_VD = 4, 128, 64, 128
MLA_Q_RANK, MLA_KV_RANK = 384, 128
ROPE_THETA = 10000.0
RWKV_HD, RWKV_HEADS = 64, 8
RWKV_GN_EPS = 64e-5
GDN_HD, GDN_HEADS, GDN_CONV = 128, 4, 4
N_GROUPS, EXP_PER_GROUP, TOP_K, D_EXPERT = 4, 8, 2, 512
N_EXPERTS = N_GROUPS * EXP_PER_GROUP
CHUNK = 64
LN_EPS = 1e-5
RMS_EPS = 1e-6
LOG2E = math.log2(math.e)

LANE = 128
VMEM_LIMIT_BYTES = 56 * 1024 * 1024
ROW_TILE = 512
ATTN_TILE = 512
ATTN_KV_TILE = 256
SCAN_TILE = 128
SCAN_BATCH_ROWS = 1
MOE_TILE = 512
MOE_PARTS = 4
LN2_PARTS = 4
ONES_ROWS = 16


def _cparams(*sem):
    return pltpu.CompilerParams(dimension_semantics=sem, vmem_limit_bytes=VMEM_LIMIT_BYTES)


def _bdot(a, b):
    return jnp.dot(a.astype(BF16), b.astype(BF16), preferred_element_type=F32)


def _bdot_nt(a, b):
    return lax.dot_general(a.astype(BF16), b.astype(BF16), (((1,), (1,)), ((), ())), preferred_element_type=F32)


def _bdot_tn(a, b):
    return lax.dot_general(a.astype(BF16), b.astype(BF16), (((0,), (0,)), ((), ())), preferred_element_type=F32)


def _split3(x):
    hi = x.astype(BF16)
    r1 = x - hi.astype(F32)
    mid = r1.astype(BF16)
    lo = (r1 - mid.astype(F32)).astype(BF16)
    return hi, mid, lo


def _dot_exact_lhs(m, x):
    hi, mid, lo = _split3(x)
    d = lambda p: jnp.dot(m, p, preferred_element_type=F32)
    return d(hi) + d(mid) + d(lo)


def _sigmoid(x):
    return 1.0 / (1.0 + jnp.exp(-x))


def _softplus(x):
    return jnp.maximum(x, 0.0) + jnp.log(1.0 + jnp.exp(-jnp.abs(x)))


def _chunk_masks(n, c):
    r = lax.broadcasted_iota(jnp.int32, (n, n), 0)
    col = lax.broadcasted_iota(jnp.int32, (n, n), 1)
    same = (r // c) == (col // c)
    lower = jnp.logical_and(same, col <= r)
    strict = jnp.logical_and(same, col < r)
    return r, col, same, lower, strict


def _tri_inv_masks(r, col, c):
    blk = lambda s: (r // s) == (col // s)
    one = lambda cond: jnp.where(cond, 1.0, 0.0).astype(F32)
    offs = []
    s = 8
    while s < c:
        offs.append(one(jnp.logical_and(blk(2 * s), jnp.logical_not(blk(s)))))
        s *= 2
    return one(r == col), one(blk(8)), offs


def _tri_inv(ms, masks):
    eye, blk8, offs = masks
    bf = lambda a: a.astype(BF16)
    dot = lambda a, b: jnp.dot(a, b, preferred_element_type=F32)
    mdf = [m * blk8 for m in ms]
    mds = [bf(m) for m in mdf]
    xs = [eye + m for m in mdf]
    m2s = [bf(dot(md, md)) for md in mds]
    yield
    xs = [x + dot(m2, bf(x)) for x, m2 in zip(xs, m2s)]
    yield
    m4s = [bf(dot(m2, m2)) for m2 in m2s]
    yield
    xs = [x + dot(m4, bf(x)) for x, m4 in zip(xs, m4s)]
    yield
    n = ms[0].shape[0]
    s = 8
    for off in offs:
        xbs = [bf(x) for x in xs]
        if s % 16:
            ts = [bf(dot(xb, bf(m * off))) for xb, m in zip(xbs, ms)]
            yield
            xs = [x + dot(t, xb) for x, t, xb in zip(xs, ts, xbs)]
        else:
            low = lambda a: jnp.concatenate([a[j + s : j + 2 * s] for j in range(0, n, 2 * s)], axis=0)
            ts = [bf(dot(low(xb), bf(m * off))) for xb, m in zip(xbs, ms)]
            yield
            us = [dot(t, xb) for t, xb in zip(ts, xbs)]
            xs = [
                jnp.concatenate(
                    [p for j in range(0, n, 2 * s) for p in (x[j : j + s], x[j + s : j + 2 * s] + u[j // 2 : j // 2 + s])],
                    axis=0,
                )
                for x, u in zip(xs, us)
            ]
        yield
        s *= 2
    return xs


def _interleave(*stage_generators):
    live = list(stage_generators)
    while live:
        for g in list(live):
            try:
                next(g)
            except StopIteration:
                live.remove(g)


def _inproj_kernel(x_ref, w_ref, o_ref, *tail_refs, tail):
    acc = jnp.dot(x_ref[...], w_ref[...], preferred_element_type=F32)
    n = acc.shape[1]
    if tail:
        o_ref[...] = acc[:, : n - tail].astype(o_ref.dtype)
        tail_refs[0][...] = acc[:, n - tail :]
    else:
        o_ref[...] = acc.astype(o_ref.dtype)


def _inproj(xb, w, tail):
    t, d = xb.shape
    n = w.shape[1]
    tm = min(ROW_TILE, t)
    out_shape = [jax.ShapeDtypeStruct((t, n - tail), BF16)]
    out_specs = [pl.BlockSpec((tm, n - tail), lambda i: (i, 0))]
    if tail:
        out_shape.append(jax.ShapeDtypeStruct((t, tail), F32))
        out_specs.append(pl.BlockSpec((tm, tail), lambda i: (i, 0)))
    return pl.pallas_call(
        functools.partial(_inproj_kernel, tail=tail),
        grid=(t // tm,),
        in_specs=[pl.BlockSpec((tm, d), lambda i: (i, 0)), pl.BlockSpec((d, n), lambda i: (0, 0))],
        out_specs=out_specs,
        out_shape=out_shape,
        compiler_params=_cparams("parallel"),
        name="inproj",
    )(xb, w)


def _attn_kernel(*refs, heads, dk, dv, chunk, tq, tk, seq, use_bias):
    if use_bias:
        q_ref, k_ref, vt_ref, gate_ref, gb_ref, g_ref, o_ref, kb_ref, m_ref, acc_ref = refs
    else:
        q_ref, k_ref, vt_ref, g_ref, o_ref, m_ref, acc_ref = refs
    paired = dk < LANE
    dkp = LANE if paired else dk
    dva = dv + ONES_ROWS
    i = pl.program_id(1)
    lane = lax.broadcasted_iota(jnp.int32, (tq, LANE), 1)

    if use_bias:
        @pl.when(i == 0)
        def _():
            r = lax.broadcasted_iota(jnp.int32, (tk, tk), 0)
            c = lax.broadcasted_iota(jnp.int32, (tk, tk), 1)
            tri = jnp.where(c <= r, 1.0, 0.0).astype(BF16)
            head_lane = lax.broadcasted_iota(jnp.int32, (tk, LANE), 1) < heads
            carry = jnp.zeros((1, LANE), F32)
            for j in range(seq // tk):
                rows = slice(j * tk, (j + 1) * tk)
                log_f = -_softplus(-(gate_ref[rows, :] + gb_ref[...])) * LOG2E
                cum = _dot_exact_lhs(tri, log_f) + carry
                carry = cum[tk - 1 : tk, :]
                hi, mid, lo = (jnp.where(head_lane, p.astype(F32), 0.0) for p in _split3(cum))
                pieces = hi + pltpu.roll(mid, heads, axis=1) + pltpu.roll(lo, 2 * heads, axis=1)
                kb_ref[rows, :] = (-pieces).astype(BF16)

    m_ref[...] = jnp.full(m_ref.shape, -1e30, F32)
    acc_ref[...] = jnp.zeros(acc_ref.shape, F32)

    qs = []
    for h in range(heads):
        if paired:
            slab = q_ref[:, (h // 2) * LANE : (h // 2 + 1) * LANE]
            q = jnp.where(lane // dk == h % 2, slab, jnp.zeros_like(slab))
        else:
            q = q_ref[:, h * dkp : (h + 1) * dkp]
        if use_bias:
            pick = jnp.logical_and(lane % heads == h, lane < 3 * heads)
            q = jnp.concatenate([q, jnp.where(pick, 1.0, 0.0).astype(q.dtype)], axis=1)
        qs.append(q)
    ones_rows = jnp.ones((ONES_ROWS, tk), BF16)

    def step(off, key_shift):
        masked = key_shift is not None
        ql = key_shift if masked else 0
        if masked:
            kr = lax.broadcasted_iota(jnp.int32, (tk, tq - ql), 0) + key_shift
            qc = lax.broadcasted_iota(jnp.int32, (tk, tq - ql), 1) + ql
            allowed = (kr // chunk) <= (qc // chunk)
        sts = []
        for h in range(heads):
            slab = h // 2 if paired else h
            k = k_ref[pl.ds(off, tk), slab * dkp : (slab + 1) * dkp]
            if use_bias:
                k = jnp.concatenate([k, kb_ref[pl.ds(off, tk), :]], axis=1)
            q = qs[h][ql:, :]
            sts.append(lax.dot_general(k, q, (((1,), (1,)), ((), ())), preferred_element_type=F32))
        ps, alphas = [], []
        for h in range(heads):
            st = sts[h]
            if masked:
                st = jnp.where(allowed, st, -1e30)
            m_old = m_ref[h, 0:1, ql:]
            m_new = jnp.maximum(m_old, jnp.max(st, axis=0, keepdims=True))
            m_ref[h, 0:1, ql:] = m_new
            ps.append(jnp.exp2(st - m_new).astype(BF16))
            alphas.append(jnp.exp2(m_old - m_new))
        for h in range(heads):
            vt_h = jnp.concatenate([vt_ref[h * dv : (h + 1) * dv, pl.ds(off, tk)], ones_rows], axis=0)
            rows = slice(h * dva, (h + 1) * dva)
            acc_ref[rows, ql:] = alphas[h] * acc_ref[rows, ql:] + jnp.dot(vt_h, ps[h], preferred_element_type=F32)

    def body(j, carry):
        step(pl.multiple_of(j * tk, tk), None)
        return carry

    lax.fori_loop(0, i * (tq // tk), body, 0)
    for d in range(tq // tk):
        step(pl.multiple_of(i * tq + d * tk, tk), d * tk)

    ot = jnp.concatenate(
        [acc_ref[h * dva : h * dva + dv, :] / acc_ref[h * dva + dv : h * dva + dv + 1, :] for h in range(heads)], axis=0
    )
    ot = ot * lax.rsqrt(jnp.mean(ot * ot, axis=0, keepdims=True) + RMS_EPS)
    o_ref[...] = (ot.T * g_ref[...]).astype(o_ref.dtype)


def _attention(q_arr, q_col, k_arr, k_col, vt, gate, gain, *, batch, seq, heads, dk, dv, chunk):
    bias = gate
    assert dk % LANE == 0 or (2 * dk == LANE and heads % 2 == 0)
    tq = min(ATTN_TILE, seq)
    tk = min(ATTN_KV_TILE, tq)
    nq = seq // tq
    t = batch * seq
    in_specs = [
        pl.BlockSpec((tq, heads * dk), lambda b, i: (b * nq + i, q_col)),
        pl.BlockSpec((seq, heads * dk), lambda b, i: (b, k_col)),
        pl.BlockSpec((None, heads * dv, seq), lambda b, i: (b, 0, 0)),
    ]
    args = [q_arr, k_arr, vt]
    scratch = []
    if bias is not None:
        in_specs += [pl.BlockSpec((seq, LANE), lambda b, i: (b, 0)), pl.BlockSpec((1, LANE), lambda b, i: (0, 0))]
        args += list(gate)
        assert 3 * heads <= LANE
        scratch.append(pltpu.VMEM((seq, LANE), BF16))
    in_specs.append(pl.BlockSpec((1, heads * dv), lambda b, i: (0, 0)))
    args.append(gain)
    scratch += [pltpu.VMEM((heads, 8, tq), F32), pltpu.VMEM((heads * (dv + ONES_ROWS), tq), F32)]
    return pl.pallas_call(
        functools.partial(_attn_kernel, heads=heads, dk=dk, dv=dv, chunk=chunk, tq=tq, tk=tk, seq=seq, use_bias=bias is not None),
        grid=(batch, nq),
        in_specs=in_specs,
        out_specs=pl.BlockSpec((tq, heads * dv), lambda b, i: (b * nq + i, 0)),
        out_shape=jax.ShapeDtypeStruct((t, heads * dv), BF16),
        scratch_shapes=scratch,
        compiler_params=_cparams("parallel", "arbitrary"),
        name="attention",
    )(*args)


def _mla_prep_kernel(u_ref, cs_ref, sn_ref, qg_ref, kvg_ref, wq_ref, wkv_ref, q_ref, k_ref, v_ref):
    u = u_ref[...].astype(F32)
    cs = cs_ref[...]
    sn = sn_ref[...]

    def rms(x, g):
        return x * lax.rsqrt(jnp.mean(x * x, axis=-1, keepdims=True) + RMS_EPS) * g

    qo = _bdot(rms(u[:, :MLA_Q_RANK], qg_ref[...]), wq_ref[...])
    kvo = _bdot(rms(u[:, MLA_Q_RANK : MLA_Q_RANK + MLA_KV_RANK], kvg_ref[...]), wkv_ref[...])
    c0 = MLA_Q_RANK + MLA_KV_RANK
    kpe = (u[:, c0 : c0 + LANE] * cs + u[:, c0 + LANE : c0 + 2 * LANE] * sn).astype(k_ref.dtype)
    nn = MLA_HEADS * MLA_NOPE
    for h in range(MLA_HEADS):
        a = h * 2 * LANE
        q_ref[:, a : a + LANE] = qo[:, h * LANE : (h + 1) * LANE].astype(q_ref.dtype)
        qpe = qo[:, nn + h * LANE : nn + (h + 1) * LANE] * cs + qo[:, 2 * nn + h * LANE : 2 * nn + (h + 1) * LANE] * sn
        q_ref[:, a + LANE : a + 2 * LANE] = qpe.astype(q_ref.dtype)
        k_ref[:, a : a + LANE] = kvo[:, h * LANE : (h + 1) * LANE].astype(k_ref.dtype)
        k_ref[:, a + LANE : a + 2 * LANE] = kpe
    v_ref[...] = kvo[:, nn:].T.astype(v_ref.dtype)


def _mla_prep(u_mla, cs, sn, qg, kvg, wq, wkv, *, batch, seq):
    t = u_mla.shape[0]
    tm = min(ROW_TILE, seq)
    nt = seq // tm
    row = lambda n: pl.BlockSpec((tm, n), lambda i: (i, 0))
    full = lambda a: pl.BlockSpec(a.shape, lambda i: (0,) * a.ndim)
    wide = MLA_HEADS * 2 * LANE
    dvs = MLA_HEADS * MLA_VD
    return pl.pallas_call(
        _mla_prep_kernel,
        grid=(t // tm,),
        in_specs=[row(u_mla.shape[1]), row(LANE), row(LANE), full(qg), full(kvg), full(wq), full(wkv)],
        out_specs=[row(wide), row(wide), pl.BlockSpec((None, dvs, tm), lambda i: (i // nt, 0, i % nt))],
        out_shape=[
            jax.ShapeDtypeStruct((t, wide), BF16),
            jax.ShapeDtypeStruct((t, wide), BF16),
            jax.ShapeDtypeStruct((batch, dvs, seq), BF16),
        ],
        compiler_params=_cparams("parallel"),
        name="mla_prep",
    )(u_mla, cs, sn, qg, kvg, wq, wkv)


def _gdn_stages(u_ref, t_ref, cw_ref, alog_ref, dtb_ref, ng_ref, o_ref, xs_ref, st_ref, *, tt, c):
    i = pl.program_id(1)
    gw, hd = GROUP_W, GDN_HD

    @pl.when(i == 0)
    def _():
        xs_ref[0:8, :] = jnp.zeros((8, 3 * gw), F32)
        st_ref[...] = jnp.zeros(st_ref.shape, F32)

    xs_ref[8 : 8 + tt, :] = u_ref[:, : 3 * gw].astype(F32)
    cw = cw_ref[...]
    conv = cw[0:1, :] * xs_ref[5 : 5 + tt, :]
    for j in range(1, GDN_CONV):
        conv = conv + cw[j : j + 1, :] * xs_ref[5 + j : 5 + j + tt, :]
    xs_ref[0:8, :] = xs_ref[tt : tt + 8, :]
    qkv = conv * _sigmoid(conv)
    yield

    tail = t_ref[...]
    beta = _sigmoid(tail[:, :LANE])
    g = -jnp.exp(alog_ref[...]) * _softplus(tail[:, LANE:] + dtb_ref[...])

    r, col, same, lower, strict = _chunk_masks(tt, c)
    tri = jnp.where(lower, 1.0, 0.0).astype(BF16)
    ones_blk = jnp.where(same, 1.0, 0.0).astype(BF16)
    gc = _dot_exact_lhs(tri, g)
    gtot = _dot_exact_lhs(ones_blk, g)
    gct = gc.T
    yield

    def l2n(x):
        return x * lax.rsqrt(jnp.sum(x * x, axis=-1, keepdims=True) + 1e-6)

    heads = range(GDN_HEADS)
    bf = lambda a: a.astype(BF16)
    lmats, attns, kbs, vbs, qds, kds, egs = [], [], [], [], [], [], []
    for h in heads:
        qf = l2n(qkv[:, h * hd : (h + 1) * hd]) * (hd**-0.5)
        qh = bf(qf)
        kf = l2n(qkv[:, gw + h * hd : gw + (h + 1) * hd])
        kh = bf(kf)
        vh = qkv[:, 2 * gw + h * hd : 2 * gw + (h + 1) * hd]
        gcol = gc[:, h : h + 1]
        dec = jnp.exp(jnp.minimum(gcol - gct[h : h + 1, :], 0.0))
        bcol = beta[:, h : h + 1]
        kb = kf * bcol
        eg = jnp.exp(gcol)
        lmats.append(jnp.where(strict, _bdot_nt(kb, kh) * dec, 0.0))
        attns.append(bf(jnp.where(lower, _bdot_nt(qh, kh) * dec, 0.0)))
        kbs.append(bf(kb * eg))
        vbs.append(bf(vh * bcol))
        qds.append(bf(qf * eg))
        kds.append(bf(kf * jnp.exp(gtot[:, h : h + 1] - gcol)))
        yield
    tinvs = yield from _tri_inv([-m for m in lmats], _tri_inv_masks(r, col, c))
    tinvs = [bf(t) for t in tinvs]
    uws = [jnp.dot(t, jnp.concatenate([vb, kb], axis=1), preferred_element_type=F32) for t, vb, kb in zip(tinvs, vbs, kbs)]
    uvals = [uw[:, :hd] for uw in uws]
    wcums = [bf(uw[:, hd:]) for uw in uws]
    yield
    states = [st_ref[h] for h in heads]
    outs = [[] for _ in heads]
    for cc in range(tt // c):
        rs = slice(cc * c, (cc + 1) * c)
        sbs = [bf(s) for s in states]
        vnews = [uvals[h][rs] - jnp.dot(wcums[h][rs], sbs[h], preferred_element_type=F32) for h in heads]
        yield
        for h in heads:
            outs[h].append(jnp.dot(qds[h][rs], sbs[h], preferred_element_type=F32) + _bdot(attns[h][rs, rs], vnews[h]))
            glast = jnp.exp(gtot[cc * c : cc * c + 1, h : h + 1])
            states[h] = states[h] * glast + _bdot_tn(kds[h][rs], vnews[h])
        yield
    for h in heads:
        st_ref[h] = states[h]
        o = jnp.concatenate(outs[h], axis=0)
        o = o * lax.rsqrt(jnp.mean(o * o, axis=-1, keepdims=True) + RMS_EPS) * ng_ref[...]
        z = u_ref[:, 3 * gw + h * hd : 3 * gw + (h + 1) * hd].astype(F32)
        o_ref[:, h * hd : (h + 1) * hd] = (o * (z * _sigmoid(z))).astype(o_ref.dtype)


def _rwkv_stages(u_ref, mu_ref, w0_ref, w2_ref, a0_ref, a2_ref, g2_ref, kk_ref, ka_ref, rk_ref, lng_ref, lnb_ref,
                 e_ref, o_ref, xs_ref, st_ref, *, tt, c):
    i = pl.program_id(1)
    gw, hd = GROUP_W, RWKV_HD

    @pl.when(i == 0)
    def _():
        xs_ref[0:8, :] = jnp.zeros((8, xs_ref.shape[1]), F32)
        st_ref[...] = jnp.zeros(st_ref.shape, F32)

    u = u_ref[...].astype(F32)
    xs_ref[8 : 8 + tt, :] = u
    prev = xs_ref[7 : 7 + tt, :]
    xs_ref[0:8, :] = xs_ref[tt : tt + 8, :]
    x = u + mu_ref[...] * (prev - u)
    rr = x[:, :gw]
    k = x[:, gw : 2 * gw]
    v = x[:, 2 * gw : 3 * gw]
    wa = x[:, 3 * gw : 3 * gw + LANE]
    glo = x[:, 3 * gw + LANE :]
    w = -_softplus(-(w0_ref[...] + _bdot(jnp.tanh(wa), w2_ref[...]))) - 0.5
    ld = -jnp.exp(w)
    gate_a = _sigmoid(a0_ref[...] + _bdot(wa, a2_ref[...]))
    gate_g = _bdot(_sigmoid(glo), g2_ref[...])
    e = e_ref[...]

    def segsum(y):
        hi = y.astype(BF16)
        lo = (y - hi.astype(F32)).astype(BF16)
        return jnp.dot(hi, e, preferred_element_type=F32) + jnp.dot(lo, e, preferred_element_type=F32)

    kkr = k * kk_ref[...]
    kk = kkr * lax.rsqrt(segsum(kkr * kkr) + 1e-6)
    k2 = k * (1.0 + (gate_a - 1.0) * ka_ref[...])
    bonus = segsum(rr * k2 * rk_ref[...]) * v
    yield

    r, col, same, lower, strict = _chunk_masks(tt, c)
    tri = jnp.where(lower, 1.0, 0.0).astype(BF16)
    ones_blk = jnp.where(same, 1.0, 0.0).astype(BF16)
    cum = _dot_exact_lhs(tri, ld)
    ctot = _dot_exact_lhs(ones_blk, ld)
    encum = jnp.exp(-cum)
    edec = jnp.exp(ctot - cum)
    kka = kk * gate_a
    rt = rr * jnp.exp(cum)
    at = -kk * jnp.exp(cum - ld)
    bt = kka * encum
    kt = k2 * encum
    bd = kka * edec
    kd = k2 * edec
    pc = jnp.exp(ctot)
    yield

    heads = range(RWKV_HEADS)
    bf = lambda a: a.astype(BF16)
    dot = lambda a, b: jnp.dot(a, b, preferred_element_type=F32)
    dot_nt = lambda a, b: lax.dot_general(a, b, (((1,), (1,)), ((), ())), preferred_element_type=F32)
    dot_tn = lambda a, b: lax.dot_general(a, b, (((0,), (0,)), ((), ())), preferred_element_type=F32)
    at_b, rt_b, bt_b, kt_b, v_b, bd_b, kd_b = bf(at), bf(rt), bf(bt), bf(kt), bf(v), bf(bd), bf(kd)
    sls = [slice(h * hd, (h + 1) * hd) for h in heads]
    strict_f = jnp.where(strict, 1.0, 0.0).astype(F32)
    lower_f = jnp.where(lower, 1.0, 0.0).astype(F32)
    mabs = [dot_nt(at_b[:, sl], bt_b[:, sl]) * strict_f for sl in sls]
    yield
    maks = [bf(dot_nt(at_b[:, sl], kt_b[:, sl]) * strict_f) for sl in sls]
    yield
    arbs = [bf(dot_nt(rt_b[:, sl], bt_b[:, sl]) * lower_f) for sl in sls]
    yield
    arks = [bf(dot_nt(rt_b[:, sl], kt_b[:, sl]) * lower_f) for sl in sls]
    yield
    tinvs = yield from _tri_inv(mabs, _tri_inv_masks(r, col, c))
    tinvs = [bf(t) for t in tinvs]
    wmats = [bf(dot(t, at_b[:, sl])) for t, sl in zip(tinvs, sls)]
    mkvs = [bf(dot(m, v_b[:, sl])) for m, sl in zip(maks, sls)]
    yield
    umats = [dot(t, mkv) for t, mkv in zip(tinvs, mkvs)]
    yconsts = [dot(m, v_b[:, sl]) for m, sl in zip(arks, sls)]
    yield
    states = [st_ref[h] for h in heads]
    outs = [[] for _ in heads]
    for cc in range(tt // c):
        rs = slice(cc * c, (cc + 1) * c)
        sbs = [bf(s) for s in states]
        sas = [dot_nt(wmats[h][rs], sbs[h]) + umats[h][rs] for h in heads]
        yield
        for h in heads:
            sab = bf(sas[h])
            outs[h].append(dot_nt(rt_b[rs, sls[h]], sbs[h]) + dot(arbs[h][rs, rs], sab) + yconsts[h][rs])
            states[h] = (states[h] * pc[cc * c : cc * c + 1, sls[h]] + dot_tn(sab, bd_b[rs, sls[h]])
                         + dot_tn(v_b[rs, sls[h]], kd_b[rs, sls[h]]))
        yield
    for h in heads:
        st_ref[h] = states[h]
    y = jnp.concatenate([jnp.concatenate(o, axis=0) for o in outs], axis=-1)
    mean = segsum(y) * (1.0 / hd)
    d = y - mean
    var = segsum(d * d) * (1.0 / hd)
    yn = d * lax.rsqrt(var + RWKV_GN_EPS) * lng_ref[...] + lnb_ref[...]
    o_ref[...] = ((yn + bonus) * gate_g).astype(o_ref.dtype)


def _scan_mixers_kernel(*refs, n_rwkv_in, n_gdn_in, nb, tt, c):
    rwkv_in = refs[:n_rwkv_in]
    gdn_in = refs[n_rwkv_in : n_rwkv_in + n_gdn_in]
    o_rwkv, o_gdn, xs_rwkv, st_rwkv, xs_gdn, st_gdn = refs[n_rwkv_in + n_gdn_in :]
    gens = []
    for s in range(nb):
        gens.append(_rwkv_stages(rwkv_in[0].at[s], *rwkv_in[1:], o_rwkv.at[s], xs_rwkv.at[s], st_rwkv.at[s], tt=tt, c=c))
        gens.append(_gdn_stages(gdn_in[0].at[s], gdn_in[1].at[s], *gdn_in[2:], o_gdn.at[s], xs_gdn.at[s], st_gdn.at[s],
                                tt=tt, c=c))
    _interleave(*gens)


def _scan_mixers(rwkv_args, gdn_args, *, batch, seq):
    tt = min(SCAN_TILE, seq)
    nt = seq // tt
    t = batch * seq
    nb = SCAN_BATCH_ROWS if batch % SCAN_BATCH_ROWS == 0 else 1
    by_batch = lambda a: a.reshape(batch // nb, nb, seq, a.shape[1])
    tile = lambda a: pl.BlockSpec((None, nb, tt, a.shape[1]), lambda b, i: (b, 0, i, 0))
    full = lambda a: pl.BlockSpec(a.shape, lambda b, i: (0,) * a.ndim)
    u_rwkv, u_gdn, gdn_tail = rwkv_args[0], gdn_args[0], gdn_args[1]
    in_specs = [tile(u_rwkv)] + [full(p) for p in rwkv_args[1:]] + [tile(u_gdn), tile(gdn_tail)] + [full(p) for p in gdn_args[2:]]
    out = pl.BlockSpec((None, nb, tt, GROUP_W), lambda b, i: (b, 0, i, 0))
    y_rwkv, y_gdn = pl.pallas_call(
        functools.partial(_scan_mixers_kernel, n_rwkv_in=len(rwkv_args), n_gdn_in=len(gdn_args), nb=nb, tt=tt, c=CHUNK),
        grid=(batch // nb, nt),
        in_specs=in_specs,
        out_specs=[out, out],
        out_shape=[jax.ShapeDtypeStruct((batch // nb, nb, seq, GROUP_W), BF16)] * 2,
        scratch_shapes=[
            pltpu.VMEM((nb, tt + 8, u_rwkv.shape[1]), F32),
            pltpu.VMEM((nb, RWKV_HEADS, RWKV_HD, RWKV_HD), F32),
            pltpu.VMEM((nb, tt + 8, 3 * GROUP_W), F32),
            pltpu.VMEM((nb, GDN_HEADS, GDN_HD, GDN_HD), F32),
        ],
        compiler_params=_cparams("parallel", "arbitrary"),
        name="rwkv7_gdn",
    )(by_batch(u_rwkv), *rwkv_args[1:], by_batch(u_gdn), by_batch(gdn_tail), *gdn_args[2:])
    return y_rwkv.reshape(t, GROUP_W), y_gdn.reshape(t, GROUP_W)


def _layer_norm(h, g, b):
    mu = jnp.mean(h, axis=-1, keepdims=True)
    d = h - mu
    var = jnp.mean(d * d, axis=-1, keepdims=True)
    return d * lax.rsqrt(var + LN_EPS) * g + b


def _route_rows(lg):
    neg = -1e30
    lane = lax.broadcasted_iota(jnp.int32, lg.shape, 1)
    lane_f = lane.astype(F32)
    first = lambda hit: jnp.min(jnp.where(hit, lane_f, float(LANE)), axis=-1, keepdims=True)
    is_g = lane < N_GROUPS
    gl = jnp.where(is_g, lg, neg)
    gmax = jnp.max(gl, axis=-1, keepdims=True)
    p_grp = 1.0 / jnp.sum(jnp.where(is_g, jnp.exp(gl - gmax), 0.0), axis=-1, keepdims=True)
    grp = first(gl == gmax)
    lo = N_GROUPS + grp * EXP_PER_GROUP
    el = jnp.where(jnp.logical_and(lane_f >= lo, lane_f < lo + EXP_PER_GROUP), lg, neg)
    v1 = jnp.max(el, axis=-1, keepdims=True)
    i1 = first(el == v1)
    el2 = jnp.where(lane_f == i1, neg, el)
    v2 = jnp.max(el2, axis=-1, keepdims=True)
    i2 = first(el2 == v2)
    r = jnp.exp(v2 - v1)
    g1 = p_grp / (1.0 + r)
    out = jnp.where(lane == 0, i1 - N_GROUPS, 0.0)
    out = jnp.where(lane == 1, i2 - N_GROUPS, out)
    out = jnp.where(lane == 2, g1, out)
    return jnp.where(lane == 3, g1 * r, out)


def _outproj_kernel(yf_ref, ym_ref, yr_ref, yg_ref, w_ref, x_ref, g_ref, b_ref, wr_ref, rb_ref, tril_ref,
                    xo_ref, xb_ref, lg_ref, cnt_ref, *, alpha):
    mixed = jnp.concatenate([yf_ref[...], ym_ref[...], yr_ref[...], yg_ref[...]], axis=1)
    acc = jnp.dot(mixed, w_ref[...], preferred_element_type=F32)
    xn = _layer_norm(alpha * x_ref[...] + acc, g_ref[...], b_ref[...])
    xo_ref[...] = xn
    xh = xn.astype(BF16)
    xb_ref[...] = xh
    xl = (xn - xh.astype(F32)).astype(BF16)
    d = lambda a, b: jnp.dot(a, b, preferred_element_type=F32)
    both_w = d(xh, wr_ref[...])
    routed = _route_rows(both_w[:, :LANE] + both_w[:, LANE:] + d(xl, wr_ref[:, :LANE]) + rb_ref[...])

    @pl.when(pl.program_id(0) == 0)
    def _():
        cnt_ref[...] = jnp.zeros(cnt_ref.shape, F32)

    lane = lax.broadcasted_iota(jnp.int32, routed.shape, 1).astype(F32)
    oh = [jnp.where(lane == routed[:, k : k + 1], 1.0, 0.0) for k in range(TOP_K)]
    both = oh[0] + oh[1]
    before = jnp.dot(tril_ref[...], both.astype(BF16), preferred_element_type=F32) + cnt_ref[0:1, :]
    for k in range(TOP_K):
        rank = jnp.sum(before * oh[k], axis=-1, keepdims=True)
        routed = jnp.where(lane == 2 * TOP_K + k, rank, routed)
    cnt_ref[...] = cnt_ref[...] + jnp.sum(both, axis=0, keepdims=True)
    lg_ref[...] = routed


def _outproj(ys, w_out, x, ln_g, ln_b, w_router, r_bias, alpha):
    t, d = x.shape
    tm = min(ROW_TILE, t)
    row = lambda n: pl.BlockSpec((tm, n), lambda i: (i, 0))
    full = lambda a: pl.BlockSpec(a.shape, lambda i: (0,) * a.ndim)
    idx = jnp.arange(tm)
    tril = (idx[None, :] < idx[:, None]).astype(BF16)
    return pl.pallas_call(
        functools.partial(_outproj_kernel, alpha=alpha),
        grid=(t // tm,),
        in_specs=[row(GROUP_W)] * 4
        + [full(w_out), row(d), full(ln_g), full(ln_b), full(w_router), full(r_bias), full(tril)],
        out_specs=[row(d), row(d), row(LANE), pl.BlockSpec((8, LANE), lambda i: (0, 0))],
        out_shape=[
            jax.ShapeDtypeStruct((t, d), F32),
            jax.ShapeDtypeStruct((t, d), BF16),
            jax.ShapeDtypeStruct((t, LANE), F32),
            jax.ShapeDtypeStruct((8, LANE), F32),
        ],
        compiler_params=_cparams("arbitrary"),
        name="outproj_ln_router",
    )(*ys, w_out, x, ln_g, ln_b, w_router, r_bias, tril)


def _expert_kernel(be_ref, nu_ref, x_ref, wg_ref, wu_ref, wd_ref, *rest, first_block):
    o_ref, wg_s, wu_s, wd_s = rest[-4:]
    i = pl.program_id(0)
    blk = i + first_block
    used = blk < nu_ref[0]

    @pl.when(jnp.logical_and(used, jnp.logical_or(i == 0, be_ref[blk] != be_ref[jnp.maximum(blk - 1, 0)])))
    def _():
        wg_s[...] = wg_ref[...].astype(BF16)
        wu_s[...] = wu_ref[...].astype(BF16)
        wd_s[...] = wd_ref[...].astype(BF16)

    @pl.when(used)
    def _():
        x = x_ref[...]
        a = jnp.dot(x, wg_s[...], preferred_element_type=F32)
        b = jnp.dot(x, wu_s[...], preferred_element_type=F32)
        hmid = (a * _sigmoid(a) * b).astype(BF16)
        o_ref[...] = jnp.dot(hmid, wd_s[...], preferred_element_type=F32).astype(o_ref.dtype)

    @pl.when(jnp.logical_not(used))
    def _():
        o_ref[...] = jnp.zeros(o_ref.shape, o_ref.dtype)


def _experts(block_weight, n_used, xs_part, w_gate, w_up, w_down, *, first_block, n_slots, y_prev):
    d = xs_part.shape[1]
    tb = MOE_TILE
    de = w_gate.shape[-1]
    weight = lambda shape: pl.BlockSpec((None,) + shape, lambda i, be, nu: (be[i + first_block], 0, 0))
    in_specs = [pl.BlockSpec((tb, d), lambda i, be, nu: (i, 0)), weight((d, de)), weight((d, de)), weight((de, d))]
    args = [block_weight, n_used, xs_part, w_gate, w_up, w_down]
    aliases = {}
    if y_prev is not None:
        in_specs.append(pl.BlockSpec(memory_space=pl.ANY))
        aliases = {len(args): 0}
        args.append(y_prev)
    grid_spec = pltpu.PrefetchScalarGridSpec(
        num_scalar_prefetch=2,
        grid=(xs_part.shape[0] // tb,),
        in_specs=in_specs,
        out_specs=pl.BlockSpec((tb, d), lambda i, be, nu: (i + first_block, 0)),
        scratch_shapes=[pltpu.VMEM((d, de), BF16), pltpu.VMEM((d, de), BF16), pltpu.VMEM((de, d), BF16)],
    )
    return pl.pallas_call(
        functools.partial(_expert_kernel, first_block=first_block),
        grid_spec=grid_spec,
        out_shape=jax.ShapeDtypeStruct((n_slots, d), BF16),
        input_output_aliases=aliases,
        compiler_params=_cparams("arbitrary"),
        name="experts",
    )(*args)


def _ln2_kernel(x_ref, y1_ref, y2_ref, r_ref, g_ref, b_ref, *rest, alpha):
    xo_ref, xb_ref = rest[-2:]
    routed = r_ref[...]
    moe = routed[:, TOP_K : TOP_K + 1] * y1_ref[...].astype(F32) + routed[:, TOP_K + 1 : TOP_K + 2] * y2_ref[...].astype(F32)
    xn = _layer_norm(alpha * x_ref[...] + moe, g_ref[...], b_ref[...])
    xo_ref[...] = xn
    xb_ref[...] = xn.astype(BF16)


def _ln2(x, y1, y2, routed, g, b, alpha, *, row0, prev):
    t, d = x.shape
    tm = min(ROW_TILE, y1.shape[0])
    b0 = row0 // tm
    row = lambda n: pl.BlockSpec((tm, n), lambda i: (i + b0, 0))
    part = pl.BlockSpec((tm, d), lambda i: (i, 0))
    full = lambda a: pl.BlockSpec(a.shape, lambda i: (0,) * a.ndim)
    in_specs = [row(d), part, part, row(LANE), full(g), full(b)]
    args = [x, y1, y2, routed, g, b]
    aliases = {}
    if prev is not None:
        in_specs += [pl.BlockSpec(memory_space=pl.ANY)] * 2
        aliases = {len(args): 0, len(args) + 1: 1}
        args += list(prev)
    return pl.pallas_call(
        functools.partial(_ln2_kernel, alpha=alpha),
        grid=(y1.shape[0] // tm,),
        in_specs=in_specs,
        out_specs=[row(d), row(d)],
        out_shape=[jax.ShapeDtypeStruct((t, d), F32), jax.ShapeDtypeStruct((t, d), BF16)],
        input_output_aliases=aliases,
        compiler_params=_cparams("parallel"),
        name="residual_ln2",
    )(*args)


def _pad_cols(w, n):
    return jnp.pad(w, [(0, 0)] * (w.ndim - 1) + [(0, n - w.shape[-1])])


def _rot_half_cols(w):
    half = w.shape[-1] // 2
    return jnp.concatenate([-w[..., half:], w[..., :half]], axis=-1)


def _prep_weights(p):
    w_in = p["w_in"]
    c_fox = 3 * GROUP_W + FOX_HEADS
    c_mla = MLA_Q_RANK + MLA_KV_RANK + MLA_ROPE
    c_rwkv = 3 * GROUP_W + 2 * 64 + 128
    o_mla = c_fox
    o_rwkv = o_mla + c_mla
    o_gdn = o_rwkv + c_rwkv
    row = lambda a: a[:, None, :].astype(F32)

    w_fox = jnp.concatenate(
        [w_in[..., :GROUP_W] * (FOX_HD**-0.5 * LOG2E), w_in[..., GROUP_W : 3 * GROUP_W], _pad_cols(w_in[..., 3 * GROUP_W : c_fox], LANE)],
        axis=-1,
    )
    kpe_w = w_in[..., o_mla + MLA_Q_RANK + MLA_KV_RANK : o_mla + c_mla]
    w_mla = jnp.concatenate(
        [w_in[..., o_mla : o_mla + MLA_Q_RANK + MLA_KV_RANK], _pad_cols(kpe_w, LANE), _pad_cols(_rot_half_cols(kpe_w), LANE)],
        axis=-1,
    )
    w_rwkv = w_in[..., o_rwkv:o_gdn]
    g0 = o_gdn + 4 * GROUP_W
    w_gdn = jnp.concatenate(
        [w_in[..., o_gdn:g0], _pad_cols(w_in[..., g0 : g0 + GDN_HEADS], LANE), _pad_cols(w_in[..., g0 + GDN_HEADS :], LANE)],
        axis=-1,
    )

    nl = w_in.shape[0]
    scale = (MLA_NOPE + MLA_ROPE) ** -0.5 * LOG2E
    wq = p["mla_w_uq"].reshape(nl, MLA_Q_RANK, MLA_HEADS, MLA_NOPE + MLA_ROPE) * scale
    wq_nope = wq[..., :MLA_NOPE].reshape(nl, MLA_Q_RANK, -1)
    wq_pe = wq[..., MLA_NOPE:]
    wq_p = jnp.concatenate(
        [wq_nope, _pad_cols(wq_pe, LANE).reshape(nl, MLA_Q_RANK, -1), _pad_cols(_rot_half_cols(wq_pe), LANE).reshape(nl, MLA_Q_RANK, -1)],
        axis=-1,
    )
    wkv = p["mla_w_ukv"].reshape(nl, MLA_KV_RANK, MLA_HEADS, MLA_NOPE + MLA_VD)
    wkv_p = jnp.concatenate([wkv[..., :MLA_NOPE].reshape(nl, MLA_KV_RANK, -1), wkv[..., MLA_NOPE:].reshape(nl, MLA_KV_RANK, -1)], axis=-1)

    zeros64 = jnp.zeros((nl, 64, GROUP_W), F32)
    w_router = _pad_cols(jnp.concatenate([p["moe_w_grp"], p["moe_w_exp"]], axis=-1), LANE)
    wr_hi = w_router.astype(BF16)
    wr_lo = (w_router - wr_hi.astype(F32)).astype(BF16)
    return dict(
        w_fox=w_fox.astype(BF16), w_mla=w_mla.astype(BF16), w_rwkv=w_rwkv.astype(BF16), w_gdn=w_gdn.astype(BF16),
        layer=jnp.arange(nl, dtype=jnp.int32),
        fox_b_f=row(_pad_cols(p["fox_b_f"], LANE)), fox_out_g=row(p["fox_out_g"]),
        mla_qg=row(p["mla_q_norm_g"]), mla_kvg=row(p["mla_kv_norm_g"]), mla_wq=wq_p.astype(BF16), mla_wkv=wkv_p.astype(BF16),
        mla_out_g=row(p["mla_out_g"]),
        rwkv_mu=row(p["rwkv_mu"]), rwkv_w0=row(p["rwkv_w0"]),
        rwkv_w2=jnp.concatenate([p["rwkv_w2"], zeros64], axis=1).astype(BF16),
        rwkv_a0=row(p["rwkv_a0"]), rwkv_a2=jnp.concatenate([zeros64, p["rwkv_a2"]], axis=1).astype(BF16),
        rwkv_g2=p["rwkv_g2"].astype(BF16), rwkv_k_k=row(p["rwkv_k_k"]), rwkv_k_a=row(p["rwkv_k_a"]),
        rwkv_r_k=row(p["rwkv_r_k"]), rwkv_ln_g=row(p["rwkv_ln_g"]), rwkv_ln_b=row(p["rwkv_ln_b"]),
        gdn_conv_w=p["gdn_conv_w"].astype(F32), gdn_a_log=row(_pad_cols(p["gdn_a_log"], LANE)),
        gdn_dt_bias=row(_pad_cols(p["gdn_dt_bias"], LANE)), gdn_norm_g=row(p["gdn_norm_g"]),
        w_out=p["w_out"].astype(BF16), ln1_g=row(p["ln1_g"]), ln1_b=row(p["ln1_b"]),
        w_router=jnp.concatenate([wr_hi, wr_lo], axis=-1), r_bias=row(_pad_cols(jnp.concatenate([p["moe_b_grp"], p["moe_b_exp"]], axis=-1), LANE)),
        ln2_g=row(p["ln2_g"]), ln2_b=row(p["ln2_b"]),
    )


def _route(routed, counts, tb):
    t = routed.shape[0]
    a = t * TOP_K
    n_blocks = (a + N_EXPERTS * (tb - 1) + tb - 1) // tb
    n_slots = n_blocks * tb
    expert = routed[:, :TOP_K].astype(jnp.int32)
    rank = routed[:, 2 * TOP_K : 3 * TOP_K].astype(jnp.int32)
    padded = (counts + tb - 1) // tb * tb
    pend = jnp.cumsum(padded)
    pstart = pend - padded
    ids = jnp.arange(N_EXPERTS, dtype=jnp.int32)
    slot_of_assignment = jnp.sum(jnp.where(expert[..., None] == ids, pstart, 0), axis=-1) + rank
    block_start = jnp.arange(n_blocks, dtype=jnp.int32) * tb
    block_expert = jnp.minimum(jnp.sum(block_start[:, None] >= pend, axis=-1), N_EXPERTS - 1).astype(jnp.int32)
    fill_end = jnp.cumsum(padded - counts)
    filler = jnp.arange(n_slots - a, dtype=jnp.int32)
    fill_key = jnp.sum(filler[:, None] >= fill_end, axis=-1).astype(jnp.int32)
    keys = jnp.concatenate([expert.reshape(a), fill_key])
    vals = jnp.concatenate([jnp.arange(a, dtype=jnp.int32) // TOP_K, filler % t])
    _, token_of_slot = lax.sort((keys, vals), num_keys=1, is_stable=True)
    n_used = (pend[-1:] // tb).astype(jnp.int32)
    return token_of_slot, block_expert, n_used, slot_of_assignment


def _layer(x, xb, cs, sn, w, moe_w, seg_ones, *, batch, seq, alpha):
    u_fox, fox_tail = _inproj(xb, w["w_fox"], LANE)
    (u_mla,) = _inproj(xb, w["w_mla"], 0)
    (u_rwkv,) = _inproj(xb, w["w_rwkv"], 0)
    u_gdn, gdn_tail = _inproj(xb, w["w_gdn"], 2 * LANE)

    vt_fox = u_fox[:, 2 * GROUP_W :].reshape(batch, seq, GROUP_W).transpose(0, 2, 1)
    y_fox = _attention(u_fox, 0, u_fox, 1, vt_fox, (fox_tail, w["fox_b_f"]), w["fox_out_g"], batch=batch, seq=seq,
                       heads=FOX_HEADS, dk=FOX_HD, dv=FOX_HD, chunk=1)

    q_mla, k_mla, vt_mla = _mla_prep(u_mla, cs, sn, w["mla_qg"], w["mla_kvg"], w["mla_wq"], w["mla_wkv"],
                                     batch=batch, seq=seq)
    y_mla = _attention(q_mla, 0, k_mla, 0, vt_mla, None, w["mla_out_g"], batch=batch, seq=seq,
                       heads=MLA_HEADS, dk=2 * LANE, dv=MLA_VD, chunk=CHUNK)

    y_rwkv, y_gdn = _scan_mixers(
        (u_rwkv, w["rwkv_mu"], w["rwkv_w0"], w["rwkv_w2"], w["rwkv_a0"], w["rwkv_a2"], w["rwkv_g2"],
         w["rwkv_k_k"], w["rwkv_k_a"], w["rwkv_r_k"], w["rwkv_ln_g"], w["rwkv_ln_b"], seg_ones),
        (u_gdn, gdn_tail, w["gdn_conv_w"], w["gdn_a_log"], w["gdn_dt_bias"], w["gdn_norm_g"]),
        batch=batch, seq=seq)

    x1, x1b, routed, counts = _outproj((y_fox, y_mla, y_rwkv, y_gdn), w["w_out"], x, w["ln1_g"], w["ln1_b"],
                                       w["w_router"], w["r_bias"], alpha)

    token_of_slot, block_expert, n_used, slot_of_assignment = _route(
        routed, counts[0, :N_EXPERTS].astype(jnp.int32), MOE_TILE)
    block_weight = block_expert + w["layer"] * N_EXPERTS
    n_slots = token_of_slot.shape[0]
    part = pl.cdiv(n_slots // MOE_TILE, MOE_PARTS) * MOE_TILE
    y_slots = None
    for lo in range(0, n_slots, part):
        y_slots = _experts(block_weight, n_used, x1b[token_of_slot[lo : lo + part]], *moe_w,
                           first_block=lo // MOE_TILE, n_slots=n_slots, y_prev=y_slots)
    out = None
    rows = x1.shape[0] // LN2_PARTS
    for r0 in range(0, x1.shape[0], rows):
        slots = slot_of_assignment[r0 : r0 + rows]
        out = _ln2(x1, y_slots[slots[:, 0]], y_slots[slots[:, 1]], routed, w["ln2_g"], w["ln2_b"], alpha,
                   row0=r0, prev=out)
    return tuple(out)


def kernel(x, positions, w_in, fox_b_f, fox_out_g, mla_q_norm_g, mla_kv_norm_g, mla_w_uq, mla_w_ukv, mla_out_g, rwkv_mu, rwkv_w0, rwkv_w2, rwkv_a0, rwkv_a2, rwkv_g2, rwkv_k_k, rwkv_k_a, rwkv_r_k, rwkv_ln_g, rwkv_ln_b, gdn_conv_w, gdn_a_log, gdn_dt_bias, gdn_norm_g, w_out, ln1_g, ln1_b, moe_w_grp, moe_b_grp, moe_w_exp, moe_b_exp, moe_w_gate, moe_w_up, moe_w_down, ln2_g, ln2_b):
    batch, seq, d = x.shape
    depth = w_in.shape[0]
    alpha = (2 * depth) ** 0.25
    params = dict(
        w_in=w_in, fox_b_f=fox_b_f, fox_out_g=fox_out_g, mla_q_norm_g=mla_q_norm_g, mla_kv_norm_g=mla_kv_norm_g,
        mla_w_uq=mla_w_uq, mla_w_ukv=mla_w_ukv, mla_out_g=mla_out_g, rwkv_mu=rwkv_mu, rwkv_w0=rwkv_w0, rwkv_w2=rwkv_w2,
        rwkv_a0=rwkv_a0, rwkv_a2=rwkv_a2, rwkv_g2=rwkv_g2, rwkv_k_k=rwkv_k_k, rwkv_k_a=rwkv_k_a, rwkv_r_k=rwkv_r_k,
        rwkv_ln_g=rwkv_ln_g, rwkv_ln_b=rwkv_ln_b, gdn_conv_w=gdn_conv_w, gdn_a_log=gdn_a_log, gdn_dt_bias=gdn_dt_bias,
        gdn_norm_g=gdn_norm_g, w_out=w_out, ln1_g=ln1_g, ln1_b=ln1_b, moe_w_grp=moe_w_grp, moe_b_grp=moe_b_grp,
        moe_w_exp=moe_w_exp, moe_b_exp=moe_b_exp, moe_w_gate=moe_w_gate, moe_w_up=moe_w_up, moe_w_down=moe_w_down,
        ln2_g=ln2_g, ln2_b=ln2_b,
    )
    weights = _prep_weights(params)

    half = MLA_ROPE // 2
    inv_freq = ROPE_THETA ** (-jnp.arange(half, dtype=F32) / half)
    ang = positions.astype(F32).reshape(batch * seq, 1) * inv_freq
    zpad = jnp.zeros((batch * seq, LANE - MLA_ROPE), F32)
    cs = jnp.concatenate([jnp.cos(ang), jnp.cos(ang), zpad], axis=-1)
    sn = jnp.concatenate([jnp.sin(ang), jnp.sin(ang), zpad], axis=-1)

    xf = x.reshape(batch * seq, d).astype(F32)

    stack = lambda a: a.reshape((-1,) + a.shape[2:])
    moe_w = (stack(moe_w_gate), stack(moe_w_up), stack(moe_w_down))
    seg = jnp.arange(GROUP_W) // RWKV_HD
    seg_ones = (seg[:, None] == seg[None, :]).astype(BF16)

    def body(carry, w):
        xc, xcb = carry
        return _layer(xc, xcb, cs, sn, w, moe_w, seg_ones, batch=batch, seq=seq, alpha=alpha), None

    (xf, _), _ = lax.scan(body, (xf, xf.astype(BF16)), weights)
    return xf.reshape(batch, seq, d).astype(x.dtype)
```
